```python
import math
import jax
import jax.numpy as jnp
from jax import lax
import numpy as np

D_MODEL = 1024
BATCH = 16
SEQ = 2048
DEPTH = 1

A_HEADS = 8
A_HEAD_DIM = 64
A_WIDTH = A_HEADS * A_HEAD_DIM
DILATED_PATTERNS = ((128, 1), (512, 4), (2048, 16))
BAND_BLOCK = 128

B_HEADS = 4
QK_NOPE_DIM = 128
QK_ROPE_DIM = 64
QK_HEAD_DIM = QK_NOPE_DIM + QK_ROPE_DIM
V_HEAD_DIM = 128
Q_LORA_RANK = 256
KV_LORA_RANK = 256
B_WIDTH = B_HEADS * V_HEAD_DIM
ROPE_THETA = 10000.0
Q_BLOCK = 128

MIX_WIDTH = A_WIDTH + B_WIDTH
IN_PROJ_WIDTH = 3 * A_WIDTH + Q_LORA_RANK + KV_LORA_RANK + QK_ROPE_DIM

NUM_BUCKETS = 32
MAX_DISTANCE = 2048

N_EXPERTS = 32
TOP_K = 4
D_FF = D_MODEL
SWIGLU_LIMIT = 7.0
SWIGLU_ALPHA = 1.702
EXPERT_BLOCK = 256

RMS_EPS = 1e-6

kernel_name = 'hybrid_dilated_mla_moe_block'


def rms_norm(x, g):
    xf = x.astype(jnp.float32)
    y = xf * lax.rsqrt(jnp.mean(xf * xf, axis=-1, keepdims=True) + RMS_EPS)
    return (y * g.astype(jnp.float32)).astype(x.dtype)


def t5_bucket(dist):
    max_exact = NUM_BUCKETS // 2
    df = jnp.maximum(dist, 1).astype(jnp.float32)
    log_bucket = max_exact + (jnp.log(df / max_exact) / math.log(MAX_DISTANCE / max_exact)
                              * (NUM_BUCKETS - max_exact)).astype(jnp.int32)
    log_bucket = jnp.minimum(log_bucket, NUM_BUCKETS - 1)
    return jnp.where(dist < max_exact, dist, log_bucket)


def apply_rope(x, cos, sin):
    x1, x2 = jnp.split(x, 2, axis=-1)
    c = cos[None, :, None, :].astype(x.dtype)
    s = sin[None, :, None, :].astype(x.dtype)
    return jnp.concatenate([x1 * c - x2 * s, x1 * s + x2 * c], axis=-1)


def dilated_branch(q, k, v, rel_bias, window, dilation):
    b, s, h, dh = q.shape
    steps = window // dilation
    n = b * dilation
    length = s // dilation
    nb = -(-length // BAND_BLOCK)
    lp = nb * BAND_BLOCK

    def to_classes(t):
        return t.reshape(b, length, dilation, h, dh).transpose(0, 2, 1, 3, 4).reshape(n, length, h, dh)

    def band(t):
        tp = jnp.pad(to_classes(t), ((0, 0), (BAND_BLOCK, lp - length), (0, 0), (0, 0)))
        tp = tp.reshape(n, nb + 1, BAND_BLOCK, h, dh)
        return jnp.concatenate([tp[:, :-1], tp[:, 1:]], axis=2)

    qb = jnp.pad(to_classes(q), ((0, 0), (0, lp - length), (0, 0), (0, 0))).reshape(n, nb, BAND_BLOCK, h, dh)
    kb = band(k)
    vb = band(v)

    qi = jnp.arange(BAND_BLOCK)[:, None]
    kj = jnp.arange(2 * BAND_BLOCK)[None, :]
    back = qi - kj + BAND_BLOCK
    blk = jnp.arange(nb)[:, None, None]
    valid = (back >= 0) & (back <= steps) & (blk * BAND_BLOCK + kj >= BAND_BLOCK)
    bias = rel_bias[t5_bucket(jnp.maximum(back, 0) * dilation)].astype(jnp.float32).transpose(2, 0, 1)

    scores = jnp.einsum('nbqhd,nbkhd->nbhqk', qb, kb).astype(jnp.float32) * (dh ** -0.5) + bias
    scores = jnp.where(valid[None, :, None], scores, -jnp.inf)
    m = jnp.max(scores, axis=-1, keepdims=True)
    p = jnp.exp(scores - m)
    den = jnp.sum(p, axis=-1)
    out = jnp.einsum('nbhqk,nbkhd->nbqhd', p, vb.astype(jnp.float32)) / den.transpose(0, 1, 3, 2)[..., None]
    lse = (m[..., 0] + jnp.log(den)).transpose(0, 1, 3, 2)

    def from_classes(t):
        t = t.reshape((n, lp) + t.shape[3:])[:, :length]
        t = t.reshape((b, dilation, length) + t.shape[2:])
        return jnp.swapaxes(t, 1, 2).reshape((b, s) + t.shape[3:])

    return from_classes(out), from_classes(lse)


def dilated_attention(q, k, v, rel_bias):
    results = [dilated_branch(q, k, v, rel_bias, w, d) for (w, d) in DILATED_PATTERNS]
    outs = jnp.stack([r[0] for r in results])
    lses = jnp.stack([r[1] for r in results])
    weights = jax.nn.softmax(lses, axis=0)
    return jnp.einsum('pbsh,pbshd->bshd', weights, outs).astype(q.dtype)


def causal_block_attention(q, k, v, scale):
    b, s, h, dqk = q.shape
    nq = s // Q_BLOCK
    qb = q.reshape(b, nq, Q_BLOCK, h, dqk).transpose(1, 0, 2, 3, 4)
    kpos = jnp.arange(s)

    def one_block(args):
        i, qblk = args
        qpos = i * Q_BLOCK + jnp.arange(Q_BLOCK)
        sc = jnp.einsum('bqhd,bkhd->bhqk', qblk, k).astype(jnp.float32) * scale
        sc = jnp.where(kpos[None, :] <= qpos[:, None], sc, -jnp.inf)
        p = jax.nn.softmax(sc, axis=-1)
        return jnp.einsum('bhqk,bkhd->bqhd', p.astype(v.dtype), v)

    out = lax.map(one_block, (jnp.arange(nq, dtype=jnp.int32), qb))
    return out.transpose(1, 0, 2, 3, 4).reshape(b, s, h, v.shape[-1])


def routed_experts(h, router_w, router_b, w_gate_up, b_gate_up, w_down, b_down):
    t, d = h.shape
    logits = (h @ router_w).astype(jnp.float32) + router_b.astype(jnp.float32)
    top_val, top_idx = lax.top_k(logits, TOP_K)
    gates = jax.nn.softmax(top_val, axis=-1)

    n_assign = t * TOP_K
    flat_e = top_idx.reshape(-1)
    order = jnp.argsort(flat_e)
    sorted_e = flat_e[order]
    tok = (order // TOP_K).astype(jnp.int32)
    counts = jnp.bincount(flat_e, length=N_EXPERTS)
    padded = (counts + EXPERT_BLOCK - 1) // EXPERT_BLOCK * EXPERT_BLOCK
    pad_end = jnp.cumsum(padded)
    pad_start = pad_end - padded
    start = jnp.cumsum(counts) - counts
    dest = pad_start[sorted_e] + (jnp.arange(n_assign) - start[sorted_e])

    n_blocks = -(-n_assign // EXPERT_BLOCK) + N_EXPERTS
    n_slots = n_blocks * EXPERT_BLOCK
    slot_tok = jnp.zeros((n_slots,), jnp.int32).at[dest].set(tok)
    slot_gate = jnp.zeros((n_slots,), jnp.float32).at[dest].set(gates.reshape(-1)[order])
    block_expert = jnp.clip(jnp.searchsorted(pad_end, jnp.arange(n_blocks) * EXPERT_BLOCK, side='right'),
                            0, N_EXPERTS - 1).astype(jnp.int32)
    xs = h[slot_tok].reshape(n_blocks, EXPERT_BLOCK, d)

    def expert_block(args):
        xb, e = args
        gu = xb @ w_gate_up[e] + b_gate_up[e]
        gate = jnp.minimum(gu[:, :D_FF], SWIGLU_LIMIT)
        up = jnp.clip(gu[:, D_FF:], -SWIGLU_LIMIT, SWIGLU_LIMIT)
        glu = gate * jax.nn.sigmoid(SWIGLU_ALPHA * gate)
        return ((up + 1.0) * glu) @ w_down[e] + b_down[e]

    ys = lax.map(expert_block, (xs, block_expert)).reshape(n_slots, d)
    ys = ys * slot_gate[:, None].astype(ys.dtype)
    return jax.ops.segment_sum(ys, slot_tok, num_segments=t)


def setup_inputs(seed: int = 0) -> dict:
    key = jax.random.key(seed)
    ks = jax.random.split(key, 20)
    f32 = jnp.float32

    def nrm(k, shape, scale):
        return jax.random.normal(k, shape, f32) * scale

    def gain(k, shape):
        return 1.0 + 0.02 * jax.random.normal(k, shape, f32)

    L = DEPTH
    return {
        'x': nrm(ks[0], (BATCH, SEQ, D_MODEL), 1.0),
        'attn_norm_g': gain(ks[1], (L, D_MODEL)),
        'w_in': nrm(ks[2], (L, D_MODEL, IN_PROJ_WIDTH), D_MODEL ** -0.5),
        'a_q_norm_g': gain(ks[3], (L, A_HEAD_DIM)),
        'a_k_norm_g': gain(ks[4], (L, A_HEAD_DIM)),
        'rel_bias': nrm(ks[5], (NUM_BUCKETS, A_HEADS), 0.5),
        'q_a_norm_g': gain(ks[6], (L, Q_LORA_RANK)),
        'w_q_b': nrm(ks[7], (L, Q_LORA_RANK, B_HEADS * QK_HEAD_DIM), Q_LORA_RANK ** -0.5),
        'kv_a_norm_g': gain(ks[8], (L, KV_LORA_RANK)),
        'w_kv_b': nrm(ks[9], (L, KV_LORA_RANK, B_HEADS * (QK_NOPE_DIM + V_HEAD_DIM)), KV_LORA_RANK ** -0.5),
        'b_q_norm_g': gain(ks[10], (L, QK_HEAD_DIM)),
        'b_k_norm_g': gain(ks[11], (L, QK_HEAD_DIM)),
        'w_out': nrm(ks[12], (L, MIX_WIDTH, D_MODEL), MIX_WIDTH ** -0.5),
        'ffn_norm_g': gain(ks[13], (L, D_MODEL)),
        'router_w': nrm(ks[14], (L, D_MODEL, N_EXPERTS), D_MODEL ** -0.5),
        'router_b': nrm(ks[15], (L, N_EXPERTS), 0.01),
        'w_gate_up': nrm(ks[16], (L, N_EXPERTS, D_MODEL, 2 * D_FF), D_MODEL ** -0.5),
        'b_gate_up': nrm(ks[17], (L, N_EXPERTS, 2 * D_FF), 0.02),
        'w_down': nrm(ks[18], (L, N_EXPERTS, D_FF, D_MODEL), D_FF ** -0.5),
        'b_down': nrm(ks[19], (L, N_EXPERTS, D_MODEL), 0.02),
    }


def reference(x, attn_norm_g, w_in, a_q_norm_g, a_k_norm_g, rel_bias, q_a_norm_g, w_q_b,
              kv_a_norm_g, w_kv_b, b_q_norm_g, b_k_norm_g, w_out, ffn_norm_g, router_w,
              router_b, w_gate_up, b_gate_up, w_down, b_down):
    b, s, d = x.shape
    pos = jnp.arange(s, dtype=jnp.float32)
    inv_freq = ROPE_THETA ** (-jnp.arange(0, QK_ROPE_DIM, 2, dtype=jnp.float32) / QK_ROPE_DIM)
    ang = pos[:, None] * inv_freq[None, :]
    cos, sin = jnp.cos(ang), jnp.sin(ang)
    splits = [A_WIDTH, 2 * A_WIDTH, 3 * A_WIDTH, 3 * A_WIDTH + Q_LORA_RANK,
              3 * A_WIDTH + Q_LORA_RANK + KV_LORA_RANK]

    for layer in range(DEPTH):
        h = rms_norm(x, attn_norm_g[layer])
        proj = h @ w_in[layer]
        qa, ka, va, cq, ckv, kpe = jnp.split(proj, splits, axis=-1)

        qa = rms_norm(qa.reshape(b, s, A_HEADS, A_HEAD_DIM), a_q_norm_g[layer])
        ka = rms_norm(ka.reshape(b, s, A_HEADS, A_HEAD_DIM), a_k_norm_g[layer])
        va = va.reshape(b, s, A_HEADS, A_HEAD_DIM)
        out_a = dilated_attention(qa, ka, va, rel_bias).reshape(b, s, A_WIDTH)

        qb = (rms_norm(cq, q_a_norm_g[layer]) @ w_q_b[layer]).reshape(b, s, B_HEADS, QK_HEAD_DIM)
        kv = (rms_norm(ckv, kv_a_norm_g[layer]) @ w_kv_b[layer]).reshape(b, s, B_HEADS, QK_NOPE_DIM + V_HEAD_DIM)
        k_nope, vb = kv[..., :QK_NOPE_DIM], kv[..., QK_NOPE_DIM:]
        k_pe = jnp.broadcast_to(kpe[:, :, None, :], (b, s, B_HEADS, QK_ROPE_DIM))
        kb = jnp.concatenate([k_nope, k_pe], axis=-1)
        qb = rms_norm(qb, b_q_norm_g[layer])
        kb = rms_norm(kb, b_k_norm_g[layer])
        qb = jnp.concatenate([qb[..., :QK_NOPE_DIM], apply_rope(qb[..., QK_NOPE_DIM:], cos, sin)], axis=-1)
        kb = jnp.concatenate([kb[..., :QK_NOPE_DIM], apply_rope(kb[..., QK_NOPE_DIM:], cos, sin)], axis=-1)
        out_b = causal_block_attention(qb, kb, vb, QK_HEAD_DIM ** -0.5).reshape(b, s, B_WIDTH)

        x = x + jnp.concatenate([out_a, out_b], axis=-1) @ w_out[layer]

        h2 = rms_norm(x, ffn_norm_g[layer]).reshape(b * s, d)
        y = routed_experts(h2, router_w[layer], router_b[layer], w_gate_up[layer], b_gate_up[layer],
                           w_down[layer], b_down[layer])
        x = x + y.reshape(b, s, d)
    return x
```

```python
import functools
import math

import jax
import jax.numpy as jnp
from jax import lax
from jax.experimental import pallas as pl
from jax.experimental.pallas import tpu as pltpu

A_HEADS = 8
A_HEAD_DIM = 64
A_WIDTH = A_HEADS * A_HEAD_DIM
DILATED_PATTERNS = ((128, 1), (512, 4), (2048, 16))
BAND_BLOCK = 128

B_HEADS = 4
QK_NOPE_DIM = 128
QK_ROPE_DIM = 64
QK_HEAD_DIM = QK_NOPE_DIM + QK_ROPE_DIM
V_HEAD_DIM = 128
Q_LORA_RANK = 256
KV_LORA_RANK = 256
B_WIDTH = B_HEADS * V_HEAD_DIM
ROPE_THETA = 10000.0

NUM_BUCKETS = 32
MAX_DISTANCE = 2048

N_EXPERTS = 32
TOP_K = 4
SWIGLU_LIMIT = 7.0
SWIGLU_ALPHA = 1.702
RMS_EPS = 1e-6

LANES = 128
MASK_VALUE = -1e30
PROJ_WIDTH = 3 * A_WIDTH + Q_LORA_RANK + KV_LORA_RANK + 2 * QK_ROPE_DIM
VMEM_LIMIT = 56 * 1024 * 1024

F32 = jnp.float32
BF16 = jnp.bfloat16


def _dot(a, b):
    return jnp.dot(a, b, preferred_element_type=F32)


def _dot_nt(a, b):
    return lax.dot_general(a, b, (((1,), (1,)), ((), ())), preferred_element_type=F32)


def _split_dot(x, m):
    hi = x.astype(BF16)
    lo = (x - hi.astype(F32)).astype(BF16)
    return _dot(hi, m) + _dot(lo, m)


def _params(n_parallel=1):
    return pltpu.CompilerParams(
        dimension_semantics=("arbitrary",) * n_parallel, vmem_limit_bytes=VMEM_LIMIT)


def _in_proj_kernel(x_ref, g_ref, w_ref, o_ref):
    x = x_ref[...]
    ms = jnp.mean(x * x, axis=-1, keepdims=True)
    h = (x * lax.rsqrt(ms + RMS_EPS) * g_ref[...]).astype(BF16)
    o_ref[...] = _dot(h, w_ref[...])


def _in_proj(x2, g, w, tm):
    t, d = x2.shape
    n = w.shape[1]
    return pl.pallas_call(
        _in_proj_kernel,
        grid=(t // tm,),
        in_specs=[pl.BlockSpec((tm, d), lambda i: (i, 0)),
                  pl.BlockSpec((1, d), lambda i: (0, 0)),
                  pl.BlockSpec((d, n), lambda i: (0, 0))],
        out_specs=pl.BlockSpec((tm, n), lambda i: (i, 0)),
        out_shape=jax.ShapeDtypeStruct((t, n), F32),
        compiler_params=_params(),
        name="in_proj",
    )(x2, g, w)


def _mla_prep_kernel(c_ref, kpe_ref, cs_ref, wq_ref, wkv_ref, gc_ref, gqk_ref, q_ref, k_ref, v_ref,
                     *, q_scale):
    c = c_ref[...]
    gc = gc_ref[...]

    def lora_norm(z, g):
        ms = jnp.mean(z * z, axis=-1, keepdims=True)
        return (z * lax.rsqrt(ms + RMS_EPS) * g).astype(BF16)

    cq = lora_norm(c[:, :Q_LORA_RANK], gc[:, :Q_LORA_RANK])
    ckv = lora_norm(c[:, Q_LORA_RANK:], gc[:, Q_LORA_RANK:])
    qb = _dot(cq, wq_ref[...])
    kvb = _dot(ckv, wkv_ref[...])
    kper = kpe_ref[...]
    cs = cs_ref[...]
    gqk = gqk_ref[...]
    tm = c.shape[0]

    row = lax.broadcasted_iota(jnp.int32, (LANES, LANES), 0)
    ones_all = jnp.ones((LANES, LANES), BF16)
    ones_lo = jnp.where(row < QK_ROPE_DIM, 1.0, 0.0).astype(BF16)
    lane = lax.broadcasted_iota(jnp.int32, (tm, LANES), 1)

    def rope(z, g_row):
        t = z * (g_row * cs)
        return jnp.where(lane < QK_ROPE_DIM, t + pltpu.roll(t, QK_ROPE_DIM, 1), 0.0)

    k_rope = rope(kper, gqk[3:4])
    kpe_ss = _split_dot(kper * kper, ones_lo)
    for h in range(B_HEADS):
        qn = qb[:, 2 * LANES * h: 2 * LANES * h + LANES]
        qr = qb[:, 2 * LANES * h + LANES: 2 * LANES * (h + 1)]
        ss = _split_dot(qn * qn, ones_all) + _split_dot(qr * qr, ones_lo)
        rs = lax.rsqrt(ss * (1.0 / QK_HEAD_DIM) + RMS_EPS) * q_scale
        q_ref[h, :, :LANES] = (qn * gqk[0:1] * rs).astype(BF16)
        q_ref[h, :, LANES:] = (rope(qr, gqk[1:2]) * rs).astype(BF16)
        kn = kvb[:, LANES * h: LANES * (h + 1)]
        ssk = _split_dot(kn * kn, ones_all) + kpe_ss
        rsk = lax.rsqrt(ssk * (1.0 / QK_HEAD_DIM) + RMS_EPS)
        k_ref[h, :, :LANES] = (kn * gqk[2:3] * rsk).astype(BF16)
        k_ref[h, :, LANES:] = (k_rope * rsk).astype(BF16)
    v_ref[...] = kvb[:, B_HEADS * QK_NOPE_DIM:].astype(BF16)


def _mla_prep(proj, cs, wq, wkv, gc, gqk, seq, tm):
    t = proj.shape[0]
    c_width = Q_LORA_RANK + KV_LORA_RANK
    c_block = 3 * A_WIDTH // c_width
    kpe_block = (3 * A_WIDTH + c_width) // LANES
    pos_blocks = seq // tm
    kernel = functools.partial(_mla_prep_kernel, q_scale=QK_HEAD_DIM ** -0.5)
    return pl.pallas_call(
        kernel,
        grid=(t // tm,),
        in_specs=[pl.BlockSpec((tm, c_width), lambda i: (i, c_block)),
                  pl.BlockSpec((tm, LANES), lambda i: (i, kpe_block)),
                  pl.BlockSpec((tm, LANES), lambda i: (i % pos_blocks, 0)),
                  pl.BlockSpec(wq.shape, lambda i: (0, 0)),
                  pl.BlockSpec(wkv.shape, lambda i: (0, 0)),
                  pl.BlockSpec(gc.shape, lambda i: (0, 0)),
                  pl.BlockSpec(gqk.shape, lambda i: (0, 0))],
        out_specs=[pl.BlockSpec((B_HEADS, tm, 2 * LANES), lambda i: (0, i, 0)),
                   pl.BlockSpec((B_HEADS, tm, 2 * LANES), lambda i: (0, i, 0)),
                   pl.BlockSpec((tm, B_WIDTH), lambda i: (i, 0))],
        out_shape=[jax.ShapeDtypeStruct((B_HEADS, t, 2 * LANES), BF16),
                   jax.ShapeDtypeStruct((B_HEADS, t, 2 * LANES), BF16),
                   jax.ShapeDtypeStruct((t, B_WIDTH), BF16)],
        compiler_params=_params(),
        name="mla_prep",
    )(proj, proj, cs, wq, wkv, gc, gqk)


def _mla_attention_kernel(q_ref, k_ref, v_ref, o_ref, *, tq):
    seq = q_ref.shape[0]
    row = lax.broadcasted_iota(jnp.int32, (tq, tq), 0)
    col = lax.broadcasted_iota(jnp.int32, (tq, tq), 1)
    diag_mask = jnp.where(col <= row, 0.0, MASK_VALUE)
    for i in range(seq // tq):
        q = q_ref[i * tq:(i + 1) * tq, :]
        s_diag = _dot_nt(q, k_ref[i * tq:(i + 1) * tq, :]) + diag_mask
        m = jnp.max(s_diag, axis=-1, keepdims=True)
        if i > 0:
            s_past = _dot_nt(q, k_ref[:i * tq, :])
            m = jnp.maximum(m, jnp.max(s_past, axis=-1, keepdims=True))
        p_diag = jnp.exp(s_diag - m)
        l = jnp.sum(p_diag, axis=-1, keepdims=True)
        o = _dot(p_diag.astype(BF16), v_ref[i * tq:(i + 1) * tq, :])
        if i > 0:
            p_past = jnp.exp(s_past - m)
            l = l + jnp.sum(p_past, axis=-1, keepdims=True)
            o = o + _dot(p_past.astype(BF16), v_ref[:i * tq, :])
        o_ref[i * tq:(i + 1) * tq, :] = (o / l).astype(o_ref.dtype)


def _mla_attention(qb, kb, vb, batch, seq, tq):
    t = vb.shape[0]
    kernel = functools.partial(_mla_attention_kernel, tq=tq)
    return pl.pallas_call(
        kernel,
        grid=(batch, B_HEADS),
        in_specs=[pl.BlockSpec((None, seq, 2 * LANES), lambda b, h: (h, b, 0)),
                  pl.BlockSpec((None, seq, 2 * LANES), lambda b, h: (h, b, 0)),
                  pl.BlockSpec((seq, V_HEAD_DIM), lambda b, h: (b, h))],
        out_specs=pl.BlockSpec((seq, V_HEAD_DIM), lambda b, h: (b, h)),
        out_shape=jax.ShapeDtypeStruct((t, B_WIDTH), BF16),
        compiler_params=_params(2),
        name="mla_attention",
    )(qb, kb, vb)


def _dilated_kernel(q_ref, k_ref, v_ref, bd_ref, g_ref, bias_ref, o_ref,
                    qs, ks, vs0, vs1, acc_o, acc_l, acc_m):
    seq = q_ref.shape[0]
    n_tiles = seq // BAND_BLOCK
    lane = lax.broadcasted_iota(jnp.int32, (BAND_BLOCK, LANES), 1)
    low = lane < A_HEAD_DIM
    bd = bd_ref[...]
    g = g_ref[...]

    def head_norm(z, g_row):
        ss = _split_dot(z * z, bd)
        return z * lax.rsqrt(ss * (1.0 / A_HEAD_DIM) + RMS_EPS) * g_row

    lane_s = lax.broadcasted_iota(jnp.int32, (seq, LANES), 1)
    qs[...] = head_norm(q_ref[...], g[0:1]) * (A_HEAD_DIM ** -0.5)
    ks[...] = head_norm(k_ref[...], g[1:2])
    v = v_ref[...]
    vs0[...] = jnp.where(lane_s < A_HEAD_DIM, v, 1.0)
    vs1[...] = jnp.where(lane_s < A_HEAD_DIM, 1.0, v)

    for p, (window, dil) in enumerate(DILATED_PATTERNS):
        blocks_per_class = n_tiles // dil
        span = BAND_BLOCK * dil

        def rows_at(start, dil=dil):
            if dil == 1:
                return pl.ds(start, BAND_BLOCK)
            return pl.ds(start, BAND_BLOCK, stride=dil)

        def tile(i, carry, p=p, dil=dil, blocks_per_class=blocks_per_class, span=span,
                 rows_at=rows_at):
            r = i // blocks_per_class
            j = i % blocks_per_class
            cur = rows_at(r + j * span)
            prev = rows_at(r + jnp.maximum(j - 1, 0) * span)
            no_prev = jnp.where(j == 0, MASK_VALUE, 0.0)
            q = qs[cur, :]
            k_cur = ks[cur, :].astype(BF16)
            k_prev = ks[prev, :].astype(BF16)
            outs = []
            maxes = []
            for a, vs in enumerate((vs0, vs1)):
                in_head = low if a == 0 else jnp.logical_not(low)
                qm = jnp.where(in_head, q, 0.0).astype(BF16)
                s_cur = _dot_nt(qm, k_cur) + bias_ref[p, a, 1]
                s_prev = _dot_nt(qm, k_prev) + bias_ref[p, a, 0] + no_prev
                m = jnp.max(jnp.maximum(s_cur, s_prev), axis=-1, keepdims=True)
                p_cur = jnp.exp(s_cur - m).astype(BF16)
                p_prev = jnp.exp(s_prev - m).astype(BF16)
                outs.append(_dot(p_cur, vs[cur, :].astype(BF16))
                            + _dot(p_prev, vs[prev, :].astype(BF16)))
                maxes.append(m)
            acc_o[p, cur, :] = jnp.where(low, outs[0], outs[1])
            acc_l[p, cur, :] = pltpu.roll(jnp.where(low, outs[1], outs[0]), A_HEAD_DIM, 1)
            acc_m[p, cur, :] = jnp.where(low, maxes[0], maxes[1])
            return carry

        lax.fori_loop(0, n_tiles, tile, 0)

    m_all = jnp.maximum(jnp.maximum(acc_m[0], acc_m[1]), acc_m[2])
    num = jnp.zeros((seq, LANES), F32)
    den = jnp.zeros((seq, LANES), F32)
    for p in range(len(DILATED_PATTERNS)):
        w = jnp.exp(acc_m[p] - m_all)
        num = num + w * acc_o[p]
        den = den + w * acc_l[p]
    o_ref[...] = (num / den).astype(o_ref.dtype)


def _dilated_attention(proj, bd, g, bias, batch, seq):
    t = proj.shape[0]
    pairs = A_WIDTH // LANES
    n_pat = len(DILATED_PATTERNS)
    return pl.pallas_call(
        _dilated_kernel,
        grid=(batch, pairs),
        in_specs=[pl.BlockSpec((seq, LANES), lambda b, c: (b, c)),
                  pl.BlockSpec((seq, LANES), lambda b, c: (b, pairs + c)),
                  pl.BlockSpec((seq, LANES), lambda b, c: (b, 2 * pairs + c)),
                  pl.BlockSpec((LANES, LANES), lambda b, c: (0, 0)),
                  pl.BlockSpec((None, 2, LANES), lambda b, c: (c, 0, 0)),
                  pl.BlockSpec((None, n_pat, 2, 2, BAND_BLOCK, BAND_BLOCK),
                               lambda b, c: (c, 0, 0, 0, 0, 0))],
        out_specs=pl.BlockSpec((seq, LANES), lambda b, c: (b, c)),
        out_shape=jax.ShapeDtypeStruct((t, A_WIDTH), BF16),
        scratch_shapes=[pltpu.VMEM((seq, LANES), F32)] * 4
                       + [pltpu.VMEM((n_pat, seq, LANES), F32)] * 3,
        compiler_params=_params(2),
        name="dilated_attention",
    )(proj, proj, proj, bd, g, bias)


def _t5_bucket(dist):
    max_exact = NUM_BUCKETS // 2
    df = jnp.maximum(dist, 1).astype(F32)
    log_bucket = max_exact + (jnp.log(df / max_exact) / math.log(MAX_DISTANCE / max_exact)
                              * (NUM_BUCKETS - max_exact)).astype(jnp.int32)
    log_bucket = jnp.minimum(log_bucket, NUM_BUCKETS - 1)
    return jnp.where(dist < max_exact, dist, log_bucket)


def _band_bias(rel_bias):
    qi = jnp.arange(BAND_BLOCK)[:, None]
    kj = jnp.arange(BAND_BLOCK)[None, :]
    tables = []
    for window, dil in DILATED_PATTERNS:
        steps = window // dil
        halves = []
        for back in (qi - kj + BAND_BLOCK, qi - kj):
            valid = (back >= 0) & (back <= steps)
            b = rel_bias[_t5_bucket(jnp.maximum(back, 0) * dil)].astype(F32)
            halves.append(jnp.where(valid[:, :, None], b, MASK_VALUE))
        tables.append(jnp.stack(halves))
    tab = jnp.stack(tables)
    tab = tab.transpose(4, 0, 1, 2, 3)
    tab = tab.reshape(A_HEADS // 2, 2, len(DILATED_PATTERNS), 2, BAND_BLOCK, BAND_BLOCK)
    return tab.transpose(0, 2, 1, 3, 4, 5)


def _out_proj_kernel(x_ref, a_ref, b_ref, w_ref, g_ref, rw_ref, rb_ref, x1_ref, h_ref, lg_ref):
    w = w_ref[...]
    x1 = x_ref[...] + _dot(a_ref[...], w[:A_WIDTH]) + _dot(b_ref[...], w[A_WIDTH:])
    x1_ref[...] = x1
    ms = jnp.mean(x1 * x1, axis=-1, keepdims=True)
    h = x1 * lax.rsqrt(ms + RMS_EPS) * g_ref[...]
    hi = h.astype(BF16)
    h_ref[...] = hi
    lo = (h - hi.astype(F32)).astype(BF16)
    rw_hi = rw_ref[0]
    lg_ref[...] = _dot(hi, rw_hi) + _dot(lo, rw_hi) + _dot(hi, rw_ref[1]) + rb_ref[...]


def _out_proj(x2, out_a, out_b, w, g, rw, rb, tm):
    t, d = x2.shape
    return pl.pallas_call(
        _out_proj_kernel,
        grid=(t // tm,),
        in_specs=[pl.BlockSpec((tm, d), lambda i: (i, 0)),
                  pl.BlockSpec((tm, A_WIDTH), lambda i: (i, 0)),
                  pl.BlockSpec((tm, B_WIDTH), lambda i: (i, 0)),
                  pl.BlockSpec(w.shape, lambda i: (0, 0)),
                  pl.BlockSpec((1, d), lambda i: (0, 0)),
                  pl.BlockSpec(rw.shape, lambda i: (0, 0, 0)),
                  pl.BlockSpec((1, LANES), lambda i: (0, 0))],
        out_specs=[pl.BlockSpec((tm, d), lambda i: (i, 0)),
                   pl.BlockSpec((tm, d), lambda i: (i, 0)),
                   pl.BlockSpec((tm, LANES), lambda i: (i, 0))],
        out_shape=[jax.ShapeDtypeStruct((t, d), F32),
                   jax.ShapeDtypeStruct((t, d), BF16),
                   jax.ShapeDtypeStruct((t, LANES), F32)],
        compiler_params=_params(),
        name="out_proj",
    )(x2, out_a, out_b, w, g, rw, rb)


def _expert_ffn_kernel(be_ref, nb_ref, x_ref, wgu_ref, bgu_ref, wd_ref, bd_ref, y_ref, wgu_s, wd_s):
    i = pl.program_id(0)
    d_ff = wd_ref.shape[0]

    @pl.when(i < nb_ref[0])
    def _():
        changed = jnp.logical_or(i == 0, be_ref[i] != be_ref[jnp.maximum(i - 1, 0)])

        @pl.when(changed)
        def _():
            wgu_s[...] = wgu_ref[...].astype(BF16)
            wd_s[...] = wd_ref[...].astype(BF16)

        gu = _dot(x_ref[...], wgu_s[...]) + bgu_ref[...]
        gate = jnp.minimum(gu[:, :d_ff], SWIGLU_LIMIT)
        up = jnp.clip(gu[:, d_ff:], -SWIGLU_LIMIT, SWIGLU_LIMIT)
        glu = gate * jax.nn.sigmoid(SWIGLU_ALPHA * gate)
        act = ((up + 1.0) * glu).astype(BF16)
        y_ref[...] = _dot(act, wd_s[...]) + bd_ref[...]


def _expert_ffn(block_expert, n_used, xs, w_gate_up, b_gate_up, w_down, b_down, bm):
    n_slots, d = xs.shape
    n_e, _, two_ff = w_gate_up.shape
    d_ff = two_ff // 2
    n_blocks = n_slots // bm
    grid_spec = pltpu.PrefetchScalarGridSpec(
        num_scalar_prefetch=2,
        grid=(n_blocks,),
        in_specs=[pl.BlockSpec((bm, d), lambda i, be, nb: (i, 0)),
                  pl.BlockSpec((None, d, two_ff), lambda i, be, nb: (be[i], 0, 0)),
                  pl.BlockSpec((None, 1, two_ff), lambda i, be, nb: (be[i], 0, 0)),
                  pl.BlockSpec((None, d_ff, d), lambda i, be, nb: (be[i], 0, 0)),
                  pl.BlockSpec((None, 1, d), lambda i, be, nb: (be[i], 0, 0))],
        out_specs=pl.BlockSpec((bm, d), lambda i, be, nb: (i, 0)),
        scratch_shapes=[pltpu.VMEM((d, two_ff), BF16), pltpu.VMEM((d_ff, d), BF16)],
    )
    return pl.pallas_call(
        _expert_ffn_kernel,
        grid_spec=grid_spec,
        out_shape=jax.ShapeDtypeStruct((n_slots, d), F32),
        compiler_params=_params(),
        name="expert_ffn",
    )(block_expert, n_used, xs, w_gate_up, b_gate_up.reshape(n_e, 1, two_ff),
      w_down, b_down.reshape(n_e, 1, d))


def _route(logits, bm):
    t = logits.shape[0]
    top_val, top_idx = lax.top_k(logits, TOP_K)
    gates = jax.nn.softmax(top_val, axis=-1)
    n_assign = t * TOP_K
    flat_e = top_idx.reshape(-1)
    order = jnp.argsort(flat_e)
    sorted_e = flat_e[order]
    tok = (order // TOP_K).astype(jnp.int32)
    counts = jnp.bincount(flat_e, length=N_EXPERTS)
    padded = (counts + bm - 1) // bm * bm
    pad_end = jnp.cumsum(padded)
    pad_start = pad_end - padded
    start = jnp.cumsum(counts) - counts
    dest_sorted = (pad_start[sorted_e] + (jnp.arange(n_assign) - start[sorted_e])).astype(jnp.int32)
    n_blocks = -(-n_assign // bm) + N_EXPERTS
    slot_tok = jnp.zeros((n_blocks * bm,), jnp.int32).at[dest_sorted].set(tok)
    dest = jnp.zeros((n_assign,), jnp.int32).at[order].set(dest_sorted).reshape(t, TOP_K)
    block_expert = jnp.clip(jnp.searchsorted(pad_end, jnp.arange(n_blocks) * bm, side='right'),
                            0, N_EXPERTS - 1).astype(jnp.int32)
    n_used = (pad_end[-1] // bm).astype(jnp.int32).reshape(1)
    return gates, slot_tok, dest, block_expert, n_used


def _rot_cols(w):
    half = w.shape[-1] // 2
    return jnp.concatenate([-w[..., half:], w[..., :half]], axis=-1)


def _swap_halves(g):
    half = g.shape[-1] // 2
    return jnp.concatenate([g[..., half:], g[..., :half]], axis=-1)


def kernel(x, attn_norm_g, w_in, a_q_norm_g, a_k_norm_g, rel_bias, q_a_norm_g, w_q_b, kv_a_norm_g,
           w_kv_b, b_q_norm_g, b_k_norm_g, w_out, ffn_norm_g, router_w, router_b, w_gate_up,
           b_gate_up, w_down, b_down):
    batch, seq, d = x.shape
    t = batch * seq
    depth = w_in.shape[0]
    tm = 512
    expert_bm = 256

    pos = jnp.arange(seq, dtype=F32)
    inv_freq = ROPE_THETA ** (-jnp.arange(0, QK_ROPE_DIM, 2, dtype=F32) / QK_ROPE_DIM)
    ang = pos[:, None] * inv_freq[None, :]
    cos, sin = jnp.cos(ang), jnp.sin(ang)
    cs = jnp.concatenate([cos, cos, sin, sin], axis=-1)

    row = jnp.arange(LANES)[:, None] // A_HEAD_DIM
    col = jnp.arange(LANES)[None, :] // A_HEAD_DIM
    head_block_ones = (row == col).astype(BF16)
    band_bias = _band_bias(rel_bias)

    x2 = x.reshape(t, d)
    for layer in range(depth):
        kpe_off = 3 * A_WIDTH + Q_LORA_RANK + KV_LORA_RANK
        w_kpe = w_in[layer][:, kpe_off:]
        w_in_r = jnp.concatenate([w_in[layer][:, :kpe_off], w_kpe, _rot_cols(w_kpe)], axis=1).astype(BF16)

        wq = w_q_b[layer].reshape(Q_LORA_RANK, B_HEADS, QK_HEAD_DIM)
        wq_rope = wq[..., QK_NOPE_DIM:]
        wq_r = jnp.concatenate([wq, _rot_cols(wq_rope)], axis=-1).reshape(Q_LORA_RANK, -1).astype(BF16)
        wkv = w_kv_b[layer].reshape(KV_LORA_RANK, B_HEADS, QK_NOPE_DIM + V_HEAD_DIM)
        wkv_r = jnp.concatenate([wkv[..., :QK_NOPE_DIM].reshape(KV_LORA_RANK, -1),
                                 wkv[..., QK_NOPE_DIM:].reshape(KV_LORA_RANK, -1)], axis=1).astype(BF16)
        gc = jnp.concatenate([q_a_norm_g[layer], kv_a_norm_g[layer]])[None, :]
        gq, gk = b_q_norm_g[layer], b_k_norm_g[layer]

        def rope_gain(gr):
            return jnp.concatenate([gr, _swap_halves(gr)])

        gqk = jnp.stack([gq[:QK_NOPE_DIM], rope_gain(gq[QK_NOPE_DIM:]),
                         gk[:QK_NOPE_DIM], rope_gain(gk[QK_NOPE_DIM:])])
        g_a = jnp.stack([jnp.tile(a_q_norm_g[layer], 2), jnp.tile(a_k_norm_g[layer], 2)])
        g_a = jnp.broadcast_to(g_a[None], (A_WIDTH // LANES, 2, LANES))

        proj = _in_proj(x2, attn_norm_g[layer][None, :], w_in_r, tm)
        qb, kb, vb = _mla_prep(proj, cs, wq_r, wkv_r, gc, gqk, seq, tm)
        out_b = _mla_attention(qb, kb, vb, batch, seq, 256)
        out_a = _dilated_attention(proj, head_block_ones, g_a, band_bias, batch, seq)

        rw = jnp.pad(router_w[layer], ((0, 0), (0, LANES - N_EXPERTS)))
        rw_hi = rw.astype(BF16)
        rw_lo = (rw - rw_hi.astype(F32)).astype(BF16)
        rb = jnp.pad(router_b[layer], (0, LANES - N_EXPERTS))[None, :]
        x1, h2, logits = _out_proj(x2, out_a, out_b, w_out[layer].astype(BF16), ffn_norm_g[layer][None, :],
                                   jnp.stack([rw_hi, rw_lo]), rb, tm)

        gates, slot_tok, dest, block_expert, n_used = _route(logits[:, :N_EXPERTS], expert_bm)
        xs = jnp.take(h2, slot_tok, axis=0)
        ys = _expert_ffn(block_expert, n_used, xs, w_gate_up[layer], b_gate_up[layer],
                         w_down[layer], b_down[layer], expert_bm)
        y = jnp.einsum('tk,tkd->td', gates, jnp.take(ys, dest.reshape(-1), axis=0).reshape(t, TOP_K, d))
        x2 = x1 + y
    return x2.reshape(batch, seq, d)
```

```python
import functools
import math

import jax
import jax.numpy as jnp
from jax import lax
from jax.experimental import pallas as pl
from jax.experimental.pallas import tpu as pltpu

A_HEADS = 8
A_HEAD_DIM = 64
A_WIDTH = A_HEADS * A_HEAD_DIM
DILATED_PATTERNS = ((128, 1), (512, 4), (2048, 16))
BAND_BLOCK = 128

B_HEADS = 4
QK_NOPE_DIM = 128
QK_ROPE_DIM = 64
QK_HEAD_DIM = QK_NOPE_DIM + QK_ROPE_DIM
V_HEAD_DIM = 128
Q_LORA_RANK = 256
KV_LORA_RANK = 256
B_WIDTH = B_HEADS * V_HEAD_DIM
ROPE_THETA = 10000.0

NUM_BUCKETS = 32
MAX_DISTANCE = 2048

N_EXPERTS = 32
TOP_K = 4
SWIGLU_LIMIT = 7.0
SWIGLU_ALPHA = 1.702
RMS_EPS = 1e-6

LANES = 128
MASK_VALUE = -1e30
PROJ_WIDTH = 3 * A_WIDTH + Q_LORA_RANK + KV_LORA_RANK + 2 * QK_ROPE_DIM
VMEM_LIMIT = 56 * 1024 * 1024

F32 = jnp.float32
BF16 = jnp.bfloat16


def _dot(a, b):
    return jnp.dot(a, b, preferred_element_type=F32)


def _dot_nt(a, b):
    return lax.dot_general(a, b, (((1,), (1,)), ((), ())), preferred_element_type=F32)


def _split_dot(x, m):
    hi = x.astype(BF16)
    lo = (x - hi.astype(F32)).astype(BF16)
    return _dot(hi, m) + _dot(lo, m)


def _params(n_parallel=1):
    return pltpu.CompilerParams(
        dimension_semantics=("arbitrary",) * n_parallel, vmem_limit_bytes=VMEM_LIMIT)


def _in_proj_kernel(x_ref, g_ref, w_ref, o_ref):
    x = x_ref[...]
    ms = jnp.mean(x * x, axis=-1, keepdims=True)
    h = (x * lax.rsqrt(ms + RMS_EPS) * g_ref[...]).astype(BF16)
    o_ref[...] = _dot(h, w_ref[...])


def _in_proj(x2, g, w, tm):
    t, d = x2.shape
    n = w.shape[1]
    return pl.pallas_call(
        _in_proj_kernel,
        grid=(t // tm,),
        in_specs=[pl.BlockSpec((tm, d), lambda i: (i, 0)),
                  pl.BlockSpec((1, d), lambda i: (0, 0)),
                  pl.BlockSpec((d, n), lambda i: (0, 0))],
        out_specs=pl.BlockSpec((tm, n), lambda i: (i, 0)),
        out_shape=jax.ShapeDtypeStruct((t, n), F32),
        compiler_params=_params(),
        name="in_proj",
    )(x2, g, w)


def _mla_prep_kernel(c_ref, kpe_ref, cs_ref, wq_ref, wkv_ref, gc_ref, gqk_ref, q_ref, k_ref, v_ref,
                     *, q_scale):
    c = c_ref[...]
    gc = gc_ref[...]

    def lora_norm(z, g):
        ms = jnp.mean(z * z, axis=-1, keepdims=True)
        return (z * lax.rsqrt(ms + RMS_EPS) * g).astype(BF16)

    cq = lora_norm(c[:, :Q_LORA_RANK], gc[:, :Q_LORA_RANK])
    ckv = lora_norm(c[:, Q_LORA_RANK:], gc[:, Q_LORA_RANK:])
    qb = _dot(cq, wq_ref[...])
    kvb = _dot(ckv, wkv_ref[...])
    kper = kpe_ref[...]
    cs = cs_ref[...]
    gqk = gqk_ref[...]
    tm = c.shape[0]

    row = lax.broadcasted_iota(jnp.int32, (LANES, LANES), 0)
    ones_all = jnp.ones((LANES, LANES), BF16)
    ones_lo = jnp.where(row < QK_ROPE_DIM, 1.0, 0.0).astype(BF16)
    lane = lax.broadcasted_iota(jnp.int32, (tm, LANES), 1)

    def rope(z, g_row):
        t = z * (g_row * cs)
        return jnp.where(lane < QK_ROPE_DIM, t + pltpu.roll(t, QK_ROPE_DIM, 1), 0.0)

    k_rope = rope(kper, gqk[3:4])
    kpe_ss = _split_dot(kper * kper, ones_lo)
    for h in range(B_HEADS):
        qn = qb[:, 2 * LANES * h: 2 * LANES * h + LANES]
        qr = qb[:, 2 * LANES * h + LANES: 2 * LANES * (h + 1)]
        ss = _split_dot(qn * qn, ones_all) + _split_dot(qr * qr, ones_lo)
        rs = lax.rsqrt(ss * (1.0 / QK_HEAD_DIM) + RMS_EPS) * q_scale
        q_ref[h, :, :LANES] = (qn * gqk[0:1] * rs).astype(BF16)
        q_ref[h, :, LANES:] = (rope(qr, gqk[1:2]) * rs).astype(BF16)
        kn = kvb[:, LANES * h: LANES * (h + 1)]
        ssk = _split_dot(kn * kn, ones_all) + kpe_ss
        rsk = lax.rsqrt(ssk * (1.0 / QK_HEAD_DIM) + RMS_EPS)
        k_ref[h, :, :LANES] = (kn * gqk[2:3] * rsk).astype(BF16)
        k_ref[h, :, LANES:] = (k_rope * rsk).astype(BF16)
    v_ref[...] = kvb[:, B_HEADS * QK_NOPE_DIM:].astype(BF16)


def _mla_prep(proj, cs, wq, wkv, gc, gqk, seq, tm):
    t = proj.shape[0]
    c_width = Q_LORA_RANK + KV_LORA_RANK
    c_block = 3 * A_WIDTH // c_width
    kpe_block = (3 * A_WIDTH + c_width) // LANES
    pos_blocks = seq // tm
    kernel = functools.partial(_mla_prep_kernel, q_scale=QK_HEAD_DIM ** -0.5)
    return pl.pallas_call(
        kernel,
        grid=(t // tm,),
        in_specs=[pl.BlockSpec((tm, c_width), lambda i: (i, c_block)),
                  pl.BlockSpec((tm, LANES), lambda i: (i, kpe_block)),
                  pl.BlockSpec((tm, LANES), lambda i: (i % pos_blocks, 0)),
                  pl.BlockSpec(wq.shape, lambda i: (0, 0)),
                  pl.BlockSpec(wkv.shape, lambda i: (0, 0)),
                  pl.BlockSpec(gc.shape, lambda i: (0, 0)),
                  pl.BlockSpec(gqk.shape, lambda i: (0, 0))],
        out_specs=[pl.BlockSpec((B_HEADS, tm, 2 * LANES), lambda i: (0, i, 0)),
                   pl.BlockSpec((B_HEADS, tm, 2 * LANES), lambda i: (0, i, 0)),
                   pl.BlockSpec((tm, B_WIDTH), lambda i: (i, 0))],
        out_shape=[jax.ShapeDtypeStruct((B_HEADS, t, 2 * LANES), BF16),
                   jax.ShapeDtypeStruct((B_HEADS, t, 2 * LANES), BF16),
                   jax.ShapeDtypeStruct((t, B_WIDTH), BF16)],
        compiler_params=_params(),
        name="mla_prep",
    )(proj, proj, cs, wq, wkv, gc, gqk)


def _mla_attention_kernel(q_ref, k_ref, v_ref, o_ref, *, tq):
    seq = q_ref.shape[0]
    row = lax.broadcasted_iota(jnp.int32, (tq, tq), 0)
    col = lax.broadcasted_iota(jnp.int32, (tq, tq), 1)
    diag_mask = jnp.where(col <= row, 0.0, MASK_VALUE)
    for i in range(seq // tq):
        q = q_ref[i * tq:(i + 1) * tq, :]
        s_diag = _dot_nt(q, k_ref[i * tq:(i + 1) * tq, :]) + diag_mask
        m = jnp.max(s_diag, axis=-1, keepdims=True)
        if i > 0:
            s_past = _dot_nt(q, k_ref[:i * tq, :])
            m = jnp.maximum(m, jnp.max(s_past, axis=-1, keepdims=True))
        p_diag = jnp.exp(s_diag - m)
        l = jnp.sum(p_diag, axis=-1, keepdims=True)
        o = _dot(p_diag.astype(BF16), v_ref[i * tq:(i + 1) * tq, :])
        if i > 0:
            p_past = jnp.exp(s_past - m)
            l = l + jnp.sum(p_past, axis=-1, keepdims=True)
            o = o + _dot(p_past.astype(BF16), v_ref[:i * tq, :])
        o_ref[i * tq:(i + 1) * tq, :] = (o / l).astype(o_ref.dtype)


def _mla_attention(qb, kb, vb, batch, seq, tq):
    t = vb.shape[0]
    kernel = functools.partial(_mla_attention_kernel, tq=tq)
    return pl.pallas_call(
        kernel,
        grid=(batch, B_HEADS),
        in_specs=[pl.BlockSpec((None, seq, 2 * LANES), lambda b, h: (h, b, 0)),
                  pl.BlockSpec((None, seq, 2 * LANES), lambda b, h: (h, b, 0)),
                  pl.BlockSpec((seq, V_HEAD_DIM), lambda b, h: (b, h))],
        out_specs=pl.BlockSpec((seq, V_HEAD_DIM), lambda b, h: (b, h)),
        out_shape=jax.ShapeDtypeStruct((t, B_WIDTH), BF16),
        compiler_params=_params(2),
        name="mla_attention",
    )(qb, kb, vb)


def _dilated_kernel(q_ref, k_ref, v_ref, bd_ref, g_ref, bias_ref, o_ref,
                    qs, ks, vs0, vs1, acc_o, acc_l, acc_m):
    seq = q_ref.shape[0]
    n_tiles = seq // BAND_BLOCK
    lane = lax.broadcasted_iota(jnp.int32, (BAND_BLOCK, LANES), 1)
    low = lane < A_HEAD_DIM
    bd = bd_ref[...]
    g = g_ref[...]

    def head_norm(z, g_row):
        ss = _split_dot(z * z, bd)
        return z * lax.rsqrt(ss * (1.0 / A_HEAD_DIM) + RMS_EPS) * g_row

    lane_s = lax.broadcasted_iota(jnp.int32, (seq, LANES), 1)
    qs[...] = head_norm(q_ref[...], g[0:1]) * (A_HEAD_DIM ** -0.5)
    ks[...] = head_norm(k_ref[...], g[1:2])
    v = v_ref[...]
    vs0[...] = jnp.where(lane_s < A_HEAD_DIM, v, 1.0)
    vs1[...] = jnp.where(lane_s < A_HEAD_DIM, 1.0, v)

    for p, (window, dil) in enumerate(DILATED_PATTERNS):
        blocks_per_class = n_tiles // dil
        span = BAND_BLOCK * dil

        def rows_at(start, dil=dil):
            if dil == 1:
                return pl.ds(start, BAND_BLOCK)
            return pl.ds(start, BAND_BLOCK, stride=dil)

        def tile(i, carry, p=p, dil=dil, blocks_per_class=blocks_per_class, span=span,
                 rows_at=rows_at):
            r = i // blocks_per_class
            j = i % blocks_per_class
            cur = rows_at(r + j * span)
            prev = rows_at(r + jnp.maximum(j - 1, 0) * span)
            no_prev = jnp.where(j == 0, MASK_VALUE, 0.0)
            q = qs[cur, :]
            k_cur = ks[cur, :].astype(BF16)
            k_prev = ks[prev, :].astype(BF16)
            outs = []
            maxes = []
            for a, vs in enumerate((vs0, vs1)):
                in_head = low if a == 0 else jnp.logical_not(low)
                qm = jnp.where(in_head, q, 0.0).astype(BF16)
                s_cur = _dot_nt(qm, k_cur) + bias_ref[p, a, 1]
                s_prev = _dot_nt(qm, k_prev) + bias_ref[p, a, 0] + no_prev
                m = jnp.max(jnp.maximum(s_cur, s_prev), axis=-1, keepdims=True)
                p_cur = jnp.exp(s_cur - m).astype(BF16)
                p_prev = jnp.exp(s_prev - m).astype(BF16)
                outs.append(_dot(p_cur, vs[cur, :].astype(BF16))
                            + _dot(p_prev, vs[prev, :].astype(BF16)))
                maxes.append(m)
            acc_o[p, cur, :] = jnp.where(low, outs[0], outs[1])
            acc_l[p, cur, :] = pltpu.roll(jnp.where(low, outs[1], outs[0]), A_HEAD_DIM, 1)
            acc_m[p, cur, :] = jnp.where(low, maxes[0], maxes[1])
            return carry

        lax.fori_loop(0, n_tiles, tile, 0)

    m_all = jnp.maximum(jnp.maximum(acc_m[0], acc_m[1]), acc_m[2])
    num = jnp.zeros((seq, LANES), F32)
    den = jnp.zeros((seq, LANES), F32)
    for p in range(len(DILATED_PATTERNS)):
        w = jnp.exp(acc_m[p] - m_all)
        num = num + w * acc_o[p]
        den = den + w * acc_l[p]
    o_ref[...] = (num / den).astype(o_ref.dtype)


def _dilated_attention(proj, bd, g, bias, batch, seq):
    t = proj.shape[0]
    pairs = A_WIDTH // LANES
    n_pat = len(DILATED_PATTERNS)
    return pl.pallas_call(
        _dilated_kernel,
        grid=(batch, pairs),
        in_specs=[pl.BlockSpec((seq, LANES), lambda b, c: (b, c)),
                  pl.BlockSpec((seq, LANES), lambda b, c: (b, pairs + c)),
                  pl.BlockSpec((seq, LANES), lambda b, c: (b, 2 * pairs + c)),
                  pl.BlockSpec((LANES, LANES), lambda b, c: (0, 0)),
                  pl.BlockSpec((None, 2, LANES), lambda b, c: (c, 0, 0)),
                  pl.BlockSpec((None, n_pat, 2, 2, BAND_BLOCK, BAND_BLOCK),
                               lambda b, c: (c, 0, 0, 0, 0, 0))],
        out_specs=pl.BlockSpec((seq, LANES), lambda b, c: (b, c)),
        out_shape=jax.ShapeDtypeStruct((t, A_WIDTH), BF16),
        scratch_shapes=[pltpu.VMEM((seq, LANES), F32)] * 4
                       + [pltpu.VMEM((n_pat, seq, LANES), F32)] * 3,
        compiler_params=_params(2),
        name="dilated_attention",
    )(proj, proj, proj, bd, g, bias)


def _t5_bucket(dist):
    max_exact = NUM_BUCKETS // 2
    df = jnp.maximum(dist, 1).astype(F32)
    log_bucket = max_exact + (jnp.log(df / max_exact) / math.log(MAX_DISTANCE / max_exact)
                              * (NUM_BUCKETS - max_exact)).astype(jnp.int32)
    log_bucket = jnp.minimum(log_bucket, NUM_BUCKETS - 1)
    return jnp.where(dist < max_exact, dist, log_bucket)


def _band_bias(rel_bias):
    qi = jnp.arange(BAND_BLOCK)[:, None]
    kj = jnp.arange(BAND_BLOCK)[None, :]
    tables = []
    for window, dil in DILATED_PATTERNS:
        steps = window // dil
        halves = []
        for back in (qi - kj + BAND_BLOCK, qi - kj):
            valid = (back >= 0) & (back <= steps)
            b = rel_bias[_t5_bucket(jnp.maximum(back, 0) * dil)].astype(F32)
            halves.append(jnp.where(valid[:, :, None], b, MASK_VALUE))
        tables.append(jnp.stack(halves))
    tab = jnp.stack(tables)
    tab = tab.transpose(4, 0, 1, 2, 3)
    tab = tab.reshape(A_HEADS // 2, 2, len(DILATED_PATTERNS), 2, BAND_BLOCK, BAND_BLOCK)
    return tab.transpose(0, 2, 1, 3, 4, 5)


def _pack_pairs(v):
    half = v.shape[1] // 2
    bits = pltpu.bitcast(v, jnp.uint32)
    return (bits[:, :half] >> 16) | (bits[:, half:] & jnp.uint32(0xFFFF0000))


def _unpack_pairs(p):
    return (pltpu.bitcast(p << 16, F32), pltpu.bitcast(p & jnp.uint32(0xFFFF0000), F32))


def _out_proj_kernel(x_ref, a_ref, b_ref, w_ref, g_ref, rw_ref, rb_ref,
                     x1_ref, hp_ref, route_ref, cnt_ref, carry):
    i = pl.program_id(0)
    tm = x_ref.shape[0]

    @pl.when(i == 0)
    def _():
        carry[...] = jnp.zeros_like(carry)

    w = w_ref[...]
    x1 = x_ref[...] + _dot(a_ref[...], w[:A_WIDTH]) + _dot(b_ref[...], w[A_WIDTH:])
    x1_ref[...] = x1
    ms = jnp.mean(x1 * x1, axis=-1, keepdims=True)
    h = x1 * lax.rsqrt(ms + RMS_EPS) * g_ref[...]
    hi = h.astype(BF16)
    hi_f = hi.astype(F32)
    hp_ref[...] = _pack_pairs(hi_f)
    lo = (h - hi_f).astype(BF16)
    rw_hi = rw_ref[0]
    logits = _dot(hi, rw_hi) + _dot(lo, rw_hi) + _dot(hi, rw_ref[1]) + rb_ref[...]

    lane = lax.broadcasted_iota(jnp.int32, (tm, LANES), 1).astype(F32)
    remaining = logits
    vals, hots = [], []
    for _ in range(TOP_K):
        m = jnp.max(remaining, axis=-1, keepdims=True)
        first = jnp.min(jnp.where(remaining == m, lane, float(LANES)), axis=-1, keepdims=True)
        hot = lane == first
        remaining = jnp.where(hot, -jnp.inf, remaining)
        vals.append(m)
        hots.append(hot)
    exps = [jnp.exp(v - vals[0]) for v in vals]
    den = exps[0] + exps[1] + exps[2] + exps[3]

    chosen = jnp.zeros((tm, LANES), F32)
    for hot in hots:
        chosen = chosen + jnp.where(hot, 1.0, 0.0)
    r = lax.broadcasted_iota(jnp.int32, (tm, tm), 0)
    c = lax.broadcasted_iota(jnp.int32, (tm, tm), 1)
    earlier = jnp.where(r > c, 1.0, 0.0).astype(BF16)
    before = carry[...] + _dot(earlier, chosen.astype(BF16))
    carry[...] = carry[...] + jnp.sum(chosen, axis=0, keepdims=True)
    cnt_ref[...] = jnp.broadcast_to(carry[...], cnt_ref.shape)

    route = jnp.zeros((tm, LANES), F32)
    for k in range(TOP_K):
        first = jnp.sum(jnp.where(hots[k], lane, 0.0), axis=-1, keepdims=True)
        rank = jnp.sum(jnp.where(hots[k], before, 0.0), axis=-1, keepdims=True)
        route = route + jnp.where(lane == float(k), first, 0.0)
        route = route + jnp.where(lane == float(TOP_K + k), rank, 0.0)
        route = route + jnp.where(lane == float(2 * TOP_K + k), exps[k] / den, 0.0)
    route_ref[...] = route


def _out_proj(x2, out_a, out_b, w, g, rw, rb, tm):
    t, d = x2.shape
    return pl.pallas_call(
        _out_proj_kernel,
        grid=(t // tm,),
        in_specs=[pl.BlockSpec((tm, d), lambda i: (i, 0)),
                  pl.BlockSpec((tm, A_WIDTH), lambda i: (i, 0)),
                  pl.BlockSpec((tm, B_WIDTH), lambda i: (i, 0)),
                  pl.BlockSpec(w.shape, lambda i: (0, 0)),
                  pl.BlockSpec((1, d), lambda i: (0, 0)),
                  pl.BlockSpec(rw.shape, lambda i: (0, 0, 0)),
                  pl.BlockSpec((1, LANES), lambda i: (0, 0))],
        out_specs=[pl.BlockSpec((tm, d), lambda i: (i, 0)),
                   pl.BlockSpec((tm, d // 2), lambda i: (i, 0)),
                   pl.BlockSpec((tm, LANES), lambda i: (i, 0)),
                   pl.BlockSpec((8, LANES), lambda i: (0, 0))],
        out_shape=[jax.ShapeDtypeStruct((t, d), F32),
                   jax.ShapeDtypeStruct((t, d // 2), jnp.uint32),
                   jax.ShapeDtypeStruct((t, LANES), F32),
                   jax.ShapeDtypeStruct((8, LANES), F32)],
        scratch_shapes=[pltpu.VMEM((1, LANES), F32)],
        compiler_params=_params(),
        name="out_proj",
    )(x2, out_a, out_b, w, g, rw, rb)


def _dispatch_kernel(pad_ref, end_ref, nb_ref, dest_hbm, h_ref, xs_hbm, dest_s, zeros, sem_i, sem_z, sem_o,
                     *, bm):
    i = pl.program_id(0)
    tm = h_ref.shape[0]
    n_blocks = xs_hbm.shape[0] // bm
    idx_copy = pltpu.make_async_copy(dest_hbm.at[i], dest_s, sem_i)
    idx_copy.start()

    @pl.when(i == 0)
    def _():
        zeros[...] = jnp.zeros_like(zeros)

        def zero_block(start):
            return pltpu.make_async_copy(zeros, xs_hbm.at[pl.ds(pl.multiple_of(start, bm), bm), :], sem_z)

        for e in range(N_EXPERTS):
            @pl.when(pad_ref[e] > 0)
            def _():
                zero_block(end_ref[e] - bm).start()

        def start_unused(b, carry):
            zero_block(b * bm).start()
            return carry

        def wait_unused(b, carry):
            zero_block(b * bm).wait()
            return carry

        lax.fori_loop(nb_ref[0], n_blocks, start_unused, 0)
        for e in range(N_EXPERTS):
            @pl.when(pad_ref[e] > 0)
            def _():
                zero_block(end_ref[e] - bm).wait()
        lax.fori_loop(nb_ref[0], n_blocks, wait_unused, 0)

    idx_copy.wait()

    def row_copy(t, k):
        return pltpu.make_async_copy(h_ref.at[pl.ds(t, 1), :],
                                     xs_hbm.at[pl.ds(dest_s[t * TOP_K + k], 1), :], sem_o)

    def body(t, carry):
        for k in range(TOP_K):
            row_copy(t, k).start()
        return carry

    lax.fori_loop(0, tm, body, 0, unroll=8)
    for _ in range(TOP_K):
        pltpu.make_async_copy(h_ref, xs_hbm.at[pl.ds(0, tm), :], sem_o).wait()


def _dispatch(padded, pad_end, n_used, dest, hp, n_slots, tm, bm):
    t, half = hp.shape
    grid_spec = pltpu.PrefetchScalarGridSpec(
        num_scalar_prefetch=3,
        grid=(t // tm,),
        in_specs=[pl.BlockSpec(memory_space=pl.ANY),
                  pl.BlockSpec((tm, half), lambda i, p, e, n: (i, 0))],
        out_specs=pl.BlockSpec(memory_space=pl.ANY),
        scratch_shapes=[pltpu.SMEM((tm * TOP_K,), jnp.int32),
                        pltpu.VMEM((bm, half), jnp.uint32),
                        pltpu.SemaphoreType.DMA(()),
                        pltpu.SemaphoreType.DMA(()),
                        pltpu.SemaphoreType.DMA(())],
    )
    return pl.pallas_call(
        functools.partial(_dispatch_kernel, bm=bm),
        grid_spec=grid_spec,
        out_shape=jax.ShapeDtypeStruct((n_slots, half), jnp.uint32),
        compiler_params=_params(),
        name="dispatch",
    )(padded, pad_end, n_used, dest, hp)


def _expert_ffn_kernel(be_ref, nb_ref, x_ref, wgu_ref, bgu_ref, wd_ref, bd_ref, y_ref, wgu_s, wd_s):
    i = pl.program_id(0)
    d_ff = wd_ref.shape[0]
    half = x_ref.shape[1]

    @pl.when(i >= nb_ref[0])
    def _():
        y_ref[...] = jnp.zeros_like(y_ref)

    @pl.when(i < nb_ref[0])
    def _():
        changed = jnp.logical_or(i == 0, be_ref[i] != be_ref[jnp.maximum(i - 1, 0)])

        @pl.when(changed)
        def _():
            wgu_s[...] = wgu_ref[...].astype(BF16)
            wd_s[...] = wd_ref[...].astype(BF16)

        x_lo, x_hi = _unpack_pairs(x_ref[...])
        gu = (_dot(x_lo.astype(BF16), wgu_s[:half, :]) + _dot(x_hi.astype(BF16), wgu_s[half:, :])
              + bgu_ref[...])
        gate = jnp.minimum(gu[:, :d_ff], SWIGLU_LIMIT)
        up = jnp.clip(gu[:, d_ff:], -SWIGLU_LIMIT, SWIGLU_LIMIT)
        glu = gate * jax.nn.sigmoid(SWIGLU_ALPHA * gate)
        act = ((up + 1.0) * glu).astype(BF16)
        y = _dot(act, wd_s[...]) + bd_ref[...]
        y_ref[...] = _pack_pairs(y.astype(BF16).astype(F32))


def _expert_ffn(block_expert, n_used, xs, w_gate_up, b_gate_up, w_down, b_down, bm):
    n_slots, half = xs.shape
    n_e, d, two_ff = w_gate_up.shape
    d_ff = two_ff // 2
    n_blocks = n_slots // bm

    def used(i, nb):
        return jnp.minimum(i, jnp.maximum(nb[0] - 1, 0))

    grid_spec = pltpu.PrefetchScalarGridSpec(
        num_scalar_prefetch=2,
        grid=(n_blocks,),
        in_specs=[pl.BlockSpec((bm, half), lambda i, be, nb: (used(i, nb), 0)),
                  pl.BlockSpec((None, d, two_ff), lambda i, be, nb: (be[i], 0, 0)),
                  pl.BlockSpec((None, 1, two_ff), lambda i, be, nb: (be[i], 0, 0)),
                  pl.BlockSpec((None, d_ff, d), lambda i, be, nb: (be[i], 0, 0)),
                  pl.BlockSpec((None, 1, d), lambda i, be, nb: (be[i], 0, 0))],
        out_specs=pl.BlockSpec((bm, half), lambda i, be, nb: (i, 0)),
        scratch_shapes=[pltpu.VMEM((d, two_ff), BF16), pltpu.VMEM((d_ff, d), BF16)],
    )
    return pl.pallas_call(
        _expert_ffn_kernel,
        grid_spec=grid_spec,
        out_shape=jax.ShapeDtypeStruct((n_slots, half), jnp.uint32),
        compiler_params=_params(),
        name="expert_ffn",
    )(block_expert, n_used, xs, w_gate_up, b_gate_up.reshape(n_e, 1, two_ff),
      w_down, b_down.reshape(n_e, 1, d))


def _combine_kernel(dest_hbm, route_ref, x1_ref, ys_hbm, o_ref, dest_s, rows, sem_i, sem_g):
    i = pl.program_id(0)
    tm = x1_ref.shape[0]
    half = x1_ref.shape[1] // 2
    idx_copy = pltpu.make_async_copy(dest_hbm.at[i], dest_s, sem_i)
    idx_copy.start()
    idx_copy.wait()

    def row_copy(t, k):
        return pltpu.make_async_copy(ys_hbm.at[pl.ds(dest_s[t * TOP_K + k], 1), :],
                                     rows.at[k, pl.ds(t, 1), :], sem_g)

    def body(t, carry):
        for k in range(TOP_K):
            row_copy(t, k).start()
        return carry

    lax.fori_loop(0, tm, body, 0, unroll=8)
    for k in range(TOP_K):
        pltpu.make_async_copy(ys_hbm.at[pl.ds(0, tm), :], rows.at[k], sem_g).wait()

    route = route_ref[...]
    y_lo = jnp.zeros((tm, half), F32)
    y_hi = jnp.zeros((tm, half), F32)
    for k in range(TOP_K):
        gate = route[:, 2 * TOP_K + k: 2 * TOP_K + k + 1]
        lo, hi = _unpack_pairs(rows[k])
        y_lo = y_lo + gate * lo
        y_hi = y_hi + gate * hi
    o_ref[:, :half] = x1_ref[:, :half] + y_lo
    o_ref[:, half:] = x1_ref[:, half:] + y_hi


def _combine(dest, route, x1, ys, tm):
    t, d = x1.shape
    half = d // 2
    return pl.pallas_call(
        _combine_kernel,
        grid=(t // tm,),
        in_specs=[pl.BlockSpec(memory_space=pl.ANY),
                  pl.BlockSpec((tm, LANES), lambda i: (i, 0)),
                  pl.BlockSpec((tm, d), lambda i: (i, 0)),
                  pl.BlockSpec(memory_space=pl.ANY)],
        out_specs=pl.BlockSpec((tm, d), lambda i: (i, 0)),
        out_shape=jax.ShapeDtypeStruct((t, d), F32),
        scratch_shapes=[pltpu.SMEM((tm * TOP_K,), jnp.int32),
                        pltpu.VMEM((TOP_K, tm, half), jnp.uint32),
                        pltpu.SemaphoreType.DMA(()),
                        pltpu.SemaphoreType.DMA(())],
        compiler_params=_params(),
        name="combine",
    )(dest, route, x1, ys)


def _slot_layout(route, counts, tm, bm):
    t = route.shape[0]
    counts = counts.astype(jnp.int32)
    padded = (counts + bm - 1) // bm * bm
    pad_end = jnp.cumsum(padded).astype(jnp.int32)
    pad_start = pad_end - padded
    idx = route[:, :TOP_K].astype(jnp.int32)
    rank = route[:, TOP_K:2 * TOP_K].astype(jnp.int32)
    dest = (jnp.take(pad_start, idx) + rank).reshape(t // tm, tm * TOP_K)
    n_blocks = -(-t * TOP_K // bm) + N_EXPERTS
    block_expert = jnp.clip(jnp.searchsorted(pad_end, jnp.arange(n_blocks) * bm, side='right'),
                            0, N_EXPERTS - 1).astype(jnp.int32)
    n_used = (pad_end[-1:] // bm).astype(jnp.int32)
    return padded, pad_end, dest, block_expert, n_used, n_blocks * bm


def _rot_cols(w):
    half = w.shape[-1] // 2
    return jnp.concatenate([-w[..., half:], w[..., :half]], axis=-1)


def _swap_halves(g):
    half = g.shape[-1] // 2
    return jnp.concatenate([g[..., half:], g[..., :half]], axis=-1)


def kernel(x, attn_norm_g, w_in, a_q_norm_g, a_k_norm_g, rel_bias, q_a_norm_g, w_q_b, kv_a_norm_g,
           w_kv_b, b_q_norm_g, b_k_norm_g, w_out, ffn_norm_g, router_w, router_b, w_gate_up,
           b_gate_up, w_down, b_down):
    batch, seq, d = x.shape
    t = batch * seq
    depth = w_in.shape[0]
    tm = 512
    expert_bm = 256

    pos = jnp.arange(seq, dtype=F32)
    inv_freq = ROPE_THETA ** (-jnp.arange(0, QK_ROPE_DIM, 2, dtype=F32) / QK_ROPE_DIM)
    ang = pos[:, None] * inv_freq[None, :]
    cos, sin = jnp.cos(ang), jnp.sin(ang)
    cs = jnp.concatenate([cos, cos, sin, sin], axis=-1)

    row = jnp.arange(LANES)[:, None] // A_HEAD_DIM
    col = jnp.arange(LANES)[None, :] // A_HEAD_DIM
    head_block_ones = (row == col).astype(BF16)
    band_bias = _band_bias(rel_bias)

    x2 = x.reshape(t, d)
    for layer in range(depth):
        kpe_off = 3 * A_WIDTH + Q_LORA_RANK + KV_LORA_RANK
        w_kpe = w_in[layer][:, kpe_off:]
        w_in_r = jnp.concatenate([w_in[layer][:, :kpe_off], w_kpe, _rot_cols(w_kpe)], axis=1).astype(BF16)

        wq = w_q_b[layer].reshape(Q_LORA_RANK, B_HEADS, QK_HEAD_DIM)
        wq_rope = wq[..., QK_NOPE_DIM:]
        wq_r = jnp.concatenate([wq, _rot_cols(wq_rope)], axis=-1).reshape(Q_LORA_RANK, -1).astype(BF16)
        wkv = w_kv_b[layer].reshape(KV_LORA_RANK, B_HEADS, QK_NOPE_DIM + V_HEAD_DIM)
        wkv_r = jnp.concatenate([wkv[..., :QK_NOPE_DIM].reshape(KV_LORA_RANK, -1),
                                 wkv[..., QK_NOPE_DIM:].reshape(KV_LORA_RANK, -1)], axis=1).astype(BF16)
        gc = jnp.concatenate([q_a_norm_g[layer], kv_a_norm_g[layer]])[None, :]
        gq, gk = b_q_norm_g[layer], b_k_norm_g[layer]

        def rope_gain(gr):
            return jnp.concatenate([gr, _swap_halves(gr)])

        gqk = jnp.stack([gq[:QK_NOPE_DIM], rope_gain(gq[QK_NOPE_DIM:]),
                         gk[:QK_NOPE_DIM], rope_gain(gk[QK_NOPE_DIM:])])
        g_a = jnp.stack([jnp.tile(a_q_norm_g[layer], 2), jnp.tile(a_k_norm_g[layer], 2)])
        g_a = jnp.broadcast_to(g_a[None], (A_WIDTH // LANES, 2, LANES))

        proj = _in_proj(x2, attn_norm_g[layer][None, :], w_in_r, tm)
        qb, kb, vb = _mla_prep(proj, cs, wq_r, wkv_r, gc, gqk, seq, tm)
        out_b = _mla_attention(qb, kb, vb, batch, seq, 256)
        out_a = _dilated_attention(proj, head_block_ones, g_a, band_bias, batch, seq)

        rw = jnp.pad(router_w[layer], ((0, 0), (0, LANES - N_EXPERTS)))
        rw_hi = rw.astype(BF16)
        rw_lo = (rw - rw_hi.astype(F32)).astype(BF16)
        rb = jnp.pad(router_b[layer], (0, LANES - N_EXPERTS), constant_values=-jnp.inf)[None, :]
        x1, hp, route, counts = _out_proj(x2, out_a, out_b, w_out[layer].astype(BF16),
                                          ffn_norm_g[layer][None, :], jnp.stack([rw_hi, rw_lo]), rb, tm)

        padded, pad_end, dest, block_expert, n_used, n_slots = _slot_layout(
            route, counts[0, :N_EXPERTS], tm, expert_bm)
        xs = _dispatch(padded, pad_end, n_used, dest, hp, n_slots, tm, expert_bm)
        ys = _expert_ffn(block_expert, n_used, xs, w_gate_up[layer], b_gate_up[layer],
                         w_down[layer], b_down[layer], expert_bm)
        x2 = _combine(dest, route, x1, ys, tm)
    return x2.reshape(batch, seq, d)
```

```python
import functools
import math

import jax
import jax.numpy as jnp
from jax import lax
from jax.experimental import pallas as pl
from jax.experimental.pallas import tpu as pltpu

A_HEADS = 8
A_HEAD_DIM = 64
A_WIDTH = A_HEADS * A_HEAD_DIM
DILATED_PATTERNS = ((128, 1), (512, 4), (2048, 16))
BAND_BLOCK = 128

B_HEADS = 4
QK_NOPE_DIM = 128
QK_ROPE_DIM = 64
QK_HEAD_DIM = QK_NOPE_DIM + QK_ROPE_DIM
V_HEAD_DIM = 128
Q_LORA_RANK = 256
KV_LORA_RANK = 256
B_WIDTH = B_HEADS * V_HEAD_DIM
ROPE_THETA = 10000.0

NUM_BUCKETS = 32
MAX_DISTANCE = 2048

N_EXPERTS = 32
TOP_K = 4
SWIGLU_LIMIT = 7.0
SWIGLU_ALPHA = 1.702
RMS_EPS = 1e-6

LANES = 128
MASK_VALUE = -1e30
PROJ_WIDTH = 3 * A_WIDTH + Q_LORA_RANK + KV_LORA_RANK + 2 * QK_ROPE_DIM
VMEM_LIMIT = 56 * 1024 * 1024

F32 = jnp.float32
BF16 = jnp.bfloat16


def _dot(a, b):
    return jnp.dot(a, b, preferred_element_type=F32)


def _dot_nt(a, b):
    return lax.dot_general(a, b, (((1,), (1,)), ((), ())), preferred_element_type=F32)


def _split_dot(x, m):
    hi = x.astype(BF16)
    lo = (x - hi.astype(F32)).astype(BF16)
    return _dot(hi, m) + _dot(lo, m)


def _params(n_parallel=1):
    return pltpu.CompilerParams(
        dimension_semantics=("arbitrary",) * n_parallel, vmem_limit_bytes=VMEM_LIMIT)


def _in_proj_kernel(x_ref, g_ref, w_ref, o_ref):
    x = x_ref[...]
    ms = jnp.mean(x * x, axis=-1, keepdims=True)
    h = (x * lax.rsqrt(ms + RMS_EPS) * g_ref[...]).astype(BF16)
    o_ref[...] = _dot(h, w_ref[...])


def _in_proj(x2, g, w, tm):
    t, d = x2.shape
    n = w.shape[1]
    return pl.pallas_call(
        _in_proj_kernel,
        grid=(t // tm,),
        in_specs=[pl.BlockSpec((tm, d), lambda i: (i, 0)),
                  pl.BlockSpec((1, d), lambda i: (0, 0)),
                  pl.BlockSpec((d, n), lambda i: (0, 0))],
        out_specs=pl.BlockSpec((tm, n), lambda i: (i, 0)),
        out_shape=jax.ShapeDtypeStruct((t, n), F32),
        compiler_params=_params(),
        name="in_proj",
    )(x2, g, w)


def _mla_prep_kernel(c_ref, kpe_ref, cs_ref, wq_ref, wkv_ref, gc_ref, gqk_ref, q_ref, k_ref, v_ref,
                     *, q_scale):
    c = c_ref[...]
    gc = gc_ref[...]

    def lora_norm(z, g):
        ms = jnp.mean(z * z, axis=-1, keepdims=True)
        return (z * lax.rsqrt(ms + RMS_EPS) * g).astype(BF16)

    cq = lora_norm(c[:, :Q_LORA_RANK], gc[:, :Q_LORA_RANK])
    ckv = lora_norm(c[:, Q_LORA_RANK:], gc[:, Q_LORA_RANK:])
    qb = _dot(cq, wq_ref[...])
    kvb = _dot(ckv, wkv_ref[...])
    kper = kpe_ref[...]
    cs = cs_ref[...]
    gqk = gqk_ref[...]
    tm = c.shape[0]

    row = lax.broadcasted_iota(jnp.int32, (LANES, LANES), 0)
    ones_all = jnp.ones((LANES, LANES), BF16)
    ones_lo = jnp.where(row < QK_ROPE_DIM, 1.0, 0.0).astype(BF16)
    lane = lax.broadcasted_iota(jnp.int32, (tm, LANES), 1)

    def rope(z, g_row):
        t = z * (g_row * cs)
        return jnp.where(lane < QK_ROPE_DIM, t + pltpu.roll(t, QK_ROPE_DIM, 1), 0.0)

    k_rope = rope(kper, gqk[3:4])
    kpe_ss = _split_dot(kper * kper, ones_lo)
    for h in range(B_HEADS):
        qn = qb[:, 2 * LANES * h: 2 * LANES * h + LANES]
        qr = qb[:, 2 * LANES * h + LANES: 2 * LANES * (h + 1)]
        ss = _split_dot(qn * qn, ones_all) + _split_dot(qr * qr, ones_lo)
        rs = lax.rsqrt(ss * (1.0 / QK_HEAD_DIM) + RMS_EPS) * q_scale
        q_ref[h, :, :LANES] = (qn * gqk[0:1] * rs).astype(BF16)
        q_ref[h, :, LANES:] = (rope(qr, gqk[1:2]) * rs).astype(BF16)
        kn = kvb[:, LANES * h: LANES * (h + 1)]
        ssk = _split_dot(kn * kn, ones_all) + kpe_ss
        rsk = lax.rsqrt(ssk * (1.0 / QK_HEAD_DIM) + RMS_EPS)
        k_ref[h, :, :LANES] = (kn * gqk[2:3] * rsk).astype(BF16)
        k_ref[h, :, LANES:] = (k_rope * rsk).astype(BF16)
    v_ref[...] = kvb[:, B_HEADS * QK_NOPE_DIM:].astype(BF16)


def _mla_prep(proj, cs, wq, wkv, gc, gqk, seq, tm):
    t = proj.shape[0]
    c_width = Q_LORA_RANK + KV_LORA_RANK
    c_block = 3 * A_WIDTH // c_width
    kpe_block = (3 * A_WIDTH + c_width) // LANES
    pos_blocks = seq // tm
    kernel = functools.partial(_mla_prep_kernel, q_scale=QK_HEAD_DIM ** -0.5)
    return pl.pallas_call(
        kernel,
        grid=(t // tm,),
        in_specs=[pl.BlockSpec((tm, c_width), lambda i: (i, c_block)),
                  pl.BlockSpec((tm, LANES), lambda i: (i, kpe_block)),
                  pl.BlockSpec((tm, LANES), lambda i: (i % pos_blocks, 0)),
                  pl.BlockSpec(wq.shape, lambda i: (0, 0)),
                  pl.BlockSpec(wkv.shape, lambda i: (0, 0)),
                  pl.BlockSpec(gc.shape, lambda i: (0, 0)),
                  pl.BlockSpec(gqk.shape, lambda i: (0, 0))],
        out_specs=[pl.BlockSpec((B_HEADS, tm, 2 * LANES), lambda i: (0, i, 0)),
                   pl.BlockSpec((B_HEADS, tm, 2 * LANES), lambda i: (0, i, 0)),
                   pl.BlockSpec((tm, B_WIDTH), lambda i: (i, 0))],
        out_shape=[jax.ShapeDtypeStruct((B_HEADS, t, 2 * LANES), BF16),
                   jax.ShapeDtypeStruct((B_HEADS, t, 2 * LANES), BF16),
                   jax.ShapeDtypeStruct((t, B_WIDTH), BF16)],
        compiler_params=_params(),
        name="mla_prep",
    )(proj, proj, cs, wq, wkv, gc, gqk)


def _mla_attention_kernel(q_ref, k_ref, v_ref, o_ref, *, tq):
    seq = q_ref.shape[0]
    row = lax.broadcasted_iota(jnp.int32, (tq, tq), 0)
    col = lax.broadcasted_iota(jnp.int32, (tq, tq), 1)
    diag_mask = jnp.where(col <= row, 0.0, MASK_VALUE)
    for i in range(seq // tq):
        q = q_ref[i * tq:(i + 1) * tq, :]
        s_diag = _dot_nt(q, k_ref[i * tq:(i + 1) * tq, :]) + diag_mask
        m = jnp.max(s_diag, axis=-1, keepdims=True)
        if i > 0:
            s_past = _dot_nt(q, k_ref[:i * tq, :])
            m = jnp.maximum(m, jnp.max(s_past, axis=-1, keepdims=True))
        p_diag = jnp.exp(s_diag - m)
        l = jnp.sum(p_diag, axis=-1, keepdims=True)
        o = _dot(p_diag.astype(BF16), v_ref[i * tq:(i + 1) * tq, :])
        if i > 0:
            p_past = jnp.exp(s_past - m)
            l = l + jnp.sum(p_past, axis=-1, keepdims=True)
            o = o + _dot(p_past.astype(BF16), v_ref[:i * tq, :])
        o_ref[i * tq:(i + 1) * tq, :] = (o / l).astype(o_ref.dtype)


def _mla_attention(qb, kb, vb, batch, seq, tq):
    t = vb.shape[0]
    kernel = functools.partial(_mla_attention_kernel, tq=tq)
    return pl.pallas_call(
        kernel,
        grid=(batch, B_HEADS),
        in_specs=[pl.BlockSpec((None, seq, 2 * LANES), lambda b, h: (h, b, 0)),
                  pl.BlockSpec((None, seq, 2 * LANES), lambda b, h: (h, b, 0)),
                  pl.BlockSpec((seq, V_HEAD_DIM), lambda b, h: (b, h))],
        out_specs=pl.BlockSpec((seq, V_HEAD_DIM), lambda b, h: (b, h)),
        out_shape=jax.ShapeDtypeStruct((t, B_WIDTH), BF16),
        compiler_params=_params(2),
        name="mla_attention",
    )(qb, kb, vb)


def _dilated_kernel(q_ref, k_ref, v_ref, bd_ref, g_ref, bias_ref, o_ref, qs, ks, acc_o, acc_l, acc_m,
                    *, unroll):
    seq = q_ref.shape[0]
    n_tiles = seq // BAND_BLOCK
    low = lax.broadcasted_iota(jnp.int32, (BAND_BLOCK, LANES), 1) < A_HEAD_DIM
    bd = bd_ref[...]
    g = g_ref[...]

    def head_norm(z, g_row):
        ss = _split_dot(z * z, bd)
        return z * lax.rsqrt(ss * (1.0 / A_HEAD_DIM) + RMS_EPS) * g_row

    qs[...] = head_norm(q_ref[...], g[0:1]) * (A_HEAD_DIM ** -0.5)
    ks[...] = head_norm(k_ref[...], g[1:2])
    ones = jnp.ones((2 * BAND_BLOCK, LANES), BF16)

    for p, (window, dil) in enumerate(DILATED_PATTERNS):
        blocks_per_class = n_tiles // dil
        span = BAND_BLOCK * dil

        def rows_at(start, dil=dil):
            if dil == 1:
                return pl.ds(start, BAND_BLOCK)
            return pl.ds(start, BAND_BLOCK, stride=dil)

        def tile(i, carry, p=p, blocks_per_class=blocks_per_class, span=span, rows_at=rows_at):
            r = i // blocks_per_class
            j = i % blocks_per_class
            cur = rows_at(r + j * span)
            prev = rows_at(r + jnp.maximum(j - 1, 0) * span)
            q = qs[cur, :]
            q2 = jnp.concatenate([jnp.where(low, q, 0.0), jnp.where(low, 0.0, q)], axis=0).astype(BF16)
            k_band = jnp.concatenate([ks[prev, :], ks[cur, :]], axis=0).astype(BF16)
            s = _dot_nt(q2, k_band) + bias_ref[p, jnp.where(j == 0, 1, 0)]
            m = jnp.max(s, axis=-1, keepdims=True)
            pr = jnp.exp(s - m).astype(BF16)
            v_band = jnp.concatenate([v_ref[prev, :], v_ref[cur, :]], axis=0).astype(BF16)
            o = _dot(pr, jnp.concatenate([v_band, ones], axis=1))
            top, bot = o[:BAND_BLOCK], o[BAND_BLOCK:]
            acc_o[p, cur, :] = jnp.where(low, top[:, :LANES], bot[:, :LANES])
            acc_l[p, cur, :] = jnp.where(low, top[:, LANES:], bot[:, LANES:])
            acc_m[p, cur, :] = jnp.where(low, m[:BAND_BLOCK], m[BAND_BLOCK:])
            return carry

        lax.fori_loop(0, n_tiles, tile, 0, unroll=unroll)

    m_all = jnp.maximum(jnp.maximum(acc_m[0], acc_m[1]), acc_m[2])
    num = jnp.zeros((seq, LANES), F32)
    den = jnp.zeros((seq, LANES), F32)
    for p in range(len(DILATED_PATTERNS)):
        w = jnp.exp(acc_m[p] - m_all)
        num = num + w * acc_o[p]
        den = den + w * acc_l[p]
    o_ref[...] = (num / den).astype(o_ref.dtype)


def _dilated_attention(proj, bd, g, bias, batch, seq):
    t = proj.shape[0]
    pairs = A_WIDTH // LANES
    n_pat = len(DILATED_PATTERNS)
    return pl.pallas_call(
        functools.partial(_dilated_kernel, unroll=8),
        grid=(batch, pairs),
        in_specs=[pl.BlockSpec((seq, LANES), lambda b, c: (b, c)),
                  pl.BlockSpec((seq, LANES), lambda b, c: (b, pairs + c)),
                  pl.BlockSpec((seq, LANES), lambda b, c: (b, 2 * pairs + c)),
                  pl.BlockSpec((LANES, LANES), lambda b, c: (0, 0)),
                  pl.BlockSpec((None, 2, LANES), lambda b, c: (c, 0, 0)),
                  pl.BlockSpec((None, n_pat, 2, 2 * BAND_BLOCK, 2 * BAND_BLOCK),
                               lambda b, c: (c, 0, 0, 0, 0))],
        out_specs=pl.BlockSpec((seq, LANES), lambda b, c: (b, c)),
        out_shape=jax.ShapeDtypeStruct((t, A_WIDTH), BF16),
        scratch_shapes=[pltpu.VMEM((seq, LANES), F32)] * 2
                       + [pltpu.VMEM((n_pat, seq, LANES), F32)] * 3,
        compiler_params=_params(2),
        name="dilated_attention",
    )(proj, proj, proj, bd, g, bias)


def _t5_bucket(dist):
    max_exact = NUM_BUCKETS // 2
    df = jnp.maximum(dist, 1).astype(F32)
    log_bucket = max_exact + (jnp.log(df / max_exact) / math.log(MAX_DISTANCE / max_exact)
                              * (NUM_BUCKETS - max_exact)).astype(jnp.int32)
    log_bucket = jnp.minimum(log_bucket, NUM_BUCKETS - 1)
    return jnp.where(dist < max_exact, dist, log_bucket)


def _toeplitz(u):
    n = BAND_BLOCK
    lead = u.shape[:-1]
    w = jnp.concatenate([u[..., ::-1], jnp.zeros(lead + (1,), u.dtype)], axis=-1)
    r = jnp.broadcast_to(w[..., None, :], lead + (n, 2 * n)).reshape(lead + (2 * n * n,))
    return r[..., :n * (2 * n - 1)].reshape(lead + (n, 2 * n - 1))[..., n - 1:]


def _band_bias(rel_bias):
    n = BAND_BLOCK
    tables = []
    for window, dil in DILATED_PATTERNS:
        steps = window // dil
        back = jnp.arange(2 * n)
        by_back = rel_bias[_t5_bucket(back * dil)].astype(F32).T
        by_back = jnp.where(back <= steps, by_back, MASK_VALUE)
        masked = jnp.full((A_HEADS, n - 1), MASK_VALUE, F32)
        cur = _toeplitz(jnp.concatenate([masked, by_back[:, :n]], axis=1))
        prev = _toeplitz(by_back[:, 1:])
        normal = jnp.concatenate([prev, cur], axis=-1)
        first = jnp.concatenate([jnp.full_like(prev, MASK_VALUE), cur], axis=-1)
        tables.append(jnp.stack([normal, first], axis=1))
    tab = jnp.stack(tables, axis=1)
    tab = tab.reshape(A_HEADS // 2, 2, len(DILATED_PATTERNS), 2, n, 2 * n)
    return tab.transpose(0, 2, 3, 1, 4, 5).reshape(A_HEADS // 2, len(DILATED_PATTERNS), 2, 2 * n, 2 * n)


def _pack_pairs(v):
    half = v.shape[1] // 2
    bits = pltpu.bitcast(v, jnp.uint32)
    return (bits[:, :half] >> 16) | (bits[:, half:] & jnp.uint32(0xFFFF0000))


def _unpack_pairs(p):
    return (pltpu.bitcast(p << 16, F32), pltpu.bitcast(p & jnp.uint32(0xFFFF0000), F32))


def _out_proj_kernel(x_ref, a_ref, b_ref, w_ref, g_ref, rw_ref, rb_ref,
                     x1_ref, hp_ref, route_ref, cnt_ref, carry):
    i = pl.program_id(0)
    tm = x_ref.shape[0]

    @pl.when(i == 0)
    def _():
        carry[...] = jnp.zeros_like(carry)

    w = w_ref[...]
    x1 = x_ref[...] + _dot(a_ref[...], w[:A_WIDTH]) + _dot(b_ref[...], w[A_WIDTH:])
    x1_ref[...] = x1
    ms = jnp.mean(x1 * x1, axis=-1, keepdims=True)
    h = x1 * lax.rsqrt(ms + RMS_EPS) * g_ref[...]
    hi = h.astype(BF16)
    hi_f = hi.astype(F32)
    hp_ref[...] = _pack_pairs(hi_f)
    lo = (h - hi_f).astype(BF16)
    rw_hi = rw_ref[0]
    logits = _dot(hi, rw_hi) + _dot(lo, rw_hi) + _dot(hi, rw_ref[1]) + rb_ref[...]

    lane = lax.broadcasted_iota(jnp.int32, (tm, LANES), 1).astype(F32)
    remaining = logits
    vals, hots = [], []
    for _ in range(TOP_K):
        m = jnp.max(remaining, axis=-1, keepdims=True)
        first = jnp.min(jnp.where(remaining == m, lane, float(LANES)), axis=-1, keepdims=True)
        hot = lane == first
        remaining = jnp.where(hot, -jnp.inf, remaining)
        vals.append(m)
        hots.append(hot)
    exps = [jnp.exp(v - vals[0]) for v in vals]
    den = exps[0] + exps[1] + exps[2] + exps[3]

    chosen = jnp.zeros((tm, LANES), F32)
    for hot in hots:
        chosen = chosen + jnp.where(hot, 1.0, 0.0)
    r = lax.broadcasted_iota(jnp.int32, (tm, tm), 0)
    c = lax.broadcasted_iota(jnp.int32, (tm, tm), 1)
    earlier = jnp.where(r > c, 1.0, 0.0).astype(BF16)
    before = carry[...] + _dot(earlier, chosen.astype(BF16))
    carry[...] = carry[...] + jnp.sum(chosen, axis=0, keepdims=True)
    cnt_ref[...] = jnp.broadcast_to(carry[...], cnt_ref.shape)

    route = jnp.zeros((tm, LANES), F32)
    for k in range(TOP_K):
        first = jnp.sum(jnp.where(hots[k], lane, 0.0), axis=-1, keepdims=True)
        rank = jnp.sum(jnp.where(hots[k], before, 0.0), axis=-1, keepdims=True)
        route = route + jnp.where(lane == float(k), first, 0.0)
        route = route + jnp.where(lane == float(TOP_K + k), rank, 0.0)
        route = route + jnp.where(lane == float(2 * TOP_K + k), exps[k] / den, 0.0)
    route_ref[...] = route


def _out_proj(x2, out_a, out_b, w, g, rw, rb, tm):
    t, d = x2.shape
    return pl.pallas_call(
        _out_proj_kernel,
        grid=(t // tm,),
        in_specs=[pl.BlockSpec((tm, d), lambda i: (i, 0)),
                  pl.BlockSpec((tm, A_WIDTH), lambda i: (i, 0)),
                  pl.BlockSpec((tm, B_WIDTH), lambda i: (i, 0)),
                  pl.BlockSpec(w.shape, lambda i: (0, 0)),
                  pl.BlockSpec((1, d), lambda i: (0, 0)),
                  pl.BlockSpec(rw.shape, lambda i: (0, 0, 0)),
                  pl.BlockSpec((1, LANES), lambda i: (0, 0))],
        out_specs=[pl.BlockSpec((tm, d), lambda i: (i, 0)),
                   pl.BlockSpec((tm, d // 2), lambda i: (i, 0)),
                   pl.BlockSpec((tm, LANES), lambda i: (i, 0)),
                   pl.BlockSpec((8, LANES), lambda i: (0, 0))],
        out_shape=[jax.ShapeDtypeStruct((t, d), F32),
                   jax.ShapeDtypeStruct((t, d // 2), jnp.uint32),
                   jax.ShapeDtypeStruct((t, LANES), F32),
                   jax.ShapeDtypeStruct((8, LANES), F32)],
        scratch_shapes=[pltpu.VMEM((1, LANES), F32)],
        compiler_params=_params(),
        name="out_proj",
    )(x2, out_a, out_b, w, g, rw, rb)


def _dispatch_kernel(pad_ref, end_ref, nb_ref, dest_hbm, h_ref, xs_hbm, dest_s, zeros, sem_i, sem_z, sem_o,
                     *, bm):
    i = pl.program_id(0)
    tm = h_ref.shape[0]
    n_blocks = xs_hbm.shape[0] // bm
    idx_copy = pltpu.make_async_copy(dest_hbm.at[i], dest_s, sem_i)
    idx_copy.start()

    @pl.when(i == 0)
    def _():
        zeros[...] = jnp.zeros_like(zeros)

        def zero_block(start):
            return pltpu.make_async_copy(zeros, xs_hbm.at[pl.ds(pl.multiple_of(start, bm), bm), :], sem_z)

        for e in range(N_EXPERTS):
            @pl.when(pad_ref[e] > 0)
            def _():
                zero_block(end_ref[e] - bm).start()

        def start_unused(b, carry):
            zero_block(b * bm).start()
            return carry

        def wait_unused(b, carry):
            zero_block(b * bm).wait()
            return carry

        lax.fori_loop(nb_ref[0], n_blocks, start_unused, 0)
        for e in range(N_EXPERTS):
            @pl.when(pad_ref[e] > 0)
            def _():
                zero_block(end_ref[e] - bm).wait()
        lax.fori_loop(nb_ref[0], n_blocks, wait_unused, 0)

    idx_copy.wait()

    def row_copy(t, k):
        return pltpu.make_async_copy(h_ref.at[pl.ds(t, 1), :],
                                     xs_hbm.at[pl.ds(dest_s[t * TOP_K + k], 1), :], sem_o)

    def body(t, carry):
        for k in range(TOP_K):
            row_copy(t, k).start()
        return carry

    lax.fori_loop(0, tm, body, 0, unroll=8)
    for _ in range(TOP_K):
        pltpu.make_async_copy(h_ref, xs_hbm.at[pl.ds(0, tm), :], sem_o).wait()


def _dispatch(padded, pad_end, n_used, dest, hp, n_slots, tm, bm):
    t, half = hp.shape
    grid_spec = pltpu.PrefetchScalarGridSpec(
        num_scalar_prefetch=3,
        grid=(t // tm,),
        in_specs=[pl.BlockSpec(memory_space=pl.ANY),
                  pl.BlockSpec((tm, half), lambda i, p, e, n: (i, 0))],
        out_specs=pl.BlockSpec(memory_space=pl.ANY),
        scratch_shapes=[pltpu.SMEM((tm * TOP_K,), jnp.int32),
                        pltpu.VMEM((bm, half), jnp.uint32),
                        pltpu.SemaphoreType.DMA(()),
                        pltpu.SemaphoreType.DMA(()),
                        pltpu.SemaphoreType.DMA(())],
    )
    return pl.pallas_call(
        functools.partial(_dispatch_kernel, bm=bm),
        grid_spec=grid_spec,
        out_shape=jax.ShapeDtypeStruct((n_slots, half), jnp.uint32),
        compiler_params=_params(),
        name="dispatch",
    )(padded, pad_end, n_used, dest, hp)


def _expert_ffn_kernel(be_ref, nb_ref, x_ref, wgu_ref, bgu_ref, wd_ref, bd_ref, y_ref, wgu_s, wd_s):
    i = pl.program_id(0)
    d_ff = wd_ref.shape[0]
    half = x_ref.shape[1]

    @pl.when(i >= nb_ref[0])
    def _():
        y_ref[...] = jnp.zeros_like(y_ref)

    @pl.when(i < nb_ref[0])
    def _():
        changed = jnp.logical_or(i == 0, be_ref[i] != be_ref[jnp.maximum(i - 1, 0)])

        @pl.when(changed)
        def _():
            wgu_s[...] = wgu_ref[...].astype(BF16)
            wd_s[...] = wd_ref[...].astype(BF16)

        x_lo, x_hi = _unpack_pairs(x_ref[...])
        gu = (_dot(x_lo.astype(BF16), wgu_s[:half, :]) + _dot(x_hi.astype(BF16), wgu_s[half:, :])
              + bgu_ref[...])
        gate = jnp.minimum(gu[:, :d_ff], SWIGLU_LIMIT)
        up = jnp.clip(gu[:, d_ff:], -SWIGLU_LIMIT, SWIGLU_LIMIT)
        glu = gate * jax.nn.sigmoid(SWIGLU_ALPHA * gate)
        act = ((up + 1.0) * glu).astype(BF16)
        y = _dot(act, wd_s[...]) + bd_ref[...]
        y_ref[...] = _pack_pairs(y.astype(BF16).astype(F32))


def _expert_ffn(block_expert, n_used, xs, w_gate_up, b_gate_up, w_down, b_down, bm):
    n_slots, half = xs.shape
    n_e, d, two_ff = w_gate_up.shape
    d_ff = two_ff // 2
    n_blocks = n_slots // bm

    def used(i, nb):
        return jnp.minimum(i, jnp.maximum(nb[0] - 1, 0))

    grid_spec = pltpu.PrefetchScalarGridSpec(
        num_scalar_prefetch=2,
        grid=(n_blocks,),
        in_specs=[pl.BlockSpec((bm, half), lambda i, be, nb: (used(i, nb), 0)),
                  pl.BlockSpec((None, d, two_ff), lambda i, be, nb: (be[i], 0, 0)),
                  pl.BlockSpec((None, 1, two_ff), lambda i, be, nb: (be[i], 0, 0)),
                  pl.BlockSpec((None, d_ff, d), lambda i, be, nb: (be[i], 0, 0)),
                  pl.BlockSpec((None, 1, d), lambda i, be, nb: (be[i], 0, 0))],
        out_specs=pl.BlockSpec((bm, half), lambda i, be, nb: (i, 0)),
        scratch_shapes=[pltpu.VMEM((d, two_ff), BF16), pltpu.VMEM((d_ff, d), BF16)],
    )
    return pl.pallas_call(
        _expert_ffn_kernel,
        grid_spec=grid_spec,
        out_shape=jax.ShapeDtypeStruct((n_slots, half), jnp.uint32),
        compiler_params=_params(),
        name="expert_ffn",
    )(block_expert, n_used, xs, w_gate_up, b_gate_up.reshape(n_e, 1, two_ff),
      w_down, b_down.reshape(n_e, 1, d))


def _combine_kernel(dest_hbm, route_ref, x1_ref, ys_hbm, o_ref, dest_s, rows, sem_i, sem_g):
    i = pl.program_id(0)
    tm = x1_ref.shape[0]
    half = x1_ref.shape[1] // 2
    idx_copy = pltpu.make_async_copy(dest_hbm.at[i], dest_s, sem_i)
    idx_copy.start()
    idx_copy.wait()

    def row_copy(t, k):
        return pltpu.make_async_copy(ys_hbm.at[pl.ds(dest_s[t * TOP_K + k], 1), :],
                                     rows.at[k, pl.ds(t, 1), :], sem_g)

    def body(t, carry):
        for k in range(TOP_K):
            row_copy(t, k).start()
        return carry

    lax.fori_loop(0, tm, body, 0, unroll=8)
    for k in range(TOP_K):
        pltpu.make_async_copy(ys_hbm.at[pl.ds(0, tm), :], rows.at[k], sem_g).wait()

    route = route_ref[...]
    y_lo = jnp.zeros((tm, half), F32)
    y_hi = jnp.zeros((tm, half), F32)
    for k in range(TOP_K):
        gate = route[:, 2 * TOP_K + k: 2 * TOP_K + k + 1]
        lo, hi = _unpack_pairs(rows[k])
        y_lo = y_lo + gate * lo
        y_hi = y_hi + gate * hi
    o_ref[:, :half] = x1_ref[:, :half] + y_lo
    o_ref[:, half:] = x1_ref[:, half:] + y_hi


def _combine(dest, route, x1, ys, tm):
    t, d = x1.shape
    half = d // 2
    return pl.pallas_call(
        _combine_kernel,
        grid=(t // tm,),
        in_specs=[pl.BlockSpec(memory_space=pl.ANY),
                  pl.BlockSpec((tm, LANES), lambda i: (i, 0)),
                  pl.BlockSpec((tm, d), lambda i: (i, 0)),
                  pl.BlockSpec(memory_space=pl.ANY)],
        out_specs=pl.BlockSpec((tm, d), lambda i: (i, 0)),
        out_shape=jax.ShapeDtypeStruct((t, d), F32),
        scratch_shapes=[pltpu.SMEM((tm * TOP_K,), jnp.int32),
                        pltpu.VMEM((TOP_K, tm, half), jnp.uint32),
                        pltpu.SemaphoreType.DMA(()),
                        pltpu.SemaphoreType.DMA(())],
        compiler_params=_params(),
        name="combine",
    )(dest, route, x1, ys)


def _slot_layout(route, counts, tm, bm):
    t = route.shape[0]
    counts = counts.astype(jnp.int32)
    padded = (counts + bm - 1) // bm * bm
    pad_end = jnp.cumsum(padded).astype(jnp.int32)
    pad_start = pad_end - padded
    idx = route[:, :TOP_K].astype(jnp.int32)
    rank = route[:, TOP_K:2 * TOP_K].astype(jnp.int32)
    dest = (jnp.take(pad_start, idx) + rank).reshape(t // tm, tm * TOP_K)
    n_blocks = -(-t * TOP_K // bm) + N_EXPERTS
    block_start = jnp.arange(n_blocks, dtype=jnp.int32) * bm
    block_expert = jnp.minimum(jnp.sum(pad_end[None, :] <= block_start[:, None], axis=1),
                               N_EXPERTS - 1).astype(jnp.int32)
    n_used = (pad_end[-1:] // bm).astype(jnp.int32)
    return padded, pad_end, dest, block_expert, n_used, n_blocks * bm


def _rot_cols(w):
    half = w.shape[-1] // 2
    return jnp.concatenate([-w[..., half:], w[..., :half]], axis=-1)


def _swap_halves(g):
    half = g.shape[-1] // 2
    return jnp.concatenate([g[..., half:], g[..., :half]], axis=-1)


def kernel(x, attn_norm_g, w_in, a_q_norm_g, a_k_norm_g, rel_bias, q_a_norm_g, w_q_b, kv_a_norm_g,
           w_kv_b, b_q_norm_g, b_k_norm_g, w_out, ffn_norm_g, router_w, router_b, w_gate_up,
           b_gate_up, w_down, b_down):
    batch, seq, d = x.shape
    t = batch * seq
    depth = w_in.shape[0]
    tm = 512
    expert_bm = 256

    pos = jnp.arange(seq, dtype=F32)
    inv_freq = ROPE_THETA ** (-jnp.arange(0, QK_ROPE_DIM, 2, dtype=F32) / QK_ROPE_DIM)
    ang = pos[:, None] * inv_freq[None, :]
    cos, sin = jnp.cos(ang), jnp.sin(ang)
    cs = jnp.concatenate([cos, cos, sin, sin], axis=-1)

    row = jnp.arange(LANES)[:, None] // A_HEAD_DIM
    col = jnp.arange(LANES)[None, :] // A_HEAD_DIM
    head_block_ones = (row == col).astype(BF16)
    band_bias = _band_bias(rel_bias)

    x2 = x.reshape(t, d)
    for layer in range(depth):
        kpe_off = 3 * A_WIDTH + Q_LORA_RANK + KV_LORA_RANK
        w_kpe = w_in[layer][:, kpe_off:]
        w_in_r = jnp.concatenate([w_in[layer][:, :kpe_off], w_kpe, _rot_cols(w_kpe)], axis=1).astype(BF16)

        wq = w_q_b[layer].reshape(Q_LORA_RANK, B_HEADS, QK_HEAD_DIM)
        wq_rope = wq[..., QK_NOPE_DIM:]
        wq_r = jnp.concatenate([wq, _rot_cols(wq_rope)], axis=-1).reshape(Q_LORA_RANK, -1).astype(BF16)
        wkv = w_kv_b[layer].reshape(KV_LORA_RANK, B_HEADS, QK_NOPE_DIM + V_HEAD_DIM)
        wkv_r = jnp.concatenate([wkv[..., :QK_NOPE_DIM].reshape(KV_LORA_RANK, -1),
                                 wkv[..., QK_NOPE_DIM:].reshape(KV_LORA_RANK, -1)], axis=1).astype(BF16)
        gc = jnp.concatenate([q_a_norm_g[layer], kv_a_norm_g[layer]])[None, :]
        gq, gk = b_q_norm_g[layer], b_k_norm_g[layer]

        def rope_gain(gr):
            return jnp.concatenate([gr, _swap_halves(gr)])

        gqk = jnp.stack([gq[:QK_NOPE_DIM], rope_gain(gq[QK_NOPE_DIM:]),
                         gk[:QK_NOPE_DIM], rope_gain(gk[QK_NOPE_DIM:])])
        g_a = jnp.stack([jnp.tile(a_q_norm_g[layer], 2), jnp.tile(a_k_norm_g[layer], 2)])
        g_a = jnp.broadcast_to(g_a[None], (A_WIDTH // LANES, 2, LANES))

        proj = _in_proj(x2, attn_norm_g[layer][None, :], w_in_r, tm)
        qb, kb, vb = _mla_prep(proj, cs, wq_r, wkv_r, gc, gqk, seq, tm)
        out_b = _mla_attention(qb, kb, vb, batch, seq, 256)
        out_a = _dilated_attention(proj, head_block_ones, g_a, band_bias, batch, seq)

        rw = jnp.pad(router_w[layer], ((0, 0), (0, LANES - N_EXPERTS)))
        rw_hi = rw.astype(BF16)
        rw_lo = (rw - rw_hi.astype(F32)).astype(BF16)
        rb = jnp.pad(router_b[layer], (0, LANES - N_EXPERTS), constant_values=-jnp.inf)[None, :]
        x1, hp, route, counts = _out_proj(x2, out_a, out_b, w_out[layer].astype(BF16),
                                          ffn_norm_g[layer][None, :], jnp.stack([rw_hi, rw_lo]), rb, tm)

        padded, pad_end, dest, block_expert, n_used, n_slots = _slot_layout(
            route, counts[0, :N_EXPERTS], tm, expert_bm)
        xs = _dispatch(padded, pad_end, n_used, dest, hp, n_slots, tm, expert_bm)
        ys = _expert_ffn(block_expert, n_used, xs, w_gate_up[layer], b_gate_up[layer],
                         w_down[layer], b_down[layer], expert_bm)
        x2 = _combine(dest, route, x1, ys, tm)
    return x2.reshape(batch, seq, d)
```

```python
import functools
import math

import jax
import jax.numpy as jnp
from jax import lax
from jax.experimental import pallas as pl
from jax.experimental.pallas import tpu as pltpu

A_HEADS = 8
A_HEAD_DIM = 64
A_WIDTH = A_HEADS * A_HEAD_DIM
DILATED_PATTERNS = ((128, 1), (512, 4), (2048, 16))
BAND_BLOCK = 128

B_HEADS = 4
QK_NOPE_DIM = 128
QK_ROPE_DIM = 64
QK_HEAD_DIM = QK_NOPE_DIM + QK_ROPE_DIM
V_HEAD_DIM = 128
Q_LORA_RANK = 256
KV_LORA_RANK = 256
B_WIDTH = B_HEADS * V_HEAD_DIM
ROPE_THETA = 10000.0

NUM_BUCKETS = 32
MAX_DISTANCE = 2048

N_EXPERTS = 32
TOP_K = 4
SWIGLU_LIMIT = 7.0
SWIGLU_ALPHA = 1.702
RMS_EPS = 1e-6

LANES = 128
MASK_VALUE = -1e30
PROJ_WIDTH = 3 * A_WIDTH + Q_LORA_RANK + KV_LORA_RANK + 2 * QK_ROPE_DIM
VMEM_LIMIT = 56 * 1024 * 1024

F32 = jnp.float32
BF16 = jnp.bfloat16


def _dot(a, b):
    return jnp.dot(a, b, preferred_element_type=F32)


def _dot_nt(a, b):
    return lax.dot_general(a, b, (((1,), (1,)), ((), ())), preferred_element_type=F32)


def _split_dot(x, m):
    hi = x.astype(BF16)
    lo = (x - hi.astype(F32)).astype(BF16)
    return _dot(hi, m) + _dot(lo, m)


def _params(n_parallel=1):
    return pltpu.CompilerParams(
        dimension_semantics=("arbitrary",) * n_parallel, vmem_limit_bytes=VMEM_LIMIT)


def _in_proj_kernel(x_ref, g_ref, w_ref, o_ref):
    x = x_ref[...]
    ms = jnp.mean(x * x, axis=-1, keepdims=True)
    h = (x * lax.rsqrt(ms + RMS_EPS) * g_ref[...]).astype(BF16)
    o_ref[...] = _dot(h, w_ref[...])


def _in_proj(x2, g, w, tm):
    t, d = x2.shape
    n = w.shape[1]
    return pl.pallas_call(
        _in_proj_kernel,
        grid=(t // tm,),
        in_specs=[pl.BlockSpec((tm, d), lambda i: (i, 0)),
                  pl.BlockSpec((1, d), lambda i: (0, 0)),
                  pl.BlockSpec((d, n), lambda i: (0, 0))],
        out_specs=pl.BlockSpec((tm, n), lambda i: (i, 0)),
        out_shape=jax.ShapeDtypeStruct((t, n), F32),
        compiler_params=_params(),
        name="in_proj",
    )(x2, g, w)


def _mla_prep_kernel(c_ref, kpe_ref, cs_ref, wq_ref, wkv_ref, gc_ref, gqk_ref, q_ref, k_ref, v_ref,
                     *, q_scale):
    c = c_ref[...]
    gc = gc_ref[...]

    def lora_norm(z, g):
        ms = jnp.mean(z * z, axis=-1, keepdims=True)
        return (z * lax.rsqrt(ms + RMS_EPS) * g).astype(BF16)

    cq = lora_norm(c[:, :Q_LORA_RANK], gc[:, :Q_LORA_RANK])
    ckv = lora_norm(c[:, Q_LORA_RANK:], gc[:, Q_LORA_RANK:])
    qb = _dot(cq, wq_ref[...])
    kvb = _dot(ckv, wkv_ref[...])
    kper = kpe_ref[...]
    cs = cs_ref[...]
    gqk = gqk_ref[...]
    tm = c.shape[0]

    row = lax.broadcasted_iota(jnp.int32, (LANES, LANES), 0)
    ones_all = jnp.ones((LANES, LANES), BF16)
    ones_lo = jnp.where(row < QK_ROPE_DIM, 1.0, 0.0).astype(BF16)
    lane = lax.broadcasted_iota(jnp.int32, (tm, LANES), 1)

    def rope(z, g_row):
        t = z * (g_row * cs)
        return jnp.where(lane < QK_ROPE_DIM, t + pltpu.roll(t, QK_ROPE_DIM, 1), 0.0)

    k_rope = rope(kper, gqk[3:4])
    kpe_ss = _split_dot(kper * kper, ones_lo)
    for h in range(B_HEADS):
        qn = qb[:, 2 * LANES * h: 2 * LANES * h + LANES]
        qr = qb[:, 2 * LANES * h + LANES: 2 * LANES * (h + 1)]
        ss = _split_dot(qn * qn, ones_all) + _split_dot(qr * qr, ones_lo)
        rs = lax.rsqrt(ss * (1.0 / QK_HEAD_DIM) + RMS_EPS) * q_scale
        q_ref[h, :, :LANES] = (qn * gqk[0:1] * rs).astype(BF16)
        q_ref[h, :, LANES:] = (rope(qr, gqk[1:2]) * rs).astype(BF16)
        kn = kvb[:, LANES * h: LANES * (h + 1)]
        ssk = _split_dot(kn * kn, ones_all) + kpe_ss
        rsk = lax.rsqrt(ssk * (1.0 / QK_HEAD_DIM) + RMS_EPS)
        k_ref[h, :, :LANES] = (kn * gqk[2:3] * rsk).astype(BF16)
        k_ref[h, :, LANES:] = (k_rope * rsk).astype(BF16)
    v_ref[...] = kvb[:, B_HEADS * QK_NOPE_DIM:].astype(BF16)


def _mla_prep(proj, cs, wq, wkv, gc, gqk, seq, tm):
    t = proj.shape[0]
    c_width = Q_LORA_RANK + KV_LORA_RANK
    c_block = 3 * A_WIDTH // c_width
    kpe_block = (3 * A_WIDTH + c_width) // LANES
    pos_blocks = seq // tm
    kernel = functools.partial(_mla_prep_kernel, q_scale=QK_HEAD_DIM ** -0.5)
    return pl.pallas_call(
        kernel,
        grid=(t // tm,),
        in_specs=[pl.BlockSpec((tm, c_width), lambda i: (i, c_block)),
                  pl.BlockSpec((tm, LANES), lambda i: (i, kpe_block)),
                  pl.BlockSpec((tm, LANES), lambda i: (i % pos_blocks, 0)),
                  pl.BlockSpec(wq.shape, lambda i: (0, 0)),
                  pl.BlockSpec(wkv.shape, lambda i: (0, 0)),
                  pl.BlockSpec(gc.shape, lambda i: (0, 0)),
                  pl.BlockSpec(gqk.shape, lambda i: (0, 0))],
        out_specs=[pl.BlockSpec((B_HEADS, tm, 2 * LANES), lambda i: (0, i, 0)),
                   pl.BlockSpec((B_HEADS, tm, 2 * LANES), lambda i: (0, i, 0)),
                   pl.BlockSpec((tm, B_WIDTH), lambda i: (i, 0))],
        out_shape=[jax.ShapeDtypeStruct((B_HEADS, t, 2 * LANES), BF16),
                   jax.ShapeDtypeStruct((B_HEADS, t, 2 * LANES), BF16),
                   jax.ShapeDtypeStruct((t, B_WIDTH), BF16)],
        compiler_params=_params(),
        name="mla_prep",
    )(proj, proj, cs, wq, wkv, gc, gqk)


def _mla_attention_kernel(q_ref, k_ref, v_ref, o_ref, *, tq):
    seq = q_ref.shape[0]
    row = lax.broadcasted_iota(jnp.int32, (tq, tq), 0)
    col = lax.broadcasted_iota(jnp.int32, (tq, tq), 1)
    diag_mask = jnp.where(col <= row, 0.0, MASK_VALUE)
    for i in range(seq // tq):
        q = q_ref[i * tq:(i + 1) * tq, :]
        s_diag = _dot_nt(q, k_ref[i * tq:(i + 1) * tq, :]) + diag_mask
        m = jnp.max(s_diag, axis=-1, keepdims=True)
        if i > 0:
            s_past = _dot_nt(q, k_ref[:i * tq, :])
            m = jnp.maximum(m, jnp.max(s_past, axis=-1, keepdims=True))
        p_diag = jnp.exp(s_diag - m)
        l = jnp.sum(p_diag, axis=-1, keepdims=True)
        o = _dot(p_diag.astype(BF16), v_ref[i * tq:(i + 1) * tq, :])
        if i > 0:
            p_past = jnp.exp(s_past - m)
            l = l + jnp.sum(p_past, axis=-1, keepdims=True)
            o = o + _dot(p_past.astype(BF16), v_ref[:i * tq, :])
        o_ref[i * tq:(i + 1) * tq, :] = (o / l).astype(o_ref.dtype)


def _mla_attention(qb, kb, vb, batch, seq, tq):
    t = vb.shape[0]
    kernel = functools.partial(_mla_attention_kernel, tq=tq)
    return pl.pallas_call(
        kernel,
        grid=(batch, B_HEADS),
        in_specs=[pl.BlockSpec((None, seq, 2 * LANES), lambda b, h: (h, b, 0)),
                  pl.BlockSpec((None, seq, 2 * LANES), lambda b, h: (h, b, 0)),
                  pl.BlockSpec((seq, V_HEAD_DIM), lambda b, h: (b, h))],
        out_specs=pl.BlockSpec((seq, V_HEAD_DIM), lambda b, h: (b, h)),
        out_shape=jax.ShapeDtypeStruct((t, B_WIDTH), BF16),
        compiler_params=_params(2),
        name="mla_attention",
    )(qb, kb, vb)


def _dilated_kernel(q_ref, k_ref, v_ref, bd_ref, g_ref, bias_ref, o_ref, qs, ks, acc_o, acc_l, acc_m,
                    *, unroll):
    seq = q_ref.shape[0]
    n_tiles = seq // BAND_BLOCK
    low = lax.broadcasted_iota(jnp.int32, (BAND_BLOCK, LANES), 1) < A_HEAD_DIM
    bd = bd_ref[...]
    g = g_ref[...]

    def head_norm(z, g_row):
        ss = _split_dot(z * z, bd)
        return z * lax.rsqrt(ss * (1.0 / A_HEAD_DIM) + RMS_EPS) * g_row

    qs[...] = head_norm(q_ref[...], g[0:1]) * (A_HEAD_DIM ** -0.5)
    ks[...] = head_norm(k_ref[...], g[1:2])
    ones = jnp.ones((2 * BAND_BLOCK, LANES), BF16)

    for p, (window, dil) in enumerate(DILATED_PATTERNS):
        blocks_per_class = n_tiles // dil
        span = BAND_BLOCK * dil

        def rows_at(start, dil=dil):
            if dil == 1:
                return pl.ds(start, BAND_BLOCK)
            return pl.ds(start, BAND_BLOCK, stride=dil)

        def tile(i, carry, p=p, blocks_per_class=blocks_per_class, span=span, rows_at=rows_at):
            r = i // blocks_per_class
            j = i % blocks_per_class
            cur = rows_at(r + j * span)
            prev = rows_at(r + jnp.maximum(j - 1, 0) * span)
            q = qs[cur, :]
            q2 = jnp.concatenate([jnp.where(low, q, 0.0), jnp.where(low, 0.0, q)], axis=0).astype(BF16)
            k_band = jnp.concatenate([ks[prev, :], ks[cur, :]], axis=0).astype(BF16)
            s = _dot_nt(q2, k_band) + bias_ref[p, jnp.where(j == 0, 1, 0)]
            m = jnp.max(s, axis=-1, keepdims=True)
            pr = jnp.exp(s - m).astype(BF16)
            v_band = jnp.concatenate([v_ref[prev, :], v_ref[cur, :]], axis=0).astype(BF16)
            o = _dot(pr, jnp.concatenate([v_band, ones], axis=1))
            top, bot = o[:BAND_BLOCK], o[BAND_BLOCK:]
            acc_o[p, cur, :] = jnp.where(low, top[:, :LANES], bot[:, :LANES])
            acc_l[p, cur, :] = jnp.where(low, top[:, LANES:], bot[:, LANES:])
            acc_m[p, cur, :] = jnp.where(low, m[:BAND_BLOCK], m[BAND_BLOCK:])
            return carry

        lax.fori_loop(0, n_tiles, tile, 0, unroll=unroll)

    m_all = jnp.maximum(jnp.maximum(acc_m[0], acc_m[1]), acc_m[2])
    num = jnp.zeros((seq, LANES), F32)
    den = jnp.zeros((seq, LANES), F32)
    for p in range(len(DILATED_PATTERNS)):
        w = jnp.exp(acc_m[p] - m_all)
        num = num + w * acc_o[p]
        den = den + w * acc_l[p]
    o_ref[...] = (num / den).astype(o_ref.dtype)


def _dilated_attention(proj, bd, g, bias, batch, seq):
    t = proj.shape[0]
    pairs = A_WIDTH // LANES
    n_pat = len(DILATED_PATTERNS)
    return pl.pallas_call(
        functools.partial(_dilated_kernel, unroll=8),
        grid=(batch, pairs),
        in_specs=[pl.BlockSpec((seq, LANES), lambda b, c: (b, c)),
                  pl.BlockSpec((seq, LANES), lambda b, c: (b, pairs + c)),
                  pl.BlockSpec((seq, LANES), lambda b, c: (b, 2 * pairs + c)),
                  pl.BlockSpec((LANES, LANES), lambda b, c: (0, 0)),
                  pl.BlockSpec((None, 2, LANES), lambda b, c: (c, 0, 0)),
                  pl.BlockSpec((None, n_pat, 2, 2 * BAND_BLOCK, 2 * BAND_BLOCK),
                               lambda b, c: (c, 0, 0, 0, 0))],
        out_specs=pl.BlockSpec((seq, LANES), lambda b, c: (b, c)),
        out_shape=jax.ShapeDtypeStruct((t, A_WIDTH), BF16),
        scratch_shapes=[pltpu.VMEM((seq, LANES), F32)] * 2
                       + [pltpu.VMEM((n_pat, seq, LANES), F32)] * 3,
        compiler_params=_params(2),
        name="dilated_attention",
    )(proj, proj, proj, bd, g, bias)


def _t5_bucket(dist):
    max_exact = NUM_BUCKETS // 2
    df = jnp.maximum(dist, 1).astype(F32)
    log_bucket = max_exact + (jnp.log(df / max_exact) / math.log(MAX_DISTANCE / max_exact)
                              * (NUM_BUCKETS - max_exact)).astype(jnp.int32)
    log_bucket = jnp.minimum(log_bucket, NUM_BUCKETS - 1)
    return jnp.where(dist < max_exact, dist, log_bucket)


def _toeplitz(u):
    n = BAND_BLOCK
    lead = u.shape[:-1]
    w = jnp.concatenate([u[..., ::-1], jnp.zeros(lead + (1,), u.dtype)], axis=-1)
    r = jnp.broadcast_to(w[..., None, :], lead + (n, 2 * n)).reshape(lead + (2 * n * n,))
    return r[..., :n * (2 * n - 1)].reshape(lead + (n, 2 * n - 1))[..., n - 1:]


def _band_bias(rel_bias):
    n = BAND_BLOCK
    tables = []
    for window, dil in DILATED_PATTERNS:
        steps = window // dil
        back = jnp.arange(2 * n)
        by_back = rel_bias[_t5_bucket(back * dil)].astype(F32).T
        by_back = jnp.where(back <= steps, by_back, MASK_VALUE)
        masked = jnp.full((A_HEADS, n - 1), MASK_VALUE, F32)
        cur = _toeplitz(jnp.concatenate([masked, by_back[:, :n]], axis=1))
        prev = _toeplitz(by_back[:, 1:])
        normal = jnp.concatenate([prev, cur], axis=-1)
        first = jnp.concatenate([jnp.full_like(prev, MASK_VALUE), cur], axis=-1)
        tables.append(jnp.stack([normal, first], axis=1))
    tab = jnp.stack(tables, axis=1)
    tab = tab.reshape(A_HEADS // 2, 2, len(DILATED_PATTERNS), 2, n, 2 * n)
    return tab.transpose(0, 2, 3, 1, 4, 5).reshape(A_HEADS // 2, len(DILATED_PATTERNS), 2, 2 * n, 2 * n)


def _pack_pairs(v):
    half = v.shape[1] // 2
    bits = pltpu.bitcast(v, jnp.uint32)
    return (bits[:, :half] >> 16) | (bits[:, half:] & jnp.uint32(0xFFFF0000))


def _unpack_pairs(p):
    return (pltpu.bitcast(p << 16, F32), pltpu.bitcast(p & jnp.uint32(0xFFFF0000), F32))


def _out_proj_kernel(x_ref, a_ref, b_ref, w_ref, g_ref, rw_ref, rb_ref,
                     x1_ref, hp_ref, route_ref, cnt_ref, carry):
    i = pl.program_id(0)
    tm = x_ref.shape[0]

    @pl.when(i == 0)
    def _():
        carry[...] = jnp.zeros_like(carry)

    w = w_ref[...]
    x1 = x_ref[...] + _dot(a_ref[...], w[:A_WIDTH]) + _dot(b_ref[...], w[A_WIDTH:])
    x1_ref[...] = x1
    ms = jnp.mean(x1 * x1, axis=-1, keepdims=True)
    h = x1 * lax.rsqrt(ms + RMS_EPS) * g_ref[...]
    hi = h.astype(BF16)
    hi_f = hi.astype(F32)
    hp_ref[...] = _pack_pairs(hi_f)
    lo = (h - hi_f).astype(BF16)
    rw_hi = rw_ref[0]
    logits = _dot(hi, rw_hi) + _dot(lo, rw_hi) + _dot(hi, rw_ref[1]) + rb_ref[...]

    lane = lax.broadcasted_iota(jnp.int32, (tm, LANES), 1).astype(F32)
    remaining = logits
    vals, hots = [], []
    for _ in range(TOP_K):
        m = jnp.max(remaining, axis=-1, keepdims=True)
        first = jnp.min(jnp.where(remaining == m, lane, float(LANES)), axis=-1, keepdims=True)
        hot = lane == first
        remaining = jnp.where(hot, -jnp.inf, remaining)
        vals.append(m)
        hots.append(hot)
    exps = [jnp.exp(v - vals[0]) for v in vals]
    den = exps[0] + exps[1] + exps[2] + exps[3]

    chosen = jnp.zeros((tm, LANES), F32)
    for hot in hots:
        chosen = chosen + jnp.where(hot, 1.0, 0.0)
    r = lax.broadcasted_iota(jnp.int32, (tm, tm), 0)
    c = lax.broadcasted_iota(jnp.int32, (tm, tm), 1)
    earlier = jnp.where(r > c, 1.0, 0.0).astype(BF16)
    before = carry[...] + _dot(earlier, chosen.astype(BF16))
    carry[...] = carry[...] + jnp.sum(chosen, axis=0, keepdims=True)
    cnt_ref[...] = jnp.broadcast_to(carry[...], cnt_ref.shape)

    route = jnp.zeros((tm, LANES), F32)
    for k in range(TOP_K):
        first = jnp.sum(jnp.where(hots[k], lane, 0.0), axis=-1, keepdims=True)
        rank = jnp.sum(jnp.where(hots[k], before, 0.0), axis=-1, keepdims=True)
        route = route + jnp.where(lane == float(k), first, 0.0)
        route = route + jnp.where(lane == float(TOP_K + k), rank, 0.0)
        route = route + jnp.where(lane == float(2 * TOP_K + k), exps[k] / den, 0.0)
    route_ref[...] = route


def _out_proj(x2, out_a, out_b, w, g, rw, rb, tm):
    t, d = x2.shape
    return pl.pallas_call(
        _out_proj_kernel,
        grid=(t // tm,),
        in_specs=[pl.BlockSpec((tm, d), lambda i: (i, 0)),
                  pl.BlockSpec((tm, A_WIDTH), lambda i: (i, 0)),
                  pl.BlockSpec((tm, B_WIDTH), lambda i: (i, 0)),
                  pl.BlockSpec(w.shape, lambda i: (0, 0)),
                  pl.BlockSpec((1, d), lambda i: (0, 0)),
                  pl.BlockSpec(rw.shape, lambda i: (0, 0, 0)),
                  pl.BlockSpec((1, LANES), lambda i: (0, 0))],
        out_specs=[pl.BlockSpec((tm, d), lambda i: (i, 0)),
                   pl.BlockSpec((tm, d // 2), lambda i: (i, 0)),
                   pl.BlockSpec((tm, LANES), lambda i: (i, 0)),
                   pl.BlockSpec((8, LANES), lambda i: (0, 0))],
        out_shape=[jax.ShapeDtypeStruct((t, d), F32),
                   jax.ShapeDtypeStruct((t, d // 2), jnp.uint32),
                   jax.ShapeDtypeStruct((t, LANES), F32),
                   jax.ShapeDtypeStruct((8, LANES), F32)],
        scratch_shapes=[pltpu.VMEM((1, LANES), F32)],
        compiler_params=_params(),
        name="out_proj",
    )(x2, out_a, out_b, w, g, rw, rb)


def _dispatch_kernel(pad_ref, end_ref, nb_ref, dest_hbm, h_ref, xs_hbm, dest_s, zeros, sem_i, sem_z, sem_o,
                     *, bm):
    i = pl.program_id(0)
    steps = pl.num_programs(0)
    tm = h_ref.shape[0]
    n_blocks = xs_hbm.shape[0] // bm

    def idx_copy(step):
        s = step % 2
        return pltpu.make_async_copy(dest_hbm.at[step], dest_s.at[s], sem_i.at[s])

    @pl.when(i == 0)
    def _():
        idx_copy(0).start()
        zeros[...] = jnp.zeros_like(zeros)

        def zero_block(start):
            return pltpu.make_async_copy(zeros, xs_hbm.at[pl.ds(pl.multiple_of(start, bm), bm), :], sem_z)

        for e in range(N_EXPERTS):
            @pl.when(pad_ref[e] > 0)
            def _():
                zero_block(end_ref[e] - bm).start()

        def start_unused(b, carry):
            zero_block(b * bm).start()
            return carry

        def wait_unused(b, carry):
            zero_block(b * bm).wait()
            return carry

        lax.fori_loop(nb_ref[0], n_blocks, start_unused, 0)
        for e in range(N_EXPERTS):
            @pl.when(pad_ref[e] > 0)
            def _():
                zero_block(end_ref[e] - bm).wait()
        lax.fori_loop(nb_ref[0], n_blocks, wait_unused, 0)

    @pl.when(i + 1 < steps)
    def _():
        idx_copy(i + 1).start()

    idx_copy(i).wait()
    slot = i % 2

    def body(t, carry):
        for k in range(TOP_K):
            pltpu.make_async_copy(h_ref.at[pl.ds(t, 1), :],
                                  xs_hbm.at[pl.ds(dest_s[slot, t * TOP_K + k], 1), :],
                                  sem_o).start(priority=k % 2)
        return carry

    lax.fori_loop(0, tm, body, 0, unroll=8)
    for _ in range(TOP_K):
        pltpu.make_async_copy(h_ref, xs_hbm.at[pl.ds(0, tm), :], sem_o).wait()


def _dispatch(padded, pad_end, n_used, dest, hp, n_slots, tm, bm):
    t, half = hp.shape
    grid_spec = pltpu.PrefetchScalarGridSpec(
        num_scalar_prefetch=3,
        grid=(t // tm,),
        in_specs=[pl.BlockSpec(memory_space=pl.ANY),
                  pl.BlockSpec((tm, half), lambda i, p, e, n: (i, 0))],
        out_specs=pl.BlockSpec(memory_space=pl.ANY),
        scratch_shapes=[pltpu.SMEM((2, tm * TOP_K), jnp.int32),
                        pltpu.VMEM((bm, half), jnp.uint32),
                        pltpu.SemaphoreType.DMA((2,)),
                        pltpu.SemaphoreType.DMA(()),
                        pltpu.SemaphoreType.DMA(())],
    )
    return pl.pallas_call(
        functools.partial(_dispatch_kernel, bm=bm),
        grid_spec=grid_spec,
        out_shape=jax.ShapeDtypeStruct((n_slots, half), jnp.uint32),
        compiler_params=_params(),
        name="dispatch",
    )(padded, pad_end, n_used, dest, hp)


def _expert_ffn_kernel(be_ref, nb_ref, x_ref, wgu_ref, bgu_ref, wd_ref, bd_ref, y_ref, wgu_s, wd_s):
    i = pl.program_id(0)
    d_ff = wd_ref.shape[0]
    half = x_ref.shape[1]

    @pl.when(i >= nb_ref[0])
    def _():
        y_ref[...] = jnp.zeros_like(y_ref)

    @pl.when(i < nb_ref[0])
    def _():
        changed = jnp.logical_or(i == 0, be_ref[i] != be_ref[jnp.maximum(i - 1, 0)])

        @pl.when(changed)
        def _():
            wgu_s[...] = wgu_ref[...].astype(BF16)
            wd_s[...] = wd_ref[...].astype(BF16)

        x_lo, x_hi = _unpack_pairs(x_ref[...])
        gu = (_dot(x_lo.astype(BF16), wgu_s[:half, :]) + _dot(x_hi.astype(BF16), wgu_s[half:, :])
              + bgu_ref[...])
        gate = jnp.minimum(gu[:, :d_ff], SWIGLU_LIMIT)
        up = jnp.clip(gu[:, d_ff:], -SWIGLU_LIMIT, SWIGLU_LIMIT)
        glu = gate * jax.nn.sigmoid(SWIGLU_ALPHA * gate)
        act = ((up + 1.0) * glu).astype(BF16)
        y = _dot(act, wd_s[...]) + bd_ref[...]
        y_ref[...] = _pack_pairs(y.astype(BF16).astype(F32))


def _expert_ffn(block_expert, n_used, xs, w_gate_up, b_gate_up, w_down, b_down, bm):
    n_slots, half = xs.shape
    n_e, d, two_ff = w_gate_up.shape
    d_ff = two_ff // 2
    n_blocks = n_slots // bm

    def used(i, nb):
        return jnp.minimum(i, jnp.maximum(nb[0] - 1, 0))

    grid_spec = pltpu.PrefetchScalarGridSpec(
        num_scalar_prefetch=2,
        grid=(n_blocks,),
        in_specs=[pl.BlockSpec((bm, half), lambda i, be, nb: (used(i, nb), 0)),
                  pl.BlockSpec((None, d, two_ff), lambda i, be, nb: (be[i], 0, 0)),
                  pl.BlockSpec((None, 1, two_ff), lambda i, be, nb: (be[i], 0, 0)),
                  pl.BlockSpec((None, d_ff, d), lambda i, be, nb: (be[i], 0, 0)),
                  pl.BlockSpec((None, 1, d), lambda i, be, nb: (be[i], 0, 0))],
        out_specs=pl.BlockSpec((bm, half), lambda i, be, nb: (i, 0)),
        scratch_shapes=[pltpu.VMEM((d, two_ff), BF16), pltpu.VMEM((d_ff, d), BF16)],
    )
    return pl.pallas_call(
        _expert_ffn_kernel,
        grid_spec=grid_spec,
        out_shape=jax.ShapeDtypeStruct((n_slots, half), jnp.uint32),
        compiler_params=_params(),
        name="expert_ffn",
    )(block_expert, n_used, xs, w_gate_up, b_gate_up.reshape(n_e, 1, two_ff),
      w_down, b_down.reshape(n_e, 1, d))


def _combine_kernel(dest_hbm, route_ref, x1_ref, ys_hbm, o_ref, dest_s, rows, sem_i, sem_g):
    i = pl.program_id(0)
    steps = pl.num_programs(0)
    tm = x1_ref.shape[0]
    half = x1_ref.shape[1] // 2

    def idx_copy(step):
        s = step % 2
        return pltpu.make_async_copy(dest_hbm.at[step], dest_s.at[s], sem_i.at[s])

    def start_gather(step):
        s = step % 2

        def body(t, carry):
            for k in range(TOP_K):
                pltpu.make_async_copy(ys_hbm.at[pl.ds(dest_s[s, t * TOP_K + k], 1), :],
                                      rows.at[s, k, pl.ds(t, 1), :], sem_g.at[s]).start(priority=k % 2)
            return carry

        lax.fori_loop(0, tm, body, 0, unroll=8)

    @pl.when(i == 0)
    def _():
        idx_copy(0).start()
        idx_copy(0).wait()

        @pl.when(steps > 1)
        def _():
            idx_copy(1).start()

        start_gather(0)

    @pl.when(i + 1 < steps)
    def _():
        idx_copy(i + 1).wait()

        @pl.when(i + 2 < steps)
        def _():
            idx_copy(i + 2).start()

        start_gather(i + 1)

    slot = i % 2
    for k in range(TOP_K):
        pltpu.make_async_copy(ys_hbm.at[pl.ds(0, tm), :], rows.at[slot, k], sem_g.at[slot]).wait()

    route = route_ref[...]
    y_lo = y_hi = None
    for k in range(TOP_K):
        gate = route[:, 2 * TOP_K + k: 2 * TOP_K + k + 1]
        lo, hi = _unpack_pairs(rows[slot, k])
        y_lo = gate * lo if k == 0 else y_lo + gate * lo
        y_hi = gate * hi if k == 0 else y_hi + gate * hi
    o_ref[:, :half] = x1_ref[:, :half] + y_lo
    o_ref[:, half:] = x1_ref[:, half:] + y_hi


def _combine(dest, route, x1, ys, tm):
    t, d = x1.shape
    half = d // 2
    return pl.pallas_call(
        _combine_kernel,
        grid=(t // tm,),
        in_specs=[pl.BlockSpec(memory_space=pl.ANY),
                  pl.BlockSpec((tm, LANES), lambda i: (i, 0)),
                  pl.BlockSpec((tm, d), lambda i: (i, 0)),
                  pl.BlockSpec(memory_space=pl.ANY)],
        out_specs=pl.BlockSpec((tm, d), lambda i: (i, 0)),
        out_shape=jax.ShapeDtypeStruct((t, d), F32),
        scratch_shapes=[pltpu.SMEM((2, tm * TOP_K), jnp.int32),
                        pltpu.VMEM((2, TOP_K, tm, half), jnp.uint32),
                        pltpu.SemaphoreType.DMA((2,)),
                        pltpu.SemaphoreType.DMA((2,))],
        compiler_params=_params(),
        name="combine",
    )(dest, route, x1, ys)


def _slot_layout(route, counts, tm, bm):
    t = route.shape[0]
    counts = counts.astype(jnp.int32)
    padded = (counts + bm - 1) // bm * bm
    pad_end = jnp.cumsum(padded).astype(jnp.int32)
    pad_start = pad_end - padded
    idx = route[:, :TOP_K].astype(jnp.int32)
    rank = route[:, TOP_K:2 * TOP_K].astype(jnp.int32)
    dest = (jnp.take(pad_start, idx) + rank).reshape(t // tm, tm * TOP_K)
    n_blocks = -(-t * TOP_K // bm) + N_EXPERTS
    block_start = jnp.arange(n_blocks, dtype=jnp.int32) * bm
    block_expert = jnp.minimum(jnp.sum(pad_end[None, :] <= block_start[:, None], axis=1),
                               N_EXPERTS - 1).astype(jnp.int32)
    n_used = (pad_end[-1:] // bm).astype(jnp.int32)
    return padded, pad_end, dest, block_expert, n_used, n_blocks * bm


def _rot_cols(w):
    half = w.shape[-1] // 2
    return jnp.concatenate([-w[..., half:], w[..., :half]], axis=-1)


def _swap_halves(g):
    half = g.shape[-1] // 2
    return jnp.concatenate([g[..., half:], g[..., :half]], axis=-1)


def kernel(x, attn_norm_g, w_in, a_q_norm_g, a_k_norm_g, rel_bias, q_a_norm_g, w_q_b, kv_a_norm_g,
           w_kv_b, b_q_norm_g, b_k_norm_g, w_out, ffn_norm_g, router_w, router_b, w_gate_up,
           b_gate_up, w_down, b_down):
    batch, seq, d = x.shape
    t = batch * seq
    depth = w_in.shape[0]
    tm = 512
    expert_bm = 256

    pos = jnp.arange(seq, dtype=F32)
    inv_freq = ROPE_THETA ** (-jnp.arange(0, QK_ROPE_DIM, 2, dtype=F32) / QK_ROPE_DIM)
    ang = pos[:, None] * inv_freq[None, :]
    cos, sin = jnp.cos(ang), jnp.sin(ang)
    cs = jnp.concatenate([cos, cos, sin, sin], axis=-1)

    row = jnp.arange(LANES)[:, None] // A_HEAD_DIM
    col = jnp.arange(LANES)[None, :] // A_HEAD_DIM
    head_block_ones = (row == col).astype(BF16)
    band_bias = _band_bias(rel_bias)

    x2 = x.reshape(t, d)
    for layer in range(depth):
        kpe_off = 3 * A_WIDTH + Q_LORA_RANK + KV_LORA_RANK
        w_kpe = w_in[layer][:, kpe_off:]
        w_in_r = jnp.concatenate([w_in[layer][:, :kpe_off], w_kpe, _rot_cols(w_kpe)], axis=1).astype(BF16)

        wq = w_q_b[layer].reshape(Q_LORA_RANK, B_HEADS, QK_HEAD_DIM)
        wq_rope = wq[..., QK_NOPE_DIM:]
        wq_r = jnp.concatenate([wq, _rot_cols(wq_rope)], axis=-1).reshape(Q_LORA_RANK, -1).astype(BF16)
        wkv = w_kv_b[layer].reshape(KV_LORA_RANK, B_HEADS, QK_NOPE_DIM + V_HEAD_DIM)
        wkv_r = jnp.concatenate([wkv[..., :QK_NOPE_DIM].reshape(KV_LORA_RANK, -1),
                                 wkv[..., QK_NOPE_DIM:].reshape(KV_LORA_RANK, -1)], axis=1).astype(BF16)
        gc = jnp.concatenate([q_a_norm_g[layer], kv_a_norm_g[layer]])[None, :]
        gq, gk = b_q_norm_g[layer], b_k_norm_g[layer]

        def rope_gain(gr):
            return jnp.concatenate([gr, _swap_halves(gr)])

        gqk = jnp.stack([gq[:QK_NOPE_DIM], rope_gain(gq[QK_NOPE_DIM:]),
                         gk[:QK_NOPE_DIM], rope_gain(gk[QK_NOPE_DIM:])])
        g_a = jnp.stack([jnp.tile(a_q_norm_g[layer], 2), jnp.tile(a_k_norm_g[layer], 2)])
        g_a = jnp.broadcast_to(g_a[None], (A_WIDTH // LANES, 2, LANES))

        proj = _in_proj(x2, attn_norm_g[layer][None, :], w_in_r, tm)
        qb, kb, vb = _mla_prep(proj, cs, wq_r, wkv_r, gc, gqk, seq, tm)
        out_b = _mla_attention(qb, kb, vb, batch, seq, 256)
        out_a = _dilated_attention(proj, head_block_ones, g_a, band_bias, batch, seq)

        rw = jnp.pad(router_w[layer], ((0, 0), (0, LANES - N_EXPERTS)))
        rw_hi = rw.astype(BF16)
        rw_lo = (rw - rw_hi.astype(F32)).astype(BF16)
        rb = jnp.pad(router_b[layer], (0, LANES - N_EXPERTS), constant_values=-jnp.inf)[None, :]
        x1, hp, route, counts = _out_proj(x2, out_a, out_b, w_out[layer].astype(BF16),
                                          ffn_norm_g[layer][None, :], jnp.stack([rw_hi, rw_lo]), rb, tm)

        padded, pad_end, dest, block_expert, n_used, n_slots = _slot_layout(
            route, counts[0, :N_EXPERTS], tm, expert_bm)
        xs = _dispatch(padded, pad_end, n_used, dest, hp, n_slots, tm, expert_bm)
        ys = _expert_ffn(block_expert, n_used, xs, w_gate_up[layer], b_gate_up[layer],
                         w_down[layer], b_down[layer], expert_bm)
        x2 = _combine(dest, route, x1, ys, tm)
    return x2.reshape(batch, seq, d)
```

```python
import functools
import math

import jax
import jax.numpy as jnp
from jax import lax
from jax.experimental import pallas as pl
from jax.experimental.pallas import tpu as pltpu

A_HEADS = 8
A_HEAD_DIM = 64
A_WIDTH = A_HEADS * A_HEAD_DIM
DILATED_PATTERNS = ((128, 1), (512, 4), (2048, 16))
BAND_BLOCK = 128

B_HEADS = 4
QK_NOPE_DIM = 128
QK_ROPE_DIM = 64
QK_HEAD_DIM = QK_NOPE_DIM + QK_ROPE_DIM
V_HEAD_DIM = 128
Q_LORA_RANK = 256
KV_LORA_RANK = 256
B_WIDTH = B_HEADS * V_HEAD_DIM
ROPE_THETA = 10000.0

NUM_BUCKETS = 32
MAX_DISTANCE = 2048

N_EXPERTS = 32
TOP_K = 4
SWIGLU_LIMIT = 7.0
SWIGLU_ALPHA = 1.702
RMS_EPS = 1e-6

LANES = 128
MASK_VALUE = -1e30
PROJ_WIDTH = 3 * A_WIDTH + Q_LORA_RANK + KV_LORA_RANK + 2 * QK_ROPE_DIM
VMEM_LIMIT = 56 * 1024 * 1024

F32 = jnp.float32
BF16 = jnp.bfloat16


def _dot(a, b):
    return jnp.dot(a, b, preferred_element_type=F32)


def _dot_nt(a, b):
    return lax.dot_general(a, b, (((1,), (1,)), ((), ())), preferred_element_type=F32)


def _split_dot(x, m):
    hi = x.astype(BF16)
    lo = (x - hi.astype(F32)).astype(BF16)
    return _dot(hi, m) + _dot(lo, m)


def _params(n_parallel=1):
    return pltpu.CompilerParams(
        dimension_semantics=("arbitrary",) * n_parallel, vmem_limit_bytes=VMEM_LIMIT)


def _in_proj_kernel(x_ref, g_ref, w_ref, o_ref):
    x = x_ref[...]
    ms = jnp.mean(x * x, axis=-1, keepdims=True)
    h = (x * lax.rsqrt(ms + RMS_EPS) * g_ref[...]).astype(BF16)
    o_ref[...] = _dot(h, w_ref[...])


def _in_proj(x2, g, w, tm):
    t, d = x2.shape
    n = w.shape[1]
    return pl.pallas_call(
        _in_proj_kernel,
        grid=(t // tm,),
        in_specs=[pl.BlockSpec((tm, d), lambda i: (i, 0)),
                  pl.BlockSpec((1, d), lambda i: (0, 0)),
                  pl.BlockSpec((d, n), lambda i: (0, 0))],
        out_specs=pl.BlockSpec((tm, n), lambda i: (i, 0)),
        out_shape=jax.ShapeDtypeStruct((t, n), F32),
        compiler_params=_params(),
        name="in_proj",
    )(x2, g, w)


def _mla_prep_kernel(c_ref, kpe_ref, cs_ref, wq_ref, wkv_ref, gc_ref, gqk_ref, q_ref, k_ref, v_ref,
                     *, q_scale):
    c = c_ref[...]
    gc = gc_ref[...]

    def lora_norm(z, g):
        ms = jnp.mean(z * z, axis=-1, keepdims=True)
        return (z * lax.rsqrt(ms + RMS_EPS) * g).astype(BF16)

    cq = lora_norm(c[:, :Q_LORA_RANK], gc[:, :Q_LORA_RANK])
    ckv = lora_norm(c[:, Q_LORA_RANK:], gc[:, Q_LORA_RANK:])
    qb = _dot(cq, wq_ref[...])
    kvb = _dot(ckv, wkv_ref[...])
    kper = kpe_ref[...]
    cs = cs_ref[...]
    gqk = gqk_ref[...]
    tm = c.shape[0]

    row = lax.broadcasted_iota(jnp.int32, (LANES, LANES), 0)
    ones_all = jnp.ones((LANES, LANES), BF16)
    ones_lo = jnp.where(row < QK_ROPE_DIM, 1.0, 0.0).astype(BF16)
    lane = lax.broadcasted_iota(jnp.int32, (tm, LANES), 1)

    def rope(z, g_row):
        t = z * (g_row * cs)
        return jnp.where(lane < QK_ROPE_DIM, t + pltpu.roll(t, QK_ROPE_DIM, 1), 0.0)

    k_rope = rope(kper, gqk[3:4])
    kpe_ss = _split_dot(kper * kper, ones_lo)
    for h in range(B_HEADS):
        qn = qb[:, 2 * LANES * h: 2 * LANES * h + LANES]
        qr = qb[:, 2 * LANES * h + LANES: 2 * LANES * (h + 1)]
        ss = _split_dot(qn * qn, ones_all) + _split_dot(qr * qr, ones_lo)
        rs = lax.rsqrt(ss * (1.0 / QK_HEAD_DIM) + RMS_EPS) * q_scale
        q_ref[h, :, :LANES] = (qn * gqk[0:1] * rs).astype(BF16)
        q_ref[h, :, LANES:] = (rope(qr, gqk[1:2]) * rs).astype(BF16)
        kn = kvb[:, LANES * h: LANES * (h + 1)]
        ssk = _split_dot(kn * kn, ones_all) + kpe_ss
        rsk = lax.rsqrt(ssk * (1.0 / QK_HEAD_DIM) + RMS_EPS)
        k_ref[h, :, :LANES] = (kn * gqk[2:3] * rsk).astype(BF16)
        k_ref[h, :, LANES:] = (k_rope * rsk).astype(BF16)
    v_ref[...] = kvb[:, B_HEADS * QK_NOPE_DIM:].astype(BF16)


def _mla_prep(proj, cs, wq, wkv, gc, gqk, seq, tm):
    t = proj.shape[0]
    c_width = Q_LORA_RANK + KV_LORA_RANK
    c_block = 3 * A_WIDTH // c_width
    kpe_block = (3 * A_WIDTH + c_width) // LANES
    pos_blocks = seq // tm
    kernel = functools.partial(_mla_prep_kernel, q_scale=QK_HEAD_DIM ** -0.5)
    return pl.pallas_call(
        kernel,
        grid=(t // tm,),
        in_specs=[pl.BlockSpec((tm, c_width), lambda i: (i, c_block)),
                  pl.BlockSpec((tm, LANES), lambda i: (i, kpe_block)),
                  pl.BlockSpec((tm, LANES), lambda i: (i % pos_blocks, 0)),
                  pl.BlockSpec(wq.shape, lambda i: (0, 0)),
                  pl.BlockSpec(wkv.shape, lambda i: (0, 0)),
                  pl.BlockSpec(gc.shape, lambda i: (0, 0)),
                  pl.BlockSpec(gqk.shape, lambda i: (0, 0))],
        out_specs=[pl.BlockSpec((B_HEADS, tm, 2 * LANES), lambda i: (0, i, 0)),
                   pl.BlockSpec((B_HEADS, tm, 2 * LANES), lambda i: (0, i, 0)),
                   pl.BlockSpec((tm, B_WIDTH), lambda i: (i, 0))],
        out_shape=[jax.ShapeDtypeStruct((B_HEADS, t, 2 * LANES), BF16),
                   jax.ShapeDtypeStruct((B_HEADS, t, 2 * LANES), BF16),
                   jax.ShapeDtypeStruct((t, B_WIDTH), BF16)],
        compiler_params=_params(),
        name="mla_prep",
    )(proj, proj, cs, wq, wkv, gc, gqk)


def _mla_attention_kernel(q_ref, k_ref, v_ref, o_ref, *, tq):
    seq = q_ref.shape[0]
    row = lax.broadcasted_iota(jnp.int32, (tq, tq), 0)
    col = lax.broadcasted_iota(jnp.int32, (tq, tq), 1)
    diag_mask = jnp.where(col <= row, 0.0, MASK_VALUE)
    for i in range(seq // tq):
        q = q_ref[i * tq:(i + 1) * tq, :]
        s_diag = _dot_nt(q, k_ref[i * tq:(i + 1) * tq, :]) + diag_mask
        m = jnp.max(s_diag, axis=-1, keepdims=True)
        if i > 0:
            s_past = _dot_nt(q, k_ref[:i * tq, :])
            m = jnp.maximum(m, jnp.max(s_past, axis=-1, keepdims=True))
        p_diag = jnp.exp(s_diag - m)
        l = jnp.sum(p_diag, axis=-1, keepdims=True)
        o = _dot(p_diag.astype(BF16), v_ref[i * tq:(i + 1) * tq, :])
        if i > 0:
            p_past = jnp.exp(s_past - m)
            l = l + jnp.sum(p_past, axis=-1, keepdims=True)
            o = o + _dot(p_past.astype(BF16), v_ref[:i * tq, :])
        o_ref[i * tq:(i + 1) * tq, :] = (o / l).astype(o_ref.dtype)


def _mla_attention(qb, kb, vb, batch, seq, tq):
    t = vb.shape[0]
    kernel = functools.partial(_mla_attention_kernel, tq=tq)
    return pl.pallas_call(
        kernel,
        grid=(batch, B_HEADS),
        in_specs=[pl.BlockSpec((None, seq, 2 * LANES), lambda b, h: (h, b, 0)),
                  pl.BlockSpec((None, seq, 2 * LANES), lambda b, h: (h, b, 0)),
                  pl.BlockSpec((seq, V_HEAD_DIM), lambda b, h: (b, h))],
        out_specs=pl.BlockSpec((seq, V_HEAD_DIM), lambda b, h: (b, h)),
        out_shape=jax.ShapeDtypeStruct((t, B_WIDTH), BF16),
        compiler_params=_params(2),
        name="mla_attention",
    )(qb, kb, vb)


def _dilated_kernel(q_ref, k_ref, v_ref, bd_ref, g_ref, bias_ref, o_ref, qs, ks, acc_o, acc_l, acc_m,
                    *, unroll):
    seq = q_ref.shape[0]
    n_tiles = seq // BAND_BLOCK
    low = lax.broadcasted_iota(jnp.int32, (BAND_BLOCK, LANES), 1) < A_HEAD_DIM
    bd = bd_ref[...]
    g = g_ref[...]

    def head_norm(z, g_row):
        ss = _split_dot(z * z, bd)
        return z * lax.rsqrt(ss * (1.0 / A_HEAD_DIM) + RMS_EPS) * g_row

    qs[...] = head_norm(q_ref[...], g[0:1]) * (A_HEAD_DIM ** -0.5)
    ks[...] = head_norm(k_ref[...], g[1:2])
    ones = jnp.ones((2 * BAND_BLOCK, LANES), BF16)

    for p, (window, dil) in enumerate(DILATED_PATTERNS):
        blocks_per_class = n_tiles // dil
        span = BAND_BLOCK * dil

        def rows_at(start, dil=dil):
            if dil == 1:
                return pl.ds(start, BAND_BLOCK)
            return pl.ds(start, BAND_BLOCK, stride=dil)

        def tile(i, carry, p=p, blocks_per_class=blocks_per_class, span=span, rows_at=rows_at):
            r = i // blocks_per_class
            j = i % blocks_per_class
            cur = rows_at(r + j * span)
            prev = rows_at(r + jnp.maximum(j - 1, 0) * span)
            q = qs[cur, :]
            q2 = jnp.concatenate([jnp.where(low, q, 0.0), jnp.where(low, 0.0, q)], axis=0).astype(BF16)
            k_band = jnp.concatenate([ks[prev, :], ks[cur, :]], axis=0).astype(BF16)
            s = _dot_nt(q2, k_band) + bias_ref[p, jnp.where(j == 0, 1, 0)]
            m = jnp.max(s, axis=-1, keepdims=True)
            pr = jnp.exp(s - m).astype(BF16)
            v_band = jnp.concatenate([v_ref[prev, :], v_ref[cur, :]], axis=0).astype(BF16)
            o = _dot(pr, jnp.concatenate([v_band, ones], axis=1))
            top, bot = o[:BAND_BLOCK], o[BAND_BLOCK:]
            acc_o[p, cur, :] = jnp.where(low, top[:, :LANES], bot[:, :LANES])
            acc_l[p, cur, :] = jnp.where(low, top[:, LANES:], bot[:, LANES:])
            acc_m[p, cur, :] = jnp.where(low, m[:BAND_BLOCK], m[BAND_BLOCK:])
            return carry

        lax.fori_loop(0, n_tiles, tile, 0, unroll=unroll)

    m_all = jnp.maximum(jnp.maximum(acc_m[0], acc_m[1]), acc_m[2])
    num = jnp.zeros((seq, LANES), F32)
    den = jnp.zeros((seq, LANES), F32)
    for p in range(len(DILATED_PATTERNS)):
        w = jnp.exp(acc_m[p] - m_all)
        num = num + w * acc_o[p]
        den = den + w * acc_l[p]
    o_ref[...] = (num / den).astype(o_ref.dtype)


def _dilated_attention(proj, bd, g, bias, batch, seq):
    t = proj.shape[0]
    pairs = A_WIDTH // LANES
    n_pat = len(DILATED_PATTERNS)
    return pl.pallas_call(
        functools.partial(_dilated_kernel, unroll=8),
        grid=(batch, pairs),
        in_specs=[pl.BlockSpec((seq, LANES), lambda b, c: (b, c)),
                  pl.BlockSpec((seq, LANES), lambda b, c: (b, pairs + c)),
                  pl.BlockSpec((seq, LANES), lambda b, c: (b, 2 * pairs + c)),
                  pl.BlockSpec((LANES, LANES), lambda b, c: (0, 0)),
                  pl.BlockSpec((None, 2, LANES), lambda b, c: (c, 0, 0)),
                  pl.BlockSpec((None, n_pat, 2, 2 * BAND_BLOCK, 2 * BAND_BLOCK),
                               lambda b, c: (c, 0, 0, 0, 0))],
        out_specs=pl.BlockSpec((seq, LANES), lambda b, c: (b, c)),
        out_shape=jax.ShapeDtypeStruct((t, A_WIDTH), BF16),
        scratch_shapes=[pltpu.VMEM((seq, LANES), F32)] * 2
                       + [pltpu.VMEM((n_pat, seq, LANES), F32)] * 3,
        compiler_params=_params(2),
        name="dilated_attention",
    )(proj, proj, proj, bd, g, bias)


def _t5_bucket(dist):
    max_exact = NUM_BUCKETS // 2
    df = jnp.maximum(dist, 1).astype(F32)
    log_bucket = max_exact + (jnp.log(df / max_exact) / math.log(MAX_DISTANCE / max_exact)
                              * (NUM_BUCKETS - max_exact)).astype(jnp.int32)
    log_bucket = jnp.minimum(log_bucket, NUM_BUCKETS - 1)
    return jnp.where(dist < max_exact, dist, log_bucket)


def _toeplitz(u):
    n = BAND_BLOCK
    lead = u.shape[:-1]
    w = jnp.concatenate([u[..., ::-1], jnp.zeros(lead + (1,), u.dtype)], axis=-1)
    r = jnp.broadcast_to(w[..., None, :], lead + (n, 2 * n)).reshape(lead + (2 * n * n,))
    return r[..., :n * (2 * n - 1)].reshape(lead + (n, 2 * n - 1))[..., n - 1:]


def _band_bias(rel_bias):
    n = BAND_BLOCK
    tables = []
    for window, dil in DILATED_PATTERNS:
        steps = window // dil
        back = jnp.arange(2 * n)
        by_back = rel_bias[_t5_bucket(back * dil)].astype(F32).T
        by_back = jnp.where(back <= steps, by_back, MASK_VALUE)
        masked = jnp.full((A_HEADS, n - 1), MASK_VALUE, F32)
        cur = _toeplitz(jnp.concatenate([masked, by_back[:, :n]], axis=1))
        prev = _toeplitz(by_back[:, 1:])
        normal = jnp.concatenate([prev, cur], axis=-1)
        first = jnp.concatenate([jnp.full_like(prev, MASK_VALUE), cur], axis=-1)
        tables.append(jnp.stack([normal, first], axis=1))
    tab = jnp.stack(tables, axis=1)
    tab = tab.reshape(A_HEADS // 2, 2, len(DILATED_PATTERNS), 2, n, 2 * n)
    return tab.transpose(0, 2, 3, 1, 4, 5).reshape(A_HEADS // 2, len(DILATED_PATTERNS), 2, 2 * n, 2 * n)


def _pack_pairs(v):
    half = v.shape[1] // 2
    bits = pltpu.bitcast(v, jnp.uint32)
    return (bits[:, :half] >> 16) | (bits[:, half:] & jnp.uint32(0xFFFF0000))


def _unpack_pairs(p):
    return (pltpu.bitcast(p << 16, F32), pltpu.bitcast(p & jnp.uint32(0xFFFF0000), F32))


def _out_proj_kernel(x_ref, a_ref, b_ref, w_ref, g_ref, rw_ref, rb_ref,
                     x1_ref, hp_ref, route_ref, cnt_ref, carry):
    i = pl.program_id(0)
    tm = x_ref.shape[0]

    @pl.when(i == 0)
    def _():
        carry[...] = jnp.zeros_like(carry)

    w = w_ref[...]
    x1 = x_ref[...] + _dot(a_ref[...], w[:A_WIDTH]) + _dot(b_ref[...], w[A_WIDTH:])
    x1_ref[...] = x1
    ms = jnp.mean(x1 * x1, axis=-1, keepdims=True)
    h = x1 * lax.rsqrt(ms + RMS_EPS) * g_ref[...]
    hi = h.astype(BF16)
    hi_f = hi.astype(F32)
    hp_ref[...] = _pack_pairs(hi_f)
    lo = (h - hi_f).astype(BF16)
    rw_hi = rw_ref[0]
    logits = _dot(hi, rw_hi) + _dot(lo, rw_hi) + _dot(hi, rw_ref[1]) + rb_ref[...]

    lane = lax.broadcasted_iota(jnp.int32, (tm, LANES), 1).astype(F32)
    remaining = logits
    vals, hots = [], []
    for _ in range(TOP_K):
        m = jnp.max(remaining, axis=-1, keepdims=True)
        first = jnp.min(jnp.where(remaining == m, lane, float(LANES)), axis=-1, keepdims=True)
        hot = lane == first
        remaining = jnp.where(hot, -jnp.inf, remaining)
        vals.append(m)
        hots.append(hot)
    exps = [jnp.exp(v - vals[0]) for v in vals]
    den = exps[0] + exps[1] + exps[2] + exps[3]

    chosen = jnp.zeros((tm, LANES), F32)
    for hot in hots:
        chosen = chosen + jnp.where(hot, 1.0, 0.0)
    r = lax.broadcasted_iota(jnp.int32, (tm, tm), 0)
    c = lax.broadcasted_iota(jnp.int32, (tm, tm), 1)
    earlier = jnp.where(r > c, 1.0, 0.0).astype(BF16)
    before = carry[...] + _dot(earlier, chosen.astype(BF16))
    carry[...] = carry[...] + jnp.sum(chosen, axis=0, keepdims=True)
    cnt_ref[...] = jnp.broadcast_to(carry[...], cnt_ref.shape)

    route = jnp.zeros((tm, LANES), F32)
    for k in range(TOP_K):
        first = jnp.sum(jnp.where(hots[k], lane, 0.0), axis=-1, keepdims=True)
        rank = jnp.sum(jnp.where(hots[k], before, 0.0), axis=-1, keepdims=True)
        route = route + jnp.where(lane == float(k), first, 0.0)
        route = route + jnp.where(lane == float(TOP_K + k), rank, 0.0)
        route = route + jnp.where(lane == float(2 * TOP_K + k), exps[k] / den, 0.0)
    route_ref[...] = route


def _out_proj(x2, out_a, out_b, w, g, rw, rb, tm):
    t, d = x2.shape
    return pl.pallas_call(
        _out_proj_kernel,
        grid=(t // tm,),
        in_specs=[pl.BlockSpec((tm, d), lambda i: (i, 0)),
                  pl.BlockSpec((tm, A_WIDTH), lambda i: (i, 0)),
                  pl.BlockSpec((tm, B_WIDTH), lambda i: (i, 0)),
                  pl.BlockSpec(w.shape, lambda i: (0, 0)),
                  pl.BlockSpec((1, d), lambda i: (0, 0)),
                  pl.BlockSpec(rw.shape, lambda i: (0, 0, 0)),
                  pl.BlockSpec((1, LANES), lambda i: (0, 0))],
        out_specs=[pl.BlockSpec((tm, d), lambda i: (i, 0)),
                   pl.BlockSpec((tm, d // 2), lambda i: (i, 0)),
                   pl.BlockSpec((tm, LANES), lambda i: (i, 0)),
                   pl.BlockSpec((8, LANES), lambda i: (0, 0))],
        out_shape=[jax.ShapeDtypeStruct((t, d), F32),
                   jax.ShapeDtypeStruct((t, d // 2), jnp.uint32),
                   jax.ShapeDtypeStruct((t, LANES), F32),
                   jax.ShapeDtypeStruct((8, LANES), F32)],
        scratch_shapes=[pltpu.VMEM((1, LANES), F32)],
        compiler_params=_params(),
        name="out_proj",
    )(x2, out_a, out_b, w, g, rw, rb)


def _dispatch_kernel(pad_ref, end_ref, nb_ref, dest_hbm, h_ref, xs_hbm, dest_s0, dest_s1, zeros,
                     sem_i, sem_z, sem_o, *, bm):
    i = pl.program_id(0)
    steps = pl.num_programs(0)
    tm = h_ref.shape[0]
    n_blocks = xs_hbm.shape[0] // bm
    dest_s = (dest_s0, dest_s1)

    def idx_copy(step, slot):
        return pltpu.make_async_copy(dest_hbm.at[step], dest_s[slot], sem_i.at[slot])

    @pl.when(i == 0)
    def _():
        idx_copy(0, 0).start()
        zeros[...] = jnp.zeros_like(zeros)

        def zero_block(start):
            return pltpu.make_async_copy(zeros, xs_hbm.at[pl.ds(pl.multiple_of(start, bm), bm), :], sem_z)

        for e in range(N_EXPERTS):
            @pl.when(pad_ref[e] > 0)
            def _():
                zero_block(end_ref[e] - bm).start()

        def start_unused(b, carry):
            zero_block(b * bm).start()
            return carry

        def wait_unused(b, carry):
            zero_block(b * bm).wait()
            return carry

        lax.fori_loop(nb_ref[0], n_blocks, start_unused, 0)
        for e in range(N_EXPERTS):
            @pl.when(pad_ref[e] > 0)
            def _():
                zero_block(end_ref[e] - bm).wait()
        lax.fori_loop(nb_ref[0], n_blocks, wait_unused, 0)

    for slot in range(2):
        @pl.when(i % 2 == slot)
        def _(slot=slot):
            @pl.when(i + 1 < steps)
            def _():
                idx_copy(i + 1, 1 - slot).start()

            idx_copy(i, slot).wait()

            def body(t, carry):
                for k in range(TOP_K):
                    pltpu.make_async_copy(h_ref.at[pl.ds(t, 1), :],
                                          xs_hbm.at[pl.ds(dest_s[slot][t * TOP_K + k], 1), :],
                                          sem_o).start(priority=k % 2)
                return carry

            lax.fori_loop(0, tm, body, 0, unroll=8)
    for _ in range(TOP_K):
        pltpu.make_async_copy(h_ref, xs_hbm.at[pl.ds(0, tm), :], sem_o).wait()


def _dispatch(padded, pad_end, n_used, dest, hp, n_slots, tm, bm):
    t, half = hp.shape
    grid_spec = pltpu.PrefetchScalarGridSpec(
        num_scalar_prefetch=3,
        grid=(t // tm,),
        in_specs=[pl.BlockSpec(memory_space=pl.ANY),
                  pl.BlockSpec((tm, half), lambda i, p, e, n: (i, 0))],
        out_specs=pl.BlockSpec(memory_space=pl.ANY),
        scratch_shapes=[pltpu.SMEM((tm * TOP_K,), jnp.int32),
                        pltpu.SMEM((tm * TOP_K,), jnp.int32),
                        pltpu.VMEM((bm, half), jnp.uint32),
                        pltpu.SemaphoreType.DMA((2,)),
                        pltpu.SemaphoreType.DMA(()),
                        pltpu.SemaphoreType.DMA(())],
    )
    return pl.pallas_call(
        functools.partial(_dispatch_kernel, bm=bm),
        grid_spec=grid_spec,
        out_shape=jax.ShapeDtypeStruct((n_slots, half), jnp.uint32),
        compiler_params=_params(),
        name="dispatch",
    )(padded, pad_end, n_used, dest, hp)


def _expert_ffn_kernel(be_ref, nb_ref, x_ref, wgu_ref, bgu_ref, wd_ref, bd_ref, y_ref, wgu_s, wd_s):
    i = pl.program_id(0)
    d_ff = wd_ref.shape[0]
    half = x_ref.shape[1]

    @pl.when(i >= nb_ref[0])
    def _():
        y_ref[...] = jnp.zeros_like(y_ref)

    @pl.when(i < nb_ref[0])
    def _():
        changed = jnp.logical_or(i == 0, be_ref[i] != be_ref[jnp.maximum(i - 1, 0)])

        @pl.when(changed)
        def _():
            wgu_s[...] = wgu_ref[...].astype(BF16)
            wd_s[...] = wd_ref[...].astype(BF16)

        x_lo, x_hi = _unpack_pairs(x_ref[...])
        gu = (_dot(x_lo.astype(BF16), wgu_s[:half, :]) + _dot(x_hi.astype(BF16), wgu_s[half:, :])
              + bgu_ref[...])
        gate = jnp.minimum(gu[:, :d_ff], SWIGLU_LIMIT)
        up = jnp.clip(gu[:, d_ff:], -SWIGLU_LIMIT, SWIGLU_LIMIT)
        glu = gate * jax.nn.sigmoid(SWIGLU_ALPHA * gate)
        act = ((up + 1.0) * glu).astype(BF16)
        y = _dot(act, wd_s[...]) + bd_ref[...]
        y_ref[...] = _pack_pairs(y.astype(BF16).astype(F32))


def _expert_ffn(block_expert, n_used, xs, w_gate_up, b_gate_up, w_down, b_down, bm):
    n_slots, half = xs.shape
    n_e, d, two_ff = w_gate_up.shape
    d_ff = two_ff // 2
    n_blocks = n_slots // bm

    def used(i, nb):
        return jnp.minimum(i, jnp.maximum(nb[0] - 1, 0))

    grid_spec = pltpu.PrefetchScalarGridSpec(
        num_scalar_prefetch=2,
        grid=(n_blocks,),
        in_specs=[pl.BlockSpec((bm, half), lambda i, be, nb: (used(i, nb), 0)),
                  pl.BlockSpec((None, d, two_ff), lambda i, be, nb: (be[i], 0, 0)),
                  pl.BlockSpec((None, 1, two_ff), lambda i, be, nb: (be[i], 0, 0)),
                  pl.BlockSpec((None, d_ff, d), lambda i, be, nb: (be[i], 0, 0)),
                  pl.BlockSpec((None, 1, d), lambda i, be, nb: (be[i], 0, 0))],
        out_specs=pl.BlockSpec((bm, half), lambda i, be, nb: (i, 0)),
        scratch_shapes=[pltpu.VMEM((d, two_ff), BF16), pltpu.VMEM((d_ff, d), BF16)],
    )
    return pl.pallas_call(
        _expert_ffn_kernel,
        grid_spec=grid_spec,
        out_shape=jax.ShapeDtypeStruct((n_slots, half), jnp.uint32),
        compiler_params=_params(),
        name="expert_ffn",
    )(block_expert, n_used, xs, w_gate_up, b_gate_up.reshape(n_e, 1, two_ff),
      w_down, b_down.reshape(n_e, 1, d))


def _combine_kernel(dest_hbm, route_ref, x1_ref, ys_hbm, o_ref, dest_s0, dest_s1, rows, sem_i, sem_g):
    j = pl.program_id(0)
    blocks = pl.num_programs(0) - 1
    tm = x1_ref.shape[0]
    half = x1_ref.shape[1] // 2

    dest_s = (dest_s0, dest_s1)

    def idx_copy(block, s):
        return pltpu.make_async_copy(dest_hbm.at[block], dest_s[s], sem_i.at[s])

    @pl.when(j == 0)
    def _():
        idx_copy(0, 0).start()

    @pl.when(j < blocks)
    def _():
        for s in range(2):
            @pl.when(j % 2 == s)
            def _(s=s):
                idx_copy(j, s).wait()

                @pl.when(j + 1 < blocks)
                def _():
                    idx_copy(j + 1, 1 - s).start()

                def body(t, carry):
                    for k in range(TOP_K):
                        pltpu.make_async_copy(ys_hbm.at[pl.ds(dest_s[s][t * TOP_K + k], 1), :],
                                              rows.at[s, k, pl.ds(t, 1), :],
                                              sem_g.at[s]).start(priority=k % 2)
                    return carry

                lax.fori_loop(0, tm, body, 0, unroll=8)

    @pl.when(j >= 1)
    def _():
        slot = (j - 1) % 2
        for k in range(TOP_K):
            pltpu.make_async_copy(ys_hbm.at[pl.ds(0, tm), :], rows.at[slot, k], sem_g.at[slot]).wait()

        route = route_ref[...]
        y_lo = y_hi = None
        for k in range(TOP_K):
            gate = route[:, 2 * TOP_K + k: 2 * TOP_K + k + 1]
            lo, hi = _unpack_pairs(rows[slot, k])
            y_lo = gate * lo if k == 0 else y_lo + gate * lo
            y_hi = gate * hi if k == 0 else y_hi + gate * hi
        o_ref[:, :half] = x1_ref[:, :half] + y_lo
        o_ref[:, half:] = x1_ref[:, half:] + y_hi


def _combine(dest, route, x1, ys, tm):
    t, d = x1.shape
    half = d // 2

    def summed(j):
        return (jnp.maximum(j - 1, 0), 0)

    return pl.pallas_call(
        _combine_kernel,
        grid=(t // tm + 1,),
        in_specs=[pl.BlockSpec(memory_space=pl.ANY),
                  pl.BlockSpec((tm, LANES), summed),
                  pl.BlockSpec((tm, d), summed),
                  pl.BlockSpec(memory_space=pl.ANY)],
        out_specs=pl.BlockSpec((tm, d), summed),
        out_shape=jax.ShapeDtypeStruct((t, d), F32),
        scratch_shapes=[pltpu.SMEM((tm * TOP_K,), jnp.int32),
                        pltpu.SMEM((tm * TOP_K,), jnp.int32),
                        pltpu.VMEM((2, TOP_K, tm, half), jnp.uint32),
                        pltpu.SemaphoreType.DMA((2,)),
                        pltpu.SemaphoreType.DMA((2,))],
        compiler_params=_params(),
        name="combine",
    )(dest, route, x1, ys)


def _slot_layout(route, counts, tm, bm):
    t = route.shape[0]
    counts = counts.astype(jnp.int32)
    padded = (counts + bm - 1) // bm * bm
    pad_end = jnp.cumsum(padded).astype(jnp.int32)
    pad_start = pad_end - padded
    idx = route[:, :TOP_K].astype(jnp.int32)
    rank = route[:, TOP_K:2 * TOP_K].astype(jnp.int32)
    dest = (jnp.take(pad_start, idx) + rank).reshape(t // tm, tm * TOP_K)
    n_blocks = -(-t * TOP_K // bm) + N_EXPERTS
    block_start = jnp.arange(n_blocks, dtype=jnp.int32) * bm
    block_expert = jnp.minimum(jnp.sum(pad_end[None, :] <= block_start[:, None], axis=1),
                               N_EXPERTS - 1).astype(jnp.int32)
    n_used = (pad_end[-1:] // bm).astype(jnp.int32)
    return padded, pad_end, dest, block_expert, n_used, n_blocks * bm


def _rot_cols(w):
    half = w.shape[-1] // 2
    return jnp.concatenate([-w[..., half:], w[..., :half]], axis=-1)


def _swap_halves(g):
    half = g.shape[-1] // 2
    return jnp.concatenate([g[..., half:], g[..., :half]], axis=-1)


def kernel(x, attn_norm_g, w_in, a_q_norm_g, a_k_norm_g, rel_bias, q_a_norm_g, w_q_b, kv_a_norm_g,
           w_kv_b, b_q_norm_g, b_k_norm_g, w_out, ffn_norm_g, router_w, router_b, w_gate_up,
           b_gate_up, w_down, b_down):
    batch, seq, d = x.shape
    t = batch * seq
    depth = w_in.shape[0]
    tm = 512
    expert_bm = 256

    pos = jnp.arange(seq, dtype=F32)
    inv_freq = ROPE_THETA ** (-jnp.arange(0, QK_ROPE_DIM, 2, dtype=F32) / QK_ROPE_DIM)
    ang = pos[:, None] * inv_freq[None, :]
    cos, sin = jnp.cos(ang), jnp.sin(ang)
    cs = jnp.concatenate([cos, cos, sin, sin], axis=-1)

    row = jnp.arange(LANES)[:, None] // A_HEAD_DIM
    col = jnp.arange(LANES)[None, :] // A_HEAD_DIM
    head_block_ones = (row == col).astype(BF16)
    band_bias = _band_bias(rel_bias)

    x2 = x.reshape(t, d)
    for layer in range(depth):
        kpe_off = 3 * A_WIDTH + Q_LORA_RANK + KV_LORA_RANK
        w_kpe = w_in[layer][:, kpe_off:]
        w_in_r = jnp.concatenate([w_in[layer][:, :kpe_off], w_kpe, _rot_cols(w_kpe)], axis=1).astype(BF16)

        wq = w_q_b[layer].reshape(Q_LORA_RANK, B_HEADS, QK_HEAD_DIM)
        wq_rope = wq[..., QK_NOPE_DIM:]
        wq_r = jnp.concatenate([wq, _rot_cols(wq_rope)], axis=-1).reshape(Q_LORA_RANK, -1).astype(BF16)
        wkv = w_kv_b[layer].reshape(KV_LORA_RANK, B_HEADS, QK_NOPE_DIM + V_HEAD_DIM)
        wkv_r = jnp.concatenate([wkv[..., :QK_NOPE_DIM].reshape(KV_LORA_RANK, -1),
                                 wkv[..., QK_NOPE_DIM:].reshape(KV_LORA_RANK, -1)], axis=1).astype(BF16)
        gc = jnp.concatenate([q_a_norm_g[layer], kv_a_norm_g[layer]])[None, :]
        gq, gk = b_q_norm_g[layer], b_k_norm_g[layer]

        def rope_gain(gr):
            return jnp.concatenate([gr, _swap_halves(gr)])

        gqk = jnp.stack([gq[:QK_NOPE_DIM], rope_gain(gq[QK_NOPE_DIM:]),
                         gk[:QK_NOPE_DIM], rope_gain(gk[QK_NOPE_DIM:])])
        g_a = jnp.stack([jnp.tile(a_q_norm_g[layer], 2), jnp.tile(a_k_norm_g[layer], 2)])
        g_a = jnp.broadcast_to(g_a[None], (A_WIDTH // LANES, 2, LANES))

        proj = _in_proj(x2, attn_norm_g[layer][None, :], w_in_r, tm)
        qb, kb, vb = _mla_prep(proj, cs, wq_r, wkv_r, gc, gqk, seq, tm)
        out_b = _mla_attention(qb, kb, vb, batch, seq, 256)
        out_a = _dilated_attention(proj, head_block_ones, g_a, band_bias, batch, seq)

        rw = jnp.pad(router_w[layer], ((0, 0), (0, LANES - N_EXPERTS)))
        rw_hi = rw.astype(BF16)
        rw_lo = (rw - rw_hi.astype(F32)).astype(BF16)
        rb = jnp.pad(router_b[layer], (0, LANES - N_EXPERTS), constant_values=-jnp.inf)[None, :]
        x1, hp, route, counts = _out_proj(x2, out_a, out_b, w_out[layer].astype(BF16),
                                          ffn_norm_g[layer][None, :], jnp.stack([rw_hi, rw_lo]), rb, tm)

        padded, pad_end, dest, block_expert, n_used, n_slots = _slot_layout(
            route, counts[0, :N_EXPERTS], tm, expert_bm)
        xs = _dispatch(padded, pad_end, n_used, dest, hp, n_slots, tm, expert_bm)
        ys = _expert_ffn(block_expert, n_used, xs, w_gate_up[layer], b_gate_up[layer],
                         w_down[layer], b_down[layer], expert_bm)
        x2 = _combine(dest, route, x1, ys, tm)
    return x2.reshape(batch, seq, d)
```

```python
import functools
import math

import jax
import jax.numpy as jnp
from jax import lax
from jax.experimental import pallas as pl
from jax.experimental.pallas import tpu as pltpu

A_HEADS = 8
A_HEAD_DIM = 64
A_WIDTH = A_HEADS * A_HEAD_DIM
DILATED_PATTERNS = ((128, 1), (512, 4), (2048, 16))
BAND_BLOCK = 128

B_HEADS = 4
QK_NOPE_DIM = 128
QK_ROPE_DIM = 64
QK_HEAD_DIM = QK_NOPE_DIM + QK_ROPE_DIM
V_HEAD_DIM = 128
Q_LORA_RANK = 256
KV_LORA_RANK = 256
B_WIDTH = B_HEADS * V_HEAD_DIM
ROPE_THETA = 10000.0

NUM_BUCKETS = 32
MAX_DISTANCE = 2048

N_EXPERTS = 32
TOP_K = 4
SWIGLU_LIMIT = 7.0
SWIGLU_ALPHA = 1.702
RMS_EPS = 1e-6

LANES = 128
MASK_VALUE = -1e30
LOG2E = math.log2(math.e)
PROJ_WIDTH = 3 * A_WIDTH + Q_LORA_RANK + KV_LORA_RANK + 2 * QK_ROPE_DIM
VMEM_LIMIT = 56 * 1024 * 1024

F32 = jnp.float32
BF16 = jnp.bfloat16


def _dot(a, b):
    return jnp.dot(a, b, preferred_element_type=F32)


def _dot_nt(a, b):
    return lax.dot_general(a, b, (((1,), (1,)), ((), ())), preferred_element_type=F32)


def _split_dot(x, m):
    hi = x.astype(BF16)
    lo = (x - hi.astype(F32)).astype(BF16)
    return _dot(hi, m) + _dot(lo, m)


def _params(n_parallel=1):
    return pltpu.CompilerParams(
        dimension_semantics=("arbitrary",) * n_parallel, vmem_limit_bytes=VMEM_LIMIT)


def _in_proj_kernel(x_ref, g_ref, w_ref, o_ref):
    x = x_ref[...]
    ms = jnp.mean(x * x, axis=-1, keepdims=True)
    h = (x * lax.rsqrt(ms + RMS_EPS) * g_ref[...]).astype(BF16)
    o_ref[...] = _dot(h, w_ref[...])


def _in_proj(x2, g, w, tm):
    t, d = x2.shape
    n = w.shape[1]
    return pl.pallas_call(
        _in_proj_kernel,
        grid=(t // tm,),
        in_specs=[pl.BlockSpec((tm, d), lambda i: (i, 0)),
                  pl.BlockSpec((1, d), lambda i: (0, 0)),
                  pl.BlockSpec((d, n), lambda i: (0, 0))],
        out_specs=pl.BlockSpec((tm, n), lambda i: (i, 0)),
        out_shape=jax.ShapeDtypeStruct((t, n), F32),
        compiler_params=_params(),
        name="in_proj",
    )(x2, g, w)


def _mla_prep_kernel(c_ref, kpe_ref, cs_ref, wq_ref, wkv_ref, gc_ref, gqk_ref, q_ref, k_ref, v_ref,
                     *, q_scale):
    c = c_ref[...]
    gc = gc_ref[...]

    def lora_norm(z, g):
        ms = jnp.mean(z * z, axis=-1, keepdims=True)
        return (z * lax.rsqrt(ms + RMS_EPS) * g).astype(BF16)

    cq = lora_norm(c[:, :Q_LORA_RANK], gc[:, :Q_LORA_RANK])
    ckv = lora_norm(c[:, Q_LORA_RANK:], gc[:, Q_LORA_RANK:])
    qb = _dot(cq, wq_ref[...])
    kvb = _dot(ckv, wkv_ref[...])
    kper = kpe_ref[...]
    cs = cs_ref[...]
    gqk = gqk_ref[...]
    tm = c.shape[0]

    row = lax.broadcasted_iota(jnp.int32, (LANES, LANES), 0)
    ones_all = jnp.ones((LANES, LANES), BF16)
    ones_lo = jnp.where(row < QK_ROPE_DIM, 1.0, 0.0).astype(BF16)
    lane = lax.broadcasted_iota(jnp.int32, (tm, LANES), 1)

    def rope(z, g_row):
        t = z * (g_row * cs)
        return jnp.where(lane < QK_ROPE_DIM, t + pltpu.roll(t, QK_ROPE_DIM, 1), 0.0)

    k_rope = rope(kper, gqk[3:4])
    kpe_ss = _split_dot(kper * kper, ones_lo)
    for h in range(B_HEADS):
        qn = qb[:, 2 * LANES * h: 2 * LANES * h + LANES]
        qr = qb[:, 2 * LANES * h + LANES: 2 * LANES * (h + 1)]
        ss = _split_dot(qn * qn, ones_all) + _split_dot(qr * qr, ones_lo)
        rs = lax.rsqrt(ss * (1.0 / QK_HEAD_DIM) + RMS_EPS) * q_scale
        q_ref[h, :, :LANES] = (qn * gqk[0:1] * rs).astype(BF16)
        q_ref[h, :, LANES:] = (rope(qr, gqk[1:2]) * rs).astype(BF16)
        kn = kvb[:, LANES * h: LANES * (h + 1)]
        ssk = _split_dot(kn * kn, ones_all) + kpe_ss
        rsk = lax.rsqrt(ssk * (1.0 / QK_HEAD_DIM) + RMS_EPS)
        k_ref[h, :, :LANES] = (kn * gqk[2:3] * rsk).astype(BF16)
        k_ref[h, :, LANES:] = (k_rope * rsk).astype(BF16)
    v_ref[...] = kvb[:, B_HEADS * QK_NOPE_DIM:].astype(BF16)


def _mla_prep(proj, cs, wq, wkv, gc, gqk, seq, tm):
    t = proj.shape[0]
    c_width = Q_LORA_RANK + KV_LORA_RANK
    c_block = 3 * A_WIDTH // c_width
    kpe_block = (3 * A_WIDTH + c_width) // LANES
    pos_blocks = seq // tm
    kernel = functools.partial(_mla_prep_kernel, q_scale=QK_HEAD_DIM ** -0.5 * LOG2E)
    return pl.pallas_call(
        kernel,
        grid=(t // tm,),
        in_specs=[pl.BlockSpec((tm, c_width), lambda i: (i, c_block)),
                  pl.BlockSpec((tm, LANES), lambda i: (i, kpe_block)),
                  pl.BlockSpec((tm, LANES), lambda i: (i % pos_blocks, 0)),
                  pl.BlockSpec(wq.shape, lambda i: (0, 0)),
                  pl.BlockSpec(wkv.shape, lambda i: (0, 0)),
                  pl.BlockSpec(gc.shape, lambda i: (0, 0)),
                  pl.BlockSpec(gqk.shape, lambda i: (0, 0))],
        out_specs=[pl.BlockSpec((B_HEADS, tm, 2 * LANES), lambda i: (0, i, 0)),
                   pl.BlockSpec((B_HEADS, tm, 2 * LANES), lambda i: (0, i, 0)),
                   pl.BlockSpec((tm, B_WIDTH), lambda i: (i, 0))],
        out_shape=[jax.ShapeDtypeStruct((B_HEADS, t, 2 * LANES), BF16),
                   jax.ShapeDtypeStruct((B_HEADS, t, 2 * LANES), BF16),
                   jax.ShapeDtypeStruct((t, B_WIDTH), BF16)],
        compiler_params=_params(),
        name="mla_prep",
    )(proj, proj, cs, wq, wkv, gc, gqk)


def _mla_attention_kernel(q_ref, k_ref, v_ref, o_ref, *, tq):
    seq = q_ref.shape[0]
    row = lax.broadcasted_iota(jnp.int32, (tq, tq), 0)
    col = lax.broadcasted_iota(jnp.int32, (tq, tq), 1)
    diag_mask = jnp.where(col <= row, 0.0, MASK_VALUE)
    v_ext = jnp.concatenate([v_ref[...], jnp.ones(v_ref.shape, v_ref.dtype)], axis=1)
    dv = v_ref.shape[1]
    for i in range(seq // tq):
        q = q_ref[i * tq:(i + 1) * tq, :]
        s_diag = _dot_nt(q, k_ref[i * tq:(i + 1) * tq, :]) + diag_mask
        m = jnp.max(s_diag, axis=-1, keepdims=True)
        if i > 0:
            s_past = _dot_nt(q, k_ref[:i * tq, :])
            m = jnp.maximum(m, jnp.max(s_past, axis=-1, keepdims=True))
        o = _dot(jnp.exp2(s_diag - m).astype(BF16), v_ext[i * tq:(i + 1) * tq, :])
        if i > 0:
            o = o + _dot(jnp.exp2(s_past - m).astype(BF16), v_ext[:i * tq, :])
        o_ref[i * tq:(i + 1) * tq, :] = (o[:, :dv] / o[:, dv:]).astype(o_ref.dtype)


def _mla_attention(qb, kb, vb, batch, seq, tq):
    t = vb.shape[0]
    kernel = functools.partial(_mla_attention_kernel, tq=tq)
    return pl.pallas_call(
        kernel,
        grid=(batch, B_HEADS),
        in_specs=[pl.BlockSpec((None, seq, 2 * LANES), lambda b, h: (h, b, 0)),
                  pl.BlockSpec((None, seq, 2 * LANES), lambda b, h: (h, b, 0)),
                  pl.BlockSpec((seq, V_HEAD_DIM), lambda b, h: (b, h))],
        out_specs=pl.BlockSpec((seq, V_HEAD_DIM), lambda b, h: (b, h)),
        out_shape=jax.ShapeDtypeStruct((t, B_WIDTH), BF16),
        compiler_params=_params(2),
        name="mla_attention",
    )(qb, kb, vb)


def _dilated_kernel(q_ref, k_ref, v_ref, bd_ref, g_ref, bias_ref, o_ref, qs, ks, acc_o, acc_l, acc_m,
                    *, unroll):
    seq = q_ref.shape[0]
    n_tiles = seq // BAND_BLOCK
    low = lax.broadcasted_iota(jnp.int32, (BAND_BLOCK, LANES), 1) < A_HEAD_DIM
    bd = bd_ref[...]
    g = g_ref[...]

    def head_norm(z, g_row):
        ss = _split_dot(z * z, bd)
        return z * lax.rsqrt(ss * (1.0 / A_HEAD_DIM) + RMS_EPS) * g_row

    qs[...] = head_norm(q_ref[...], g[0:1]) * (A_HEAD_DIM ** -0.5 * LOG2E)
    ks[...] = head_norm(k_ref[...], g[1:2])
    ones = jnp.ones((2 * BAND_BLOCK, LANES), BF16)

    for p, (window, dil) in enumerate(DILATED_PATTERNS):
        blocks_per_class = n_tiles // dil
        span = BAND_BLOCK * dil

        def rows_at(start, dil=dil):
            if dil == 1:
                return pl.ds(start, BAND_BLOCK)
            return pl.ds(start, BAND_BLOCK, stride=dil)

        def tile(i, carry, p=p, blocks_per_class=blocks_per_class, span=span, rows_at=rows_at):
            r = i // blocks_per_class
            j = i % blocks_per_class
            cur = rows_at(r + j * span)
            prev = rows_at(r + jnp.maximum(j - 1, 0) * span)
            q = qs[cur, :]
            q2 = jnp.concatenate([jnp.where(low, q, 0.0), jnp.where(low, 0.0, q)], axis=0).astype(BF16)
            k_band = jnp.concatenate([ks[prev, :], ks[cur, :]], axis=0).astype(BF16)
            s = _dot_nt(q2, k_band) + bias_ref[p, jnp.where(j == 0, 1, 0)]
            m = jnp.max(s, axis=-1, keepdims=True)
            pr = jnp.exp2(s - m).astype(BF16)
            v_band = jnp.concatenate([v_ref[prev, :], v_ref[cur, :]], axis=0).astype(BF16)
            o = _dot(pr, jnp.concatenate([v_band, ones], axis=1))
            top, bot = o[:BAND_BLOCK], o[BAND_BLOCK:]
            acc_o[p, cur, :] = jnp.where(low, top[:, :LANES], bot[:, :LANES])
            acc_l[p, cur, :] = jnp.where(low, top[:, LANES:], bot[:, LANES:])
            acc_m[p, cur, :] = jnp.where(low, m[:BAND_BLOCK], m[BAND_BLOCK:])
            return carry

        lax.fori_loop(0, n_tiles, tile, 0, unroll=unroll)

    m_all = jnp.maximum(jnp.maximum(acc_m[0], acc_m[1]), acc_m[2])
    num = jnp.zeros((seq, LANES), F32)
    den = jnp.zeros((seq, LANES), F32)
    for p in range(len(DILATED_PATTERNS)):
        w = jnp.exp2(acc_m[p] - m_all)
        num = num + w * acc_o[p]
        den = den + w * acc_l[p]
    o_ref[...] = (num / den).astype(o_ref.dtype)


def _dilated_attention(proj, bd, g, bias, batch, seq):
    t = proj.shape[0]
    pairs = A_WIDTH // LANES
    n_pat = len(DILATED_PATTERNS)
    return pl.pallas_call(
        functools.partial(_dilated_kernel, unroll=8),
        grid=(batch, pairs),
        in_specs=[pl.BlockSpec((seq, LANES), lambda b, c: (b, c)),
                  pl.BlockSpec((seq, LANES), lambda b, c: (b, pairs + c)),
                  pl.BlockSpec((seq, LANES), lambda b, c: (b, 2 * pairs + c)),
                  pl.BlockSpec((LANES, LANES), lambda b, c: (0, 0)),
                  pl.BlockSpec((None, 2, LANES), lambda b, c: (c, 0, 0)),
                  pl.BlockSpec((None, n_pat, 2, 2 * BAND_BLOCK, 2 * BAND_BLOCK),
                               lambda b, c: (c, 0, 0, 0, 0))],
        out_specs=pl.BlockSpec((seq, LANES), lambda b, c: (b, c)),
        out_shape=jax.ShapeDtypeStruct((t, A_WIDTH), BF16),
        scratch_shapes=[pltpu.VMEM((seq, LANES), F32)] * 2
                       + [pltpu.VMEM((n_pat, seq, LANES), F32)] * 3,
        compiler_params=_params(2),
        name="dilated_attention",
    )(proj, proj, proj, bd, g, bias)


def _t5_bucket(dist):
    max_exact = NUM_BUCKETS // 2
    df = jnp.maximum(dist, 1).astype(F32)
    log_bucket = max_exact + (jnp.log(df / max_exact) / math.log(MAX_DISTANCE / max_exact)
                              * (NUM_BUCKETS - max_exact)).astype(jnp.int32)
    log_bucket = jnp.minimum(log_bucket, NUM_BUCKETS - 1)
    return jnp.where(dist < max_exact, dist, log_bucket)


def _toeplitz(u):
    n = BAND_BLOCK
    lead = u.shape[:-1]
    w = jnp.concatenate([u[..., ::-1], jnp.zeros(lead + (1,), u.dtype)], axis=-1)
    r = jnp.broadcast_to(w[..., None, :], lead + (n, 2 * n)).reshape(lead + (2 * n * n,))
    return r[..., :n * (2 * n - 1)].reshape(lead + (n, 2 * n - 1))[..., n - 1:]


def _band_bias(rel_bias):
    n = BAND_BLOCK
    tables = []
    for window, dil in DILATED_PATTERNS:
        steps = window // dil
        back = jnp.arange(2 * n)
        by_back = rel_bias[_t5_bucket(back * dil)].astype(F32).T * LOG2E
        by_back = jnp.where(back <= steps, by_back, MASK_VALUE)
        masked = jnp.full((A_HEADS, n - 1), MASK_VALUE, F32)
        cur = _toeplitz(jnp.concatenate([masked, by_back[:, :n]], axis=1))
        prev = _toeplitz(by_back[:, 1:])
        normal = jnp.concatenate([prev, cur], axis=-1)
        first = jnp.concatenate([jnp.full_like(prev, MASK_VALUE), cur], axis=-1)
        tables.append(jnp.stack([normal, first], axis=1))
    tab = jnp.stack(tables, axis=1)
    tab = tab.reshape(A_HEADS // 2, 2, len(DILATED_PATTERNS), 2, n, 2 * n)
    return tab.transpose(0, 2, 3, 1, 4, 5).reshape(A_HEADS // 2, len(DILATED_PATTERNS), 2, 2 * n, 2 * n)


def _pack_pairs(v):
    half = v.shape[1] // 2
    bits = pltpu.bitcast(v, jnp.uint32)
    return (bits[:, :half] >> 16) | (bits[:, half:] & jnp.uint32(0xFFFF0000))


def _unpack_pairs(p):
    return (pltpu.bitcast(p << 16, F32), pltpu.bitcast(p & jnp.uint32(0xFFFF0000), F32))


def _out_proj_kernel(x_ref, a_ref, b_ref, w_ref, g_ref, rw_ref, rb_ref,
                     x1_ref, hp_ref, route_ref, cnt_ref, carry):
    i = pl.program_id(0)
    tm = x_ref.shape[0]

    @pl.when(i == 0)
    def _():
        carry[...] = jnp.zeros_like(carry)

    w = w_ref[...]
    x1 = x_ref[...] + _dot(a_ref[...], w[:A_WIDTH]) + _dot(b_ref[...], w[A_WIDTH:])
    x1_ref[...] = x1
    ms = jnp.mean(x1 * x1, axis=-1, keepdims=True)
    h = x1 * lax.rsqrt(ms + RMS_EPS) * g_ref[...]
    hi = h.astype(BF16)
    hi_f = hi.astype(F32)
    hp_ref[...] = _pack_pairs(hi_f)
    lo = (h - hi_f).astype(BF16)
    rw_hi = rw_ref[0]
    logits = _dot(hi, rw_hi) + _dot(lo, rw_hi) + _dot(hi, rw_ref[1]) + rb_ref[...]

    lane = lax.broadcasted_iota(jnp.int32, (tm, LANES), 1).astype(F32)
    remaining = logits
    vals, hots = [], []
    for _ in range(TOP_K):
        m = jnp.max(remaining, axis=-1, keepdims=True)
        first = jnp.min(jnp.where(remaining == m, lane, float(LANES)), axis=-1, keepdims=True)
        hot = lane == first
        remaining = jnp.where(hot, -jnp.inf, remaining)
        vals.append(m)
        hots.append(hot)
    exps = [jnp.exp(v - vals[0]) for v in vals]
    den = exps[0] + exps[1] + exps[2] + exps[3]

    chosen = jnp.zeros((tm, LANES), F32)
    for hot in hots:
        chosen = chosen + jnp.where(hot, 1.0, 0.0)
    r = lax.broadcasted_iota(jnp.int32, (tm, tm), 0)
    c = lax.broadcasted_iota(jnp.int32, (tm, tm), 1)
    earlier = jnp.where(r > c, 1.0, 0.0).astype(BF16)
    before = carry[...] + _dot(earlier, chosen.astype(BF16))
    carry[...] = carry[...] + jnp.sum(chosen, axis=0, keepdims=True)
    cnt_ref[...] = jnp.broadcast_to(carry[...], cnt_ref.shape)

    route = jnp.zeros((tm, LANES), F32)
    for k in range(TOP_K):
        first = jnp.sum(jnp.where(hots[k], lane, 0.0), axis=-1, keepdims=True)
        rank = jnp.sum(jnp.where(hots[k], before, 0.0), axis=-1, keepdims=True)
        route = route + jnp.where(lane == float(k), first, 0.0)
        route = route + jnp.where(lane == float(TOP_K + k), rank, 0.0)
        route = route + jnp.where(lane == float(2 * TOP_K + k), exps[k] / den, 0.0)
    route_ref[...] = route


def _out_proj(x2, out_a, out_b, w, g, rw, rb, tm):
    t, d = x2.shape
    return pl.pallas_call(
        _out_proj_kernel,
        grid=(t // tm,),
        in_specs=[pl.BlockSpec((tm, d), lambda i: (i, 0)),
                  pl.BlockSpec((tm, A_WIDTH), lambda i: (i, 0)),
                  pl.BlockSpec((tm, B_WIDTH), lambda i: (i, 0)),
                  pl.BlockSpec(w.shape, lambda i: (0, 0)),
                  pl.BlockSpec((1, d), lambda i: (0, 0)),
                  pl.BlockSpec(rw.shape, lambda i: (0, 0, 0)),
                  pl.BlockSpec((1, LANES), lambda i: (0, 0))],
        out_specs=[pl.BlockSpec((tm, d), lambda i: (i, 0)),
                   pl.BlockSpec((tm, d // 2), lambda i: (i, 0)),
                   pl.BlockSpec((tm, LANES), lambda i: (i, 0)),
                   pl.BlockSpec((8, LANES), lambda i: (0, 0))],
        out_shape=[jax.ShapeDtypeStruct((t, d), F32),
                   jax.ShapeDtypeStruct((t, d // 2), jnp.uint32),
                   jax.ShapeDtypeStruct((t, LANES), F32),
                   jax.ShapeDtypeStruct((8, LANES), F32)],
        scratch_shapes=[pltpu.VMEM((1, LANES), F32)],
        compiler_params=_params(),
        name="out_proj",
    )(x2, out_a, out_b, w, g, rw, rb)


def _dispatch_kernel(pad_ref, end_ref, nb_ref, dest_hbm, h_ref, xs_hbm, dest_s0, dest_s1, zeros,
                     sem_i, sem_z, sem_o, *, bm):
    i = pl.program_id(0)
    steps = pl.num_programs(0)
    tm = h_ref.shape[0]
    n_blocks = xs_hbm.shape[0] // bm
    dest_s = (dest_s0, dest_s1)

    def idx_copy(step, slot):
        return pltpu.make_async_copy(dest_hbm.at[step], dest_s[slot], sem_i.at[slot])

    @pl.when(i == 0)
    def _():
        idx_copy(0, 0).start()
        zeros[...] = jnp.zeros_like(zeros)

        def zero_block(start):
            return pltpu.make_async_copy(zeros, xs_hbm.at[pl.ds(pl.multiple_of(start, bm), bm), :], sem_z)

        for e in range(N_EXPERTS):
            @pl.when(pad_ref[e] > 0)
            def _():
                zero_block(end_ref[e] - bm).start()

        def start_unused(b, carry):
            zero_block(b * bm).start()
            return carry

        def wait_unused(b, carry):
            zero_block(b * bm).wait()
            return carry

        lax.fori_loop(nb_ref[0], n_blocks, start_unused, 0)
        for e in range(N_EXPERTS):
            @pl.when(pad_ref[e] > 0)
            def _():
                zero_block(end_ref[e] - bm).wait()
        lax.fori_loop(nb_ref[0], n_blocks, wait_unused, 0)

    for slot in range(2):
        @pl.when(i % 2 == slot)
        def _(slot=slot):
            @pl.when(i + 1 < steps)
            def _():
                idx_copy(i + 1, 1 - slot).start()

            idx_copy(i, slot).wait()

            def body(t, carry):
                for k in range(TOP_K):
                    pltpu.make_async_copy(h_ref.at[pl.ds(t, 1), :],
                                          xs_hbm.at[pl.ds(dest_s[slot][t * TOP_K + k], 1), :],
                                          sem_o).start(priority=k % 2)
                return carry

            lax.fori_loop(0, tm, body, 0, unroll=8)
    for _ in range(TOP_K):
        pltpu.make_async_copy(h_ref, xs_hbm.at[pl.ds(0, tm), :], sem_o).wait()


def _dispatch(padded, pad_end, n_used, dest, hp, n_slots, tm, bm):
    t, half = hp.shape
    grid_spec = pltpu.PrefetchScalarGridSpec(
        num_scalar_prefetch=3,
        grid=(t // tm,),
        in_specs=[pl.BlockSpec(memory_space=pl.ANY),
                  pl.BlockSpec((tm, half), lambda i, p, e, n: (i, 0))],
        out_specs=pl.BlockSpec(memory_space=pl.ANY),
        scratch_shapes=[pltpu.SMEM((tm * TOP_K,), jnp.int32),
                        pltpu.SMEM((tm * TOP_K,), jnp.int32),
                        pltpu.VMEM((bm, half), jnp.uint32),
                        pltpu.SemaphoreType.DMA((2,)),
                        pltpu.SemaphoreType.DMA(()),
                        pltpu.SemaphoreType.DMA(())],
    )
    return pl.pallas_call(
        functools.partial(_dispatch_kernel, bm=bm),
        grid_spec=grid_spec,
        out_shape=jax.ShapeDtypeStruct((n_slots, half), jnp.uint32),
        compiler_params=_params(),
        name="dispatch",
    )(padded, pad_end, n_used, dest, hp)


EXPERT_SUB_ROWS = 256


def _expert_ffn_kernel(be_ref, nb_ref, x_ref, wgu_ref, bgu_ref, wd_ref, bd_ref, y_ref, wgu_s, wd_s):
    i = pl.program_id(0)
    d_ff = wd_ref.shape[0]
    half = x_ref.shape[1]

    @pl.when(i >= nb_ref[0])
    def _():
        y_ref[...] = jnp.zeros_like(y_ref)

    @pl.when(i < nb_ref[0])
    def _():
        changed = jnp.logical_or(i == 0, be_ref[i] != be_ref[jnp.maximum(i - 1, 0)])

        @pl.when(changed)
        def _():
            wgu_s[...] = wgu_ref[...].astype(BF16)
            wd_s[...] = wd_ref[...].astype(BF16)

        for r0 in range(0, x_ref.shape[0], EXPERT_SUB_ROWS):
            rows = slice(r0, r0 + EXPERT_SUB_ROWS)
            x_lo, x_hi = _unpack_pairs(x_ref[rows, :])
            gu = (_dot(x_lo.astype(BF16), wgu_s[:half, :]) + _dot(x_hi.astype(BF16), wgu_s[half:, :])
                  + bgu_ref[...])
            gate = jnp.minimum(gu[:, :d_ff], SWIGLU_LIMIT)
            up = jnp.clip(gu[:, d_ff:], -SWIGLU_LIMIT, SWIGLU_LIMIT)
            glu = gate * jax.nn.sigmoid(SWIGLU_ALPHA * gate)
            act = ((up + 1.0) * glu).astype(BF16)
            y = _dot(act, wd_s[...]) + bd_ref[...]
            y_ref[rows, :] = _pack_pairs(y.astype(BF16).astype(F32))


def _expert_ffn(block_expert, n_used, xs, w_gate_up, b_gate_up, w_down, b_down, bm):
    n_slots, half = xs.shape
    n_e, d, two_ff = w_gate_up.shape
    d_ff = two_ff // 2
    n_blocks = n_slots // bm

    def used(i, nb):
        return jnp.minimum(i, jnp.maximum(nb[0] - 1, 0))

    grid_spec = pltpu.PrefetchScalarGridSpec(
        num_scalar_prefetch=2,
        grid=(n_blocks,),
        in_specs=[pl.BlockSpec((bm, half), lambda i, be, nb: (used(i, nb), 0)),
                  pl.BlockSpec((None, d, two_ff), lambda i, be, nb: (be[i], 0, 0)),
                  pl.BlockSpec((None, 1, two_ff), lambda i, be, nb: (be[i], 0, 0)),
                  pl.BlockSpec((None, d_ff, d), lambda i, be, nb: (be[i], 0, 0)),
                  pl.BlockSpec((None, 1, d), lambda i, be, nb: (be[i], 0, 0))],
        out_specs=pl.BlockSpec((bm, half), lambda i, be, nb: (i, 0)),
        scratch_shapes=[pltpu.VMEM((d, two_ff), BF16), pltpu.VMEM((d_ff, d), BF16)],
    )
    return pl.pallas_call(
        _expert_ffn_kernel,
        grid_spec=grid_spec,
        out_shape=jax.ShapeDtypeStruct((n_slots, half), jnp.uint32),
        compiler_params=_params(),
        name="expert_ffn",
    )(block_expert, n_used, xs, w_gate_up, b_gate_up.reshape(n_e, 1, two_ff),
      w_down, b_down.reshape(n_e, 1, d))


def _combine_kernel(dest_hbm, route_ref, x1_ref, ys_hbm, o_ref, dest_s0, dest_s1, rows, sem_i, sem_g):
    j = pl.program_id(0)
    blocks = pl.num_programs(0) - 1
    tm = x1_ref.shape[0]
    half = x1_ref.shape[1] // 2

    dest_s = (dest_s0, dest_s1)

    def idx_copy(block, s):
        return pltpu.make_async_copy(dest_hbm.at[block], dest_s[s], sem_i.at[s])

    @pl.when(j == 0)
    def _():
        idx_copy(0, 0).start()

    @pl.when(j < blocks)
    def _():
        for s in range(2):
            @pl.when(j % 2 == s)
            def _(s=s):
                idx_copy(j, s).wait()

                @pl.when(j + 1 < blocks)
                def _():
                    idx_copy(j + 1, 1 - s).start()

                def body(t, carry):
                    for k in range(TOP_K):
                        pltpu.make_async_copy(ys_hbm.at[pl.ds(dest_s[s][t * TOP_K + k], 1), :],
                                              rows.at[s, k, pl.ds(t, 1), :],
                                              sem_g.at[s]).start(priority=k % 2)
                    return carry

                lax.fori_loop(0, tm, body, 0, unroll=8)

    @pl.when(j >= 1)
    def _():
        slot = (j - 1) % 2
        for k in range(TOP_K):
            pltpu.make_async_copy(ys_hbm.at[pl.ds(0, tm), :], rows.at[slot, k], sem_g.at[slot]).wait()

        route = route_ref[...]
        y_lo = y_hi = None
        for k in range(TOP_K):
            gate = route[:, 2 * TOP_K + k: 2 * TOP_K + k + 1]
            lo, hi = _unpack_pairs(rows[slot, k])
            y_lo = gate * lo if k == 0 else y_lo + gate * lo
            y_hi = gate * hi if k == 0 else y_hi + gate * hi
        o_ref[:, :half] = x1_ref[:, :half] + y_lo
        o_ref[:, half:] = x1_ref[:, half:] + y_hi


def _combine(dest, route, x1, ys, tm):
    t, d = x1.shape
    half = d // 2

    def summed(j):
        return (jnp.maximum(j - 1, 0), 0)

    return pl.pallas_call(
        _combine_kernel,
        grid=(t // tm + 1,),
        in_specs=[pl.BlockSpec(memory_space=pl.ANY),
                  pl.BlockSpec((tm, LANES), summed),
                  pl.BlockSpec((tm, d), summed),
                  pl.BlockSpec(memory_space=pl.ANY)],
        out_specs=pl.BlockSpec((tm, d), summed),
        out_shape=jax.ShapeDtypeStruct((t, d), F32),
        scratch_shapes=[pltpu.SMEM((tm * TOP_K,), jnp.int32),
                        pltpu.SMEM((tm * TOP_K,), jnp.int32),
                        pltpu.VMEM((2, TOP_K, tm, half), jnp.uint32),
                        pltpu.SemaphoreType.DMA((2,)),
                        pltpu.SemaphoreType.DMA((2,))],
        compiler_params=_params(),
        name="combine",
    )(dest, route, x1, ys)


def _slot_layout(route, counts, tm, bm):
    t = route.shape[0]
    counts = counts.astype(jnp.int32)
    padded = (counts + bm - 1) // bm * bm
    pad_end = jnp.cumsum(padded).astype(jnp.int32)
    pad_start = pad_end - padded
    idx = route[:, :TOP_K].astype(jnp.int32)
    rank = route[:, TOP_K:2 * TOP_K].astype(jnp.int32)
    dest = (jnp.take(pad_start, idx) + rank).reshape(t // tm, tm * TOP_K)
    n_blocks = -(-t * TOP_K // bm) + N_EXPERTS
    block_start = jnp.arange(n_blocks, dtype=jnp.int32) * bm
    block_expert = jnp.minimum(jnp.sum(pad_end[None, :] <= block_start[:, None], axis=1),
                               N_EXPERTS - 1).astype(jnp.int32)
    n_used = (pad_end[-1:] // bm).astype(jnp.int32)
    return padded, pad_end, dest, block_expert, n_used, n_blocks * bm


def _rot_cols(w):
    half = w.shape[-1] // 2
    return jnp.concatenate([-w[..., half:], w[..., :half]], axis=-1)


def _swap_halves(g):
    half = g.shape[-1] // 2
    return jnp.concatenate([g[..., half:], g[..., :half]], axis=-1)


def kernel(x, attn_norm_g, w_in, a_q_norm_g, a_k_norm_g, rel_bias, q_a_norm_g, w_q_b, kv_a_norm_g,
           w_kv_b, b_q_norm_g, b_k_norm_g, w_out, ffn_norm_g, router_w, router_b, w_gate_up,
           b_gate_up, w_down, b_down):
    batch, seq, d = x.shape
    t = batch * seq
    depth = w_in.shape[0]
    tm = 512
    expert_bm = 512

    pos = jnp.arange(seq, dtype=F32)
    inv_freq = ROPE_THETA ** (-jnp.arange(0, QK_ROPE_DIM, 2, dtype=F32) / QK_ROPE_DIM)
    ang = pos[:, None] * inv_freq[None, :]
    cos, sin = jnp.cos(ang), jnp.sin(ang)
    cs = jnp.concatenate([cos, cos, sin, sin], axis=-1)

    row = jnp.arange(LANES)[:, None] // A_HEAD_DIM
    col = jnp.arange(LANES)[None, :] // A_HEAD_DIM
    head_block_ones = (row == col).astype(BF16)
    band_bias = _band_bias(rel_bias)

    x2 = x.reshape(t, d)
    for layer in range(depth):
        kpe_off = 3 * A_WIDTH + Q_LORA_RANK + KV_LORA_RANK
        w_kpe = w_in[layer][:, kpe_off:]
        w_in_r = jnp.concatenate([w_in[layer][:, :kpe_off], w_kpe, _rot_cols(w_kpe)], axis=1).astype(BF16)

        wq = w_q_b[layer].reshape(Q_LORA_RANK, B_HEADS, QK_HEAD_DIM)
        wq_rope = wq[..., QK_NOPE_DIM:]
        wq_r = jnp.concatenate([wq, _rot_cols(wq_rope)], axis=-1).reshape(Q_LORA_RANK, -1).astype(BF16)
        wkv = w_kv_b[layer].reshape(KV_LORA_RANK, B_HEADS, QK_NOPE_DIM + V_HEAD_DIM)
        wkv_r = jnp.concatenate([wkv[..., :QK_NOPE_DIM].reshape(KV_LORA_RANK, -1),
                                 wkv[..., QK_NOPE_DIM:].reshape(KV_LORA_RANK, -1)], axis=1).astype(BF16)
        gc = jnp.concatenate([q_a_norm_g[layer], kv_a_norm_g[layer]])[None, :]
        gq, gk = b_q_norm_g[layer], b_k_norm_g[layer]

        def rope_gain(gr):
            return jnp.concatenate([gr, _swap_halves(gr)])

        gqk = jnp.stack([gq[:QK_NOPE_DIM], rope_gain(gq[QK_NOPE_DIM:]),
                         gk[:QK_NOPE_DIM], rope_gain(gk[QK_NOPE_DIM:])])
        g_a = jnp.stack([jnp.tile(a_q_norm_g[layer], 2), jnp.tile(a_k_norm_g[layer], 2)])
        g_a = jnp.broadcast_to(g_a[None], (A_WIDTH // LANES, 2, LANES))

        proj = _in_proj(x2, attn_norm_g[layer][None, :], w_in_r, tm)
        qb, kb, vb = _mla_prep(proj, cs, wq_r, wkv_r, gc, gqk, seq, tm)
        out_b = _mla_attention(qb, kb, vb, batch, seq, 256)
        out_a = _dilated_attention(proj, head_block_ones, g_a, band_bias, batch, seq)

        rw = jnp.pad(router_w[layer], ((0, 0), (0, LANES - N_EXPERTS)))
        rw_hi = rw.astype(BF16)
        rw_lo = (rw - rw_hi.astype(F32)).astype(BF16)
        rb = jnp.pad(router_b[layer], (0, LANES - N_EXPERTS), constant_values=-jnp.inf)[None, :]
        x1, hp, route, counts = _out_proj(x2, out_a, out_b, w_out[layer].astype(BF16),
                                          ffn_norm_g[layer][None, :], jnp.stack([rw_hi, rw_lo]), rb, tm)

        padded, pad_end, dest, block_expert, n_used, n_slots = _slot_layout(
            route, counts[0, :N_EXPERTS], tm, expert_bm)
        xs = _dispatch(padded, pad_end, n_used, dest, hp, n_slots, tm, expert_bm)
        ys = _expert_ffn(block_expert, n_used, xs, w_gate_up[layer], b_gate_up[layer],
                         w_down[layer], b_down[layer], expert_bm)
        x2 = _combine(dest, route, x1, ys, tm)
    return x2.reshape(batch, seq, d)
```

```python
import functools
import math

import jax
import jax.numpy as jnp
from jax import lax
from jax.experimental import pallas as pl
from jax.experimental.pallas import tpu as pltpu

A_HEADS = 8
A_HEAD_DIM = 64
A_WIDTH = A_HEADS * A_HEAD_DIM
DILATED_PATTERNS = ((128, 1), (512, 4), (2048, 16))
BAND_BLOCK = 128

B_HEADS = 4
QK_NOPE_DIM = 128
QK_ROPE_DIM = 64
QK_HEAD_DIM = QK_NOPE_DIM + QK_ROPE_DIM
V_HEAD_DIM = 128
Q_LORA_RANK = 256
KV_LORA_RANK = 256
B_WIDTH = B_HEADS * V_HEAD_DIM
ROPE_THETA = 10000.0

NUM_BUCKETS = 32
MAX_DISTANCE = 2048

N_EXPERTS = 32
TOP_K = 4
SWIGLU_LIMIT = 7.0
SWIGLU_ALPHA = 1.702
RMS_EPS = 1e-6

LANES = 128
MASK_VALUE = -1e30
LOG2E = math.log2(math.e)
PROJ_WIDTH = 3 * A_WIDTH + Q_LORA_RANK + KV_LORA_RANK + 2 * QK_ROPE_DIM
VMEM_LIMIT = 56 * 1024 * 1024

F32 = jnp.float32
BF16 = jnp.bfloat16


def _dot(a, b):
    return jnp.dot(a, b, preferred_element_type=F32)


def _dot_nt(a, b):
    return lax.dot_general(a, b, (((1,), (1,)), ((), ())), preferred_element_type=F32)


def _split_dot(x, m):
    hi = x.astype(BF16)
    lo = (x - hi.astype(F32)).astype(BF16)
    return _dot(hi, m) + _dot(lo, m)


def _params(n_parallel=1):
    return pltpu.CompilerParams(
        dimension_semantics=("arbitrary",) * n_parallel, vmem_limit_bytes=VMEM_LIMIT)


def _in_proj_kernel(x_ref, g_ref, w_ref, o_ref):
    x = x_ref[...]
    ms = jnp.mean(x * x, axis=-1, keepdims=True)
    h = (x * lax.rsqrt(ms + RMS_EPS) * g_ref[...]).astype(BF16)
    o_ref[...] = _dot(h, w_ref[...])


def _in_proj(x2, g, w, tm):
    t, d = x2.shape
    n = w.shape[1]
    return pl.pallas_call(
        _in_proj_kernel,
        grid=(t // tm,),
        in_specs=[pl.BlockSpec((tm, d), lambda i: (i, 0)),
                  pl.BlockSpec((1, d), lambda i: (0, 0)),
                  pl.BlockSpec((d, n), lambda i: (0, 0))],
        out_specs=pl.BlockSpec((tm, n), lambda i: (i, 0)),
        out_shape=jax.ShapeDtypeStruct((t, n), F32),
        compiler_params=_params(),
        name="in_proj",
    )(x2, g, w)


def _mla_prep_kernel(c_ref, kpe_ref, cs_ref, wq_ref, wkv_ref, gc_ref, gqk_ref, q_ref, k_ref, v_ref,
                     *, q_scale):
    c = c_ref[...]
    gc = gc_ref[...]

    def lora_norm(z, g):
        ms = jnp.mean(z * z, axis=-1, keepdims=True)
        return (z * lax.rsqrt(ms + RMS_EPS) * g).astype(BF16)

    cq = lora_norm(c[:, :Q_LORA_RANK], gc[:, :Q_LORA_RANK])
    ckv = lora_norm(c[:, Q_LORA_RANK:], gc[:, Q_LORA_RANK:])
    qb = _dot(cq, wq_ref[...])
    kvb = _dot(ckv, wkv_ref[...])
    kper = kpe_ref[...]
    cs = cs_ref[...]
    gqk = gqk_ref[...]
    tm = c.shape[0]

    row = lax.broadcasted_iota(jnp.int32, (LANES, LANES), 0)
    ones_all = jnp.ones((LANES, LANES), BF16)
    ones_lo = jnp.where(row < QK_ROPE_DIM, 1.0, 0.0).astype(BF16)
    lane = lax.broadcasted_iota(jnp.int32, (tm, LANES), 1)

    def rope(z, g_row):
        t = z * (g_row * cs)
        return jnp.where(lane < QK_ROPE_DIM, t + pltpu.roll(t, QK_ROPE_DIM, 1), 0.0)

    k_rope = rope(kper, gqk[3:4])
    kpe_ss = _split_dot(kper * kper, ones_lo)
    for h in range(B_HEADS):
        qn = qb[:, 2 * LANES * h: 2 * LANES * h + LANES]
        qr = qb[:, 2 * LANES * h + LANES: 2 * LANES * (h + 1)]
        ss = _split_dot(qn * qn, ones_all) + _split_dot(qr * qr, ones_lo)
        rs = lax.rsqrt(ss * (1.0 / QK_HEAD_DIM) + RMS_EPS) * q_scale
        q_ref[h, :, :LANES] = (qn * gqk[0:1] * rs).astype(BF16)
        q_ref[h, :, LANES:] = (rope(qr, gqk[1:2]) * rs).astype(BF16)
        kn = kvb[:, LANES * h: LANES * (h + 1)]
        ssk = _split_dot(kn * kn, ones_all) + kpe_ss
        rsk = lax.rsqrt(ssk * (1.0 / QK_HEAD_DIM) + RMS_EPS)
        k_ref[h, :, :LANES] = (kn * gqk[2:3] * rsk).astype(BF16)
        k_ref[h, :, LANES:] = (k_rope * rsk).astype(BF16)
    v_ref[...] = kvb[:, B_HEADS * QK_NOPE_DIM:].astype(BF16)


def _mla_prep(proj, cs, wq, wkv, gc, gqk, seq, tm):
    t = proj.shape[0]
    c_width = Q_LORA_RANK + KV_LORA_RANK
    c_block = 3 * A_WIDTH // c_width
    kpe_block = (3 * A_WIDTH + c_width) // LANES
    pos_blocks = seq // tm
    kernel = functools.partial(_mla_prep_kernel, q_scale=QK_HEAD_DIM ** -0.5 * LOG2E)
    return pl.pallas_call(
        kernel,
        grid=(t // tm,),
        in_specs=[pl.BlockSpec((tm, c_width), lambda i: (i, c_block)),
                  pl.BlockSpec((tm, LANES), lambda i: (i, kpe_block)),
                  pl.BlockSpec((tm, LANES), lambda i: (i % pos_blocks, 0)),
                  pl.BlockSpec(wq.shape, lambda i: (0, 0)),
                  pl.BlockSpec(wkv.shape, lambda i: (0, 0)),
                  pl.BlockSpec(gc.shape, lambda i: (0, 0)),
                  pl.BlockSpec(gqk.shape, lambda i: (0, 0))],
        out_specs=[pl.BlockSpec((B_HEADS, tm, 2 * LANES), lambda i: (0, i, 0)),
                   pl.BlockSpec((B_HEADS, tm, 2 * LANES), lambda i: (0, i, 0)),
                   pl.BlockSpec((tm, B_WIDTH), lambda i: (i, 0))],
        out_shape=[jax.ShapeDtypeStruct((B_HEADS, t, 2 * LANES), BF16),
                   jax.ShapeDtypeStruct((B_HEADS, t, 2 * LANES), BF16),
                   jax.ShapeDtypeStruct((t, B_WIDTH), BF16)],
        compiler_params=_params(),
        name="mla_prep",
    )(proj, proj, cs, wq, wkv, gc, gqk)


def _mla_attention_kernel(q_ref, k_ref, v_ref, o_ref, *, tq):
    seq = q_ref.shape[0]
    row = lax.broadcasted_iota(jnp.int32, (tq, tq), 0)
    col = lax.broadcasted_iota(jnp.int32, (tq, tq), 1)
    diag_mask = jnp.where(col <= row, 0.0, MASK_VALUE)
    v_ext = jnp.concatenate([v_ref[...], jnp.ones(v_ref.shape, v_ref.dtype)], axis=1)
    dv = v_ref.shape[1]
    for i in range(seq // tq):
        q = q_ref[i * tq:(i + 1) * tq, :]
        s_diag = _dot_nt(q, k_ref[i * tq:(i + 1) * tq, :]) + diag_mask
        m = jnp.max(s_diag, axis=-1, keepdims=True)
        if i > 0:
            s_past = _dot_nt(q, k_ref[:i * tq, :])
            m = jnp.maximum(m, jnp.max(s_past, axis=-1, keepdims=True))
        o = _dot(jnp.exp2(s_diag - m).astype(BF16), v_ext[i * tq:(i + 1) * tq, :])
        if i > 0:
            o = o + _dot(jnp.exp2(s_past - m).astype(BF16), v_ext[:i * tq, :])
        o_ref[i * tq:(i + 1) * tq, :] = (o[:, :dv] / o[:, dv:]).astype(o_ref.dtype)


def _mla_attention(qb, kb, vb, batch, seq, tq):
    t = vb.shape[0]
    kernel = functools.partial(_mla_attention_kernel, tq=tq)
    return pl.pallas_call(
        kernel,
        grid=(batch, B_HEADS),
        in_specs=[pl.BlockSpec((None, seq, 2 * LANES), lambda b, h: (h, b, 0)),
                  pl.BlockSpec((None, seq, 2 * LANES), lambda b, h: (h, b, 0)),
                  pl.BlockSpec((seq, V_HEAD_DIM), lambda b, h: (b, h))],
        out_specs=pl.BlockSpec((seq, V_HEAD_DIM), lambda b, h: (b, h)),
        out_shape=jax.ShapeDtypeStruct((t, B_WIDTH), BF16),
        compiler_params=_params(2),
        name="mla_attention",
    )(qb, kb, vb)


def _dilated_kernel(q_ref, k_ref, v_ref, bd_ref, g_ref, bias_ref, o_ref, qs, ks, acc_o, acc_l, acc_m,
                    *, unroll):
    seq = q_ref.shape[0]
    n_tiles = seq // BAND_BLOCK
    low = lax.broadcasted_iota(jnp.int32, (BAND_BLOCK, LANES), 1) < A_HEAD_DIM
    bd = bd_ref[...]
    g = g_ref[...]

    def head_norm(z, g_row):
        ss = _split_dot(z * z, bd)
        return z * lax.rsqrt(ss * (1.0 / A_HEAD_DIM) + RMS_EPS) * g_row

    qs[...] = head_norm(q_ref[...], g[0:1]) * (A_HEAD_DIM ** -0.5 * LOG2E)
    ks[...] = head_norm(k_ref[...], g[1:2])
    ones = jnp.ones((2 * BAND_BLOCK, LANES), BF16)

    for p, (window, dil) in enumerate(DILATED_PATTERNS):
        blocks_per_class = n_tiles // dil
        span = BAND_BLOCK * dil

        def rows_at(start, dil=dil):
            if dil == 1:
                return pl.ds(start, BAND_BLOCK)
            return pl.ds(start, BAND_BLOCK, stride=dil)

        def tile(i, carry, p=p, blocks_per_class=blocks_per_class, span=span, rows_at=rows_at):
            r = i // blocks_per_class
            j = i % blocks_per_class
            cur = rows_at(r + j * span)
            prev = rows_at(r + jnp.maximum(j - 1, 0) * span)
            q = qs[cur, :]
            q2 = jnp.concatenate([jnp.where(low, q, 0.0), jnp.where(low, 0.0, q)], axis=0).astype(BF16)
            k_band = jnp.concatenate([ks[prev, :], ks[cur, :]], axis=0).astype(BF16)
            s = _dot_nt(q2, k_band) + bias_ref[p, jnp.where(j == 0, 1, 0)]
            m = jnp.max(s, axis=-1, keepdims=True)
            pr = jnp.exp2(s - m).astype(BF16)
            v_band = jnp.concatenate([v_ref[prev, :], v_ref[cur, :]], axis=0).astype(BF16)
            o = _dot(pr, jnp.concatenate([v_band, ones], axis=1))
            top, bot = o[:BAND_BLOCK], o[BAND_BLOCK:]
            acc_o[p, cur, :] = jnp.where(low, top[:, :LANES], bot[:, :LANES])
            acc_l[p, cur, :] = jnp.where(low, top[:, LANES:], bot[:, LANES:])
            acc_m[p, cur, :] = jnp.where(low, m[:BAND_BLOCK], m[BAND_BLOCK:])
            return carry

        lax.fori_loop(0, n_tiles, tile, 0, unroll=unroll)

    m_all = jnp.maximum(jnp.maximum(acc_m[0], acc_m[1]), acc_m[2])
    num = jnp.zeros((seq, LANES), F32)
    den = jnp.zeros((seq, LANES), F32)
    for p in range(len(DILATED_PATTERNS)):
        w = jnp.exp2(acc_m[p] - m_all)
        num = num + w * acc_o[p]
        den = den + w * acc_l[p]
    o_ref[...] = (num / den).astype(o_ref.dtype)


def _dilated_attention(proj, bd, g, bias, batch, seq):
    t = proj.shape[0]
    pairs = A_WIDTH // LANES
    n_pat = len(DILATED_PATTERNS)
    return pl.pallas_call(
        functools.partial(_dilated_kernel, unroll=8),
        grid=(batch, pairs),
        in_specs=[pl.BlockSpec((seq, LANES), lambda b, c: (b, c)),
                  pl.BlockSpec((seq, LANES), lambda b, c: (b, pairs + c)),
                  pl.BlockSpec((seq, LANES), lambda b, c: (b, 2 * pairs + c)),
                  pl.BlockSpec((LANES, LANES), lambda b, c: (0, 0)),
                  pl.BlockSpec((None, 2, LANES), lambda b, c: (c, 0, 0)),
                  pl.BlockSpec((None, n_pat, 2, 2 * BAND_BLOCK, 2 * BAND_BLOCK),
                               lambda b, c: (c, 0, 0, 0, 0))],
        out_specs=pl.BlockSpec((seq, LANES), lambda b, c: (b, c)),
        out_shape=jax.ShapeDtypeStruct((t, A_WIDTH), BF16),
        scratch_shapes=[pltpu.VMEM((seq, LANES), F32)] * 2
                       + [pltpu.VMEM((n_pat, seq, LANES), F32)] * 3,
        compiler_params=_params(2),
        name="dilated_attention",
    )(proj, proj, proj, bd, g, bias)


def _t5_bucket(dist):
    max_exact = NUM_BUCKETS // 2
    df = jnp.maximum(dist, 1).astype(F32)
    log_bucket = max_exact + (jnp.log(df / max_exact) / math.log(MAX_DISTANCE / max_exact)
                              * (NUM_BUCKETS - max_exact)).astype(jnp.int32)
    log_bucket = jnp.minimum(log_bucket, NUM_BUCKETS - 1)
    return jnp.where(dist < max_exact, dist, log_bucket)


def _toeplitz(u):
    n = BAND_BLOCK
    lead = u.shape[:-1]
    w = jnp.concatenate([u[..., ::-1], jnp.zeros(lead + (1,), u.dtype)], axis=-1)
    r = jnp.broadcast_to(w[..., None, :], lead + (n, 2 * n)).reshape(lead + (2 * n * n,))
    return r[..., :n * (2 * n - 1)].reshape(lead + (n, 2 * n - 1))[..., n - 1:]


def _band_bias(rel_bias):
    n = BAND_BLOCK
    tables = []
    for window, dil in DILATED_PATTERNS:
        steps = window // dil
        back = jnp.arange(2 * n)
        by_back = rel_bias[_t5_bucket(back * dil)].astype(F32).T * LOG2E
        by_back = jnp.where(back <= steps, by_back, MASK_VALUE)
        masked = jnp.full((A_HEADS, n - 1), MASK_VALUE, F32)
        cur = _toeplitz(jnp.concatenate([masked, by_back[:, :n]], axis=1))
        prev = _toeplitz(by_back[:, 1:])
        normal = jnp.concatenate([prev, cur], axis=-1)
        first = jnp.concatenate([jnp.full_like(prev, MASK_VALUE), cur], axis=-1)
        tables.append(jnp.stack([normal, first], axis=1))
    tab = jnp.stack(tables, axis=1)
    tab = tab.reshape(A_HEADS // 2, 2, len(DILATED_PATTERNS), 2, n, 2 * n)
    return tab.transpose(0, 2, 3, 1, 4, 5).reshape(A_HEADS // 2, len(DILATED_PATTERNS), 2, 2 * n, 2 * n)


def _pack_pairs(v):
    half = v.shape[1] // 2
    bits = pltpu.bitcast(v, jnp.uint32)
    return (bits[:, :half] >> 16) | (bits[:, half:] & jnp.uint32(0xFFFF0000))


def _unpack_pairs(p):
    return (pltpu.bitcast(p << 16, F32), pltpu.bitcast(p & jnp.uint32(0xFFFF0000), F32))


def _out_proj_kernel(x_ref, a_ref, b_ref, w_ref, g_ref, rw_ref, rb_ref,
                     x1_ref, hp_ref, route_ref, cnt_ref, carry):
    i = pl.program_id(0)
    tm = x_ref.shape[0]

    @pl.when(i == 0)
    def _():
        carry[...] = jnp.zeros_like(carry)

    w = w_ref[...]
    x1 = x_ref[...] + _dot(a_ref[...], w[:A_WIDTH]) + _dot(b_ref[...], w[A_WIDTH:])
    x1_ref[...] = x1
    ms = jnp.mean(x1 * x1, axis=-1, keepdims=True)
    h = x1 * lax.rsqrt(ms + RMS_EPS) * g_ref[...]
    hi = h.astype(BF16)
    hi_f = hi.astype(F32)
    hp_ref[...] = _pack_pairs(hi_f)
    lo = (h - hi_f).astype(BF16)
    rw_hi = rw_ref[0]
    logits = _dot(hi, rw_hi) + _dot(lo, rw_hi) + _dot(hi, rw_ref[1]) + rb_ref[...]

    lane = lax.broadcasted_iota(jnp.int32, (tm, LANES), 1).astype(F32)
    remaining = logits
    vals, hots = [], []
    for _ in range(TOP_K):
        m = jnp.max(remaining, axis=-1, keepdims=True)
        first = jnp.min(jnp.where(remaining == m, lane, float(LANES)), axis=-1, keepdims=True)
        hot = lane == first
        remaining = jnp.where(hot, -jnp.inf, remaining)
        vals.append(m)
        hots.append(hot)
    exps = [jnp.exp(v - vals[0]) for v in vals]
    den = exps[0] + exps[1] + exps[2] + exps[3]

    chosen = jnp.zeros((tm, LANES), F32)
    for hot in hots:
        chosen = chosen + jnp.where(hot, 1.0, 0.0)
    r = lax.broadcasted_iota(jnp.int32, (tm, tm), 0)
    c = lax.broadcasted_iota(jnp.int32, (tm, tm), 1)
    earlier = jnp.where(r > c, 1.0, 0.0).astype(BF16)
    before = carry[...] + _dot(earlier, chosen.astype(BF16))
    carry[...] = carry[...] + jnp.sum(chosen, axis=0, keepdims=True)
    cnt_ref[...] = jnp.broadcast_to(carry[...], cnt_ref.shape)

    route = jnp.zeros((tm, LANES), F32)
    for k in range(TOP_K):
        first = jnp.sum(jnp.where(hots[k], lane, 0.0), axis=-1, keepdims=True)
        rank = jnp.sum(jnp.where(hots[k], before, 0.0), axis=-1, keepdims=True)
        route = route + jnp.where(lane == float(k), first, 0.0)
        route = route + jnp.where(lane == float(TOP_K + k), rank, 0.0)
        route = route + jnp.where(lane == float(2 * TOP_K + k), exps[k] / den, 0.0)
    route_ref[...] = route


def _out_proj(x2, out_a, out_b, w, g, rw, rb, tm):
    t, d = x2.shape
    return pl.pallas_call(
        _out_proj_kernel,
        grid=(t // tm,),
        in_specs=[pl.BlockSpec((tm, d), lambda i: (i, 0)),
                  pl.BlockSpec((tm, A_WIDTH), lambda i: (i, 0)),
                  pl.BlockSpec((tm, B_WIDTH), lambda i: (i, 0)),
                  pl.BlockSpec(w.shape, lambda i: (0, 0)),
                  pl.BlockSpec((1, d), lambda i: (0, 0)),
                  pl.BlockSpec(rw.shape, lambda i: (0, 0, 0)),
                  pl.BlockSpec((1, LANES), lambda i: (0, 0))],
        out_specs=[pl.BlockSpec((tm, d), lambda i: (i, 0)),
                   pl.BlockSpec((tm, d // 2), lambda i: (i, 0)),
                   pl.BlockSpec((tm, LANES), lambda i: (i, 0)),
                   pl.BlockSpec((8, LANES), lambda i: (0, 0))],
        out_shape=[jax.ShapeDtypeStruct((t, d), F32),
                   jax.ShapeDtypeStruct((t, d // 2), jnp.uint32),
                   jax.ShapeDtypeStruct((t, LANES), F32),
                   jax.ShapeDtypeStruct((8, LANES), F32)],
        scratch_shapes=[pltpu.VMEM((1, LANES), F32)],
        compiler_params=_params(),
        name="out_proj",
    )(x2, out_a, out_b, w, g, rw, rb)


def _dispatch_kernel(pad_ref, end_ref, nb_ref, dest_hbm, h_ref, xs_hbm, dest_s0, dest_s1, zeros,
                     sem_i, sem_z, sem_o, *, bm):
    i = pl.program_id(0)
    steps = pl.num_programs(0)
    tm = h_ref.shape[0]
    n_blocks = xs_hbm.shape[0] // bm
    dest_s = (dest_s0, dest_s1)

    def idx_copy(step, slot):
        return pltpu.make_async_copy(dest_hbm.at[step], dest_s[slot], sem_i.at[slot])

    @pl.when(i == 0)
    def _():
        idx_copy(0, 0).start()
        zeros[...] = jnp.zeros_like(zeros)

        def zero_block(start):
            return pltpu.make_async_copy(zeros, xs_hbm.at[pl.ds(pl.multiple_of(start, bm), bm), :], sem_z)

        for e in range(N_EXPERTS):
            @pl.when(pad_ref[e] > 0)
            def _():
                zero_block(end_ref[e] - bm).start()

        def start_unused(b, carry):
            zero_block(b * bm).start()
            return carry

        def wait_unused(b, carry):
            zero_block(b * bm).wait()
            return carry

        lax.fori_loop(nb_ref[0], n_blocks, start_unused, 0)
        for e in range(N_EXPERTS):
            @pl.when(pad_ref[e] > 0)
            def _():
                zero_block(end_ref[e] - bm).wait()
        lax.fori_loop(nb_ref[0], n_blocks, wait_unused, 0)

    for slot in range(2):
        @pl.when(i % 2 == slot)
        def _(slot=slot):
            @pl.when(i + 1 < steps)
            def _():
                idx_copy(i + 1, 1 - slot).start()

            idx_copy(i, slot).wait()

            def body(t, carry):
                for k in range(TOP_K):
                    pltpu.make_async_copy(h_ref.at[pl.ds(t, 1), :],
                                          xs_hbm.at[pl.ds(dest_s[slot][t * TOP_K + k], 1), :],
                                          sem_o).start(priority=k % 2)
                return carry

            lax.fori_loop(0, tm, body, 0, unroll=8)
    for _ in range(TOP_K):
        pltpu.make_async_copy(h_ref, xs_hbm.at[pl.ds(0, tm), :], sem_o).wait()


def _dispatch(padded, pad_end, n_used, dest, hp, n_slots, tm, bm):
    t, half = hp.shape
    grid_spec = pltpu.PrefetchScalarGridSpec(
        num_scalar_prefetch=3,
        grid=(t // tm,),
        in_specs=[pl.BlockSpec(memory_space=pl.ANY),
                  pl.BlockSpec((tm, half), lambda i, p, e, n: (i, 0))],
        out_specs=pl.BlockSpec(memory_space=pl.ANY),
        scratch_shapes=[pltpu.SMEM((tm * TOP_K,), jnp.int32),
                        pltpu.SMEM((tm * TOP_K,), jnp.int32),
                        pltpu.VMEM((bm, half), jnp.uint32),
                        pltpu.SemaphoreType.DMA((2,)),
                        pltpu.SemaphoreType.DMA(()),
                        pltpu.SemaphoreType.DMA(())],
    )
    return pl.pallas_call(
        functools.partial(_dispatch_kernel, bm=bm),
        grid_spec=grid_spec,
        out_shape=jax.ShapeDtypeStruct((n_slots, half), jnp.uint32),
        compiler_params=_params(),
        name="dispatch",
    )(padded, pad_end, n_used, dest, hp)


EXPERT_SUB_ROWS = 256


def _expert_ffn_kernel(be_ref, nb_ref, x_ref, wgu_ref, bgu_ref, wd_ref, bd_ref, y_ref, wgu_s, wd_s):
    i = pl.program_id(0)
    d_ff = wd_ref.shape[0]
    half = x_ref.shape[1]

    @pl.when(i >= nb_ref[0])
    def _():
        y_ref[...] = jnp.zeros_like(y_ref)

    @pl.when(i < nb_ref[0])
    def _():
        changed = jnp.logical_or(i == 0, be_ref[i] != be_ref[jnp.maximum(i - 1, 0)])

        @pl.when(changed)
        def _():
            wgu_s[...] = wgu_ref[...].astype(BF16)
            wd_s[...] = wd_ref[...].astype(BF16)

        for r0 in range(0, x_ref.shape[0], EXPERT_SUB_ROWS):
            rows = slice(r0, r0 + EXPERT_SUB_ROWS)
            x_lo, x_hi = _unpack_pairs(x_ref[rows, :])
            gu = (_dot(x_lo.astype(BF16), wgu_s[:half, :]) + _dot(x_hi.astype(BF16), wgu_s[half:, :])
                  + bgu_ref[...])
            gate = jnp.minimum(gu[:, :d_ff], SWIGLU_LIMIT)
            up = jnp.clip(gu[:, d_ff:], -SWIGLU_LIMIT, SWIGLU_LIMIT)
            glu = gate * jax.nn.sigmoid(SWIGLU_ALPHA * gate)
            act = ((up + 1.0) * glu).astype(BF16)
            y = _dot(act, wd_s[...]) + bd_ref[...]
            y_ref[rows, :] = _pack_pairs(y.astype(BF16).astype(F32))


def _expert_ffn(block_expert, n_used, xs, w_gate_up, b_gate_up, w_down, b_down, bm):
    n_slots, half = xs.shape
    n_e, d, two_ff = w_gate_up.shape
    d_ff = two_ff // 2
    n_blocks = n_slots // bm

    def used(i, nb):
        return jnp.minimum(i, jnp.maximum(nb[0] - 1, 0))

    grid_spec = pltpu.PrefetchScalarGridSpec(
        num_scalar_prefetch=2,
        grid=(n_blocks,),
        in_specs=[pl.BlockSpec((bm, half), lambda i, be, nb: (used(i, nb), 0)),
                  pl.BlockSpec((None, d, two_ff), lambda i, be, nb: (be[i], 0, 0)),
                  pl.BlockSpec((None, 1, two_ff), lambda i, be, nb: (be[i], 0, 0)),
                  pl.BlockSpec((None, d_ff, d), lambda i, be, nb: (be[i], 0, 0)),
                  pl.BlockSpec((None, 1, d), lambda i, be, nb: (be[i], 0, 0))],
        out_specs=pl.BlockSpec((bm, half), lambda i, be, nb: (i, 0)),
        scratch_shapes=[pltpu.VMEM((d, two_ff), BF16), pltpu.VMEM((d_ff, d), BF16)],
    )
    return pl.pallas_call(
        _expert_ffn_kernel,
        grid_spec=grid_spec,
        out_shape=jax.ShapeDtypeStruct((n_slots, half), jnp.uint32),
        compiler_params=_params(),
        name="expert_ffn",
    )(block_expert, n_used, xs, w_gate_up, b_gate_up.reshape(n_e, 1, two_ff),
      w_down, b_down.reshape(n_e, 1, d))


COMBINE_LOCAL_ROWS = 2560
SEG_ALIGN = 8


def _pieces(limit):
    sizes = []
    size = 1 << (limit.bit_length() - 1)
    while size >= SEG_ALIGN:
        sizes.append(size)
        size //= 2
    return sizes


def _combine_kernel(src_ref, len_ref, dst_ref, tot_ref, route_ref, x1_ref, ys_hbm, o_ref, ybuf, sem_g):
    j = pl.program_id(0)
    blocks = pl.num_programs(0) - 1
    tm = x1_ref.shape[0]
    half = x1_ref.shape[1] // 2
    local_rows = ybuf.shape[1]

    @pl.when(j == 0)
    def _():
        ybuf[...] = jnp.zeros_like(ybuf)

    @pl.when(j < blocks)
    def _():
        for s in range(2):
            @pl.when(j % 2 == s)
            def _(s=s):
                for e in range(N_EXPERTS):
                    seg = j * N_EXPERTS + e
                    n = len_ref[seg]
                    src = src_ref[seg]
                    dst = dst_ref[seg]
                    for size in _pieces(tm + 2 * SEG_ALIGN):
                        @pl.when((n & size) != 0)
                        def _(size=size, src=src, dst=dst):
                            pltpu.make_async_copy(
                                ys_hbm.at[pl.ds(pl.multiple_of(src, SEG_ALIGN), size), :],
                                ybuf.at[s, pl.ds(pl.multiple_of(dst, SEG_ALIGN), size), :],
                                sem_g.at[s]).start()
                        step = jnp.where((n & size) != 0, size, 0)
                        src = src + step
                        dst = dst + step

    @pl.when(j >= 1)
    def _():
        slot = (j - 1) % 2
        total = tot_ref[j - 1]
        for size in _pieces(local_rows):
            @pl.when((total & size) != 0)
            def _(size=size):
                pltpu.make_async_copy(ys_hbm.at[pl.ds(0, size), :], ybuf.at[slot, pl.ds(0, size), :],
                                      sem_g.at[slot]).wait()

        route = route_ref[...]
        col = lax.broadcasted_iota(jnp.int32, (tm, local_rows), 1).astype(F32)
        g = jnp.zeros((tm, local_rows), F32)
        for k in range(TOP_K):
            pos = route[:, TOP_K + k: TOP_K + k + 1]
            gate = route[:, 2 * TOP_K + k: 2 * TOP_K + k + 1]
            g = g + jnp.where(col == pos, gate, 0.0)
        g = g.astype(BF16)
        lo, hi = _unpack_pairs(ybuf[slot])
        o_ref[:, :half] = x1_ref[:, :half] + _dot(g, lo.astype(BF16))
        o_ref[:, half:] = x1_ref[:, half:] + _dot(g, hi.astype(BF16))


def _combine(tables, route, x1, ys, tm, local_rows):
    t, d = x1.shape
    half = d // 2

    def summed(j, *_):
        return (jnp.maximum(j - 1, 0), 0)

    grid_spec = pltpu.PrefetchScalarGridSpec(
        num_scalar_prefetch=4,
        grid=(t // tm + 1,),
        in_specs=[pl.BlockSpec((tm, LANES), summed),
                  pl.BlockSpec((tm, d), summed),
                  pl.BlockSpec(memory_space=pl.ANY)],
        out_specs=pl.BlockSpec((tm, d), summed),
        scratch_shapes=[pltpu.VMEM((2, local_rows, half), jnp.uint32),
                        pltpu.SemaphoreType.DMA((2,))],
    )
    return pl.pallas_call(
        _combine_kernel,
        grid_spec=grid_spec,
        out_shape=jax.ShapeDtypeStruct((t, d), F32),
        compiler_params=_params(),
        name="combine",
    )(*tables, route, x1, ys)


def _combine_tables(route, pad_start, tm, local_rows):
    t = route.shape[0]
    blocks = t // tm
    idx = route[:, :TOP_K].astype(jnp.int32).reshape(blocks, tm * TOP_K)
    rank = route[:, TOP_K:2 * TOP_K].astype(jnp.int32).reshape(blocks, tm * TOP_K)
    experts = jnp.arange(N_EXPERTS, dtype=jnp.int32)
    n = jnp.sum(idx[:, :, None] == experts[None, None, :], axis=1).astype(jnp.int32)
    before = jnp.cumsum(n, axis=0) - n
    first = pad_start[None, :] + before
    src = first // SEG_ALIGN * SEG_ALIGN
    length = jnp.where(n > 0, (first + n + SEG_ALIGN - 1) // SEG_ALIGN * SEG_ALIGN - src, 0)
    dst = jnp.cumsum(length, axis=1) - length
    total = jnp.sum(length, axis=1)
    assert local_rows >= tm * TOP_K + N_EXPERTS * 2 * (SEG_ALIGN - 1)
    local = jnp.take_along_axis(dst + first - src - before, idx, axis=1) + rank
    route = route.at[:, TOP_K:2 * TOP_K].set(local.reshape(t, TOP_K).astype(F32))
    tables = (src.reshape(-1).astype(jnp.int32), length.reshape(-1).astype(jnp.int32),
              dst.reshape(-1).astype(jnp.int32), total.astype(jnp.int32))
    return tables, route


def _slot_layout(route, counts, tm, bm):
    t = route.shape[0]
    counts = counts.astype(jnp.int32)
    padded = (counts + bm - 1) // bm * bm
    pad_end = jnp.cumsum(padded).astype(jnp.int32)
    pad_start = pad_end - padded
    idx = route[:, :TOP_K].astype(jnp.int32)
    rank = route[:, TOP_K:2 * TOP_K].astype(jnp.int32)
    dest = (jnp.take(pad_start, idx) + rank).reshape(t // tm, tm * TOP_K)
    n_blocks = -(-t * TOP_K // bm) + N_EXPERTS
    block_start = jnp.arange(n_blocks, dtype=jnp.int32) * bm
    block_expert = jnp.minimum(jnp.sum(pad_end[None, :] <= block_start[:, None], axis=1),
                               N_EXPERTS - 1).astype(jnp.int32)
    n_used = (pad_end[-1:] // bm).astype(jnp.int32)
    return padded, pad_end, dest, block_expert, n_used, n_blocks * bm


def _rot_cols(w):
    half = w.shape[-1] // 2
    return jnp.concatenate([-w[..., half:], w[..., :half]], axis=-1)


def _swap_halves(g):
    half = g.shape[-1] // 2
    return jnp.concatenate([g[..., half:], g[..., :half]], axis=-1)


def kernel(x, attn_norm_g, w_in, a_q_norm_g, a_k_norm_g, rel_bias, q_a_norm_g, w_q_b, kv_a_norm_g,
           w_kv_b, b_q_norm_g, b_k_norm_g, w_out, ffn_norm_g, router_w, router_b, w_gate_up,
           b_gate_up, w_down, b_down):
    batch, seq, d = x.shape
    t = batch * seq
    depth = w_in.shape[0]
    tm = 512
    expert_bm = 512

    pos = jnp.arange(seq, dtype=F32)
    inv_freq = ROPE_THETA ** (-jnp.arange(0, QK_ROPE_DIM, 2, dtype=F32) / QK_ROPE_DIM)
    ang = pos[:, None] * inv_freq[None, :]
    cos, sin = jnp.cos(ang), jnp.sin(ang)
    cs = jnp.concatenate([cos, cos, sin, sin], axis=-1)

    row = jnp.arange(LANES)[:, None] // A_HEAD_DIM
    col = jnp.arange(LANES)[None, :] // A_HEAD_DIM
    head_block_ones = (row == col).astype(BF16)
    band_bias = _band_bias(rel_bias)

    x2 = x.reshape(t, d)
    for layer in range(depth):
        kpe_off = 3 * A_WIDTH + Q_LORA_RANK + KV_LORA_RANK
        w_kpe = w_in[layer][:, kpe_off:]
        w_in_r = jnp.concatenate([w_in[layer][:, :kpe_off], w_kpe, _rot_cols(w_kpe)], axis=1).astype(BF16)

        wq = w_q_b[layer].reshape(Q_LORA_RANK, B_HEADS, QK_HEAD_DIM)
        wq_rope = wq[..., QK_NOPE_DIM:]
        wq_r = jnp.concatenate([wq, _rot_cols(wq_rope)], axis=-1).reshape(Q_LORA_RANK, -1).astype(BF16)
        wkv = w_kv_b[layer].reshape(KV_LORA_RANK, B_HEADS, QK_NOPE_DIM + V_HEAD_DIM)
        wkv_r = jnp.concatenate([wkv[..., :QK_NOPE_DIM].reshape(KV_LORA_RANK, -1),
                                 wkv[..., QK_NOPE_DIM:].reshape(KV_LORA_RANK, -1)], axis=1).astype(BF16)
        gc = jnp.concatenate([q_a_norm_g[layer], kv_a_norm_g[layer]])[None, :]
        gq, gk = b_q_norm_g[layer], b_k_norm_g[layer]

        def rope_gain(gr):
            return jnp.concatenate([gr, _swap_halves(gr)])

        gqk = jnp.stack([gq[:QK_NOPE_DIM], rope_gain(gq[QK_NOPE_DIM:]),
                         gk[:QK_NOPE_DIM], rope_gain(gk[QK_NOPE_DIM:])])
        g_a = jnp.stack([jnp.tile(a_q_norm_g[layer], 2), jnp.tile(a_k_norm_g[layer], 2)])
        g_a = jnp.broadcast_to(g_a[None], (A_WIDTH // LANES, 2, LANES))

        proj = _in_proj(x2, attn_norm_g[layer][None, :], w_in_r, tm)
        qb, kb, vb = _mla_prep(proj, cs, wq_r, wkv_r, gc, gqk, seq, tm)
        out_b = _mla_attention(qb, kb, vb, batch, seq, 256)
        out_a = _dilated_attention(proj, head_block_ones, g_a, band_bias, batch, seq)

        rw = jnp.pad(router_w[layer], ((0, 0), (0, LANES - N_EXPERTS)))
        rw_hi = rw.astype(BF16)
        rw_lo = (rw - rw_hi.astype(F32)).astype(BF16)
        rb = jnp.pad(router_b[layer], (0, LANES - N_EXPERTS), constant_values=-jnp.inf)[None, :]
        x1, hp, route, counts = _out_proj(x2, out_a, out_b, w_out[layer].astype(BF16),
                                          ffn_norm_g[layer][None, :], jnp.stack([rw_hi, rw_lo]), rb, tm)

        padded, pad_end, dest, block_expert, n_used, n_slots = _slot_layout(
            route, counts[0, :N_EXPERTS], tm, expert_bm)
        xs = _dispatch(padded, pad_end, n_used, dest, hp, n_slots, tm, expert_bm)
        ys = _expert_ffn(block_expert, n_used, xs, w_gate_up[layer], b_gate_up[layer],
                         w_down[layer], b_down[layer], expert_bm)
        tables, route_local = _combine_tables(route, pad_end - padded, tm, COMBINE_LOCAL_ROWS)
        x2 = _combine(tables, route_local, x1, ys, tm, COMBINE_LOCAL_ROWS)
    return x2.reshape(batch, seq, d)
```

```python
import functools
import math

import jax
import jax.numpy as jnp
from jax import lax
from jax.experimental import pallas as pl
from jax.experimental.pallas import tpu as pltpu

A_HEADS = 8
A_HEAD_DIM = 64
A_WIDTH = A_HEADS * A_HEAD_DIM
DILATED_PATTERNS = ((128, 1), (512, 4), (2048, 16))
BAND_BLOCK = 128

B_HEADS = 4
QK_NOPE_DIM = 128
QK_ROPE_DIM = 64
QK_HEAD_DIM = QK_NOPE_DIM + QK_ROPE_DIM
V_HEAD_DIM = 128
Q_LORA_RANK = 256
KV_LORA_RANK = 256
B_WIDTH = B_HEADS * V_HEAD_DIM
ROPE_THETA = 10000.0

NUM_BUCKETS = 32
MAX_DISTANCE = 2048

N_EXPERTS = 32
TOP_K = 4
SWIGLU_LIMIT = 7.0
SWIGLU_ALPHA = 1.702
RMS_EPS = 1e-6

LANES = 128
MASK_VALUE = -1e30
LOG2E = math.log2(math.e)
PROJ_WIDTH = 3 * A_WIDTH + Q_LORA_RANK + KV_LORA_RANK + 2 * QK_ROPE_DIM
VMEM_LIMIT = 56 * 1024 * 1024

F32 = jnp.float32
BF16 = jnp.bfloat16


def _dot(a, b):
    return jnp.dot(a, b, preferred_element_type=F32)


def _dot_nt(a, b):
    return lax.dot_general(a, b, (((1,), (1,)), ((), ())), preferred_element_type=F32)


def _split_dot(x, m):
    hi = x.astype(BF16)
    lo = (x - hi.astype(F32)).astype(BF16)
    return _dot(hi, m) + _dot(lo, m)


def _params(n_parallel=1):
    return pltpu.CompilerParams(
        dimension_semantics=("arbitrary",) * n_parallel, vmem_limit_bytes=VMEM_LIMIT)


def _in_proj_kernel(x_ref, g_ref, w_ref, o_ref):
    x = x_ref[...]
    ms = jnp.mean(x * x, axis=-1, keepdims=True)
    h = (x * lax.rsqrt(ms + RMS_EPS) * g_ref[...]).astype(BF16)
    o_ref[...] = _dot(h, w_ref[...])


def _in_proj(x2, g, w, tm):
    t, d = x2.shape
    n = w.shape[1]
    return pl.pallas_call(
        _in_proj_kernel,
        grid=(t // tm,),
        in_specs=[pl.BlockSpec((tm, d), lambda i: (i, 0)),
                  pl.BlockSpec((1, d), lambda i: (0, 0)),
                  pl.BlockSpec((d, n), lambda i: (0, 0))],
        out_specs=pl.BlockSpec((tm, n), lambda i: (i, 0)),
        out_shape=jax.ShapeDtypeStruct((t, n), F32),
        compiler_params=_params(),
        name="in_proj",
    )(x2, g, w)


def _mla_prep_kernel(c_ref, kpe_ref, cs_ref, wq_ref, wkv_ref, gc_ref, gqk_ref, q_ref, k_ref, v_ref,
                     *, q_scale):
    c = c_ref[...]
    gc = gc_ref[...]

    def lora_norm(z, g):
        ms = jnp.mean(z * z, axis=-1, keepdims=True)
        return (z * lax.rsqrt(ms + RMS_EPS) * g).astype(BF16)

    cq = lora_norm(c[:, :Q_LORA_RANK], gc[:, :Q_LORA_RANK])
    ckv = lora_norm(c[:, Q_LORA_RANK:], gc[:, Q_LORA_RANK:])
    qb = _dot(cq, wq_ref[...])
    kvb = _dot(ckv, wkv_ref[...])
    kper = kpe_ref[...]
    cs = cs_ref[...]
    gqk = gqk_ref[...]
    tm = c.shape[0]

    row = lax.broadcasted_iota(jnp.int32, (LANES, LANES), 0)
    ones_all = jnp.ones((LANES, LANES), BF16)
    ones_lo = jnp.where(row < QK_ROPE_DIM, 1.0, 0.0).astype(BF16)
    lane = lax.broadcasted_iota(jnp.int32, (tm, LANES), 1)

    def rope(z, g_row):
        t = z * (g_row * cs)
        return jnp.where(lane < QK_ROPE_DIM, t + pltpu.roll(t, QK_ROPE_DIM, 1), 0.0)

    k_rope = rope(kper, gqk[3:4])
    kpe_ss = _split_dot(kper * kper, ones_lo)
    for h in range(B_HEADS):
        qn = qb[:, 2 * LANES * h: 2 * LANES * h + LANES]
        qr = qb[:, 2 * LANES * h + LANES: 2 * LANES * (h + 1)]
        ss = _split_dot(qn * qn, ones_all) + _split_dot(qr * qr, ones_lo)
        rs = lax.rsqrt(ss * (1.0 / QK_HEAD_DIM) + RMS_EPS) * q_scale
        q_ref[h, :, :LANES] = (qn * gqk[0:1] * rs).astype(BF16)
        q_ref[h, :, LANES:] = (rope(qr, gqk[1:2]) * rs).astype(BF16)
        kn = kvb[:, LANES * h: LANES * (h + 1)]
        ssk = _split_dot(kn * kn, ones_all) + kpe_ss
        rsk = lax.rsqrt(ssk * (1.0 / QK_HEAD_DIM) + RMS_EPS)
        k_ref[h, :, :LANES] = (kn * gqk[2:3] * rsk).astype(BF16)
        k_ref[h, :, LANES:] = (k_rope * rsk).astype(BF16)
    v_ref[...] = kvb[:, B_HEADS * QK_NOPE_DIM:].astype(BF16)


def _mla_prep(proj, cs, wq, wkv, gc, gqk, seq, tm):
    t = proj.shape[0]
    c_width = Q_LORA_RANK + KV_LORA_RANK
    c_block = 3 * A_WIDTH // c_width
    kpe_block = (3 * A_WIDTH + c_width) // LANES
    pos_blocks = seq // tm
    kernel = functools.partial(_mla_prep_kernel, q_scale=QK_HEAD_DIM ** -0.5 * LOG2E)
    return pl.pallas_call(
        kernel,
        grid=(t // tm,),
        in_specs=[pl.BlockSpec((tm, c_width), lambda i: (i, c_block)),
                  pl.BlockSpec((tm, LANES), lambda i: (i, kpe_block)),
                  pl.BlockSpec((tm, LANES), lambda i: (i % pos_blocks, 0)),
                  pl.BlockSpec(wq.shape, lambda i: (0, 0)),
                  pl.BlockSpec(wkv.shape, lambda i: (0, 0)),
                  pl.BlockSpec(gc.shape, lambda i: (0, 0)),
                  pl.BlockSpec(gqk.shape, lambda i: (0, 0))],
        out_specs=[pl.BlockSpec((B_HEADS, tm, 2 * LANES), lambda i: (0, i, 0)),
                   pl.BlockSpec((B_HEADS, tm, 2 * LANES), lambda i: (0, i, 0)),
                   pl.BlockSpec((tm, B_WIDTH), lambda i: (i, 0))],
        out_shape=[jax.ShapeDtypeStruct((B_HEADS, t, 2 * LANES), BF16),
                   jax.ShapeDtypeStruct((B_HEADS, t, 2 * LANES), BF16),
                   jax.ShapeDtypeStruct((t, B_WIDTH), BF16)],
        compiler_params=_params(),
        name="mla_prep",
    )(proj, proj, cs, wq, wkv, gc, gqk)


def _mla_attention_kernel(q_ref, k_ref, v_ref, o_ref, *, tq):
    seq = q_ref.shape[0]
    row = lax.broadcasted_iota(jnp.int32, (tq, tq), 0)
    col = lax.broadcasted_iota(jnp.int32, (tq, tq), 1)
    diag_mask = jnp.where(col <= row, 0.0, MASK_VALUE)
    v_ext = jnp.concatenate([v_ref[...], jnp.ones(v_ref.shape, v_ref.dtype)], axis=1)
    dv = v_ref.shape[1]
    for i in range(seq // tq):
        q = q_ref[i * tq:(i + 1) * tq, :]
        s_diag = _dot_nt(q, k_ref[i * tq:(i + 1) * tq, :]) + diag_mask
        m = jnp.max(s_diag, axis=-1, keepdims=True)
        if i > 0:
            s_past = _dot_nt(q, k_ref[:i * tq, :])
            m = jnp.maximum(m, jnp.max(s_past, axis=-1, keepdims=True))
        o = _dot(jnp.exp2(s_diag - m).astype(BF16), v_ext[i * tq:(i + 1) * tq, :])
        if i > 0:
            o = o + _dot(jnp.exp2(s_past - m).astype(BF16), v_ext[:i * tq, :])
        o_ref[i * tq:(i + 1) * tq, :] = (o[:, :dv] / o[:, dv:]).astype(o_ref.dtype)


def _mla_attention(qb, kb, vb, batch, seq, tq):
    t = vb.shape[0]
    kernel = functools.partial(_mla_attention_kernel, tq=tq)
    return pl.pallas_call(
        kernel,
        grid=(batch, B_HEADS),
        in_specs=[pl.BlockSpec((None, seq, 2 * LANES), lambda b, h: (h, b, 0)),
                  pl.BlockSpec((None, seq, 2 * LANES), lambda b, h: (h, b, 0)),
                  pl.BlockSpec((seq, V_HEAD_DIM), lambda b, h: (b, h))],
        out_specs=pl.BlockSpec((seq, V_HEAD_DIM), lambda b, h: (b, h)),
        out_shape=jax.ShapeDtypeStruct((t, B_WIDTH), BF16),
        compiler_params=_params(2),
        name="mla_attention",
    )(qb, kb, vb)


def _dilated_kernel(q_ref, k_ref, v_ref, bd_ref, g_ref, bias_ref, o_ref, qs, ks, acc_o, acc_l, acc_m,
                    *, unroll):
    seq = q_ref.shape[0]
    n_tiles = seq // BAND_BLOCK
    low = lax.broadcasted_iota(jnp.int32, (BAND_BLOCK, LANES), 1) < A_HEAD_DIM
    bd = bd_ref[...]
    g = g_ref[...]

    def head_norm(z, g_row):
        ss = _split_dot(z * z, bd)
        return z * lax.rsqrt(ss * (1.0 / A_HEAD_DIM) + RMS_EPS) * g_row

    qs[...] = head_norm(q_ref[...], g[0:1]) * (A_HEAD_DIM ** -0.5 * LOG2E)
    ks[...] = head_norm(k_ref[...], g[1:2])
    ones = jnp.ones((2 * BAND_BLOCK, LANES), BF16)

    for p, (window, dil) in enumerate(DILATED_PATTERNS):
        blocks_per_class = n_tiles // dil
        span = BAND_BLOCK * dil

        def rows_at(start, dil=dil):
            if dil == 1:
                return pl.ds(start, BAND_BLOCK)
            return pl.ds(start, BAND_BLOCK, stride=dil)

        def tile(i, carry, p=p, blocks_per_class=blocks_per_class, span=span, rows_at=rows_at):
            r = i // blocks_per_class
            j = i % blocks_per_class
            cur = rows_at(r + j * span)
            prev = rows_at(r + jnp.maximum(j - 1, 0) * span)
            q = qs[cur, :]
            q2 = jnp.concatenate([jnp.where(low, q, 0.0), jnp.where(low, 0.0, q)], axis=0).astype(BF16)
            k_band = jnp.concatenate([ks[prev, :], ks[cur, :]], axis=0).astype(BF16)
            s = _dot_nt(q2, k_band) + bias_ref[p, jnp.where(j == 0, 1, 0)]
            m = jnp.max(s, axis=-1, keepdims=True)
            pr = jnp.exp2(s - m).astype(BF16)
            v_band = jnp.concatenate([v_ref[prev, :], v_ref[cur, :]], axis=0).astype(BF16)
            o = _dot(pr, jnp.concatenate([v_band, ones], axis=1))
            top, bot = o[:BAND_BLOCK], o[BAND_BLOCK:]
            acc_o[p, cur, :] = jnp.where(low, top[:, :LANES], bot[:, :LANES])
            acc_l[p, cur, :] = jnp.where(low, top[:, LANES:], bot[:, LANES:])
            acc_m[p, cur, :] = jnp.where(low, m[:BAND_BLOCK], m[BAND_BLOCK:])
            return carry

        lax.fori_loop(0, n_tiles, tile, 0, unroll=unroll)

    m_all = jnp.maximum(jnp.maximum(acc_m[0], acc_m[1]), acc_m[2])
    num = jnp.zeros((seq, LANES), F32)
    den = jnp.zeros((seq, LANES), F32)
    for p in range(len(DILATED_PATTERNS)):
        w = jnp.exp2(acc_m[p] - m_all)
        num = num + w * acc_o[p]
        den = den + w * acc_l[p]
    o_ref[...] = (num / den).astype(o_ref.dtype)


def _dilated_attention(proj, bd, g, bias, batch, seq):
    t = proj.shape[0]
    pairs = A_WIDTH // LANES
    n_pat = len(DILATED_PATTERNS)
    return pl.pallas_call(
        functools.partial(_dilated_kernel, unroll=8),
        grid=(batch, pairs),
        in_specs=[pl.BlockSpec((seq, LANES), lambda b, c: (b, c)),
                  pl.BlockSpec((seq, LANES), lambda b, c: (b, pairs + c)),
                  pl.BlockSpec((seq, LANES), lambda b, c: (b, 2 * pairs + c)),
                  pl.BlockSpec((LANES, LANES), lambda b, c: (0, 0)),
                  pl.BlockSpec((None, 2, LANES), lambda b, c: (c, 0, 0)),
                  pl.BlockSpec((None, n_pat, 2, 2 * BAND_BLOCK, 2 * BAND_BLOCK),
                               lambda b, c: (c, 0, 0, 0, 0))],
        out_specs=pl.BlockSpec((seq, LANES), lambda b, c: (b, c)),
        out_shape=jax.ShapeDtypeStruct((t, A_WIDTH), BF16),
        scratch_shapes=[pltpu.VMEM((seq, LANES), F32)] * 2
                       + [pltpu.VMEM((n_pat, seq, LANES), F32)] * 3,
        compiler_params=_params(2),
        name="dilated_attention",
    )(proj, proj, proj, bd, g, bias)


def _t5_bucket(dist):
    max_exact = NUM_BUCKETS // 2
    df = jnp.maximum(dist, 1).astype(F32)
    log_bucket = max_exact + (jnp.log(df / max_exact) / math.log(MAX_DISTANCE / max_exact)
                              * (NUM_BUCKETS - max_exact)).astype(jnp.int32)
    log_bucket = jnp.minimum(log_bucket, NUM_BUCKETS - 1)
    return jnp.where(dist < max_exact, dist, log_bucket)


def _toeplitz(u):
    n = BAND_BLOCK
    lead = u.shape[:-1]
    w = jnp.concatenate([u[..., ::-1], jnp.zeros(lead + (1,), u.dtype)], axis=-1)
    r = jnp.broadcast_to(w[..., None, :], lead + (n, 2 * n)).reshape(lead + (2 * n * n,))
    return r[..., :n * (2 * n - 1)].reshape(lead + (n, 2 * n - 1))[..., n - 1:]


def _band_bias(rel_bias):
    n = BAND_BLOCK
    tables = []
    for window, dil in DILATED_PATTERNS:
        steps = window // dil
        back = jnp.arange(2 * n)
        by_back = rel_bias[_t5_bucket(back * dil)].astype(F32).T * LOG2E
        by_back = jnp.where(back <= steps, by_back, MASK_VALUE)
        masked = jnp.full((A_HEADS, n - 1), MASK_VALUE, F32)
        cur = _toeplitz(jnp.concatenate([masked, by_back[:, :n]], axis=1))
        prev = _toeplitz(by_back[:, 1:])
        normal = jnp.concatenate([prev, cur], axis=-1)
        first = jnp.concatenate([jnp.full_like(prev, MASK_VALUE), cur], axis=-1)
        tables.append(jnp.stack([normal, first], axis=1))
    tab = jnp.stack(tables, axis=1)
    tab = tab.reshape(A_HEADS // 2, 2, len(DILATED_PATTERNS), 2, n, 2 * n)
    return tab.transpose(0, 2, 3, 1, 4, 5).reshape(A_HEADS // 2, len(DILATED_PATTERNS), 2, 2 * n, 2 * n)


def _pack_pairs(v):
    half = v.shape[1] // 2
    bits = pltpu.bitcast(v, jnp.uint32)
    return (bits[:, :half] >> 16) | (bits[:, half:] & jnp.uint32(0xFFFF0000))


def _unpack_pairs(p):
    return (pltpu.bitcast(p << 16, F32), pltpu.bitcast(p & jnp.uint32(0xFFFF0000), F32))


def _out_proj_kernel(x_ref, a_ref, b_ref, w_ref, g_ref, rw_ref, rb_ref,
                     x1_ref, hp_ref, route_ref, cnt_ref, carry):
    i = pl.program_id(0)
    tm = x_ref.shape[0]

    @pl.when(i == 0)
    def _():
        carry[...] = jnp.zeros_like(carry)

    w = w_ref[...]
    x1 = x_ref[...] + _dot(a_ref[...], w[:A_WIDTH]) + _dot(b_ref[...], w[A_WIDTH:])
    x1_ref[...] = x1
    ms = jnp.mean(x1 * x1, axis=-1, keepdims=True)
    h = x1 * lax.rsqrt(ms + RMS_EPS) * g_ref[...]
    hi = h.astype(BF16)
    hi_f = hi.astype(F32)
    hp_ref[...] = _pack_pairs(hi_f)
    lo = (h - hi_f).astype(BF16)
    rw_hi = rw_ref[0]
    logits = _dot(hi, rw_hi) + _dot(lo, rw_hi) + _dot(hi, rw_ref[1]) + rb_ref[...]

    lane = lax.broadcasted_iota(jnp.int32, (tm, LANES), 1).astype(F32)
    remaining = logits
    vals, hots = [], []
    for _ in range(TOP_K):
        m = jnp.max(remaining, axis=-1, keepdims=True)
        first = jnp.min(jnp.where(remaining == m, lane, float(LANES)), axis=-1, keepdims=True)
        hot = lane == first
        remaining = jnp.where(hot, -jnp.inf, remaining)
        vals.append(m)
        hots.append(hot)
    exps = [jnp.exp(v - vals[0]) for v in vals]
    den = exps[0] + exps[1] + exps[2] + exps[3]

    chosen = jnp.zeros((tm, LANES), F32)
    for hot in hots:
        chosen = chosen + jnp.where(hot, 1.0, 0.0)
    r = lax.broadcasted_iota(jnp.int32, (tm, tm), 0)
    c = lax.broadcasted_iota(jnp.int32, (tm, tm), 1)
    earlier = jnp.where(r > c, 1.0, 0.0).astype(BF16)
    before = carry[...] + _dot(earlier, chosen.astype(BF16))
    carry[...] = carry[...] + jnp.sum(chosen, axis=0, keepdims=True)
    cnt_ref[...] = jnp.broadcast_to(carry[...], cnt_ref.shape)

    route = jnp.zeros((tm, LANES), F32)
    for k in range(TOP_K):
        first = jnp.sum(jnp.where(hots[k], lane, 0.0), axis=-1, keepdims=True)
        rank = jnp.sum(jnp.where(hots[k], before, 0.0), axis=-1, keepdims=True)
        route = route + jnp.where(lane == float(k), first, 0.0)
        route = route + jnp.where(lane == float(TOP_K + k), rank, 0.0)
        route = route + jnp.where(lane == float(2 * TOP_K + k), exps[k] / den, 0.0)
    route_ref[...] = route


def _out_proj(x2, out_a, out_b, w, g, rw, rb, tm):
    t, d = x2.shape
    return pl.pallas_call(
        _out_proj_kernel,
        grid=(t // tm,),
        in_specs=[pl.BlockSpec((tm, d), lambda i: (i, 0)),
                  pl.BlockSpec((tm, A_WIDTH), lambda i: (i, 0)),
                  pl.BlockSpec((tm, B_WIDTH), lambda i: (i, 0)),
                  pl.BlockSpec(w.shape, lambda i: (0, 0)),
                  pl.BlockSpec((1, d), lambda i: (0, 0)),
                  pl.BlockSpec(rw.shape, lambda i: (0, 0, 0)),
                  pl.BlockSpec((1, LANES), lambda i: (0, 0))],
        out_specs=[pl.BlockSpec((tm, d), lambda i: (i, 0)),
                   pl.BlockSpec((tm, d // 2), lambda i: (i, 0)),
                   pl.BlockSpec((tm, LANES), lambda i: (i, 0)),
                   pl.BlockSpec((8, LANES), lambda i: (0, 0))],
        out_shape=[jax.ShapeDtypeStruct((t, d), F32),
                   jax.ShapeDtypeStruct((t, d // 2), jnp.uint32),
                   jax.ShapeDtypeStruct((t, LANES), F32),
                   jax.ShapeDtypeStruct((8, LANES), F32)],
        scratch_shapes=[pltpu.VMEM((1, LANES), F32)],
        compiler_params=_params(),
        name="out_proj",
    )(x2, out_a, out_b, w, g, rw, rb)


def _dispatch_kernel(pad_ref, end_ref, nb_ref, dest_hbm, h_ref, xs_hbm, dest_s0, dest_s1, zeros,
                     sem_i, sem_z, sem_o, *, bm):
    i = pl.program_id(0)
    steps = pl.num_programs(0)
    tm = h_ref.shape[0]
    n_blocks = xs_hbm.shape[0] // bm
    dest_s = (dest_s0, dest_s1)

    def idx_copy(step, slot):
        return pltpu.make_async_copy(dest_hbm.at[step], dest_s[slot], sem_i.at[slot])

    @pl.when(i == 0)
    def _():
        idx_copy(0, 0).start()
        zeros[...] = jnp.zeros_like(zeros)

        def zero_block(start):
            return pltpu.make_async_copy(zeros, xs_hbm.at[pl.ds(pl.multiple_of(start, bm), bm), :], sem_z)

        for e in range(N_EXPERTS):
            @pl.when(pad_ref[e] > 0)
            def _():
                zero_block(end_ref[e] - bm).start()

        def start_unused(b, carry):
            zero_block(b * bm).start()
            return carry

        def wait_unused(b, carry):
            zero_block(b * bm).wait()
            return carry

        lax.fori_loop(nb_ref[0], n_blocks, start_unused, 0)
        for e in range(N_EXPERTS):
            @pl.when(pad_ref[e] > 0)
            def _():
                zero_block(end_ref[e] - bm).wait()
        lax.fori_loop(nb_ref[0], n_blocks, wait_unused, 0)

    for slot in range(2):
        @pl.when(i % 2 == slot)
        def _(slot=slot):
            @pl.when(i + 1 < steps)
            def _():
                idx_copy(i + 1, 1 - slot).start()

            idx_copy(i, slot).wait()

            def body(t, carry):
                for k in range(TOP_K):
                    pltpu.make_async_copy(h_ref.at[pl.ds(t, 1), :],
                                          xs_hbm.at[pl.ds(dest_s[slot][t * TOP_K + k], 1), :],
                                          sem_o).start(priority=k % 2)
                return carry

            lax.fori_loop(0, tm, body, 0, unroll=8)
    for _ in range(TOP_K):
        pltpu.make_async_copy(h_ref, xs_hbm.at[pl.ds(0, tm), :], sem_o).wait()


def _dispatch(padded, pad_end, n_used, dest, hp, n_slots, tm, bm):
    t, half = hp.shape
    grid_spec = pltpu.PrefetchScalarGridSpec(
        num_scalar_prefetch=3,
        grid=(t // tm,),
        in_specs=[pl.BlockSpec(memory_space=pl.ANY),
                  pl.BlockSpec((tm, half), lambda i, p, e, n: (i, 0))],
        out_specs=pl.BlockSpec(memory_space=pl.ANY),
        scratch_shapes=[pltpu.SMEM((tm * TOP_K,), jnp.int32),
                        pltpu.SMEM((tm * TOP_K,), jnp.int32),
                        pltpu.VMEM((bm, half), jnp.uint32),
                        pltpu.SemaphoreType.DMA((2,)),
                        pltpu.SemaphoreType.DMA(()),
                        pltpu.SemaphoreType.DMA(())],
    )
    return pl.pallas_call(
        functools.partial(_dispatch_kernel, bm=bm),
        grid_spec=grid_spec,
        out_shape=jax.ShapeDtypeStruct((n_slots, half), jnp.uint32),
        compiler_params=_params(),
        name="dispatch",
    )(padded, pad_end, n_used, dest, hp)


EXPERT_SUB_ROWS = 256


def _expert_ffn_kernel(be_ref, nb_ref, x_ref, wgu_ref, bgu_ref, wd_ref, bd_ref, y_ref, wgu_s, wd_s):
    i = pl.program_id(0)
    d_ff = wd_ref.shape[0]
    half = x_ref.shape[1]

    @pl.when(i >= nb_ref[0])
    def _():
        y_ref[...] = jnp.zeros_like(y_ref)

    @pl.when(i < nb_ref[0])
    def _():
        changed = jnp.logical_or(i == 0, be_ref[i] != be_ref[jnp.maximum(i - 1, 0)])

        @pl.when(changed)
        def _():
            wgu_s[...] = wgu_ref[...].astype(BF16)
            wd_s[...] = wd_ref[...].astype(BF16)

        for r0 in range(0, x_ref.shape[0], EXPERT_SUB_ROWS):
            rows = slice(r0, r0 + EXPERT_SUB_ROWS)
            x_lo, x_hi = _unpack_pairs(x_ref[rows, :])
            gu = (_dot(x_lo.astype(BF16), wgu_s[:half, :]) + _dot(x_hi.astype(BF16), wgu_s[half:, :])
                  + bgu_ref[...])
            gate = jnp.minimum(gu[:, :d_ff], SWIGLU_LIMIT)
            up = jnp.clip(gu[:, d_ff:], -SWIGLU_LIMIT, SWIGLU_LIMIT)
            glu = gate * jax.nn.sigmoid(SWIGLU_ALPHA * gate)
            act = ((up + 1.0) * glu).astype(BF16)
            y = _dot(act, wd_s[...]) + bd_ref[...]
            y_ref[rows, :] = _pack_pairs(y.astype(BF16).astype(F32))


def _expert_ffn(block_expert, n_used, xs, w_gate_up, b_gate_up, w_down, b_down, bm):
    n_slots, half = xs.shape
    n_e, d, two_ff = w_gate_up.shape
    d_ff = two_ff // 2
    n_blocks = n_slots // bm

    def used(i, nb):
        return jnp.minimum(i, jnp.maximum(nb[0] - 1, 0))

    grid_spec = pltpu.PrefetchScalarGridSpec(
        num_scalar_prefetch=2,
        grid=(n_blocks,),
        in_specs=[pl.BlockSpec((bm, half), lambda i, be, nb: (used(i, nb), 0)),
                  pl.BlockSpec((None, d, two_ff), lambda i, be, nb: (be[i], 0, 0)),
                  pl.BlockSpec((None, 1, two_ff), lambda i, be, nb: (be[i], 0, 0)),
                  pl.BlockSpec((None, d_ff, d), lambda i, be, nb: (be[i], 0, 0)),
                  pl.BlockSpec((None, 1, d), lambda i, be, nb: (be[i], 0, 0))],
        out_specs=pl.BlockSpec((bm, half), lambda i, be, nb: (i, 0)),
        scratch_shapes=[pltpu.VMEM((d, two_ff), BF16), pltpu.VMEM((d_ff, d), BF16)],
    )
    return pl.pallas_call(
        _expert_ffn_kernel,
        grid_spec=grid_spec,
        out_shape=jax.ShapeDtypeStruct((n_slots, half), jnp.uint32),
        compiler_params=_params(),
        name="expert_ffn",
    )(block_expert, n_used, xs, w_gate_up, b_gate_up.reshape(n_e, 1, two_ff),
      w_down, b_down.reshape(n_e, 1, d))


COMBINE_LOCAL_ROWS = 2560
SEG_ALIGN = 8


def _pieces(limit):
    sizes = []
    size = 1 << (limit.bit_length() - 1)
    while size >= SEG_ALIGN:
        sizes.append(size)
        size //= 2
    return sizes


def _combine_kernel(src_ref, len_ref, dst_ref, tot_ref, route_ref, base_ref, x1_ref, ys_hbm, o_ref, ybuf,
                    sem_g):
    j = pl.program_id(0)
    blocks = pl.num_programs(0) - 1
    tm = x1_ref.shape[0]
    half = x1_ref.shape[1] // 2
    local_rows = ybuf.shape[1]

    @pl.when(j == 0)
    def _():
        ybuf[...] = jnp.zeros_like(ybuf)

    @pl.when(j < blocks)
    def _():
        for s in range(2):
            @pl.when(j % 2 == s)
            def _(s=s):
                for e in range(N_EXPERTS):
                    seg = j * N_EXPERTS + e
                    n = len_ref[seg]
                    src = src_ref[seg]
                    dst = dst_ref[seg]
                    for size in _pieces(tm + 2 * SEG_ALIGN):
                        @pl.when((n & size) != 0)
                        def _(size=size, src=src, dst=dst):
                            pltpu.make_async_copy(
                                ys_hbm.at[pl.ds(pl.multiple_of(src, SEG_ALIGN), size), :],
                                ybuf.at[s, pl.ds(pl.multiple_of(dst, SEG_ALIGN), size), :],
                                sem_g.at[s]).start()
                        step = jnp.where((n & size) != 0, size, 0)
                        src = src + step
                        dst = dst + step

    @pl.when(j >= 1)
    def _():
        slot = (j - 1) % 2
        total = tot_ref[j - 1]
        for size in _pieces(local_rows):
            @pl.when((total & size) != 0)
            def _(size=size):
                pltpu.make_async_copy(ys_hbm.at[pl.ds(0, size), :], ybuf.at[slot, pl.ds(0, size), :],
                                      sem_g.at[slot]).wait()

        route = route_ref[...]
        base = base_ref[...]
        lane = lax.broadcasted_iota(jnp.int32, (tm, LANES), 1).astype(F32)
        col = lax.broadcasted_iota(jnp.int32, (tm, local_rows), 1).astype(F32)
        g = jnp.zeros((tm, local_rows), F32)
        for k in range(TOP_K):
            expert = route[:, k: k + 1]
            pos = (route[:, TOP_K + k: TOP_K + k + 1]
                   + jnp.sum(jnp.where(lane == expert, base, 0.0), axis=-1, keepdims=True))
            gate = route[:, 2 * TOP_K + k: 2 * TOP_K + k + 1]
            g = g + jnp.where(col == pos, gate, 0.0)
        g = g.astype(BF16)
        lo, hi = _unpack_pairs(ybuf[slot])
        o_ref[:, :half] = x1_ref[:, :half] + _dot(g, lo.astype(BF16))
        o_ref[:, half:] = x1_ref[:, half:] + _dot(g, hi.astype(BF16))


def _combine(tables, base, route, x1, ys, tm, local_rows):
    t, d = x1.shape
    half = d // 2

    def summed(j, *_):
        return (jnp.maximum(j - 1, 0), 0)

    grid_spec = pltpu.PrefetchScalarGridSpec(
        num_scalar_prefetch=4,
        grid=(t // tm + 1,),
        in_specs=[pl.BlockSpec((tm, LANES), summed),
                  pl.BlockSpec((None, 1, LANES), lambda j, *_: (jnp.maximum(j - 1, 0), 0, 0)),
                  pl.BlockSpec((tm, d), summed),
                  pl.BlockSpec(memory_space=pl.ANY)],
        out_specs=pl.BlockSpec((tm, d), summed),
        scratch_shapes=[pltpu.VMEM((2, local_rows, half), jnp.uint32),
                        pltpu.SemaphoreType.DMA((2,))],
    )
    return pl.pallas_call(
        _combine_kernel,
        grid_spec=grid_spec,
        out_shape=jax.ShapeDtypeStruct((t, d), F32),
        compiler_params=_params(),
        name="combine",
    )(*tables, route, base, x1, ys)


def _combine_tables(route, pad_start, tm, local_rows):
    t = route.shape[0]
    blocks = t // tm
    idx = route[:, :TOP_K].astype(jnp.int32).reshape(blocks, tm * TOP_K)
    experts = jnp.arange(N_EXPERTS, dtype=jnp.int32)
    n = jnp.sum(idx[:, :, None] == experts[None, None, :], axis=1).astype(jnp.int32)
    before = jnp.cumsum(n, axis=0) - n
    first = pad_start[None, :] + before
    src = first // SEG_ALIGN * SEG_ALIGN
    length = jnp.where(n > 0, (first + n + SEG_ALIGN - 1) // SEG_ALIGN * SEG_ALIGN - src, 0)
    dst = jnp.cumsum(length, axis=1) - length
    total = jnp.sum(length, axis=1)
    assert local_rows >= tm * TOP_K + N_EXPERTS * 2 * (SEG_ALIGN - 1)
    base = jnp.pad((dst + first - src - before).astype(F32), ((0, 0), (0, LANES - N_EXPERTS)))
    tables = (src.reshape(-1).astype(jnp.int32), length.reshape(-1).astype(jnp.int32),
              dst.reshape(-1).astype(jnp.int32), total.astype(jnp.int32))
    return tables, base.reshape(blocks, 1, LANES)


def _slot_layout(route, counts, tm, bm):
    t = route.shape[0]
    counts = counts.astype(jnp.int32)
    padded = (counts + bm - 1) // bm * bm
    pad_end = jnp.cumsum(padded).astype(jnp.int32)
    pad_start = pad_end - padded
    idx = route[:, :TOP_K].astype(jnp.int32)
    rank = route[:, TOP_K:2 * TOP_K].astype(jnp.int32)
    dest = (jnp.take(pad_start, idx) + rank).reshape(t // tm, tm * TOP_K)
    n_blocks = -(-t * TOP_K // bm) + N_EXPERTS
    block_start = jnp.arange(n_blocks, dtype=jnp.int32) * bm
    block_expert = jnp.minimum(jnp.sum(pad_end[None, :] <= block_start[:, None], axis=1),
                               N_EXPERTS - 1).astype(jnp.int32)
    n_used = (pad_end[-1:] // bm).astype(jnp.int32)
    return padded, pad_end, dest, block_expert, n_used, n_blocks * bm


def _rot_cols(w):
    half = w.shape[-1] // 2
    return jnp.concatenate([-w[..., half:], w[..., :half]], axis=-1)


def _swap_halves(g):
    half = g.shape[-1] // 2
    return jnp.concatenate([g[..., half:], g[..., :half]], axis=-1)


def kernel(x, attn_norm_g, w_in, a_q_norm_g, a_k_norm_g, rel_bias, q_a_norm_g, w_q_b, kv_a_norm_g,
           w_kv_b, b_q_norm_g, b_k_norm_g, w_out, ffn_norm_g, router_w, router_b, w_gate_up,
           b_gate_up, w_down, b_down):
    batch, seq, d = x.shape
    t = batch * seq
    depth = w_in.shape[0]
    tm = 512
    expert_bm = 512

    pos = jnp.arange(seq, dtype=F32)
    inv_freq = ROPE_THETA ** (-jnp.arange(0, QK_ROPE_DIM, 2, dtype=F32) / QK_ROPE_DIM)
    ang = pos[:, None] * inv_freq[None, :]
    cos, sin = jnp.cos(ang), jnp.sin(ang)
    cs = jnp.concatenate([cos, cos, sin, sin], axis=-1)

    row = jnp.arange(LANES)[:, None] // A_HEAD_DIM
    col = jnp.arange(LANES)[None, :] // A_HEAD_DIM
    head_block_ones = (row == col).astype(BF16)
    band_bias = _band_bias(rel_bias)

    x2 = x.reshape(t, d)
    for layer in range(depth):
        kpe_off = 3 * A_WIDTH + Q_LORA_RANK + KV_LORA_RANK
        w_kpe = w_in[layer][:, kpe_off:]
        w_in_r = jnp.concatenate([w_in[layer][:, :kpe_off], w_kpe, _rot_cols(w_kpe)], axis=1).astype(BF16)

        wq = w_q_b[layer].reshape(Q_LORA_RANK, B_HEADS, QK_HEAD_DIM)
        wq_rope = wq[..., QK_NOPE_DIM:]
        wq_r = jnp.concatenate([wq, _rot_cols(wq_rope)], axis=-1).reshape(Q_LORA_RANK, -1).astype(BF16)
        wkv = w_kv_b[layer].reshape(KV_LORA_RANK, B_HEADS, QK_NOPE_DIM + V_HEAD_DIM)
        wkv_r = jnp.concatenate([wkv[..., :QK_NOPE_DIM].reshape(KV_LORA_RANK, -1),
                                 wkv[..., QK_NOPE_DIM:].reshape(KV_LORA_RANK, -1)], axis=1).astype(BF16)
        gc = jnp.concatenate([q_a_norm_g[layer], kv_a_norm_g[layer]])[None, :]
        gq, gk = b_q_norm_g[layer], b_k_norm_g[layer]

        def rope_gain(gr):
            return jnp.concatenate([gr, _swap_halves(gr)])

        gqk = jnp.stack([gq[:QK_NOPE_DIM], rope_gain(gq[QK_NOPE_DIM:]),
                         gk[:QK_NOPE_DIM], rope_gain(gk[QK_NOPE_DIM:])])
        g_a = jnp.stack([jnp.tile(a_q_norm_g[layer], 2), jnp.tile(a_k_norm_g[layer], 2)])
        g_a = jnp.broadcast_to(g_a[None], (A_WIDTH // LANES, 2, LANES))

        proj = _in_proj(x2, attn_norm_g[layer][None, :], w_in_r, tm)
        qb, kb, vb = _mla_prep(proj, cs, wq_r, wkv_r, gc, gqk, seq, tm)
        out_b = _mla_attention(qb, kb, vb, batch, seq, 256)
        out_a = _dilated_attention(proj, head_block_ones, g_a, band_bias, batch, seq)

        rw = jnp.pad(router_w[layer], ((0, 0), (0, LANES - N_EXPERTS)))
        rw_hi = rw.astype(BF16)
        rw_lo = (rw - rw_hi.astype(F32)).astype(BF16)
        rb = jnp.pad(router_b[layer], (0, LANES - N_EXPERTS), constant_values=-jnp.inf)[None, :]
        x1, hp, route, counts = _out_proj(x2, out_a, out_b, w_out[layer].astype(BF16),
                                          ffn_norm_g[layer][None, :], jnp.stack([rw_hi, rw_lo]), rb, tm)

        padded, pad_end, dest, block_expert, n_used, n_slots = _slot_layout(
            route, counts[0, :N_EXPERTS], tm, expert_bm)
        xs = _dispatch(padded, pad_end, n_used, dest, hp, n_slots, tm, expert_bm)
        ys = _expert_ffn(block_expert, n_used, xs, w_gate_up[layer], b_gate_up[layer],
                         w_down[layer], b_down[layer], expert_bm)
        tables, base = _combine_tables(route, pad_end - padded, tm, COMBINE_LOCAL_ROWS)
        x2 = _combine(tables, base, route, x1, ys, tm, COMBINE_LOCAL_ROWS)
    return x2.reshape(batch, seq, d)
```

```python
import functools
import math

import jax
import jax.numpy as jnp
from jax import lax
from jax.experimental import pallas as pl
from jax.experimental.pallas import tpu as pltpu

A_HEADS = 8
A_HEAD_DIM = 64
A_WIDTH = A_HEADS * A_HEAD_DIM
DILATED_PATTERNS = ((128, 1), (512, 4), (2048, 16))
BAND_BLOCK = 128

B_HEADS = 4
QK_NOPE_DIM = 128
QK_ROPE_DIM = 64
QK_HEAD_DIM = QK_NOPE_DIM + QK_ROPE_DIM
V_HEAD_DIM = 128
Q_LORA_RANK = 256
KV_LORA_RANK = 256
B_WIDTH = B_HEADS * V_HEAD_DIM
ROPE_THETA = 10000.0

NUM_BUCKETS = 32
MAX_DISTANCE = 2048

N_EXPERTS = 32
TOP_K = 4
SWIGLU_LIMIT = 7.0
SWIGLU_ALPHA = 1.702
RMS_EPS = 1e-6

LANES = 128
MASK_VALUE = -1e30
LOG2E = math.log2(math.e)
PROJ_WIDTH = 3 * A_WIDTH + Q_LORA_RANK + KV_LORA_RANK + 2 * QK_ROPE_DIM
VMEM_LIMIT = 56 * 1024 * 1024

F32 = jnp.float32
BF16 = jnp.bfloat16


def _dot(a, b):
    return jnp.dot(a, b, preferred_element_type=F32)


def _dot_nt(a, b):
    return lax.dot_general(a, b, (((1,), (1,)), ((), ())), preferred_element_type=F32)


def _split_dot(x, m):
    hi = x.astype(BF16)
    lo = (x - hi.astype(F32)).astype(BF16)
    return _dot(hi, m) + _dot(lo, m)


def _params(n_parallel=1):
    return pltpu.CompilerParams(
        dimension_semantics=("arbitrary",) * n_parallel, vmem_limit_bytes=VMEM_LIMIT)


def _in_proj_kernel(x_ref, g_ref, w_ref, o_ref):
    x = x_ref[...]
    ms = jnp.mean(x * x, axis=-1, keepdims=True)
    h = (x * lax.rsqrt(ms + RMS_EPS) * g_ref[...]).astype(BF16)
    o_ref[...] = _dot(h, w_ref[...])


def _in_proj(x2, g, w, tm):
    t, d = x2.shape
    n = w.shape[1]
    return pl.pallas_call(
        _in_proj_kernel,
        grid=(t // tm,),
        in_specs=[pl.BlockSpec((tm, d), lambda i: (i, 0)),
                  pl.BlockSpec((1, d), lambda i: (0, 0)),
                  pl.BlockSpec((d, n), lambda i: (0, 0))],
        out_specs=pl.BlockSpec((tm, n), lambda i: (i, 0)),
        out_shape=jax.ShapeDtypeStruct((t, n), F32),
        compiler_params=_params(),
        name="in_proj",
    )(x2, g, w)


def _mla_prep_kernel(c_ref, kpe_ref, cs_ref, wq_ref, wkv_ref, gc_ref, gqk_ref, q_ref, k_ref, v_ref,
                     *, q_scale):
    c = c_ref[...]
    gc = gc_ref[...]

    def lora_norm(z, g):
        ms = jnp.mean(z * z, axis=-1, keepdims=True)
        return (z * lax.rsqrt(ms + RMS_EPS) * g).astype(BF16)

    cq = lora_norm(c[:, :Q_LORA_RANK], gc[:, :Q_LORA_RANK])
    ckv = lora_norm(c[:, Q_LORA_RANK:], gc[:, Q_LORA_RANK:])
    qb = _dot(cq, wq_ref[...])
    kvb = _dot(ckv, wkv_ref[...])
    kper = kpe_ref[...]
    cs = cs_ref[...]
    gqk = gqk_ref[...]
    tm = c.shape[0]

    row = lax.broadcasted_iota(jnp.int32, (LANES, LANES), 0)
    ones_all = jnp.ones((LANES, LANES), BF16)
    ones_lo = jnp.where(row < QK_ROPE_DIM, 1.0, 0.0).astype(BF16)
    lane = lax.broadcasted_iota(jnp.int32, (tm, LANES), 1)

    def rope(z, g_row):
        t = z * (g_row * cs)
        return jnp.where(lane < QK_ROPE_DIM, t + pltpu.roll(t, QK_ROPE_DIM, 1), 0.0)

    k_rope = rope(kper, gqk[3:4])
    kpe_ss = _split_dot(kper * kper, ones_lo)
    for h in range(B_HEADS):
        qn = qb[:, 2 * LANES * h: 2 * LANES * h + LANES]
        qr = qb[:, 2 * LANES * h + LANES: 2 * LANES * (h + 1)]
        ss = _split_dot(qn * qn, ones_all) + _split_dot(qr * qr, ones_lo)
        rs = lax.rsqrt(ss * (1.0 / QK_HEAD_DIM) + RMS_EPS) * q_scale
        q_ref[h, :, :LANES] = (qn * gqk[0:1] * rs).astype(BF16)
        q_ref[h, :, LANES:] = (rope(qr, gqk[1:2]) * rs).astype(BF16)
        kn = kvb[:, LANES * h: LANES * (h + 1)]
        ssk = _split_dot(kn * kn, ones_all) + kpe_ss
        rsk = lax.rsqrt(ssk * (1.0 / QK_HEAD_DIM) + RMS_EPS)
        k_ref[h, :, :LANES] = (kn * gqk[2:3] * rsk).astype(BF16)
        k_ref[h, :, LANES:] = (k_rope * rsk).astype(BF16)
    v_ref[...] = kvb[:, B_HEADS * QK_NOPE_DIM:].astype(BF16)


def _mla_prep(proj, cs, wq, wkv, gc, gqk, seq, tm):
    t = proj.shape[0]
    c_width = Q_LORA_RANK + KV_LORA_RANK
    c_block = 3 * A_WIDTH // c_width
    kpe_block = (3 * A_WIDTH + c_width) // LANES
    pos_blocks = seq // tm
    kernel = functools.partial(_mla_prep_kernel, q_scale=QK_HEAD_DIM ** -0.5 * LOG2E)
    return pl.pallas_call(
        kernel,
        grid=(t // tm,),
        in_specs=[pl.BlockSpec((tm, c_width), lambda i: (i, c_block)),
                  pl.BlockSpec((tm, LANES), lambda i: (i, kpe_block)),
                  pl.BlockSpec((tm, LANES), lambda i: (i % pos_blocks, 0)),
                  pl.BlockSpec(wq.shape, lambda i: (0, 0)),
                  pl.BlockSpec(wkv.shape, lambda i: (0, 0)),
                  pl.BlockSpec(gc.shape, lambda i: (0, 0)),
                  pl.BlockSpec(gqk.shape, lambda i: (0, 0))],
        out_specs=[pl.BlockSpec((B_HEADS, tm, 2 * LANES), lambda i: (0, i, 0)),
                   pl.BlockSpec((B_HEADS, tm, 2 * LANES), lambda i: (0, i, 0)),
                   pl.BlockSpec((tm, B_WIDTH), lambda i: (i, 0))],
        out_shape=[jax.ShapeDtypeStruct((B_HEADS, t, 2 * LANES), BF16),
                   jax.ShapeDtypeStruct((B_HEADS, t, 2 * LANES), BF16),
                   jax.ShapeDtypeStruct((t, B_WIDTH), BF16)],
        compiler_params=_params(),
        name="mla_prep",
    )(proj, proj, cs, wq, wkv, gc, gqk)


def _mla_attention_kernel(q_ref, k_ref, v_ref, o_ref, *, tq):
    seq = q_ref.shape[0]
    row = lax.broadcasted_iota(jnp.int32, (tq, tq), 0)
    col = lax.broadcasted_iota(jnp.int32, (tq, tq), 1)
    diag_mask = jnp.where(col <= row, 0.0, MASK_VALUE)
    v_ext = jnp.concatenate([v_ref[...], jnp.ones(v_ref.shape, v_ref.dtype)], axis=1)
    dv = v_ref.shape[1]
    for i in range(seq // tq):
        q = q_ref[i * tq:(i + 1) * tq, :]
        s_diag = _dot_nt(q, k_ref[i * tq:(i + 1) * tq, :]) + diag_mask
        m = jnp.max(s_diag, axis=-1, keepdims=True)
        if i > 0:
            s_past = _dot_nt(q, k_ref[:i * tq, :])
            m = jnp.maximum(m, jnp.max(s_past, axis=-1, keepdims=True))
        o = _dot(jnp.exp2(s_diag - m).astype(BF16), v_ext[i * tq:(i + 1) * tq, :])
        if i > 0:
            o = o + _dot(jnp.exp2(s_past - m).astype(BF16), v_ext[:i * tq, :])
        o_ref[i * tq:(i + 1) * tq, :] = (o[:, :dv] / o[:, dv:]).astype(o_ref.dtype)


def _mla_attention(qb, kb, vb, batch, seq, tq):
    t = vb.shape[0]
    kernel = functools.partial(_mla_attention_kernel, tq=tq)
    return pl.pallas_call(
        kernel,
        grid=(batch, B_HEADS),
        in_specs=[pl.BlockSpec((None, seq, 2 * LANES), lambda b, h: (h, b, 0)),
                  pl.BlockSpec((None, seq, 2 * LANES), lambda b, h: (h, b, 0)),
                  pl.BlockSpec((seq, V_HEAD_DIM), lambda b, h: (b, h))],
        out_specs=pl.BlockSpec((seq, V_HEAD_DIM), lambda b, h: (b, h)),
        out_shape=jax.ShapeDtypeStruct((t, B_WIDTH), BF16),
        compiler_params=_params(2),
        name="mla_attention",
    )(qb, kb, vb)


def _dilated_kernel(q_ref, k_ref, v_ref, bd_ref, g_ref, bias_ref, o_ref, qs, ks, acc_o, acc_l, acc_m,
                    *, unroll):
    seq = q_ref.shape[0]
    n_tiles = seq // BAND_BLOCK
    low = lax.broadcasted_iota(jnp.int32, (BAND_BLOCK, LANES), 1) < A_HEAD_DIM
    bd = bd_ref[...]
    g = g_ref[...]

    def head_norm(z, g_row):
        ss = _split_dot(z * z, bd)
        return z * lax.rsqrt(ss * (1.0 / A_HEAD_DIM) + RMS_EPS) * g_row

    qs[...] = head_norm(q_ref[...], g[0:1]) * (A_HEAD_DIM ** -0.5 * LOG2E)
    ks[...] = head_norm(k_ref[...], g[1:2])
    ones = jnp.ones((2 * BAND_BLOCK, LANES), BF16)

    for p, (window, dil) in enumerate(DILATED_PATTERNS):
        blocks_per_class = n_tiles // dil
        span = BAND_BLOCK * dil

        def rows_at(start, dil=dil):
            if dil == 1:
                return pl.ds(start, BAND_BLOCK)
            return pl.ds(start, BAND_BLOCK, stride=dil)

        def tile(i, carry, p=p, blocks_per_class=blocks_per_class, span=span, rows_at=rows_at):
            r = i // blocks_per_class
            j = i % blocks_per_class
            cur = rows_at(r + j * span)
            prev = rows_at(r + jnp.maximum(j - 1, 0) * span)
            q = qs[cur, :]
            q2 = jnp.concatenate([jnp.where(low, q, 0.0), jnp.where(low, 0.0, q)], axis=0).astype(BF16)
            k_band = jnp.concatenate([ks[prev, :], ks[cur, :]], axis=0).astype(BF16)
            s = _dot_nt(q2, k_band) + bias_ref[p, jnp.where(j == 0, 1, 0)]
            m = jnp.max(s, axis=-1, keepdims=True)
            pr = jnp.exp2(s - m).astype(BF16)
            v_band = jnp.concatenate([v_ref[prev, :], v_ref[cur, :]], axis=0).astype(BF16)
            o = _dot(pr, jnp.concatenate([v_band, ones], axis=1))
            top, bot = o[:BAND_BLOCK], o[BAND_BLOCK:]
            acc_o[p, cur, :] = jnp.where(low, top[:, :LANES], bot[:, :LANES])
            acc_l[p, cur, :] = jnp.where(low, top[:, LANES:], bot[:, LANES:])
            acc_m[p, cur, :] = jnp.where(low, m[:BAND_BLOCK], m[BAND_BLOCK:])
            return carry

        lax.fori_loop(0, n_tiles, tile, 0, unroll=unroll)

    m_all = jnp.maximum(jnp.maximum(acc_m[0], acc_m[1]), acc_m[2])
    num = jnp.zeros((seq, LANES), F32)
    den = jnp.zeros((seq, LANES), F32)
    for p in range(len(DILATED_PATTERNS)):
        w = jnp.exp2(acc_m[p] - m_all)
        num = num + w * acc_o[p]
        den = den + w * acc_l[p]
    o_ref[...] = (num / den).astype(o_ref.dtype)


def _dilated_attention(proj, bd, g, bias, batch, seq):
    t = proj.shape[0]
    pairs = A_WIDTH // LANES
    n_pat = len(DILATED_PATTERNS)
    return pl.pallas_call(
        functools.partial(_dilated_kernel, unroll=8),
        grid=(batch, pairs),
        in_specs=[pl.BlockSpec((seq, LANES), lambda b, c: (b, c)),
                  pl.BlockSpec((seq, LANES), lambda b, c: (b, pairs + c)),
                  pl.BlockSpec((seq, LANES), lambda b, c: (b, 2 * pairs + c)),
                  pl.BlockSpec((LANES, LANES), lambda b, c: (0, 0)),
                  pl.BlockSpec((None, 2, LANES), lambda b, c: (c, 0, 0)),
                  pl.BlockSpec((None, n_pat, 2, 2 * BAND_BLOCK, 2 * BAND_BLOCK),
                               lambda b, c: (c, 0, 0, 0, 0))],
        out_specs=pl.BlockSpec((seq, LANES), lambda b, c: (b, c)),
        out_shape=jax.ShapeDtypeStruct((t, A_WIDTH), BF16),
        scratch_shapes=[pltpu.VMEM((seq, LANES), F32)] * 2
                       + [pltpu.VMEM((n_pat, seq, LANES), F32)] * 3,
        compiler_params=_params(2),
        name="dilated_attention",
    )(proj, proj, proj, bd, g, bias)


def _t5_bucket(dist):
    max_exact = NUM_BUCKETS // 2
    df = jnp.maximum(dist, 1).astype(F32)
    log_bucket = max_exact + (jnp.log(df / max_exact) / math.log(MAX_DISTANCE / max_exact)
                              * (NUM_BUCKETS - max_exact)).astype(jnp.int32)
    log_bucket = jnp.minimum(log_bucket, NUM_BUCKETS - 1)
    return jnp.where(dist < max_exact, dist, log_bucket)


def _toeplitz(u):
    n = BAND_BLOCK
    lead = u.shape[:-1]
    w = jnp.concatenate([u[..., ::-1], jnp.zeros(lead + (1,), u.dtype)], axis=-1)
    r = jnp.broadcast_to(w[..., None, :], lead + (n, 2 * n)).reshape(lead + (2 * n * n,))
    return r[..., :n * (2 * n - 1)].reshape(lead + (n, 2 * n - 1))[..., n - 1:]


def _band_bias(rel_bias):
    n = BAND_BLOCK
    tables = []
    for window, dil in DILATED_PATTERNS:
        steps = window // dil
        back = jnp.arange(2 * n)
        by_back = rel_bias[_t5_bucket(back * dil)].astype(F32).T * LOG2E
        by_back = jnp.where(back <= steps, by_back, MASK_VALUE)
        masked = jnp.full((A_HEADS, n - 1), MASK_VALUE, F32)
        cur = _toeplitz(jnp.concatenate([masked, by_back[:, :n]], axis=1))
        prev = _toeplitz(by_back[:, 1:])
        normal = jnp.concatenate([prev, cur], axis=-1)
        first = jnp.concatenate([jnp.full_like(prev, MASK_VALUE), cur], axis=-1)
        tables.append(jnp.stack([normal, first], axis=1))
    tab = jnp.stack(tables, axis=1)
    tab = tab.reshape(A_HEADS // 2, 2, len(DILATED_PATTERNS), 2, n, 2 * n)
    return tab.transpose(0, 2, 3, 1, 4, 5).reshape(A_HEADS // 2, len(DILATED_PATTERNS), 2, 2 * n, 2 * n)


def _pack_pairs(v):
    half = v.shape[1] // 2
    bits = pltpu.bitcast(v, jnp.uint32)
    return (bits[:, :half] >> 16) | (bits[:, half:] & jnp.uint32(0xFFFF0000))


def _unpack_pairs(p):
    return (pltpu.bitcast(p << 16, F32), pltpu.bitcast(p & jnp.uint32(0xFFFF0000), F32))


def _out_proj_kernel(x_ref, a_ref, b_ref, w_ref, g_ref, rw_ref, rb_ref,
                     x1_ref, hp_ref, route_ref, cnt_ref, carry):
    i = pl.program_id(0)
    tm = x_ref.shape[0]

    @pl.when(i == 0)
    def _():
        carry[...] = jnp.zeros_like(carry)

    w = w_ref[...]
    x1 = x_ref[...] + _dot(a_ref[...], w[:A_WIDTH]) + _dot(b_ref[...], w[A_WIDTH:])
    x1_ref[...] = x1
    ms = jnp.mean(x1 * x1, axis=-1, keepdims=True)
    h = x1 * lax.rsqrt(ms + RMS_EPS) * g_ref[...]
    hi = h.astype(BF16)
    hi_f = hi.astype(F32)
    hp_ref[...] = _pack_pairs(hi_f)
    lo = (h - hi_f).astype(BF16)
    rw_hi = rw_ref[0]
    logits = _dot(hi, rw_hi) + _dot(lo, rw_hi) + _dot(hi, rw_ref[1]) + rb_ref[...]

    lane = lax.broadcasted_iota(jnp.int32, (tm, LANES), 1).astype(F32)
    remaining = logits
    vals, hots = [], []
    for _ in range(TOP_K):
        m = jnp.max(remaining, axis=-1, keepdims=True)
        first = jnp.min(jnp.where(remaining == m, lane, float(LANES)), axis=-1, keepdims=True)
        hot = lane == first
        remaining = jnp.where(hot, -jnp.inf, remaining)
        vals.append(m)
        hots.append(hot)
    exps = [jnp.exp(v - vals[0]) for v in vals]
    den = exps[0] + exps[1] + exps[2] + exps[3]

    chosen = jnp.zeros((tm, LANES), F32)
    for hot in hots:
        chosen = chosen + jnp.where(hot, 1.0, 0.0)
    r = lax.broadcasted_iota(jnp.int32, (tm, tm), 0)
    c = lax.broadcasted_iota(jnp.int32, (tm, tm), 1)
    earlier = jnp.where(r > c, 1.0, 0.0).astype(BF16)
    before = carry[...] + _dot(earlier, chosen.astype(BF16))
    carry[...] = carry[...] + jnp.sum(chosen, axis=0, keepdims=True)
    cnt_ref[...] = jnp.broadcast_to(carry[...], cnt_ref.shape)

    route = jnp.zeros((tm, LANES), F32)
    for k in range(TOP_K):
        first = jnp.sum(jnp.where(hots[k], lane, 0.0), axis=-1, keepdims=True)
        rank = jnp.sum(jnp.where(hots[k], before, 0.0), axis=-1, keepdims=True)
        route = route + jnp.where(lane == float(k), first, 0.0)
        route = route + jnp.where(lane == float(TOP_K + k), rank, 0.0)
        route = route + jnp.where(lane == float(2 * TOP_K + k), exps[k] / den, 0.0)
    route_ref[...] = route


def _out_proj(x2, out_a, out_b, w, g, rw, rb, tm):
    t, d = x2.shape
    return pl.pallas_call(
        _out_proj_kernel,
        grid=(t // tm,),
        in_specs=[pl.BlockSpec((tm, d), lambda i: (i, 0)),
                  pl.BlockSpec((tm, A_WIDTH), lambda i: (i, 0)),
                  pl.BlockSpec((tm, B_WIDTH), lambda i: (i, 0)),
                  pl.BlockSpec(w.shape, lambda i: (0, 0)),
                  pl.BlockSpec((1, d), lambda i: (0, 0)),
                  pl.BlockSpec(rw.shape, lambda i: (0, 0, 0)),
                  pl.BlockSpec((1, LANES), lambda i: (0, 0))],
        out_specs=[pl.BlockSpec((tm, d), lambda i: (i, 0)),
                   pl.BlockSpec((tm, d // 2), lambda i: (i, 0)),
                   pl.BlockSpec((tm, LANES), lambda i: (i, 0)),
                   pl.BlockSpec((8, LANES), lambda i: (0, 0))],
        out_shape=[jax.ShapeDtypeStruct((t, d), F32),
                   jax.ShapeDtypeStruct((t, d // 2), jnp.uint32),
                   jax.ShapeDtypeStruct((t, LANES), F32),
                   jax.ShapeDtypeStruct((8, LANES), F32)],
        scratch_shapes=[pltpu.VMEM((1, LANES), F32)],
        compiler_params=_params(),
        name="out_proj",
    )(x2, out_a, out_b, w, g, rw, rb)


def _dispatch_kernel(pad_ref, end_ref, nb_ref, dest_hbm, h_ref, xs_hbm, dest_s0, dest_s1, zeros,
                     sem_i, sem_z, sem_o, *, bm):
    i = pl.program_id(0)
    steps = pl.num_programs(0)
    tm = h_ref.shape[0]
    n_blocks = xs_hbm.shape[0] // bm
    dest_s = (dest_s0, dest_s1)

    def idx_copy(step, slot):
        return pltpu.make_async_copy(dest_hbm.at[step], dest_s[slot], sem_i.at[slot])

    @pl.when(i == 0)
    def _():
        idx_copy(0, 0).start()
        zeros[...] = jnp.zeros_like(zeros)

        def zero_block(start):
            return pltpu.make_async_copy(zeros, xs_hbm.at[pl.ds(pl.multiple_of(start, bm), bm), :], sem_z)

        for e in range(N_EXPERTS):
            @pl.when(pad_ref[e] > 0)
            def _():
                zero_block(end_ref[e] - bm).start()

        def start_unused(b, carry):
            zero_block(b * bm).start()
            return carry

        def wait_unused(b, carry):
            zero_block(b * bm).wait()
            return carry

        lax.fori_loop(nb_ref[0], n_blocks, start_unused, 0)
        for e in range(N_EXPERTS):
            @pl.when(pad_ref[e] > 0)
            def _():
                zero_block(end_ref[e] - bm).wait()
        lax.fori_loop(nb_ref[0], n_blocks, wait_unused, 0)

    for slot in range(2):
        @pl.when(i % 2 == slot)
        def _(slot=slot):
            @pl.when(i + 1 < steps)
            def _():
                idx_copy(i + 1, 1 - slot).start()

            idx_copy(i, slot).wait()

            def body(t, carry):
                for k in range(TOP_K):
                    pltpu.make_async_copy(h_ref.at[pl.ds(t, 1), :],
                                          xs_hbm.at[pl.ds(dest_s[slot][t * TOP_K + k], 1), :],
                                          sem_o).start(priority=k % 2)
                return carry

            lax.fori_loop(0, tm, body, 0, unroll=8)
    for _ in range(TOP_K):
        pltpu.make_async_copy(h_ref, xs_hbm.at[pl.ds(0, tm), :], sem_o).wait()


def _dispatch(padded, pad_end, n_used, dest, hp, n_slots, tm, bm):
    t, half = hp.shape
    grid_spec = pltpu.PrefetchScalarGridSpec(
        num_scalar_prefetch=3,
        grid=(t // tm,),
        in_specs=[pl.BlockSpec(memory_space=pl.ANY),
                  pl.BlockSpec((tm, half), lambda i, p, e, n: (i, 0))],
        out_specs=pl.BlockSpec(memory_space=pl.ANY),
        scratch_shapes=[pltpu.SMEM((tm * TOP_K,), jnp.int32),
                        pltpu.SMEM((tm * TOP_K,), jnp.int32),
                        pltpu.VMEM((bm, half), jnp.uint32),
                        pltpu.SemaphoreType.DMA((2,)),
                        pltpu.SemaphoreType.DMA(()),
                        pltpu.SemaphoreType.DMA(())],
    )
    return pl.pallas_call(
        functools.partial(_dispatch_kernel, bm=bm),
        grid_spec=grid_spec,
        out_shape=jax.ShapeDtypeStruct((n_slots, half), jnp.uint32),
        compiler_params=_params(),
        name="dispatch",
    )(padded, pad_end, n_used, dest, hp)


EXPERT_SUB_ROWS = 256


def _expert_ffn_kernel(be_ref, nb_ref, x_ref, wgu_ref, bgu_ref, wd_ref, bd_ref, y_ref, wgu_s, wd_s):
    i = pl.program_id(0)
    d_ff = wd_ref.shape[0]
    half = x_ref.shape[1]

    @pl.when(i >= nb_ref[0])
    def _():
        y_ref[...] = jnp.zeros_like(y_ref)

    @pl.when(i < nb_ref[0])
    def _():
        changed = jnp.logical_or(i == 0, be_ref[i] != be_ref[jnp.maximum(i - 1, 0)])

        @pl.when(changed)
        def _():
            wgu_s[...] = wgu_ref[...].astype(BF16)
            wd_s[...] = wd_ref[...].astype(BF16)

        for r0 in range(0, x_ref.shape[0], EXPERT_SUB_ROWS):
            rows = slice(r0, r0 + EXPERT_SUB_ROWS)
            x_lo, x_hi = _unpack_pairs(x_ref[rows, :])
            gu = (_dot(x_lo.astype(BF16), wgu_s[:half, :]) + _dot(x_hi.astype(BF16), wgu_s[half:, :])
                  + bgu_ref[...])
            gate = jnp.minimum(gu[:, :d_ff], SWIGLU_LIMIT)
            up = jnp.clip(gu[:, d_ff:], -SWIGLU_LIMIT, SWIGLU_LIMIT)
            glu = gate * jax.nn.sigmoid(SWIGLU_ALPHA * gate)
            act = ((up + 1.0) * glu).astype(BF16)
            y = _dot(act, wd_s[...]) + bd_ref[...]
            y_ref[rows, :] = _pack_pairs(y.astype(BF16).astype(F32))


def _expert_ffn(block_expert, n_used, xs, w_gate_up, b_gate_up, w_down, b_down, bm):
    n_slots, half = xs.shape
    n_e, d, two_ff = w_gate_up.shape
    d_ff = two_ff // 2
    n_blocks = n_slots // bm

    def used(i, nb):
        return jnp.minimum(i, jnp.maximum(nb[0] - 1, 0))

    grid_spec = pltpu.PrefetchScalarGridSpec(
        num_scalar_prefetch=2,
        grid=(n_blocks,),
        in_specs=[pl.BlockSpec((bm, half), lambda i, be, nb: (used(i, nb), 0)),
                  pl.BlockSpec((None, d, two_ff), lambda i, be, nb: (be[i], 0, 0)),
                  pl.BlockSpec((None, 1, two_ff), lambda i, be, nb: (be[i], 0, 0)),
                  pl.BlockSpec((None, d_ff, d), lambda i, be, nb: (be[i], 0, 0)),
                  pl.BlockSpec((None, 1, d), lambda i, be, nb: (be[i], 0, 0))],
        out_specs=pl.BlockSpec((bm, half), lambda i, be, nb: (i, 0)),
        scratch_shapes=[pltpu.VMEM((d, two_ff), BF16), pltpu.VMEM((d_ff, d), BF16)],
    )
    return pl.pallas_call(
        _expert_ffn_kernel,
        grid_spec=grid_spec,
        out_shape=jax.ShapeDtypeStruct((n_slots, half), jnp.uint32),
        compiler_params=_params(),
        name="expert_ffn",
    )(block_expert, n_used, xs, w_gate_up, b_gate_up.reshape(n_e, 1, two_ff),
      w_down, b_down.reshape(n_e, 1, d))


COMBINE_LOCAL_ROWS = 2560
SEG_ALIGN = 8


def _pieces(limit):
    sizes = []
    size = 1 << (limit.bit_length() - 1)
    while size >= SEG_ALIGN:
        sizes.append(size)
        size //= 2
    return sizes


def _combine_kernel(src_ref, len_ref, dst_ref, tot_ref, route_ref, x1_ref, ys_hbm, o_ref, ybuf, sem_g):
    j = pl.program_id(0)
    blocks = pl.num_programs(0) - 1
    tm = x1_ref.shape[0]
    half = x1_ref.shape[1] // 2
    local_rows = ybuf.shape[1]

    @pl.when(j == 0)
    def _():
        ybuf[...] = jnp.zeros_like(ybuf)

    @pl.when(j < blocks)
    def _():
        for s in range(2):
            @pl.when(j % 2 == s)
            def _(s=s):
                for e in range(N_EXPERTS):
                    seg = j * N_EXPERTS + e
                    n = len_ref[seg]
                    src = src_ref[seg]
                    dst = dst_ref[seg]
                    for size in _pieces(tm + 2 * SEG_ALIGN):
                        @pl.when((n & size) != 0)
                        def _(size=size, src=src, dst=dst):
                            pltpu.make_async_copy(
                                ys_hbm.at[pl.ds(pl.multiple_of(src, SEG_ALIGN), size), :],
                                ybuf.at[s, pl.ds(pl.multiple_of(dst, SEG_ALIGN), size), :],
                                sem_g.at[s]).start()
                        step = jnp.where((n & size) != 0, size, 0)
                        src = src + step
                        dst = dst + step

    @pl.when(j >= 1)
    def _():
        slot = (j - 1) % 2
        total = tot_ref[j - 1]
        for size in _pieces(local_rows):
            @pl.when((total & size) != 0)
            def _(size=size):
                pltpu.make_async_copy(ys_hbm.at[pl.ds(0, size), :], ybuf.at[slot, pl.ds(0, size), :],
                                      sem_g.at[slot]).wait()

        route = route_ref[...]
        col = lax.broadcasted_iota(jnp.int32, (tm, local_rows), 1).astype(F32)
        g = jnp.zeros((tm, local_rows), F32)
        for k in range(TOP_K):
            pos = route[:, TOP_K + k: TOP_K + k + 1]
            gate = route[:, 2 * TOP_K + k: 2 * TOP_K + k + 1]
            g = jnp.where(col == pos, gate, g)
        g = g.astype(BF16)
        lo, hi = _unpack_pairs(ybuf[slot])
        o_ref[:, :half] = x1_ref[:, :half] + _dot(g, lo.astype(BF16))
        o_ref[:, half:] = x1_ref[:, half:] + _dot(g, hi.astype(BF16))


def _combine(tables, route, x1, ys, tm, local_rows):
    t, d = x1.shape
    half = d // 2

    def summed(j, *_):
        return (jnp.maximum(j - 1, 0), 0)

    grid_spec = pltpu.PrefetchScalarGridSpec(
        num_scalar_prefetch=4,
        grid=(t // tm + 1,),
        in_specs=[pl.BlockSpec((tm, LANES), summed),
                  pl.BlockSpec((tm, d), summed),
                  pl.BlockSpec(memory_space=pl.ANY)],
        out_specs=pl.BlockSpec((tm, d), summed),
        scratch_shapes=[pltpu.VMEM((2, local_rows, half), jnp.uint32),
                        pltpu.SemaphoreType.DMA((2,))],
    )
    return pl.pallas_call(
        _combine_kernel,
        grid_spec=grid_spec,
        out_shape=jax.ShapeDtypeStruct((t, d), F32),
        compiler_params=_params(),
        name="combine",
    )(*tables, route, x1, ys)


def _combine_tables(route, pad_start, tm, local_rows):
    t = route.shape[0]
    blocks = t // tm
    idx = route[:, :TOP_K].astype(jnp.int32).reshape(blocks, tm * TOP_K)
    experts = jnp.arange(N_EXPERTS, dtype=jnp.int32)
    n = jnp.sum(idx[:, :, None] == experts[None, None, :], axis=1).astype(jnp.int32)
    before = jnp.cumsum(n, axis=0) - n
    first = pad_start[None, :] + before
    src = first // SEG_ALIGN * SEG_ALIGN
    length = jnp.where(n > 0, (first + n + SEG_ALIGN - 1) // SEG_ALIGN * SEG_ALIGN - src, 0)
    dst = jnp.cumsum(length, axis=1) - length
    total = jnp.sum(length, axis=1)
    assert local_rows >= tm * TOP_K + N_EXPERTS * 2 * (SEG_ALIGN - 1)
    base = dst + first - src - before
    lookup = jnp.sum(jnp.where(idx[:, :, None] == experts[None, None, :], base[:, None, :], 0), axis=2)
    local = lookup.reshape(t, TOP_K).astype(F32) + route[:, TOP_K:2 * TOP_K]
    route = jnp.concatenate([route[:, :TOP_K], local, route[:, 2 * TOP_K:]], axis=1)
    tables = (src.reshape(-1).astype(jnp.int32), length.reshape(-1).astype(jnp.int32),
              dst.reshape(-1).astype(jnp.int32), total.astype(jnp.int32))
    return tables, route


def _slot_layout(route, counts, tm, bm):
    t = route.shape[0]
    counts = counts.astype(jnp.int32)
    padded = (counts + bm - 1) // bm * bm
    pad_end = jnp.cumsum(padded).astype(jnp.int32)
    pad_start = pad_end - padded
    idx = route[:, :TOP_K].astype(jnp.int32)
    rank = route[:, TOP_K:2 * TOP_K].astype(jnp.int32)
    dest = (jnp.take(pad_start, idx) + rank).reshape(t // tm, tm * TOP_K)
    n_blocks = -(-t * TOP_K // bm) + N_EXPERTS
    block_start = jnp.arange(n_blocks, dtype=jnp.int32) * bm
    block_expert = jnp.minimum(jnp.sum(pad_end[None, :] <= block_start[:, None], axis=1),
                               N_EXPERTS - 1).astype(jnp.int32)
    n_used = (pad_end[-1:] // bm).astype(jnp.int32)
    return padded, pad_end, dest, block_expert, n_used, n_blocks * bm


def _rot_cols(w):
    half = w.shape[-1] // 2
    return jnp.concatenate([-w[..., half:], w[..., :half]], axis=-1)


def _swap_halves(g):
    half = g.shape[-1] // 2
    return jnp.concatenate([g[..., half:], g[..., :half]], axis=-1)


def kernel(x, attn_norm_g, w_in, a_q_norm_g, a_k_norm_g, rel_bias, q_a_norm_g, w_q_b, kv_a_norm_g,
           w_kv_b, b_q_norm_g, b_k_norm_g, w_out, ffn_norm_g, router_w, router_b, w_gate_up,
           b_gate_up, w_down, b_down):
    batch, seq, d = x.shape
    t = batch * seq
    depth = w_in.shape[0]
    tm = 512
    expert_bm = 512

    pos = jnp.arange(seq, dtype=F32)
    inv_freq = ROPE_THETA ** (-jnp.arange(0, QK_ROPE_DIM, 2, dtype=F32) / QK_ROPE_DIM)
    ang = pos[:, None] * inv_freq[None, :]
    cos, sin = jnp.cos(ang), jnp.sin(ang)
    cs = jnp.concatenate([cos, cos, sin, sin], axis=-1)

    row = jnp.arange(LANES)[:, None] // A_HEAD_DIM
    col = jnp.arange(LANES)[None, :] // A_HEAD_DIM
    head_block_ones = (row == col).astype(BF16)
    band_bias = _band_bias(rel_bias)

    x2 = x.reshape(t, d)
    for layer in range(depth):
        kpe_off = 3 * A_WIDTH + Q_LORA_RANK + KV_LORA_RANK
        w_kpe = w_in[layer][:, kpe_off:]
        w_in_r = jnp.concatenate([w_in[layer][:, :kpe_off], w_kpe, _rot_cols(w_kpe)], axis=1).astype(BF16)

        wq = w_q_b[layer].reshape(Q_LORA_RANK, B_HEADS, QK_HEAD_DIM)
        wq_rope = wq[..., QK_NOPE_DIM:]
        wq_r = jnp.concatenate([wq, _rot_cols(wq_rope)], axis=-1).reshape(Q_LORA_RANK, -1).astype(BF16)
        wkv = w_kv_b[layer].reshape(KV_LORA_RANK, B_HEADS, QK_NOPE_DIM + V_HEAD_DIM)
        wkv_r = jnp.concatenate([wkv[..., :QK_NOPE_DIM].reshape(KV_LORA_RANK, -1),
                                 wkv[..., QK_NOPE_DIM:].reshape(KV_LORA_RANK, -1)], axis=1).astype(BF16)
        gc = jnp.concatenate([q_a_norm_g[layer], kv_a_norm_g[layer]])[None, :]
        gq, gk = b_q_norm_g[layer], b_k_norm_g[layer]

        def rope_gain(gr):
            return jnp.concatenate([gr, _swap_halves(gr)])

        gqk = jnp.stack([gq[:QK_NOPE_DIM], rope_gain(gq[QK_NOPE_DIM:]),
                         gk[:QK_NOPE_DIM], rope_gain(gk[QK_NOPE_DIM:])])
        g_a = jnp.stack([jnp.tile(a_q_norm_g[layer], 2), jnp.tile(a_k_norm_g[layer], 2)])
        g_a = jnp.broadcast_to(g_a[None], (A_WIDTH // LANES, 2, LANES))

        proj = _in_proj(x2, attn_norm_g[layer][None, :], w_in_r, tm)
        qb, kb, vb = _mla_prep(proj, cs, wq_r, wkv_r, gc, gqk, seq, tm)
        out_b = _mla_attention(qb, kb, vb, batch, seq, 256)
        out_a = _dilated_attention(proj, head_block_ones, g_a, band_bias, batch, seq)

        rw = jnp.pad(router_w[layer], ((0, 0), (0, LANES - N_EXPERTS)))
        rw_hi = rw.astype(BF16)
        rw_lo = (rw - rw_hi.astype(F32)).astype(BF16)
        rb = jnp.pad(router_b[layer], (0, LANES - N_EXPERTS), constant_values=-jnp.inf)[None, :]
        x1, hp, route, counts = _out_proj(x2, out_a, out_b, w_out[layer].astype(BF16),
                                          ffn_norm_g[layer][None, :], jnp.stack([rw_hi, rw_lo]), rb, tm)

        padded, pad_end, dest, block_expert, n_used, n_slots = _slot_layout(
            route, counts[0, :N_EXPERTS], tm, expert_bm)
        xs = _dispatch(padded, pad_end, n_used, dest, hp, n_slots, tm, expert_bm)
        ys = _expert_ffn(block_expert, n_used, xs, w_gate_up[layer], b_gate_up[layer],
                         w_down[layer], b_down[layer], expert_bm)
        tables, route_local = _combine_tables(route, pad_end - padded, tm, COMBINE_LOCAL_ROWS)
        x2 = _combine(tables, route_local, x1, ys, tm, COMBINE_LOCAL_ROWS)
    return x2.reshape(batch, seq, d)
```

```python
import functools
import math

import jax
import jax.numpy as jnp
from jax import lax
from jax.experimental import pallas as pl
from jax.experimental.pallas import tpu as pltpu

A_HEADS = 8
A_HEAD_DIM = 64
A_WIDTH = A_HEADS * A_HEAD_DIM
DILATED_PATTERNS = ((128, 1), (512, 4), (2048, 16))
BAND_BLOCK = 128

B_HEADS = 4
QK_NOPE_DIM = 128
QK_ROPE_DIM = 64
QK_HEAD_DIM = QK_NOPE_DIM + QK_ROPE_DIM
V_HEAD_DIM = 128
Q_LORA_RANK = 256
KV_LORA_RANK = 256
B_WIDTH = B_HEADS * V_HEAD_DIM
ROPE_THETA = 10000.0

NUM_BUCKETS = 32
MAX_DISTANCE = 2048

N_EXPERTS = 32
TOP_K = 4
SWIGLU_LIMIT = 7.0
SWIGLU_ALPHA = 1.702
RMS_EPS = 1e-6

LANES = 128
MASK_VALUE = -1e30
LOG2E = math.log2(math.e)
PROJ_WIDTH = 3 * A_WIDTH + Q_LORA_RANK + KV_LORA_RANK + 2 * QK_ROPE_DIM
VMEM_LIMIT = 56 * 1024 * 1024

F32 = jnp.float32
BF16 = jnp.bfloat16


def _dot(a, b):
    return jnp.dot(a, b, preferred_element_type=F32)


def _dot_nt(a, b):
    return lax.dot_general(a, b, (((1,), (1,)), ((), ())), preferred_element_type=F32)


def _split_dot(x, m):
    hi = x.astype(BF16)
    lo = (x - hi.astype(F32)).astype(BF16)
    return _dot(hi, m) + _dot(lo, m)


def _params(n_parallel=1):
    return pltpu.CompilerParams(
        dimension_semantics=("arbitrary",) * n_parallel, vmem_limit_bytes=VMEM_LIMIT)


def _in_proj_kernel(x_ref, g_ref, w_ref, o_ref):
    x = x_ref[...]
    ms = jnp.mean(x * x, axis=-1, keepdims=True)
    h = (x * lax.rsqrt(ms + RMS_EPS) * g_ref[...]).astype(BF16)
    o_ref[...] = _dot(h, w_ref[...])


def _in_proj(x2, g, w, tm):
    t, d = x2.shape
    n = w.shape[1]
    return pl.pallas_call(
        _in_proj_kernel,
        grid=(t // tm,),
        in_specs=[pl.BlockSpec((tm, d), lambda i: (i, 0)),
                  pl.BlockSpec((1, d), lambda i: (0, 0)),
                  pl.BlockSpec((d, n), lambda i: (0, 0))],
        out_specs=pl.BlockSpec((tm, n), lambda i: (i, 0)),
        out_shape=jax.ShapeDtypeStruct((t, n), F32),
        compiler_params=_params(),
        name="in_proj",
    )(x2, g, w)


def _mla_prep_kernel(c_ref, kpe_ref, cs_ref, wq_ref, wkv_ref, gc_ref, gqk_ref, q_ref, k_ref, v_ref,
                     *, q_scale):
    c = c_ref[...]
    gc = gc_ref[...]

    def lora_norm(z, g):
        ms = jnp.mean(z * z, axis=-1, keepdims=True)
        return (z * lax.rsqrt(ms + RMS_EPS) * g).astype(BF16)

    cq = lora_norm(c[:, :Q_LORA_RANK], gc[:, :Q_LORA_RANK])
    ckv = lora_norm(c[:, Q_LORA_RANK:], gc[:, Q_LORA_RANK:])
    qb = _dot(cq, wq_ref[...])
    kvb = _dot(ckv, wkv_ref[...])
    kper = kpe_ref[...]
    cs = cs_ref[...]
    gqk = gqk_ref[...]
    tm = c.shape[0]

    row = lax.broadcasted_iota(jnp.int32, (LANES, LANES), 0)
    ones_all = jnp.ones((LANES, LANES), BF16)
    ones_lo = jnp.where(row < QK_ROPE_DIM, 1.0, 0.0).astype(BF16)
    lane = lax.broadcasted_iota(jnp.int32, (tm, LANES), 1)

    def rope(z, g_row):
        t = z * (g_row * cs)
        return jnp.where(lane < QK_ROPE_DIM, t + pltpu.roll(t, QK_ROPE_DIM, 1), 0.0)

    k_rope = rope(kper, gqk[3:4])
    kpe_ss = _split_dot(kper * kper, ones_lo)
    for h in range(B_HEADS):
        qn = qb[:, 2 * LANES * h: 2 * LANES * h + LANES]
        qr = qb[:, 2 * LANES * h + LANES: 2 * LANES * (h + 1)]
        ss = _split_dot(qn * qn, ones_all) + _split_dot(qr * qr, ones_lo)
        rs = lax.rsqrt(ss * (1.0 / QK_HEAD_DIM) + RMS_EPS) * q_scale
        q_ref[h, :, :LANES] = (qn * gqk[0:1] * rs).astype(BF16)
        q_ref[h, :, LANES:] = (rope(qr, gqk[1:2]) * rs).astype(BF16)
        kn = kvb[:, LANES * h: LANES * (h + 1)]
        ssk = _split_dot(kn * kn, ones_all) + kpe_ss
        rsk = lax.rsqrt(ssk * (1.0 / QK_HEAD_DIM) + RMS_EPS)
        k_ref[h, :, :LANES] = (kn * gqk[2:3] * rsk).astype(BF16)
        k_ref[h, :, LANES:] = (k_rope * rsk).astype(BF16)
    v_ref[...] = kvb[:, B_HEADS * QK_NOPE_DIM:].astype(BF16)


def _mla_prep(proj, cs, wq, wkv, gc, gqk, seq, tm):
    t = proj.shape[0]
    c_width = Q_LORA_RANK + KV_LORA_RANK
    c_block = 3 * A_WIDTH // c_width
    kpe_block = (3 * A_WIDTH + c_width) // LANES
    pos_blocks = seq // tm
    kernel = functools.partial(_mla_prep_kernel, q_scale=QK_HEAD_DIM ** -0.5 * LOG2E)
    return pl.pallas_call(
        kernel,
        grid=(t // tm,),
        in_specs=[pl.BlockSpec((tm, c_width), lambda i: (i, c_block)),
                  pl.BlockSpec((tm, LANES), lambda i: (i, kpe_block)),
                  pl.BlockSpec((tm, LANES), lambda i: (i % pos_blocks, 0)),
                  pl.BlockSpec(wq.shape, lambda i: (0, 0)),
                  pl.BlockSpec(wkv.shape, lambda i: (0, 0)),
                  pl.BlockSpec(gc.shape, lambda i: (0, 0)),
                  pl.BlockSpec(gqk.shape, lambda i: (0, 0))],
        out_specs=[pl.BlockSpec((B_HEADS, tm, 2 * LANES), lambda i: (0, i, 0)),
                   pl.BlockSpec((B_HEADS, tm, 2 * LANES), lambda i: (0, i, 0)),
                   pl.BlockSpec((tm, B_WIDTH), lambda i: (i, 0))],
        out_shape=[jax.ShapeDtypeStruct((B_HEADS, t, 2 * LANES), BF16),
                   jax.ShapeDtypeStruct((B_HEADS, t, 2 * LANES), BF16),
                   jax.ShapeDtypeStruct((t, B_WIDTH), BF16)],
        compiler_params=_params(),
        name="mla_prep",
    )(proj, proj, cs, wq, wkv, gc, gqk)


def _mla_attention_kernel(q_ref, k_ref, v_ref, o_ref, *, tq):
    seq = q_ref.shape[0]
    row = lax.broadcasted_iota(jnp.int32, (tq, tq), 0)
    col = lax.broadcasted_iota(jnp.int32, (tq, tq), 1)
    diag_mask = jnp.where(col <= row, 0.0, MASK_VALUE)
    v_ext = jnp.concatenate([v_ref[...], jnp.ones(v_ref.shape, v_ref.dtype)], axis=1)
    dv = v_ref.shape[1]
    for i in range(seq // tq):
        q = q_ref[i * tq:(i + 1) * tq, :]
        s_diag = _dot_nt(q, k_ref[i * tq:(i + 1) * tq, :]) + diag_mask
        m = jnp.max(s_diag, axis=-1, keepdims=True)
        if i > 0:
            s_past = _dot_nt(q, k_ref[:i * tq, :])
            m = jnp.maximum(m, jnp.max(s_past, axis=-1, keepdims=True))
        o = _dot(jnp.exp2(s_diag - m).astype(BF16), v_ext[i * tq:(i + 1) * tq, :])
        if i > 0:
            o = o + _dot(jnp.exp2(s_past - m).astype(BF16), v_ext[:i * tq, :])
        o_ref[i * tq:(i + 1) * tq, :] = (o[:, :dv] / o[:, dv:]).astype(o_ref.dtype)


def _mla_attention(qb, kb, vb, batch, seq, tq):
    t = vb.shape[0]
    kernel = functools.partial(_mla_attention_kernel, tq=tq)
    return pl.pallas_call(
        kernel,
        grid=(batch, B_HEADS),
        in_specs=[pl.BlockSpec((None, seq, 2 * LANES), lambda b, h: (h, b, 0)),
                  pl.BlockSpec((None, seq, 2 * LANES), lambda b, h: (h, b, 0)),
                  pl.BlockSpec((seq, V_HEAD_DIM), lambda b, h: (b, h))],
        out_specs=pl.BlockSpec((seq, V_HEAD_DIM), lambda b, h: (b, h)),
        out_shape=jax.ShapeDtypeStruct((t, B_WIDTH), BF16),
        compiler_params=_params(2),
        name="mla_attention",
    )(qb, kb, vb)


def _dilated_kernel(q_ref, k_ref, v_ref, bd_ref, g_ref, bias_ref, o_ref, qs, ks, acc_o, acc_l, acc_m,
                    *, unroll):
    seq = q_ref.shape[0]
    n_tiles = seq // BAND_BLOCK
    low = lax.broadcasted_iota(jnp.int32, (BAND_BLOCK, LANES), 1) < A_HEAD_DIM
    bd = bd_ref[...]
    g = g_ref[...]

    def head_norm(z, g_row):
        ss = _split_dot(z * z, bd)
        return z * lax.rsqrt(ss * (1.0 / A_HEAD_DIM) + RMS_EPS) * g_row

    qs[...] = head_norm(q_ref[...], g[0:1]) * (A_HEAD_DIM ** -0.5 * LOG2E)
    ks[...] = head_norm(k_ref[...], g[1:2])
    ones = jnp.ones((2 * BAND_BLOCK, LANES), BF16)

    for p, (window, dil) in enumerate(DILATED_PATTERNS):
        blocks_per_class = n_tiles // dil
        span = BAND_BLOCK * dil

        def rows_at(start, dil=dil):
            if dil == 1:
                return pl.ds(start, BAND_BLOCK)
            return pl.ds(start, BAND_BLOCK, stride=dil)

        def tile(i, carry, p=p, blocks_per_class=blocks_per_class, span=span, rows_at=rows_at):
            r = i // blocks_per_class
            j = i % blocks_per_class
            cur = rows_at(r + j * span)
            prev = rows_at(r + jnp.maximum(j - 1, 0) * span)
            q = qs[cur, :]
            q2 = jnp.concatenate([jnp.where(low, q, 0.0), jnp.where(low, 0.0, q)], axis=0).astype(BF16)
            k_band = jnp.concatenate([ks[prev, :], ks[cur, :]], axis=0).astype(BF16)
            s = _dot_nt(q2, k_band) + bias_ref[p, jnp.where(j == 0, 1, 0)]
            m = jnp.max(s, axis=-1, keepdims=True)
            pr = jnp.exp2(s - m).astype(BF16)
            v_band = jnp.concatenate([v_ref[prev, :], v_ref[cur, :]], axis=0).astype(BF16)
            o = _dot(pr, jnp.concatenate([v_band, ones], axis=1))
            top, bot = o[:BAND_BLOCK], o[BAND_BLOCK:]
            acc_o[p, cur, :] = jnp.where(low, top[:, :LANES], bot[:, :LANES])
            acc_l[p, cur, :] = jnp.where(low, top[:, LANES:], bot[:, LANES:])
            acc_m[p, cur, :] = jnp.where(low, m[:BAND_BLOCK], m[BAND_BLOCK:])
            return carry

        lax.fori_loop(0, n_tiles, tile, 0, unroll=unroll)

    m_all = jnp.maximum(jnp.maximum(acc_m[0], acc_m[1]), acc_m[2])
    num = jnp.zeros((seq, LANES), F32)
    den = jnp.zeros((seq, LANES), F32)
    for p in range(len(DILATED_PATTERNS)):
        w = jnp.exp2(acc_m[p] - m_all)
        num = num + w * acc_o[p]
        den = den + w * acc_l[p]
    o_ref[...] = (num / den).astype(o_ref.dtype)


def _dilated_attention(proj, bd, g, bias, batch, seq):
    t = proj.shape[0]
    pairs = A_WIDTH // LANES
    n_pat = len(DILATED_PATTERNS)
    return pl.pallas_call(
        functools.partial(_dilated_kernel, unroll=16),
        grid=(batch, pairs),
        in_specs=[pl.BlockSpec((seq, LANES), lambda b, c: (b, c)),
                  pl.BlockSpec((seq, LANES), lambda b, c: (b, pairs + c)),
                  pl.BlockSpec((seq, LANES), lambda b, c: (b, 2 * pairs + c)),
                  pl.BlockSpec((LANES, LANES), lambda b, c: (0, 0)),
                  pl.BlockSpec((None, 2, LANES), lambda b, c: (c, 0, 0)),
                  pl.BlockSpec((None, n_pat, 2, 2 * BAND_BLOCK, 2 * BAND_BLOCK),
                               lambda b, c: (c, 0, 0, 0, 0))],
        out_specs=pl.BlockSpec((seq, LANES), lambda b, c: (b, c)),
        out_shape=jax.ShapeDtypeStruct((t, A_WIDTH), BF16),
        scratch_shapes=[pltpu.VMEM((seq, LANES), F32)] * 2
                       + [pltpu.VMEM((n_pat, seq, LANES), F32)] * 3,
        compiler_params=_params(2),
        name="dilated_attention",
    )(proj, proj, proj, bd, g, bias)


def _t5_bucket(dist):
    max_exact = NUM_BUCKETS // 2
    df = jnp.maximum(dist, 1).astype(F32)
    log_bucket = max_exact + (jnp.log(df / max_exact) / math.log(MAX_DISTANCE / max_exact)
                              * (NUM_BUCKETS - max_exact)).astype(jnp.int32)
    log_bucket = jnp.minimum(log_bucket, NUM_BUCKETS - 1)
    return jnp.where(dist < max_exact, dist, log_bucket)


def _band_bias(rel_bias):
    n = BAND_BLOCK
    qi = jnp.arange(n)[:, None]
    kj = jnp.arange(n)[None, :]
    buckets = jnp.arange(NUM_BUCKETS)
    tables = []
    for window, dil in DILATED_PATTERNS:
        steps = window // dil
        halves = []
        for back in (qi - kj + n, qi - kj):
            onehot = (_t5_bucket(jnp.maximum(back, 0) * dil)[:, :, None] == buckets).astype(F32)
            vals = jnp.einsum('qkb,bh->hqk', onehot, rel_bias.astype(F32),
                              precision=lax.Precision.HIGHEST) * LOG2E
            halves.append(jnp.where(((back >= 0) & (back <= steps))[None], vals, MASK_VALUE))
        prev, cur = halves
        normal = jnp.concatenate([prev, cur], axis=-1)
        first = jnp.concatenate([jnp.full_like(prev, MASK_VALUE), cur], axis=-1)
        tables.append(jnp.stack([normal, first], axis=1))
    tab = jnp.stack(tables, axis=1)
    tab = tab.reshape(A_HEADS // 2, 2, len(DILATED_PATTERNS), 2, n, 2 * n)
    return tab.transpose(0, 2, 3, 1, 4, 5).reshape(A_HEADS // 2, len(DILATED_PATTERNS), 2, 2 * n, 2 * n)


def _pack_pairs(v):
    half = v.shape[1] // 2
    bits = pltpu.bitcast(v, jnp.uint32)
    return (bits[:, :half] >> 16) | (bits[:, half:] & jnp.uint32(0xFFFF0000))


def _unpack_pairs(p):
    return (pltpu.bitcast(p << 16, F32), pltpu.bitcast(p & jnp.uint32(0xFFFF0000), F32))


def _out_proj_kernel(x_ref, a_ref, b_ref, w_ref, g_ref, rw_ref, rb_ref,
                     x1_ref, hp_ref, route_ref, cnt_ref, carry):
    i = pl.program_id(0)
    tm = x_ref.shape[0]

    @pl.when(i == 0)
    def _():
        carry[...] = jnp.zeros_like(carry)

    w = w_ref[...]
    x1 = x_ref[...] + _dot(a_ref[...], w[:A_WIDTH]) + _dot(b_ref[...], w[A_WIDTH:])
    x1_ref[...] = x1
    ms = jnp.mean(x1 * x1, axis=-1, keepdims=True)
    h = x1 * lax.rsqrt(ms + RMS_EPS) * g_ref[...]
    hi = h.astype(BF16)
    hi_f = hi.astype(F32)
    hp_ref[...] = _pack_pairs(hi_f)
    lo = (h - hi_f).astype(BF16)
    rw = rw_ref[...]
    hw = _dot(hi, rw)
    logits = hw[:, :LANES] + hw[:, LANES:] + _dot(lo, rw[:, :LANES]) + rb_ref[...]

    lane = lax.broadcasted_iota(jnp.int32, (tm, LANES), 1).astype(F32)
    remaining = logits
    vals, hots = [], []
    for _ in range(TOP_K):
        m = jnp.max(remaining, axis=-1, keepdims=True)
        first = jnp.min(jnp.where(remaining == m, lane, float(LANES)), axis=-1, keepdims=True)
        hot = lane == first
        remaining = jnp.where(hot, -jnp.inf, remaining)
        vals.append(m)
        hots.append(hot)
    exps = [jnp.exp(v - vals[0]) for v in vals]
    den = exps[0] + exps[1] + exps[2] + exps[3]

    chosen = jnp.zeros((tm, LANES), F32)
    for hot in hots:
        chosen = chosen + jnp.where(hot, 1.0, 0.0)
    r = lax.broadcasted_iota(jnp.int32, (tm, tm), 0)
    c = lax.broadcasted_iota(jnp.int32, (tm, tm), 1)
    earlier = jnp.where(r > c, 1.0, 0.0).astype(BF16)
    before = carry[...] + _dot(earlier, chosen.astype(BF16))
    carry[...] = carry[...] + jnp.sum(chosen, axis=0, keepdims=True)
    cnt_ref[...] = jnp.broadcast_to(carry[...], cnt_ref.shape)

    route = jnp.zeros((tm, LANES), F32)
    for k in range(TOP_K):
        first = jnp.sum(jnp.where(hots[k], lane, 0.0), axis=-1, keepdims=True)
        rank = jnp.sum(jnp.where(hots[k], before, 0.0), axis=-1, keepdims=True)
        route = route + jnp.where(lane == float(k), first, 0.0)
        route = route + jnp.where(lane == float(TOP_K + k), rank, 0.0)
        route = route + jnp.where(lane == float(2 * TOP_K + k), exps[k] / den, 0.0)
    route_ref[...] = route


def _out_proj(x2, out_a, out_b, w, g, rw, rb, tm):
    t, d = x2.shape
    return pl.pallas_call(
        _out_proj_kernel,
        grid=(t // tm,),
        in_specs=[pl.BlockSpec((tm, d), lambda i: (i, 0)),
                  pl.BlockSpec((tm, A_WIDTH), lambda i: (i, 0)),
                  pl.BlockSpec((tm, B_WIDTH), lambda i: (i, 0)),
                  pl.BlockSpec(w.shape, lambda i: (0, 0)),
                  pl.BlockSpec((1, d), lambda i: (0, 0)),
                  pl.BlockSpec(rw.shape, lambda i: (0, 0)),
                  pl.BlockSpec((1, LANES), lambda i: (0, 0))],
        out_specs=[pl.BlockSpec((tm, d), lambda i: (i, 0)),
                   pl.BlockSpec((tm, d // 2), lambda i: (i, 0)),
                   pl.BlockSpec((tm, LANES), lambda i: (i, 0)),
                   pl.BlockSpec((8, LANES), lambda i: (0, 0))],
        out_shape=[jax.ShapeDtypeStruct((t, d), F32),
                   jax.ShapeDtypeStruct((t, d // 2), jnp.uint32),
                   jax.ShapeDtypeStruct((t, LANES), F32),
                   jax.ShapeDtypeStruct((8, LANES), F32)],
        scratch_shapes=[pltpu.VMEM((1, LANES), F32)],
        compiler_params=_params(),
        name="out_proj",
    )(x2, out_a, out_b, w, g, rw, rb)


def _dispatch_kernel(pad_ref, end_ref, nb_ref, dest_hbm, h_ref, xs_hbm, dest_s0, dest_s1, zeros,
                     sem_i, sem_z, sem_o, *, bm):
    i = pl.program_id(0)
    steps = pl.num_programs(0)
    tm = h_ref.shape[0]
    n_blocks = xs_hbm.shape[0] // bm
    dest_s = (dest_s0, dest_s1)

    def idx_copy(step, slot):
        return pltpu.make_async_copy(dest_hbm.at[step], dest_s[slot], sem_i.at[slot])

    @pl.when(i == 0)
    def _():
        idx_copy(0, 0).start()
        zeros[...] = jnp.zeros_like(zeros)

        def zero_block(start):
            return pltpu.make_async_copy(zeros, xs_hbm.at[pl.ds(pl.multiple_of(start, bm), bm), :], sem_z)

        for e in range(N_EXPERTS):
            @pl.when(pad_ref[e] > 0)
            def _():
                zero_block(end_ref[e] - bm).start()

        def start_unused(b, carry):
            zero_block(b * bm).start()
            return carry

        def wait_unused(b, carry):
            zero_block(b * bm).wait()
            return carry

        lax.fori_loop(nb_ref[0], n_blocks, start_unused, 0)
        for e in range(N_EXPERTS):
            @pl.when(pad_ref[e] > 0)
            def _():
                zero_block(end_ref[e] - bm).wait()
        lax.fori_loop(nb_ref[0], n_blocks, wait_unused, 0)

    for slot in range(2):
        @pl.when(i % 2 == slot)
        def _(slot=slot):
            @pl.when(i + 1 < steps)
            def _():
                idx_copy(i + 1, 1 - slot).start()

            idx_copy(i, slot).wait()

            def body(t, carry):
                for k in range(TOP_K):
                    pltpu.make_async_copy(h_ref.at[pl.ds(t, 1), :],
                                          xs_hbm.at[pl.ds(dest_s[slot][t * TOP_K + k], 1), :],
                                          sem_o).start(priority=k % 2)
                return carry

            lax.fori_loop(0, tm, body, 0, unroll=8)
    for _ in range(TOP_K):
        pltpu.make_async_copy(h_ref, xs_hbm.at[pl.ds(0, tm), :], sem_o).wait()


def _dispatch(padded, pad_end, n_used, dest, hp, n_slots, tm, bm):
    t, half = hp.shape
    grid_spec = pltpu.PrefetchScalarGridSpec(
        num_scalar_prefetch=3,
        grid=(t // tm,),
        in_specs=[pl.BlockSpec(memory_space=pl.ANY),
                  pl.BlockSpec((tm, half), lambda i, p, e, n: (i, 0))],
        out_specs=pl.BlockSpec(memory_space=pl.ANY),
        scratch_shapes=[pltpu.SMEM((tm * TOP_K,), jnp.int32),
                        pltpu.SMEM((tm * TOP_K,), jnp.int32),
                        pltpu.VMEM((bm, half), jnp.uint32),
                        pltpu.SemaphoreType.DMA((2,)),
                        pltpu.SemaphoreType.DMA(()),
                        pltpu.SemaphoreType.DMA(())],
    )
    return pl.pallas_call(
        functools.partial(_dispatch_kernel, bm=bm),
        grid_spec=grid_spec,
        out_shape=jax.ShapeDtypeStruct((n_slots, half), jnp.uint32),
        compiler_params=_params(),
        name="dispatch",
    )(padded, pad_end, n_used, dest, hp)


EXPERT_SUB_ROWS = 256


def _expert_ffn_kernel(be_ref, nb_ref, x_ref, wgu_ref, bgu_ref, wd_ref, bd_ref, y_ref, wgu_s, wd_s):
    i = pl.program_id(0)
    d_ff = wd_ref.shape[0]
    half = x_ref.shape[1]

    @pl.when(i >= nb_ref[0])
    def _():
        y_ref[...] = jnp.zeros_like(y_ref)

    @pl.when(i < nb_ref[0])
    def _():
        changed = jnp.logical_or(i == 0, be_ref[i] != be_ref[jnp.maximum(i - 1, 0)])

        @pl.when(changed)
        def _():
            wgu_s[...] = wgu_ref[...].astype(BF16)
            wd_s[...] = wd_ref[...].astype(BF16)

        for r0 in range(0, x_ref.shape[0], EXPERT_SUB_ROWS):
            rows = slice(r0, r0 + EXPERT_SUB_ROWS)
            x_lo, x_hi = _unpack_pairs(x_ref[rows, :])
            gu = (_dot(x_lo.astype(BF16), wgu_s[:half, :]) + _dot(x_hi.astype(BF16), wgu_s[half:, :])
                  + bgu_ref[...])
            gate = jnp.minimum(gu[:, :d_ff], SWIGLU_LIMIT)
            up = jnp.clip(gu[:, d_ff:], -SWIGLU_LIMIT, SWIGLU_LIMIT)
            glu = gate * jax.nn.sigmoid(SWIGLU_ALPHA * gate)
            act = ((up + 1.0) * glu).astype(BF16)
            y = _dot(act, wd_s[...]) + bd_ref[...]
            y_ref[rows, :] = _pack_pairs(y.astype(BF16).astype(F32))


def _expert_ffn(block_expert, n_used, xs, w_gate_up, b_gate_up, w_down, b_down, bm):
    n_slots, half = xs.shape
    n_e, d, two_ff = w_gate_up.shape
    d_ff = two_ff // 2
    n_blocks = n_slots // bm

    def used(i, nb):
        return jnp.minimum(i, jnp.maximum(nb[0] - 1, 0))

    grid_spec = pltpu.PrefetchScalarGridSpec(
        num_scalar_prefetch=2,
        grid=(n_blocks,),
        in_specs=[pl.BlockSpec((bm, half), lambda i, be, nb: (used(i, nb), 0)),
                  pl.BlockSpec((None, d, two_ff), lambda i, be, nb: (be[i], 0, 0)),
                  pl.BlockSpec((None, 1, two_ff), lambda i, be, nb: (be[i], 0, 0)),
                  pl.BlockSpec((None, d_ff, d), lambda i, be, nb: (be[i], 0, 0)),
                  pl.BlockSpec((None, 1, d), lambda i, be, nb: (be[i], 0, 0))],
        out_specs=pl.BlockSpec((bm, half), lambda i, be, nb: (i, 0)),
        scratch_shapes=[pltpu.VMEM((d, two_ff), BF16), pltpu.VMEM((d_ff, d), BF16)],
    )
    return pl.pallas_call(
        _expert_ffn_kernel,
        grid_spec=grid_spec,
        out_shape=jax.ShapeDtypeStruct((n_slots, half), jnp.uint32),
        compiler_params=_params(),
        name="expert_ffn",
    )(block_expert, n_used, xs, w_gate_up, b_gate_up.reshape(n_e, 1, two_ff),
      w_down, b_down.reshape(n_e, 1, d))


COMBINE_LOCAL_ROWS = 2560
SEG_ALIGN = 8


def _pieces(limit):
    sizes = []
    size = 1 << (limit.bit_length() - 1)
    while size >= SEG_ALIGN:
        sizes.append(size)
        size //= 2
    return sizes


def _combine_kernel(src_ref, len_ref, dst_ref, tot_ref, route_ref, x1_ref, ys_hbm, o_ref, ybuf, sem_g):
    j = pl.program_id(0)
    blocks = pl.num_programs(0) - 1
    tm = x1_ref.shape[0]
    half = x1_ref.shape[1] // 2
    local_rows = ybuf.shape[1]

    @pl.when(j == 0)
    def _():
        ybuf[...] = jnp.zeros_like(ybuf)

    @pl.when(j < blocks)
    def _():
        for s in range(2):
            @pl.when(j % 2 == s)
            def _(s=s):
                for e in range(N_EXPERTS):
                    seg = j * N_EXPERTS + e
                    n = len_ref[seg]
                    src = src_ref[seg]
                    dst = dst_ref[seg]
                    for size in _pieces(tm + 2 * SEG_ALIGN):
                        @pl.when((n & size) != 0)
                        def _(size=size, src=src, dst=dst):
                            pltpu.make_async_copy(
                                ys_hbm.at[pl.ds(pl.multiple_of(src, SEG_ALIGN), size), :],
                                ybuf.at[s, pl.ds(pl.multiple_of(dst, SEG_ALIGN), size), :],
                                sem_g.at[s]).start()
                        step = jnp.where((n & size) != 0, size, 0)
                        src = src + step
                        dst = dst + step

    @pl.when(j >= 1)
    def _():
        slot = (j - 1) % 2
        total = tot_ref[j - 1]
        for size in _pieces(local_rows):
            @pl.when((total & size) != 0)
            def _(size=size):
                pltpu.make_async_copy(ys_hbm.at[pl.ds(0, size), :], ybuf.at[slot, pl.ds(0, size), :],
                                      sem_g.at[slot]).wait()

        route = route_ref[...]
        col = lax.broadcasted_iota(jnp.int32, (tm, local_rows), 1).astype(F32)
        g = jnp.zeros((tm, local_rows), F32)
        for k in range(TOP_K):
            pos = route[:, TOP_K + k: TOP_K + k + 1]
            gate = route[:, 2 * TOP_K + k: 2 * TOP_K + k + 1]
            g = jnp.where(col == pos, gate, g)
        g = g.astype(BF16)
        lo, hi = _unpack_pairs(ybuf[slot])
        o_ref[:, :half] = x1_ref[:, :half] + _dot(g, lo.astype(BF16))
        o_ref[:, half:] = x1_ref[:, half:] + _dot(g, hi.astype(BF16))


def _combine(tables, route, x1, ys, tm, local_rows):
    t, d = x1.shape
    half = d // 2

    def summed(j, *_):
        return (jnp.maximum(j - 1, 0), 0)

    grid_spec = pltpu.PrefetchScalarGridSpec(
        num_scalar_prefetch=4,
        grid=(t // tm + 1,),
        in_specs=[pl.BlockSpec((tm, LANES), summed),
                  pl.BlockSpec((tm, d), summed),
                  pl.BlockSpec(memory_space=pl.ANY)],
        out_specs=pl.BlockSpec((tm, d), summed),
        scratch_shapes=[pltpu.VMEM((2, local_rows, half), jnp.uint32),
                        pltpu.SemaphoreType.DMA((2,))],
    )
    return pl.pallas_call(
        _combine_kernel,
        grid_spec=grid_spec,
        out_shape=jax.ShapeDtypeStruct((t, d), F32),
        compiler_params=_params(),
        name="combine",
    )(*tables, route, x1, ys)


def _combine_tables(route, pad_start, tm, local_rows):
    t = route.shape[0]
    blocks = t // tm
    idx = route[:, :TOP_K].astype(jnp.int32).reshape(blocks, tm * TOP_K)
    experts = jnp.arange(N_EXPERTS, dtype=jnp.int32)
    n = jnp.sum(idx[:, :, None] == experts[None, None, :], axis=1).astype(jnp.int32)
    before = jnp.cumsum(n, axis=0) - n
    first = pad_start[None, :] + before
    src = first // SEG_ALIGN * SEG_ALIGN
    length = jnp.where(n > 0, (first + n + SEG_ALIGN - 1) // SEG_ALIGN * SEG_ALIGN - src, 0)
    dst = jnp.cumsum(length, axis=1) - length
    total = jnp.sum(length, axis=1)
    assert local_rows >= tm * TOP_K + N_EXPERTS * 2 * (SEG_ALIGN - 1)
    base = dst + first - src - before
    lookup = jnp.sum(jnp.where(idx[:, :, None] == experts[None, None, :], base[:, None, :], 0), axis=2)
    local = lookup.reshape(t, TOP_K).astype(F32) + route[:, TOP_K:2 * TOP_K]
    route = jnp.concatenate([route[:, :TOP_K], local, route[:, 2 * TOP_K:]], axis=1)
    tables = (src.reshape(-1).astype(jnp.int32), length.reshape(-1).astype(jnp.int32),
              dst.reshape(-1).astype(jnp.int32), total.astype(jnp.int32))
    return tables, route


def _slot_layout(route, counts, tm, bm):
    t = route.shape[0]
    counts = counts.astype(jnp.int32)
    padded = (counts + bm - 1) // bm * bm
    pad_end = jnp.cumsum(padded).astype(jnp.int32)
    pad_start = pad_end - padded
    idx = route[:, :TOP_K].astype(jnp.int32)
    rank = route[:, TOP_K:2 * TOP_K].astype(jnp.int32)
    dest = (jnp.take(pad_start, idx) + rank).reshape(t // tm, tm * TOP_K)
    n_blocks = -(-t * TOP_K // bm) + N_EXPERTS
    block_start = jnp.arange(n_blocks, dtype=jnp.int32) * bm
    block_expert = jnp.minimum(jnp.sum(pad_end[None, :] <= block_start[:, None], axis=1),
                               N_EXPERTS - 1).astype(jnp.int32)
    n_used = (pad_end[-1:] // bm).astype(jnp.int32)
    return padded, pad_end, dest, block_expert, n_used, n_blocks * bm


def _rot_cols(w):
    half = w.shape[-1] // 2
    return jnp.concatenate([-w[..., half:], w[..., :half]], axis=-1)


def _swap_halves(g):
    half = g.shape[-1] // 2
    return jnp.concatenate([g[..., half:], g[..., :half]], axis=-1)


def kernel(x, attn_norm_g, w_in, a_q_norm_g, a_k_norm_g, rel_bias, q_a_norm_g, w_q_b, kv_a_norm_g,
           w_kv_b, b_q_norm_g, b_k_norm_g, w_out, ffn_norm_g, router_w, router_b, w_gate_up,
           b_gate_up, w_down, b_down):
    batch, seq, d = x.shape
    t = batch * seq
    depth = w_in.shape[0]
    tm = 512
    expert_bm = 512

    pos = jnp.arange(seq, dtype=F32)
    inv_freq = ROPE_THETA ** (-jnp.arange(0, QK_ROPE_DIM, 2, dtype=F32) / QK_ROPE_DIM)
    ang = pos[:, None] * inv_freq[None, :]
    cos, sin = jnp.cos(ang), jnp.sin(ang)
    cs = jnp.concatenate([cos, cos, sin, sin], axis=-1)

    row = jnp.arange(LANES)[:, None] // A_HEAD_DIM
    col = jnp.arange(LANES)[None, :] // A_HEAD_DIM
    head_block_ones = (row == col).astype(BF16)
    band_bias = _band_bias(rel_bias)

    x2 = x.reshape(t, d)
    for layer in range(depth):
        kpe_off = 3 * A_WIDTH + Q_LORA_RANK + KV_LORA_RANK
        w_kpe = w_in[layer][:, kpe_off:]
        w_in_r = jnp.concatenate([w_in[layer][:, :kpe_off], w_kpe, _rot_cols(w_kpe)], axis=1).astype(BF16)

        wq = w_q_b[layer].reshape(Q_LORA_RANK, B_HEADS, QK_HEAD_DIM)
        wq_rope = wq[..., QK_NOPE_DIM:]
        wq_r = jnp.concatenate([wq, _rot_cols(wq_rope)], axis=-1).reshape(Q_LORA_RANK, -1).astype(BF16)
        wkv = w_kv_b[layer].reshape(KV_LORA_RANK, B_HEADS, QK_NOPE_DIM + V_HEAD_DIM)
        wkv_r = jnp.concatenate([wkv[..., :QK_NOPE_DIM].reshape(KV_LORA_RANK, -1),
                                 wkv[..., QK_NOPE_DIM:].reshape(KV_LORA_RANK, -1)], axis=1).astype(BF16)
        gc = jnp.concatenate([q_a_norm_g[layer], kv_a_norm_g[layer]])[None, :]
        gq, gk = b_q_norm_g[layer], b_k_norm_g[layer]

        def rope_gain(gr):
            return jnp.concatenate([gr, _swap_halves(gr)])

        gqk = jnp.stack([gq[:QK_NOPE_DIM], rope_gain(gq[QK_NOPE_DIM:]),
                         gk[:QK_NOPE_DIM], rope_gain(gk[QK_NOPE_DIM:])])
        g_a = jnp.stack([jnp.tile(a_q_norm_g[layer], 2), jnp.tile(a_k_norm_g[layer], 2)])
        g_a = jnp.broadcast_to(g_a[None], (A_WIDTH // LANES, 2, LANES))

        proj = _in_proj(x2, attn_norm_g[layer][None, :], w_in_r, tm)
        qb, kb, vb = _mla_prep(proj, cs, wq_r, wkv_r, gc, gqk, seq, tm)
        out_b = _mla_attention(qb, kb, vb, batch, seq, 256)
        out_a = _dilated_attention(proj, head_block_ones, g_a, band_bias, batch, seq)

        rw = jnp.pad(router_w[layer], ((0, 0), (0, LANES - N_EXPERTS)))
        rw_hi = rw.astype(BF16)
        rw_lo = (rw - rw_hi.astype(F32)).astype(BF16)
        rb = jnp.pad(router_b[layer], (0, LANES - N_EXPERTS), constant_values=-jnp.inf)[None, :]
        x1, hp, route, counts = _out_proj(x2, out_a, out_b, w_out[layer].astype(BF16),
                                          ffn_norm_g[layer][None, :], jnp.concatenate([rw_hi, rw_lo], axis=1), rb, tm)

        padded, pad_end, dest, block_expert, n_used, n_slots = _slot_layout(
            route, counts[0, :N_EXPERTS], tm, expert_bm)
        xs = _dispatch(padded, pad_end, n_used, dest, hp, n_slots, tm, expert_bm)
        ys = _expert_ffn(block_expert, n_used, xs, w_gate_up[layer], b_gate_up[layer],
                         w_down[layer], b_down[layer], expert_bm)
        tables, route_local = _combine_tables(route, pad_end - padded, tm, COMBINE_LOCAL_ROWS)
        x2 = _combine(tables, route_local, x1, ys, tm, COMBINE_LOCAL_ROWS)
    return x2.reshape(batch, seq, d)
```

```python
import functools
import math

import jax
import jax.numpy as jnp
from jax import lax
from jax.experimental import pallas as pl
from jax.experimental.pallas import tpu as pltpu

A_HEADS = 8
A_HEAD_DIM = 64
A_WIDTH = A_HEADS * A_HEAD_DIM
DILATED_PATTERNS = ((128, 1), (512, 4), (2048, 16))
BAND_BLOCK = 128

B_HEADS = 4
QK_NOPE_DIM = 128
QK_ROPE_DIM = 64
QK_HEAD_DIM = QK_NOPE_DIM + QK_ROPE_DIM
V_HEAD_DIM = 128
Q_LORA_RANK = 256
KV_LORA_RANK = 256
B_WIDTH = B_HEADS * V_HEAD_DIM
ROPE_THETA = 10000.0

NUM_BUCKETS = 32
MAX_DISTANCE = 2048

N_EXPERTS = 32
TOP_K = 4
SWIGLU_LIMIT = 7.0
SWIGLU_ALPHA = 1.702
RMS_EPS = 1e-6

LANES = 128
MASK_VALUE = -1e30
LOG2E = math.log2(math.e)
PROJ_WIDTH = 3 * A_WIDTH + Q_LORA_RANK + KV_LORA_RANK + 2 * QK_ROPE_DIM
VMEM_LIMIT = 56 * 1024 * 1024

F32 = jnp.float32
BF16 = jnp.bfloat16


def _dot(a, b):
    return jnp.dot(a, b, preferred_element_type=F32)


def _dot_nt(a, b):
    return lax.dot_general(a, b, (((1,), (1,)), ((), ())), preferred_element_type=F32)


def _split_dot(x, m):
    hi = x.astype(BF16)
    lo = (x - hi.astype(F32)).astype(BF16)
    return _dot(hi, m) + _dot(lo, m)


def _params(n_parallel=1):
    return pltpu.CompilerParams(
        dimension_semantics=("arbitrary",) * n_parallel, vmem_limit_bytes=VMEM_LIMIT)


def _in_proj_kernel(x_ref, g_ref, w_ref, o_ref):
    x = x_ref[...]
    ms = jnp.mean(x * x, axis=-1, keepdims=True)
    h = (x * lax.rsqrt(ms + RMS_EPS) * g_ref[...]).astype(BF16)
    o_ref[...] = _dot(h, w_ref[...])


def _in_proj(x2, g, w, tm):
    t, d = x2.shape
    n = w.shape[1]
    return pl.pallas_call(
        _in_proj_kernel,
        grid=(t // tm,),
        in_specs=[pl.BlockSpec((tm, d), lambda i: (i, 0)),
                  pl.BlockSpec((1, d), lambda i: (0, 0)),
                  pl.BlockSpec((d, n), lambda i: (0, 0))],
        out_specs=pl.BlockSpec((tm, n), lambda i: (i, 0)),
        out_shape=jax.ShapeDtypeStruct((t, n), F32),
        compiler_params=_params(),
        name="in_proj",
    )(x2, g, w)


def _mla_prep_kernel(c_ref, kpe_ref, cs_ref, wq_ref, wkv_ref, gc_ref, gqk_ref, q_ref, k_ref, v_ref,
                     *, q_scale):
    c = c_ref[...]
    gc = gc_ref[...]

    def lora_norm(z, g):
        ms = jnp.mean(z * z, axis=-1, keepdims=True)
        return (z * lax.rsqrt(ms + RMS_EPS) * g).astype(BF16)

    cq = lora_norm(c[:, :Q_LORA_RANK], gc[:, :Q_LORA_RANK])
    ckv = lora_norm(c[:, Q_LORA_RANK:], gc[:, Q_LORA_RANK:])
    qb = _dot(cq, wq_ref[...])
    kvb = _dot(ckv, wkv_ref[...])
    kper = kpe_ref[...]
    cs = cs_ref[...]
    gqk = gqk_ref[...]
    tm = c.shape[0]

    row = lax.broadcasted_iota(jnp.int32, (LANES, LANES), 0)
    ones_all = jnp.ones((LANES, LANES), BF16)
    ones_lo = jnp.where(row < QK_ROPE_DIM, 1.0, 0.0).astype(BF16)
    lane = lax.broadcasted_iota(jnp.int32, (tm, LANES), 1)

    def rope(z, g_row):
        t = z * (g_row * cs)
        return jnp.where(lane < QK_ROPE_DIM, t + pltpu.roll(t, QK_ROPE_DIM, 1), 0.0)

    k_rope = rope(kper, gqk[3:4])
    kpe_ss = _split_dot(kper * kper, ones_lo)
    for h in range(B_HEADS):
        qn = qb[:, 2 * LANES * h: 2 * LANES * h + LANES]
        qr = qb[:, 2 * LANES * h + LANES: 2 * LANES * (h + 1)]
        ss = _split_dot(qn * qn, ones_all) + _split_dot(qr * qr, ones_lo)
        rs = lax.rsqrt(ss * (1.0 / QK_HEAD_DIM) + RMS_EPS) * q_scale
        q_ref[h, :, :LANES] = (qn * gqk[0:1] * rs).astype(BF16)
        q_ref[h, :, LANES:] = (rope(qr, gqk[1:2]) * rs).astype(BF16)
        kn = kvb[:, LANES * h: LANES * (h + 1)]
        ssk = _split_dot(kn * kn, ones_all) + kpe_ss
        rsk = lax.rsqrt(ssk * (1.0 / QK_HEAD_DIM) + RMS_EPS)
        k_ref[h, :, :LANES] = (kn * gqk[2:3] * rsk).astype(BF16)
        k_ref[h, :, LANES:] = (k_rope * rsk).astype(BF16)
    v_ref[...] = kvb[:, B_HEADS * QK_NOPE_DIM:].astype(BF16)


def _mla_prep(proj, cs, wq, wkv, gc, gqk, seq, tm):
    t = proj.shape[0]
    c_width = Q_LORA_RANK + KV_LORA_RANK
    c_block = 3 * A_WIDTH // c_width
    kpe_block = (3 * A_WIDTH + c_width) // LANES
    pos_blocks = seq // tm
    kernel = functools.partial(_mla_prep_kernel, q_scale=QK_HEAD_DIM ** -0.5 * LOG2E)
    return pl.pallas_call(
        kernel,
        grid=(t // tm,),
        in_specs=[pl.BlockSpec((tm, c_width), lambda i: (i, c_block)),
                  pl.BlockSpec((tm, LANES), lambda i: (i, kpe_block)),
                  pl.BlockSpec((tm, LANES), lambda i: (i % pos_blocks, 0)),
                  pl.BlockSpec(wq.shape, lambda i: (0, 0)),
                  pl.BlockSpec(wkv.shape, lambda i: (0, 0)),
                  pl.BlockSpec(gc.shape, lambda i: (0, 0)),
                  pl.BlockSpec(gqk.shape, lambda i: (0, 0))],
        out_specs=[pl.BlockSpec((B_HEADS, tm, 2 * LANES), lambda i: (0, i, 0)),
                   pl.BlockSpec((B_HEADS, tm, 2 * LANES), lambda i: (0, i, 0)),
                   pl.BlockSpec((tm, B_WIDTH), lambda i: (i, 0))],
        out_shape=[jax.ShapeDtypeStruct((B_HEADS, t, 2 * LANES), BF16),
                   jax.ShapeDtypeStruct((B_HEADS, t, 2 * LANES), BF16),
                   jax.ShapeDtypeStruct((t, B_WIDTH), BF16)],
        compiler_params=_params(),
        name="mla_prep",
    )(proj, proj, cs, wq, wkv, gc, gqk)


def _mla_attention_kernel(q_ref, k_ref, v_ref, o_ref, *, tq):
    seq = q_ref.shape[0]
    row = lax.broadcasted_iota(jnp.int32, (tq, tq), 0)
    col = lax.broadcasted_iota(jnp.int32, (tq, tq), 1)
    diag_mask = jnp.where(col <= row, 0.0, MASK_VALUE)
    v_ext = jnp.concatenate([v_ref[...], jnp.ones(v_ref.shape, v_ref.dtype)], axis=1)
    dv = v_ref.shape[1]
    for i in range(seq // tq):
        q = q_ref[i * tq:(i + 1) * tq, :]
        s_diag = _dot_nt(q, k_ref[i * tq:(i + 1) * tq, :]) + diag_mask
        m = jnp.max(s_diag, axis=-1, keepdims=True)
        if i > 0:
            s_past = _dot_nt(q, k_ref[:i * tq, :])
            m = jnp.maximum(m, jnp.max(s_past, axis=-1, keepdims=True))
        o = _dot(jnp.exp2(s_diag - m).astype(BF16), v_ext[i * tq:(i + 1) * tq, :])
        if i > 0:
            o = o + _dot(jnp.exp2(s_past - m).astype(BF16), v_ext[:i * tq, :])
        o_ref[i * tq:(i + 1) * tq, :] = (o[:, :dv] / o[:, dv:]).astype(o_ref.dtype)


def _mla_attention(qb, kb, vb, batch, seq, tq):
    t = vb.shape[0]
    kernel = functools.partial(_mla_attention_kernel, tq=tq)
    return pl.pallas_call(
        kernel,
        grid=(batch, B_HEADS),
        in_specs=[pl.BlockSpec((None, seq, 2 * LANES), lambda b, h: (h, b, 0)),
                  pl.BlockSpec((None, seq, 2 * LANES), lambda b, h: (h, b, 0)),
                  pl.BlockSpec((seq, V_HEAD_DIM), lambda b, h: (b, h))],
        out_specs=pl.BlockSpec((seq, V_HEAD_DIM), lambda b, h: (b, h)),
        out_shape=jax.ShapeDtypeStruct((t, B_WIDTH), BF16),
        compiler_params=_params(2),
        name="mla_attention",
    )(qb, kb, vb)


def _dilated_kernel(q_ref, k_ref, v_ref, bd_ref, g_ref, bias_ref, o_ref, qs, ks, acc_o, acc_l, acc_m,
                    *, unroll):
    seq = q_ref.shape[0]
    n_tiles = seq // BAND_BLOCK
    low = lax.broadcasted_iota(jnp.int32, (BAND_BLOCK, LANES), 1) < A_HEAD_DIM
    bd = bd_ref[...]
    g = g_ref[...]

    def head_norm(z, g_row):
        ss = _split_dot(z * z, bd)
        return z * lax.rsqrt(ss * (1.0 / A_HEAD_DIM) + RMS_EPS) * g_row

    qs[...] = head_norm(q_ref[...], g[0:1]) * (A_HEAD_DIM ** -0.5 * LOG2E)
    ks[...] = head_norm(k_ref[...], g[1:2])
    ones = jnp.ones((2 * BAND_BLOCK, LANES), BF16)

    for p, (window, dil) in enumerate(DILATED_PATTERNS):
        blocks_per_class = n_tiles // dil
        span = BAND_BLOCK * dil

        def rows_at(start, dil=dil):
            if dil == 1:
                return pl.ds(start, BAND_BLOCK)
            return pl.ds(start, BAND_BLOCK, stride=dil)

        def tile(i, carry, p=p, blocks_per_class=blocks_per_class, span=span, rows_at=rows_at):
            r = i // blocks_per_class
            j = i % blocks_per_class
            cur = rows_at(r + j * span)
            prev = rows_at(r + jnp.maximum(j - 1, 0) * span)
            q = qs[cur, :]
            q2 = jnp.concatenate([jnp.where(low, q, 0.0), jnp.where(low, 0.0, q)], axis=0).astype(BF16)
            k_band = jnp.concatenate([ks[prev, :], ks[cur, :]], axis=0).astype(BF16)
            s = _dot_nt(q2, k_band) + bias_ref[p, jnp.where(j == 0, 1, 0)]
            m = jnp.max(s, axis=-1, keepdims=True)
            pr = jnp.exp2(s - m).astype(BF16)
            v_band = jnp.concatenate([v_ref[prev, :], v_ref[cur, :]], axis=0).astype(BF16)
            o = _dot(pr, jnp.concatenate([v_band, ones], axis=1))
            top, bot = o[:BAND_BLOCK], o[BAND_BLOCK:]
            acc_o[p, cur, :] = jnp.where(low, top[:, :LANES], bot[:, :LANES])
            acc_l[p, cur, :] = jnp.where(low, top[:, LANES:], bot[:, LANES:])
            acc_m[p, cur, :] = jnp.where(low, m[:BAND_BLOCK], m[BAND_BLOCK:])
            return carry

        lax.fori_loop(0, n_tiles, tile, 0, unroll=unroll)

    m_all = jnp.maximum(jnp.maximum(acc_m[0], acc_m[1]), acc_m[2])
    num = jnp.zeros((seq, LANES), F32)
    den = jnp.zeros((seq, LANES), F32)
    for p in range(len(DILATED_PATTERNS)):
        w = jnp.exp2(acc_m[p] - m_all)
        num = num + w * acc_o[p]
        den = den + w * acc_l[p]
    o_ref[...] = (num / den).astype(o_ref.dtype)


def _dilated_attention(proj, bd, g, bias, batch, seq):
    t = proj.shape[0]
    pairs = A_WIDTH // LANES
    n_pat = len(DILATED_PATTERNS)
    return pl.pallas_call(
        functools.partial(_dilated_kernel, unroll=16),
        grid=(batch, pairs),
        in_specs=[pl.BlockSpec((seq, LANES), lambda b, c: (b, c)),
                  pl.BlockSpec((seq, LANES), lambda b, c: (b, pairs + c)),
                  pl.BlockSpec((seq, LANES), lambda b, c: (b, 2 * pairs + c)),
                  pl.BlockSpec((LANES, LANES), lambda b, c: (0, 0)),
                  pl.BlockSpec((None, 2, LANES), lambda b, c: (c, 0, 0)),
                  pl.BlockSpec((None, n_pat, 2, 2 * BAND_BLOCK, 2 * BAND_BLOCK),
                               lambda b, c: (c, 0, 0, 0, 0))],
        out_specs=pl.BlockSpec((seq, LANES), lambda b, c: (b, c)),
        out_shape=jax.ShapeDtypeStruct((t, A_WIDTH), BF16),
        scratch_shapes=[pltpu.VMEM((seq, LANES), F32)] * 2
                       + [pltpu.VMEM((n_pat, seq, LANES), F32)] * 3,
        compiler_params=_params(2),
        name="dilated_attention",
    )(proj, proj, proj, bd, g, bias)


def _t5_bucket(dist):
    max_exact = NUM_BUCKETS // 2
    df = jnp.maximum(dist, 1).astype(F32)
    log_bucket = max_exact + (jnp.log(df / max_exact) / math.log(MAX_DISTANCE / max_exact)
                              * (NUM_BUCKETS - max_exact)).astype(jnp.int32)
    log_bucket = jnp.minimum(log_bucket, NUM_BUCKETS - 1)
    return jnp.where(dist < max_exact, dist, log_bucket)


def _band_bias(rel_bias):
    n = BAND_BLOCK
    qi = jnp.arange(n)[:, None]
    kj = jnp.arange(n)[None, :]
    buckets = jnp.arange(NUM_BUCKETS)
    tables = []
    for window, dil in DILATED_PATTERNS:
        steps = window // dil
        halves = []
        for back in (qi - kj + n, qi - kj):
            onehot = (_t5_bucket(jnp.maximum(back, 0) * dil)[:, :, None] == buckets).astype(F32)
            vals = jnp.einsum('qkb,bh->hqk', onehot, rel_bias.astype(F32),
                              precision=lax.Precision.HIGHEST) * LOG2E
            halves.append(jnp.where(((back >= 0) & (back <= steps))[None], vals, MASK_VALUE))
        prev, cur = halves
        normal = jnp.concatenate([prev, cur], axis=-1)
        first = jnp.concatenate([jnp.full_like(prev, MASK_VALUE), cur], axis=-1)
        tables.append(jnp.stack([normal, first], axis=1))
    tab = jnp.stack(tables, axis=1)
    tab = tab.reshape(A_HEADS // 2, 2, len(DILATED_PATTERNS), 2, n, 2 * n)
    return tab.transpose(0, 2, 3, 1, 4, 5).reshape(A_HEADS // 2, len(DILATED_PATTERNS), 2, 2 * n, 2 * n)


def _pack_pairs(v):
    half = v.shape[1] // 2
    bits = pltpu.bitcast(v, jnp.uint32)
    return (bits[:, :half] >> 16) | (bits[:, half:] & jnp.uint32(0xFFFF0000))


def _unpack_pairs(p):
    return (pltpu.bitcast(p << 16, F32), pltpu.bitcast(p & jnp.uint32(0xFFFF0000), F32))


def _out_proj_kernel(x_ref, a_ref, b_ref, w_ref, g_ref, rw_ref, rb_ref,
                     x1_ref, hp_ref, route_ref, cnt_ref, carry):
    i = pl.program_id(0)
    tm = x_ref.shape[0]

    @pl.when(i == 0)
    def _():
        carry[...] = jnp.zeros_like(carry)

    w = w_ref[...]
    x1 = x_ref[...] + _dot(a_ref[...], w[:A_WIDTH]) + _dot(b_ref[...], w[A_WIDTH:])
    x1_ref[...] = x1
    ms = jnp.mean(x1 * x1, axis=-1, keepdims=True)
    h = x1 * lax.rsqrt(ms + RMS_EPS) * g_ref[...]
    hi = h.astype(BF16)
    hi_f = hi.astype(F32)
    hp_ref[...] = _pack_pairs(hi_f)
    lo = (h - hi_f).astype(BF16)
    rw = rw_ref[...]
    hw = _dot(hi, rw)
    logits = hw[:, :LANES] + hw[:, LANES:] + _dot(lo, rw[:, :LANES]) + rb_ref[...]

    lane = lax.broadcasted_iota(jnp.int32, (tm, LANES), 1).astype(F32)
    remaining = logits
    vals, hots = [], []
    for _ in range(TOP_K):
        m = jnp.max(remaining, axis=-1, keepdims=True)
        first = jnp.min(jnp.where(remaining == m, lane, float(LANES)), axis=-1, keepdims=True)
        hot = lane == first
        remaining = jnp.where(hot, -jnp.inf, remaining)
        vals.append(m)
        hots.append(hot)
    exps = [jnp.exp(v - vals[0]) for v in vals]
    den = exps[0] + exps[1] + exps[2] + exps[3]

    chosen = jnp.zeros((tm, LANES), F32)
    for hot in hots:
        chosen = chosen + jnp.where(hot, 1.0, 0.0)
    r = lax.broadcasted_iota(jnp.int32, (tm, tm), 0)
    c = lax.broadcasted_iota(jnp.int32, (tm, tm), 1)
    earlier = jnp.where(r > c, 1.0, 0.0).astype(BF16)
    before = carry[...] + _dot(earlier, chosen.astype(BF16))
    carry[...] = carry[...] + jnp.sum(chosen, axis=0, keepdims=True)
    cnt_ref[...] = jnp.broadcast_to(carry[...], cnt_ref.shape)

    route = jnp.zeros((tm, LANES), F32)
    for k in range(TOP_K):
        first = jnp.sum(jnp.where(hots[k], lane, 0.0), axis=-1, keepdims=True)
        rank = jnp.sum(jnp.where(hots[k], before, 0.0), axis=-1, keepdims=True)
        route = route + jnp.where(lane == float(k), first, 0.0)
        route = route + jnp.where(lane == float(TOP_K + k), rank, 0.0)
        route = route + jnp.where(lane == float(2 * TOP_K + k), exps[k] / den, 0.0)
    route_ref[...] = route


def _out_proj(x2, out_a, out_b, w, g, rw, rb, tm):
    t, d = x2.shape
    return pl.pallas_call(
        _out_proj_kernel,
        grid=(t // tm,),
        in_specs=[pl.BlockSpec((tm, d), lambda i: (i, 0)),
                  pl.BlockSpec((tm, A_WIDTH), lambda i: (i, 0)),
                  pl.BlockSpec((tm, B_WIDTH), lambda i: (i, 0)),
                  pl.BlockSpec(w.shape, lambda i: (0, 0)),
                  pl.BlockSpec((1, d), lambda i: (0, 0)),
                  pl.BlockSpec(rw.shape, lambda i: (0, 0)),
                  pl.BlockSpec((1, LANES), lambda i: (0, 0))],
        out_specs=[pl.BlockSpec((tm, d), lambda i: (i, 0)),
                   pl.BlockSpec((tm, d // 2), lambda i: (i, 0)),
                   pl.BlockSpec((tm, LANES), lambda i: (i, 0)),
                   pl.BlockSpec((8, LANES), lambda i: (0, 0))],
        out_shape=[jax.ShapeDtypeStruct((t, d), F32),
                   jax.ShapeDtypeStruct((t, d // 2), jnp.uint32),
                   jax.ShapeDtypeStruct((t, LANES), F32),
                   jax.ShapeDtypeStruct((8, LANES), F32)],
        scratch_shapes=[pltpu.VMEM((1, LANES), F32)],
        compiler_params=_params(),
        name="out_proj",
    )(x2, out_a, out_b, w, g, rw, rb)


def _dispatch_kernel(pad_ref, end_ref, nb_ref, dest_hbm, h_ref, xs_hbm, dest_s0, dest_s1, zeros,
                     sem_i, sem_z, sem_o, *, bm):
    i = pl.program_id(0)
    steps = pl.num_programs(0)
    tm = h_ref.shape[0]
    n_blocks = xs_hbm.shape[0] // bm
    dest_s = (dest_s0, dest_s1)

    def idx_copy(step, slot):
        return pltpu.make_async_copy(dest_hbm.at[step], dest_s[slot], sem_i.at[slot])

    @pl.when(i == 0)
    def _():
        idx_copy(0, 0).start()
        zeros[...] = jnp.zeros_like(zeros)

        def zero_block(start):
            return pltpu.make_async_copy(zeros, xs_hbm.at[pl.ds(pl.multiple_of(start, bm), bm), :], sem_z)

        for e in range(N_EXPERTS):
            @pl.when(pad_ref[e] > 0)
            def _():
                zero_block(end_ref[e] - bm).start()

        def start_unused(b, carry):
            zero_block(b * bm).start()
            return carry

        def wait_unused(b, carry):
            zero_block(b * bm).wait()
            return carry

        lax.fori_loop(nb_ref[0], n_blocks, start_unused, 0)
        for e in range(N_EXPERTS):
            @pl.when(pad_ref[e] > 0)
            def _():
                zero_block(end_ref[e] - bm).wait()
        lax.fori_loop(nb_ref[0], n_blocks, wait_unused, 0)

    for slot in range(2):
        @pl.when(i % 2 == slot)
        def _(slot=slot):
            @pl.when(i + 1 < steps)
            def _():
                idx_copy(i + 1, 1 - slot).start()

            idx_copy(i, slot).wait()

            def body(t, carry):
                for k in range(TOP_K):
                    pltpu.make_async_copy(h_ref.at[pl.ds(t, 1), :],
                                          xs_hbm.at[pl.ds(dest_s[slot][t * TOP_K + k], 1), :],
                                          sem_o).start(priority=k % 2)
                return carry

            lax.fori_loop(0, tm, body, 0, unroll=8)
    for _ in range(TOP_K):
        pltpu.make_async_copy(h_ref, xs_hbm.at[pl.ds(0, tm), :], sem_o).wait()


def _dispatch(padded, pad_end, n_used, dest, hp, n_slots, tm, bm):
    t, half = hp.shape
    grid_spec = pltpu.PrefetchScalarGridSpec(
        num_scalar_prefetch=3,
        grid=(t // tm,),
        in_specs=[pl.BlockSpec(memory_space=pl.ANY),
                  pl.BlockSpec((tm, half), lambda i, p, e, n: (i, 0))],
        out_specs=pl.BlockSpec(memory_space=pl.ANY),
        scratch_shapes=[pltpu.SMEM((tm * TOP_K,), jnp.int32),
                        pltpu.SMEM((tm * TOP_K,), jnp.int32),
                        pltpu.VMEM((bm, half), jnp.uint32),
                        pltpu.SemaphoreType.DMA((2,)),
                        pltpu.SemaphoreType.DMA(()),
                        pltpu.SemaphoreType.DMA(())],
    )
    return pl.pallas_call(
        functools.partial(_dispatch_kernel, bm=bm),
        grid_spec=grid_spec,
        out_shape=jax.ShapeDtypeStruct((n_slots, half), jnp.uint32),
        compiler_params=_params(),
        name="dispatch",
    )(padded, pad_end, n_used, dest, hp)


EXPERT_SUB_ROWS = 256


def _expert_ffn_kernel(be_ref, nb_ref, x_ref, wgu_ref, bgu_ref, wd_ref, bd_ref, y_ref, wgu_s, wd_s):
    i = pl.program_id(0)
    d_ff = wd_ref.shape[0]
    half = x_ref.shape[1]

    @pl.when(i >= nb_ref[0])
    def _():
        y_ref[...] = jnp.zeros_like(y_ref)

    @pl.when(i < nb_ref[0])
    def _():
        changed = jnp.logical_or(i == 0, be_ref[i] != be_ref[jnp.maximum(i - 1, 0)])

        @pl.when(changed)
        def _():
            wgu_s[...] = wgu_ref[...].astype(BF16)
            wd_s[...] = wd_ref[...].astype(BF16)

        for r0 in range(0, x_ref.shape[0], EXPERT_SUB_ROWS):
            rows = slice(r0, r0 + EXPERT_SUB_ROWS)
            x_lo, x_hi = _unpack_pairs(x_ref[rows, :])
            gu = (_dot(x_lo.astype(BF16), wgu_s[:half, :]) + _dot(x_hi.astype(BF16), wgu_s[half:, :])
                  + bgu_ref[...])
            gate = jnp.minimum(gu[:, :d_ff], SWIGLU_LIMIT)
            up = jnp.clip(gu[:, d_ff:], -SWIGLU_LIMIT, SWIGLU_LIMIT)
            glu = gate * jax.nn.sigmoid(SWIGLU_ALPHA * gate)
            act = ((up + 1.0) * glu).astype(BF16)
            y = _dot(act, wd_s[...]) + bd_ref[...]
            y_ref[rows, :] = _pack_pairs(y.astype(BF16).astype(F32))


def _expert_ffn(block_expert, n_used, xs, w_gate_up, b_gate_up, w_down, b_down, bm):
    n_slots, half = xs.shape
    n_e, d, two_ff = w_gate_up.shape
    d_ff = two_ff // 2
    n_blocks = n_slots // bm

    def used(i, nb):
        return jnp.minimum(i, jnp.maximum(nb[0] - 1, 0))

    grid_spec = pltpu.PrefetchScalarGridSpec(
        num_scalar_prefetch=2,
        grid=(n_blocks,),
        in_specs=[pl.BlockSpec((bm, half), lambda i, be, nb: (used(i, nb), 0)),
                  pl.BlockSpec((None, d, two_ff), lambda i, be, nb: (be[i], 0, 0)),
                  pl.BlockSpec((None, 1, two_ff), lambda i, be, nb: (be[i], 0, 0)),
                  pl.BlockSpec((None, d_ff, d), lambda i, be, nb: (be[i], 0, 0)),
                  pl.BlockSpec((None, 1, d), lambda i, be, nb: (be[i], 0, 0))],
        out_specs=pl.BlockSpec((bm, half), lambda i, be, nb: (i, 0)),
        scratch_shapes=[pltpu.VMEM((d, two_ff), BF16), pltpu.VMEM((d_ff, d), BF16)],
    )
    return pl.pallas_call(
        _expert_ffn_kernel,
        grid_spec=grid_spec,
        out_shape=jax.ShapeDtypeStruct((n_slots, half), jnp.uint32),
        compiler_params=_params(),
        name="expert_ffn",
    )(block_expert, n_used, xs, w_gate_up, b_gate_up.reshape(n_e, 1, two_ff),
      w_down, b_down.reshape(n_e, 1, d))


COMBINE_LOCAL_ROWS = 2560
SEG_ALIGN = 8


def _pieces(limit):
    sizes = []
    size = 1 << (limit.bit_length() - 1)
    while size >= SEG_ALIGN:
        sizes.append(size)
        size //= 2
    return sizes


def _combine_kernel(src_ref, len_ref, dst_ref, tot_ref, route_ref, x1_ref, ys_hbm, o_ref, ybuf, sem_g):
    j = pl.program_id(0)
    blocks = pl.num_programs(0) - 1
    tm = x1_ref.shape[0]
    half = x1_ref.shape[1] // 2
    local_rows = ybuf.shape[1]

    @pl.when(j == 0)
    def _():
        ybuf[...] = jnp.zeros_like(ybuf)

    @pl.when(j < blocks)
    def _():
        for s in range(2):
            @pl.when(j % 2 == s)
            def _(s=s):
                for e in range(N_EXPERTS):
                    seg = j * N_EXPERTS + e
                    n = len_ref[seg]
                    src = src_ref[seg]
                    dst = dst_ref[seg]
                    for size in _pieces(tm + 2 * SEG_ALIGN):
                        @pl.when((n & size) != 0)
                        def _(size=size, src=src, dst=dst):
                            pltpu.make_async_copy(
                                ys_hbm.at[pl.ds(pl.multiple_of(src, SEG_ALIGN), size), :],
                                ybuf.at[s, pl.ds(pl.multiple_of(dst, SEG_ALIGN), size), :],
                                sem_g.at[s]).start()
                        step = jnp.where((n & size) != 0, size, 0)
                        src = src + step
                        dst = dst + step

    @pl.when(j >= 1)
    def _():
        slot = (j - 1) % 2
        total = tot_ref[j - 1]
        for size in _pieces(local_rows):
            @pl.when((total & size) != 0)
            def _(size=size):
                pltpu.make_async_copy(ys_hbm.at[pl.ds(0, size), :], ybuf.at[slot, pl.ds(0, size), :],
                                      sem_g.at[slot]).wait()

        route = route_ref[...]
        col = lax.broadcasted_iota(jnp.int32, (tm, local_rows), 1).astype(F32)
        g = jnp.zeros((tm, local_rows), F32)
        for k in range(TOP_K):
            pos = route[:, TOP_K + k: TOP_K + k + 1]
            gate = route[:, 2 * TOP_K + k: 2 * TOP_K + k + 1]
            g = jnp.where(col == pos, gate, g)
        g = g.astype(BF16)
        lo, hi = _unpack_pairs(ybuf[slot])
        o_ref[:, :half] = x1_ref[:, :half] + _dot(g, lo.astype(BF16))
        o_ref[:, half:] = x1_ref[:, half:] + _dot(g, hi.astype(BF16))


def _combine(tables, route, x1, ys, tm, local_rows):
    t, d = x1.shape
    half = d // 2

    def summed(j, *_):
        return (jnp.maximum(j - 1, 0), 0)

    grid_spec = pltpu.PrefetchScalarGridSpec(
        num_scalar_prefetch=4,
        grid=(t // tm + 1,),
        in_specs=[pl.BlockSpec((tm, LANES), summed),
                  pl.BlockSpec((tm, d), summed),
                  pl.BlockSpec(memory_space=pl.ANY)],
        out_specs=pl.BlockSpec((tm, d), summed),
        scratch_shapes=[pltpu.VMEM((2, local_rows, half), jnp.uint32),
                        pltpu.SemaphoreType.DMA((2,))],
    )
    return pl.pallas_call(
        _combine_kernel,
        grid_spec=grid_spec,
        out_shape=jax.ShapeDtypeStruct((t, d), F32),
        compiler_params=_params(),
        name="combine",
    )(*tables, route, x1, ys)


def _combine_tables(route, pad_start, tm, local_rows):
    t = route.shape[0]
    blocks = t // tm
    idx = route[:, :TOP_K].astype(jnp.int32).reshape(blocks, tm * TOP_K)
    experts = jnp.arange(N_EXPERTS, dtype=jnp.int32)
    onehot = (idx[:, :, None] == experts[None, None, :]).astype(F32)
    n = jnp.sum(onehot, axis=1).astype(jnp.int32)
    before = jnp.cumsum(n, axis=0) - n
    first = pad_start[None, :] + before
    src = first // SEG_ALIGN * SEG_ALIGN
    length = jnp.where(n > 0, (first + n + SEG_ALIGN - 1) // SEG_ALIGN * SEG_ALIGN - src, 0)
    dst = jnp.cumsum(length, axis=1) - length
    total = jnp.sum(length, axis=1)
    assert local_rows >= tm * TOP_K + N_EXPERTS * 2 * (SEG_ALIGN - 1)
    base = (dst + first - src - before).astype(F32)
    lookup = jnp.einsum('bne,be->bn', onehot, base, precision=lax.Precision.HIGHEST)
    local = lookup.reshape(t, TOP_K) + route[:, TOP_K:2 * TOP_K]
    route = jnp.concatenate([route[:, :TOP_K], local, route[:, 2 * TOP_K:]], axis=1)
    tables = (src.reshape(-1).astype(jnp.int32), length.reshape(-1).astype(jnp.int32),
              dst.reshape(-1).astype(jnp.int32), total.astype(jnp.int32))
    return tables, route


def _slot_layout(route, counts, tm, bm):
    t = route.shape[0]
    counts = counts.astype(jnp.int32)
    padded = (counts + bm - 1) // bm * bm
    pad_end = jnp.cumsum(padded).astype(jnp.int32)
    pad_start = pad_end - padded
    idx = route[:, :TOP_K].astype(jnp.int32)
    rank = route[:, TOP_K:2 * TOP_K].astype(jnp.int32)
    onehot = (idx[:, :, None] == jnp.arange(N_EXPERTS, dtype=jnp.int32)).astype(F32)
    start = jnp.einsum('tke,e->tk', onehot, pad_start.astype(F32), precision=lax.Precision.HIGHEST)
    dest = (start.astype(jnp.int32) + rank).reshape(t // tm, tm * TOP_K)
    n_blocks = -(-t * TOP_K // bm) + N_EXPERTS
    block_start = jnp.arange(n_blocks, dtype=jnp.int32) * bm
    block_expert = jnp.minimum(jnp.sum(pad_end[None, :] <= block_start[:, None], axis=1),
                               N_EXPERTS - 1).astype(jnp.int32)
    n_used = (pad_end[-1:] // bm).astype(jnp.int32)
    return padded, pad_end, dest, block_expert, n_used, n_blocks * bm


def _rot_cols(w):
    half = w.shape[-1] // 2
    return jnp.concatenate([-w[..., half:], w[..., :half]], axis=-1)


def _swap_halves(g):
    half = g.shape[-1] // 2
    return jnp.concatenate([g[..., half:], g[..., :half]], axis=-1)


def kernel(x, attn_norm_g, w_in, a_q_norm_g, a_k_norm_g, rel_bias, q_a_norm_g, w_q_b, kv_a_norm_g,
           w_kv_b, b_q_norm_g, b_k_norm_g, w_out, ffn_norm_g, router_w, router_b, w_gate_up,
           b_gate_up, w_down, b_down):
    batch, seq, d = x.shape
    t = batch * seq
    depth = w_in.shape[0]
    tm = 512
    expert_bm = 512

    pos = jnp.arange(seq, dtype=F32)
    inv_freq = ROPE_THETA ** (-jnp.arange(0, QK_ROPE_DIM, 2, dtype=F32) / QK_ROPE_DIM)
    ang = pos[:, None] * inv_freq[None, :]
    cos, sin = jnp.cos(ang), jnp.sin(ang)
    cs = jnp.concatenate([cos, cos, sin, sin], axis=-1)

    row = jnp.arange(LANES)[:, None] // A_HEAD_DIM
    col = jnp.arange(LANES)[None, :] // A_HEAD_DIM
    head_block_ones = (row == col).astype(BF16)
    band_bias = _band_bias(rel_bias)

    x2 = x.reshape(t, d)
    for layer in range(depth):
        kpe_off = 3 * A_WIDTH + Q_LORA_RANK + KV_LORA_RANK
        w_kpe = w_in[layer][:, kpe_off:]
        w_in_r = jnp.concatenate([w_in[layer][:, :kpe_off], w_kpe, _rot_cols(w_kpe)], axis=1).astype(BF16)

        wq = w_q_b[layer].reshape(Q_LORA_RANK, B_HEADS, QK_HEAD_DIM)
        wq_rope = wq[..., QK_NOPE_DIM:]
        wq_r = jnp.concatenate([wq, _rot_cols(wq_rope)], axis=-1).reshape(Q_LORA_RANK, -1).astype(BF16)
        wkv = w_kv_b[layer].reshape(KV_LORA_RANK, B_HEADS, QK_NOPE_DIM + V_HEAD_DIM)
        wkv_r = jnp.concatenate([wkv[..., :QK_NOPE_DIM].reshape(KV_LORA_RANK, -1),
                                 wkv[..., QK_NOPE_DIM:].reshape(KV_LORA_RANK, -1)], axis=1).astype(BF16)
        gc = jnp.concatenate([q_a_norm_g[layer], kv_a_norm_g[layer]])[None, :]
        gq, gk = b_q_norm_g[layer], b_k_norm_g[layer]

        def rope_gain(gr):
            return jnp.concatenate([gr, _swap_halves(gr)])

        gqk = jnp.stack([gq[:QK_NOPE_DIM], rope_gain(gq[QK_NOPE_DIM:]),
                         gk[:QK_NOPE_DIM], rope_gain(gk[QK_NOPE_DIM:])])
        g_a = jnp.stack([jnp.tile(a_q_norm_g[layer], 2), jnp.tile(a_k_norm_g[layer], 2)])
        g_a = jnp.broadcast_to(g_a[None], (A_WIDTH // LANES, 2, LANES))

        proj = _in_proj(x2, attn_norm_g[layer][None, :], w_in_r, tm)
        qb, kb, vb = _mla_prep(proj, cs, wq_r, wkv_r, gc, gqk, seq, tm)
        out_b = _mla_attention(qb, kb, vb, batch, seq, 512)
        out_a = _dilated_attention(proj, head_block_ones, g_a, band_bias, batch, seq)

        rw = jnp.pad(router_w[layer], ((0, 0), (0, LANES - N_EXPERTS)))
        rw_hi = rw.astype(BF16)
        rw_lo = (rw - rw_hi.astype(F32)).astype(BF16)
        rb = jnp.pad(router_b[layer], (0, LANES - N_EXPERTS), constant_values=-jnp.inf)[None, :]
        x1, hp, route, counts = _out_proj(x2, out_a, out_b, w_out[layer].astype(BF16),
                                          ffn_norm_g[layer][None, :], jnp.concatenate([rw_hi, rw_lo], axis=1), rb, tm)

        padded, pad_end, dest, block_expert, n_used, n_slots = _slot_layout(
            route, counts[0, :N_EXPERTS], tm, expert_bm)
        xs = _dispatch(padded, pad_end, n_used, dest, hp, n_slots, tm, expert_bm)
        ys = _expert_ffn(block_expert, n_used, xs, w_gate_up[layer], b_gate_up[layer],
                         w_down[layer], b_down[layer], expert_bm)
        tables, route_local = _combine_tables(route, pad_end - padded, tm, COMBINE_LOCAL_ROWS)
        x2 = _combine(tables, route_local, x1, ys, tm, COMBINE_LOCAL_ROWS)
    return x2.reshape(batch, seq, d)
```

```python
import functools
import math

import jax
import jax.numpy as jnp
from jax import lax
from jax.experimental import pallas as pl
from jax.experimental.pallas import tpu as pltpu

A_HEADS = 8
A_HEAD_DIM = 64
A_WIDTH = A_HEADS * A_HEAD_DIM
DILATED_PATTERNS = ((128, 1), (512, 4), (2048, 16))
BAND_BLOCK = 128

B_HEADS = 4
QK_NOPE_DIM = 128
QK_ROPE_DIM = 64
QK_HEAD_DIM = QK_NOPE_DIM + QK_ROPE_DIM
V_HEAD_DIM = 128
Q_LORA_RANK = 256
KV_LORA_RANK = 256
B_WIDTH = B_HEADS * V_HEAD_DIM
ROPE_THETA = 10000.0

NUM_BUCKETS = 32
MAX_DISTANCE = 2048

N_EXPERTS = 32
TOP_K = 4
SWIGLU_LIMIT = 7.0
SWIGLU_ALPHA = 1.702
RMS_EPS = 1e-6

LANES = 128
MASK_VALUE = -1e30
LOG2E = math.log2(math.e)
PROJ_WIDTH = 3 * A_WIDTH + Q_LORA_RANK + KV_LORA_RANK + 2 * QK_ROPE_DIM
VMEM_LIMIT = 56 * 1024 * 1024

F32 = jnp.float32
BF16 = jnp.bfloat16


def _dot(a, b):
    return jnp.dot(a, b, preferred_element_type=F32)


def _dot_nt(a, b):
    return lax.dot_general(a, b, (((1,), (1,)), ((), ())), preferred_element_type=F32)


def _split_dot(x, m):
    hi = x.astype(BF16)
    lo = (x - hi.astype(F32)).astype(BF16)
    return _dot(hi, m) + _dot(lo, m)


def _params(n_parallel=1):
    return pltpu.CompilerParams(
        dimension_semantics=("arbitrary",) * n_parallel, vmem_limit_bytes=VMEM_LIMIT)


def _in_proj_kernel(x_ref, g_ref, w_ref, o_ref):
    x = x_ref[...]
    ms = jnp.mean(x * x, axis=-1, keepdims=True)
    h = (x * lax.rsqrt(ms + RMS_EPS) * g_ref[...]).astype(BF16)
    o_ref[...] = _dot(h, w_ref[...])


def _in_proj(x2, g, w, tm):
    t, d = x2.shape
    n = w.shape[1]
    return pl.pallas_call(
        _in_proj_kernel,
        grid=(t // tm,),
        in_specs=[pl.BlockSpec((tm, d), lambda i: (i, 0)),
                  pl.BlockSpec((1, d), lambda i: (0, 0)),
                  pl.BlockSpec((d, n), lambda i: (0, 0))],
        out_specs=pl.BlockSpec((tm, n), lambda i: (i, 0)),
        out_shape=jax.ShapeDtypeStruct((t, n), F32),
        compiler_params=_params(),
        name="in_proj",
    )(x2, g, w)


def _mla_prep_kernel(c_ref, kpe_ref, cs_ref, wq_ref, wkv_ref, gc_ref, gqk_ref, q_ref, k_ref, v_ref,
                     *, q_scale):
    c = c_ref[...]
    gc = gc_ref[...]

    def lora_norm(z, g):
        ms = jnp.mean(z * z, axis=-1, keepdims=True)
        return (z * lax.rsqrt(ms + RMS_EPS) * g).astype(BF16)

    cq = lora_norm(c[:, :Q_LORA_RANK], gc[:, :Q_LORA_RANK])
    ckv = lora_norm(c[:, Q_LORA_RANK:], gc[:, Q_LORA_RANK:])
    qb = _dot(cq, wq_ref[...])
    kvb = _dot(ckv, wkv_ref[...])
    kper = kpe_ref[...]
    cs = cs_ref[...]
    gqk = gqk_ref[...]
    tm = c.shape[0]

    row = lax.broadcasted_iota(jnp.int32, (LANES, LANES), 0)
    ones_all = jnp.ones((LANES, LANES), BF16)
    ones_lo = jnp.where(row < QK_ROPE_DIM, 1.0, 0.0).astype(BF16)
    lane = lax.broadcasted_iota(jnp.int32, (tm, LANES), 1)

    def rope(z, g_row):
        t = z * (g_row * cs)
        return jnp.where(lane < QK_ROPE_DIM, t + pltpu.roll(t, QK_ROPE_DIM, 1), 0.0)

    k_rope = rope(kper, gqk[3:4])
    kpe_ss = _split_dot(kper * kper, ones_lo)
    for h in range(B_HEADS):
        qn = qb[:, 2 * LANES * h: 2 * LANES * h + LANES]
        qr = qb[:, 2 * LANES * h + LANES: 2 * LANES * (h + 1)]
        ss = _split_dot(qn * qn, ones_all) + _split_dot(qr * qr, ones_lo)
        rs = lax.rsqrt(ss * (1.0 / QK_HEAD_DIM) + RMS_EPS) * q_scale
        q_ref[h, :, :LANES] = (qn * gqk[0:1] * rs).astype(BF16)
        q_ref[h, :, LANES:] = (rope(qr, gqk[1:2]) * rs).astype(BF16)
        kn = kvb[:, LANES * h: LANES * (h + 1)]
        ssk = _split_dot(kn * kn, ones_all) + kpe_ss
        rsk = lax.rsqrt(ssk * (1.0 / QK_HEAD_DIM) + RMS_EPS)
        k_ref[h, :, :LANES] = (kn * gqk[2:3] * rsk).astype(BF16)
        k_ref[h, :, LANES:] = (k_rope * rsk).astype(BF16)
    v_ref[...] = kvb[:, B_HEADS * QK_NOPE_DIM:].astype(BF16)


def _mla_prep(proj, cs, wq, wkv, gc, gqk, seq, tm):
    t = proj.shape[0]
    c_width = Q_LORA_RANK + KV_LORA_RANK
    c_block = 3 * A_WIDTH // c_width
    kpe_block = (3 * A_WIDTH + c_width) // LANES
    pos_blocks = seq // tm
    kernel = functools.partial(_mla_prep_kernel, q_scale=QK_HEAD_DIM ** -0.5 * LOG2E)
    return pl.pallas_call(
        kernel,
        grid=(t // tm,),
        in_specs=[pl.BlockSpec((tm, c_width), lambda i: (i, c_block)),
                  pl.BlockSpec((tm, LANES), lambda i: (i, kpe_block)),
                  pl.BlockSpec((tm, LANES), lambda i: (i % pos_blocks, 0)),
                  pl.BlockSpec(wq.shape, lambda i: (0, 0)),
                  pl.BlockSpec(wkv.shape, lambda i: (0, 0)),
                  pl.BlockSpec(gc.shape, lambda i: (0, 0)),
                  pl.BlockSpec(gqk.shape, lambda i: (0, 0))],
        out_specs=[pl.BlockSpec((B_HEADS, tm, 2 * LANES), lambda i: (0, i, 0)),
                   pl.BlockSpec((B_HEADS, tm, 2 * LANES), lambda i: (0, i, 0)),
                   pl.BlockSpec((tm, B_WIDTH), lambda i: (i, 0))],
        out_shape=[jax.ShapeDtypeStruct((B_HEADS, t, 2 * LANES), BF16),
                   jax.ShapeDtypeStruct((B_HEADS, t, 2 * LANES), BF16),
                   jax.ShapeDtypeStruct((t, B_WIDTH), BF16)],
        compiler_params=_params(),
        name="mla_prep",
    )(proj, proj, cs, wq, wkv, gc, gqk)


def _mla_attention_kernel(q_ref, k_ref, v_ref, o_ref, *, tq):
    seq = q_ref.shape[0]
    row = lax.broadcasted_iota(jnp.int32, (tq, tq), 0)
    col = lax.broadcasted_iota(jnp.int32, (tq, tq), 1)
    diag_mask = jnp.where(col <= row, 0.0, MASK_VALUE)
    v_ext = jnp.concatenate([v_ref[...], jnp.ones(v_ref.shape, v_ref.dtype)], axis=1)
    dv = v_ref.shape[1]
    for i in range(seq // tq):
        q = q_ref[i * tq:(i + 1) * tq, :]
        s_diag = _dot_nt(q, k_ref[i * tq:(i + 1) * tq, :]) + diag_mask
        m = jnp.max(s_diag, axis=-1, keepdims=True)
        if i > 0:
            s_past = _dot_nt(q, k_ref[:i * tq, :])
            m = jnp.maximum(m, jnp.max(s_past, axis=-1, keepdims=True))
        o = _dot(jnp.exp2(s_diag - m).astype(BF16), v_ext[i * tq:(i + 1) * tq, :])
        if i > 0:
            o = o + _dot(jnp.exp2(s_past - m).astype(BF16), v_ext[:i * tq, :])
        o_ref[i * tq:(i + 1) * tq, :] = (o[:, :dv] / o[:, dv:]).astype(o_ref.dtype)


def _mla_attention(qb, kb, vb, batch, seq, tq):
    t = vb.shape[0]
    kernel = functools.partial(_mla_attention_kernel, tq=tq)
    return pl.pallas_call(
        kernel,
        grid=(batch, B_HEADS),
        in_specs=[pl.BlockSpec((None, seq, 2 * LANES), lambda b, h: (h, b, 0)),
                  pl.BlockSpec((None, seq, 2 * LANES), lambda b, h: (h, b, 0)),
                  pl.BlockSpec((seq, V_HEAD_DIM), lambda b, h: (b, h))],
        out_specs=pl.BlockSpec((seq, V_HEAD_DIM), lambda b, h: (b, h)),
        out_shape=jax.ShapeDtypeStruct((t, B_WIDTH), BF16),
        compiler_params=_params(2),
        name="mla_attention",
    )(qb, kb, vb)


def _dilated_kernel(q_ref, k_ref, v_ref, bd_ref, g_ref, bias_ref, o_ref, qs, ks, acc_o, acc_l, acc_m,
                    *, unroll):
    seq = q_ref.shape[0]
    n_tiles = seq // BAND_BLOCK
    low = lax.broadcasted_iota(jnp.int32, (BAND_BLOCK, LANES), 1) < A_HEAD_DIM
    bd = bd_ref[...]
    g = g_ref[...]

    def head_norm(z, g_row):
        ss = _split_dot(z * z, bd)
        return z * lax.rsqrt(ss * (1.0 / A_HEAD_DIM) + RMS_EPS) * g_row

    qs[...] = head_norm(q_ref[...], g[0:1]) * (A_HEAD_DIM ** -0.5 * LOG2E)
    ks[...] = head_norm(k_ref[...], g[1:2])
    ones = jnp.ones((2 * BAND_BLOCK, LANES), BF16)

    for p, (window, dil) in enumerate(DILATED_PATTERNS):
        blocks_per_class = n_tiles // dil
        span = BAND_BLOCK * dil

        def rows_at(start, dil=dil):
            if dil == 1:
                return pl.ds(start, BAND_BLOCK)
            return pl.ds(start, BAND_BLOCK, stride=dil)

        def tile(i, carry, p=p, blocks_per_class=blocks_per_class, span=span, rows_at=rows_at):
            r = i // blocks_per_class
            j = i % blocks_per_class
            cur = rows_at(r + j * span)
            prev = rows_at(r + jnp.maximum(j - 1, 0) * span)
            q = qs[cur, :]
            q2 = jnp.concatenate([jnp.where(low, q, 0.0), jnp.where(low, 0.0, q)], axis=0).astype(BF16)
            k_band = jnp.concatenate([ks[prev, :], ks[cur, :]], axis=0).astype(BF16)
            s = _dot_nt(q2, k_band) + bias_ref[p, jnp.where(j == 0, 1, 0)]
            m = jnp.max(s, axis=-1, keepdims=True)
            pr = jnp.exp2(s - m).astype(BF16)
            v_band = jnp.concatenate([v_ref[prev, :], v_ref[cur, :]], axis=0).astype(BF16)
            o = _dot(pr, jnp.concatenate([v_band, ones], axis=1))
            top, bot = o[:BAND_BLOCK], o[BAND_BLOCK:]
            acc_o[p, cur, :] = jnp.where(low, top[:, :LANES], bot[:, :LANES])
            acc_l[p, cur, :] = jnp.where(low, top[:, LANES:], bot[:, LANES:])
            acc_m[p, cur, :] = jnp.where(low, m[:BAND_BLOCK], m[BAND_BLOCK:])
            return carry

        lax.fori_loop(0, n_tiles, tile, 0, unroll=unroll)

    m_all = jnp.maximum(jnp.maximum(acc_m[0], acc_m[1]), acc_m[2])
    num = jnp.zeros((seq, LANES), F32)
    den = jnp.zeros((seq, LANES), F32)
    for p in range(len(DILATED_PATTERNS)):
        w = jnp.exp2(acc_m[p] - m_all)
        num = num + w * acc_o[p]
        den = den + w * acc_l[p]
    o_ref[...] = (num / den).astype(o_ref.dtype)


def _dilated_attention(proj, bd, g, bias, batch, seq):
    t = proj.shape[0]
    pairs = A_WIDTH // LANES
    n_pat = len(DILATED_PATTERNS)
    return pl.pallas_call(
        functools.partial(_dilated_kernel, unroll=16),
        grid=(batch, pairs),
        in_specs=[pl.BlockSpec((seq, LANES), lambda b, c: (b, c)),
                  pl.BlockSpec((seq, LANES), lambda b, c: (b, pairs + c)),
                  pl.BlockSpec((seq, LANES), lambda b, c: (b, 2 * pairs + c)),
                  pl.BlockSpec((LANES, LANES), lambda b, c: (0, 0)),
                  pl.BlockSpec((None, 2, LANES), lambda b, c: (c, 0, 0)),
                  pl.BlockSpec((None, n_pat, 2, 2 * BAND_BLOCK, 2 * BAND_BLOCK),
                               lambda b, c: (c, 0, 0, 0, 0))],
        out_specs=pl.BlockSpec((seq, LANES), lambda b, c: (b, c)),
        out_shape=jax.ShapeDtypeStruct((t, A_WIDTH), BF16),
        scratch_shapes=[pltpu.VMEM((seq, LANES), F32)] * 2
                       + [pltpu.VMEM((n_pat, seq, LANES), F32)] * 3,
        compiler_params=_params(2),
        name="dilated_attention",
    )(proj, proj, proj, bd, g, bias)


def _t5_bucket(dist):
    max_exact = NUM_BUCKETS // 2
    df = jnp.maximum(dist, 1).astype(F32)
    log_bucket = max_exact + (jnp.log(df / max_exact) / math.log(MAX_DISTANCE / max_exact)
                              * (NUM_BUCKETS - max_exact)).astype(jnp.int32)
    log_bucket = jnp.minimum(log_bucket, NUM_BUCKETS - 1)
    return jnp.where(dist < max_exact, dist, log_bucket)


def _band_bias(rel_bias):
    n = BAND_BLOCK
    qi = jnp.arange(n)[:, None]
    kj = jnp.arange(n)[None, :]
    buckets = jnp.arange(NUM_BUCKETS)
    tables = []
    for window, dil in DILATED_PATTERNS:
        steps = window // dil
        halves = []
        for back in (qi - kj + n, qi - kj):
            onehot = (_t5_bucket(jnp.maximum(back, 0) * dil)[:, :, None] == buckets).astype(F32)
            vals = jnp.einsum('qkb,bh->hqk', onehot, rel_bias.astype(F32),
                              precision=lax.Precision.HIGHEST) * LOG2E
            halves.append(jnp.where(((back >= 0) & (back <= steps))[None], vals, MASK_VALUE))
        prev, cur = halves
        normal = jnp.concatenate([prev, cur], axis=-1)
        first = jnp.concatenate([jnp.full_like(prev, MASK_VALUE), cur], axis=-1)
        tables.append(jnp.stack([normal, first], axis=1))
    tab = jnp.stack(tables, axis=1)
    tab = tab.reshape(A_HEADS // 2, 2, len(DILATED_PATTERNS), 2, n, 2 * n)
    return tab.transpose(0, 2, 3, 1, 4, 5).reshape(A_HEADS // 2, len(DILATED_PATTERNS), 2, 2 * n, 2 * n)


def _pack_pairs(v):
    half = v.shape[1] // 2
    bits = pltpu.bitcast(v, jnp.uint32)
    return (bits[:, :half] >> 16) | (bits[:, half:] & jnp.uint32(0xFFFF0000))


def _unpack_pairs(p):
    return (pltpu.bitcast(p << 16, F32), pltpu.bitcast(p & jnp.uint32(0xFFFF0000), F32))


def _out_proj_kernel(x_ref, a_ref, b_ref, w_ref, g_ref, rw_ref, rb_ref,
                     x1_ref, hp_ref, route_ref, cnt_ref, carry):
    i = pl.program_id(0)
    tm = x_ref.shape[0]

    @pl.when(i == 0)
    def _():
        carry[...] = jnp.zeros_like(carry)

    w = w_ref[...]
    x1 = x_ref[...] + _dot(a_ref[...], w[:A_WIDTH]) + _dot(b_ref[...], w[A_WIDTH:])
    x1_ref[...] = x1
    ms = jnp.mean(x1 * x1, axis=-1, keepdims=True)
    h = x1 * lax.rsqrt(ms + RMS_EPS) * g_ref[...]
    hi = h.astype(BF16)
    hi_f = hi.astype(F32)
    hp_ref[...] = _pack_pairs(hi_f)
    lo = (h - hi_f).astype(BF16)
    rw = rw_ref[...]
    hw = _dot(hi, rw)
    logits = hw[:, :LANES] + hw[:, LANES:] + _dot(lo, rw[:, :LANES]) + rb_ref[...]

    lane = lax.broadcasted_iota(jnp.int32, (tm, LANES), 1).astype(F32)
    remaining = logits
    vals, hots = [], []
    for _ in range(TOP_K):
        m = jnp.max(remaining, axis=-1, keepdims=True)
        first = jnp.min(jnp.where(remaining == m, lane, float(LANES)), axis=-1, keepdims=True)
        hot = lane == first
        remaining = jnp.where(hot, -jnp.inf, remaining)
        vals.append(m)
        hots.append(hot)
    exps = [jnp.exp(v - vals[0]) for v in vals]
    den = exps[0] + exps[1] + exps[2] + exps[3]

    chosen = jnp.zeros((tm, LANES), F32)
    for hot in hots:
        chosen = chosen + jnp.where(hot, 1.0, 0.0)
    r = lax.broadcasted_iota(jnp.int32, (tm, tm), 0)
    c = lax.broadcasted_iota(jnp.int32, (tm, tm), 1)
    earlier = jnp.where(r > c, 1.0, 0.0).astype(BF16)
    before = carry[...] + _dot(earlier, chosen.astype(BF16))
    carry[...] = carry[...] + jnp.sum(chosen, axis=0, keepdims=True)
    cnt_ref[...] = jnp.broadcast_to(carry[...], cnt_ref.shape)

    route = jnp.zeros((tm, LANES), F32)
    for k in range(TOP_K):
        first = jnp.sum(jnp.where(hots[k], lane, 0.0), axis=-1, keepdims=True)
        rank = jnp.sum(jnp.where(hots[k], before, 0.0), axis=-1, keepdims=True)
        route = route + jnp.where(lane == float(k), first, 0.0)
        route = route + jnp.where(lane == float(TOP_K + k), rank, 0.0)
        route = route + jnp.where(lane == float(2 * TOP_K + k), exps[k] / den, 0.0)
    route_ref[...] = route


def _out_proj(x2, out_a, out_b, w, g, rw, rb, tm):
    t, d = x2.shape
    return pl.pallas_call(
        _out_proj_kernel,
        grid=(t // tm,),
        in_specs=[pl.BlockSpec((tm, d), lambda i: (i, 0)),
                  pl.BlockSpec((tm, A_WIDTH), lambda i: (i, 0)),
                  pl.BlockSpec((tm, B_WIDTH), lambda i: (i, 0)),
                  pl.BlockSpec(w.shape, lambda i: (0, 0)),
                  pl.BlockSpec((1, d), lambda i: (0, 0)),
                  pl.BlockSpec(rw.shape, lambda i: (0, 0)),
                  pl.BlockSpec((1, LANES), lambda i: (0, 0))],
        out_specs=[pl.BlockSpec((tm, d), lambda i: (i, 0)),
                   pl.BlockSpec((tm, d // 2), lambda i: (i, 0)),
                   pl.BlockSpec((tm, LANES), lambda i: (i, 0)),
                   pl.BlockSpec((8, LANES), lambda i: (0, 0))],
        out_shape=[jax.ShapeDtypeStruct((t, d), F32),
                   jax.ShapeDtypeStruct((t, d // 2), jnp.uint32),
                   jax.ShapeDtypeStruct((t, LANES), F32),
                   jax.ShapeDtypeStruct((8, LANES), F32)],
        scratch_shapes=[pltpu.VMEM((1, LANES), F32)],
        compiler_params=_params(),
        name="out_proj",
    )(x2, out_a, out_b, w, g, rw, rb)


def _dispatch_kernel(pad_ref, end_ref, nb_ref, dest_hbm, h_ref, xs_hbm, dest_s0, dest_s1, zeros,
                     sem_i, sem_z, sem_o, *, bm):
    i = pl.program_id(0)
    steps = pl.num_programs(0)
    tm = h_ref.shape[0]
    n_blocks = xs_hbm.shape[0] // bm
    dest_s = (dest_s0, dest_s1)

    def idx_copy(step, slot):
        return pltpu.make_async_copy(dest_hbm.at[step], dest_s[slot], sem_i.at[slot])

    @pl.when(i == 0)
    def _():
        idx_copy(0, 0).start()
        zeros[...] = jnp.zeros_like(zeros)

        def zero_block(start):
            return pltpu.make_async_copy(zeros, xs_hbm.at[pl.ds(pl.multiple_of(start, bm), bm), :], sem_z)

        for e in range(N_EXPERTS):
            @pl.when(pad_ref[e] > 0)
            def _():
                zero_block(end_ref[e] - bm).start()

        def start_unused(b, carry):
            zero_block(b * bm).start()
            return carry

        def wait_unused(b, carry):
            zero_block(b * bm).wait()
            return carry

        lax.fori_loop(nb_ref[0], n_blocks, start_unused, 0)
        for e in range(N_EXPERTS):
            @pl.when(pad_ref[e] > 0)
            def _():
                zero_block(end_ref[e] - bm).wait()
        lax.fori_loop(nb_ref[0], n_blocks, wait_unused, 0)

    for slot in range(2):
        @pl.when(i % 2 == slot)
        def _(slot=slot):
            @pl.when(i + 1 < steps)
            def _():
                idx_copy(i + 1, 1 - slot).start()

            idx_copy(i, slot).wait()

            def body(t, carry):
                for k in range(TOP_K):
                    pltpu.make_async_copy(h_ref.at[pl.ds(t, 1), :],
                                          xs_hbm.at[pl.ds(dest_s[slot][t * TOP_K + k], 1), :],
                                          sem_o).start(priority=k % 2)
                return carry

            lax.fori_loop(0, tm, body, 0, unroll=8)
    for _ in range(TOP_K):
        pltpu.make_async_copy(h_ref, xs_hbm.at[pl.ds(0, tm), :], sem_o).wait()


def _dispatch(padded, pad_end, n_used, dest, hp, n_slots, tm, bm):
    t, half = hp.shape
    grid_spec = pltpu.PrefetchScalarGridSpec(
        num_scalar_prefetch=3,
        grid=(t // tm,),
        in_specs=[pl.BlockSpec(memory_space=pl.ANY),
                  pl.BlockSpec((tm, half), lambda i, p, e, n: (i, 0))],
        out_specs=pl.BlockSpec(memory_space=pl.ANY),
        scratch_shapes=[pltpu.SMEM((tm * TOP_K,), jnp.int32),
                        pltpu.SMEM((tm * TOP_K,), jnp.int32),
                        pltpu.VMEM((bm, half), jnp.uint32),
                        pltpu.SemaphoreType.DMA((2,)),
                        pltpu.SemaphoreType.DMA(()),
                        pltpu.SemaphoreType.DMA(())],
    )
    return pl.pallas_call(
        functools.partial(_dispatch_kernel, bm=bm),
        grid_spec=grid_spec,
        out_shape=jax.ShapeDtypeStruct((n_slots, half), jnp.uint32),
        compiler_params=_params(),
        name="dispatch",
    )(padded, pad_end, n_used, dest, hp)


EXPERT_SUB_ROWS = 256


def _expert_ffn_kernel(be_ref, nb_ref, next_ref, x_ref, wgu_hbm, bgu_ref, wd_hbm, bd_ref, y_ref,
                       wgu_f, wd_f, wgu_s, wd_s, sem_w):
    i = pl.program_id(0)
    d_ff = wd_f.shape[0]
    half = x_ref.shape[1]

    def weight_copies(e):
        return (pltpu.make_async_copy(wgu_hbm.at[e], wgu_f, sem_w.at[0]),
                pltpu.make_async_copy(wd_hbm.at[e], wd_f, sem_w.at[1]))

    @pl.when(jnp.logical_and(i == 0, nb_ref[0] > 0))
    def _():
        for copy in weight_copies(be_ref[0]):
            copy.start()

    @pl.when(i >= nb_ref[0])
    def _():
        y_ref[...] = jnp.zeros_like(y_ref)

    @pl.when(i < nb_ref[0])
    def _():
        changed = jnp.logical_or(i == 0, be_ref[i] != be_ref[jnp.maximum(i - 1, 0)])

        @pl.when(changed)
        def _():
            for copy in weight_copies(be_ref[i]):
                copy.wait()
            wgu_s[...] = wgu_f[...].astype(BF16)
            wd_s[...] = wd_f[...].astype(BF16)

            @pl.when(next_ref[i] >= 0)
            def _():
                for copy in weight_copies(next_ref[i]):
                    copy.start()

        for r0 in range(0, x_ref.shape[0], EXPERT_SUB_ROWS):
            rows = slice(r0, r0 + EXPERT_SUB_ROWS)
            x_lo, x_hi = _unpack_pairs(x_ref[rows, :])
            gu = (_dot(x_lo.astype(BF16), wgu_s[:half, :]) + _dot(x_hi.astype(BF16), wgu_s[half:, :])
                  + bgu_ref[...])
            gate = jnp.minimum(gu[:, :d_ff], SWIGLU_LIMIT)
            up = jnp.clip(gu[:, d_ff:], -SWIGLU_LIMIT, SWIGLU_LIMIT)
            glu = gate * jax.nn.sigmoid(SWIGLU_ALPHA * gate)
            act = ((up + 1.0) * glu).astype(BF16)
            y = _dot(act, wd_s[...]) + bd_ref[...]
            y_ref[rows, :] = _pack_pairs(y.astype(BF16).astype(F32))


def _expert_ffn(block_expert, n_used, next_expert, xs, w_gate_up, b_gate_up, w_down, b_down, bm):
    n_slots, half = xs.shape
    n_e, d, two_ff = w_gate_up.shape
    d_ff = two_ff // 2
    n_blocks = n_slots // bm

    def used(i, nb):
        return jnp.minimum(i, jnp.maximum(nb[0] - 1, 0))

    grid_spec = pltpu.PrefetchScalarGridSpec(
        num_scalar_prefetch=3,
        grid=(n_blocks,),
        in_specs=[pl.BlockSpec((bm, half), lambda i, be, nb, nx: (used(i, nb), 0)),
                  pl.BlockSpec(memory_space=pl.ANY),
                  pl.BlockSpec((None, 1, two_ff), lambda i, be, nb, nx: (be[i], 0, 0)),
                  pl.BlockSpec(memory_space=pl.ANY),
                  pl.BlockSpec((None, 1, d), lambda i, be, nb, nx: (be[i], 0, 0))],
        out_specs=pl.BlockSpec((bm, half), lambda i, be, nb, nx: (i, 0)),
        scratch_shapes=[pltpu.VMEM((d, two_ff), F32), pltpu.VMEM((d_ff, d), F32),
                        pltpu.VMEM((d, two_ff), BF16), pltpu.VMEM((d_ff, d), BF16),
                        pltpu.SemaphoreType.DMA((2,))],
    )
    return pl.pallas_call(
        _expert_ffn_kernel,
        grid_spec=grid_spec,
        out_shape=jax.ShapeDtypeStruct((n_slots, half), jnp.uint32),
        compiler_params=_params(),
        name="expert_ffn",
    )(block_expert, n_used, next_expert, xs, w_gate_up, b_gate_up.reshape(n_e, 1, two_ff),
      w_down, b_down.reshape(n_e, 1, d))


COMBINE_LOCAL_ROWS = 2560
SEG_ALIGN = 8


def _pieces(limit):
    sizes = []
    size = 1 << (limit.bit_length() - 1)
    while size >= SEG_ALIGN:
        sizes.append(size)
        size //= 2
    return sizes


def _combine_kernel(src_ref, len_ref, dst_ref, tot_ref, route_ref, x1_ref, ys_hbm, o_ref, ybuf, sem_g):
    j = pl.program_id(0)
    blocks = pl.num_programs(0) - 1
    tm = x1_ref.shape[0]
    half = x1_ref.shape[1] // 2
    local_rows = ybuf.shape[1]

    @pl.when(j == 0)
    def _():
        ybuf[...] = jnp.zeros_like(ybuf)

    @pl.when(j < blocks)
    def _():
        for s in range(2):
            @pl.when(j % 2 == s)
            def _(s=s):
                for e in range(N_EXPERTS):
                    seg = j * N_EXPERTS + e
                    n = len_ref[seg]
                    src = src_ref[seg]
                    dst = dst_ref[seg]
                    for size in _pieces(tm + 2 * SEG_ALIGN):
                        @pl.when((n & size) != 0)
                        def _(size=size, src=src, dst=dst):
                            pltpu.make_async_copy(
                                ys_hbm.at[pl.ds(pl.multiple_of(src, SEG_ALIGN), size), :],
                                ybuf.at[s, pl.ds(pl.multiple_of(dst, SEG_ALIGN), size), :],
                                sem_g.at[s]).start()
                        step = jnp.where((n & size) != 0, size, 0)
                        src = src + step
                        dst = dst + step

    @pl.when(j >= 1)
    def _():
        slot = (j - 1) % 2
        total = tot_ref[j - 1]
        for size in _pieces(local_rows):
            @pl.when((total & size) != 0)
            def _(size=size):
                pltpu.make_async_copy(ys_hbm.at[pl.ds(0, size), :], ybuf.at[slot, pl.ds(0, size), :],
                                      sem_g.at[slot]).wait()

        route = route_ref[...]
        col = lax.broadcasted_iota(jnp.int32, (tm, local_rows), 1).astype(F32)
        g = jnp.zeros((tm, local_rows), F32)
        for k in range(TOP_K):
            pos = route[:, TOP_K + k: TOP_K + k + 1]
            gate = route[:, 2 * TOP_K + k: 2 * TOP_K + k + 1]
            g = jnp.where(col == pos, gate, g)
        g = g.astype(BF16)
        lo, hi = _unpack_pairs(ybuf[slot])
        o_ref[:, :half] = x1_ref[:, :half] + _dot(g, lo.astype(BF16))
        o_ref[:, half:] = x1_ref[:, half:] + _dot(g, hi.astype(BF16))


def _combine(tables, route, x1, ys, tm, local_rows):
    t, d = x1.shape
    half = d // 2

    def summed(j, *_):
        return (jnp.maximum(j - 1, 0), 0)

    grid_spec = pltpu.PrefetchScalarGridSpec(
        num_scalar_prefetch=4,
        grid=(t // tm + 1,),
        in_specs=[pl.BlockSpec((tm, LANES), summed),
                  pl.BlockSpec((tm, d), summed),
                  pl.BlockSpec(memory_space=pl.ANY)],
        out_specs=pl.BlockSpec((tm, d), summed),
        scratch_shapes=[pltpu.VMEM((2, local_rows, half), jnp.uint32),
                        pltpu.SemaphoreType.DMA((2,))],
    )
    return pl.pallas_call(
        _combine_kernel,
        grid_spec=grid_spec,
        out_shape=jax.ShapeDtypeStruct((t, d), F32),
        compiler_params=_params(),
        name="combine",
    )(*tables, route, x1, ys)


def _combine_tables(route, pad_start, tm, local_rows):
    t = route.shape[0]
    blocks = t // tm
    idx = route[:, :TOP_K].astype(jnp.int32).reshape(blocks, tm * TOP_K)
    experts = jnp.arange(N_EXPERTS, dtype=jnp.int32)
    onehot = (idx[:, :, None] == experts[None, None, :]).astype(F32)
    n = jnp.sum(onehot, axis=1).astype(jnp.int32)
    before = jnp.cumsum(n, axis=0) - n
    first = pad_start[None, :] + before
    src = first // SEG_ALIGN * SEG_ALIGN
    length = jnp.where(n > 0, (first + n + SEG_ALIGN - 1) // SEG_ALIGN * SEG_ALIGN - src, 0)
    dst = jnp.cumsum(length, axis=1) - length
    total = jnp.sum(length, axis=1)
    assert local_rows >= tm * TOP_K + N_EXPERTS * 2 * (SEG_ALIGN - 1)
    base = (dst + first - src - before).astype(F32)
    lookup = jnp.einsum('bne,be->bn', onehot, base, precision=lax.Precision.HIGHEST)
    local = lookup.reshape(t, TOP_K) + route[:, TOP_K:2 * TOP_K]
    route = jnp.concatenate([route[:, :TOP_K], local, route[:, 2 * TOP_K:]], axis=1)
    tables = (src.reshape(-1).astype(jnp.int32), length.reshape(-1).astype(jnp.int32),
              dst.reshape(-1).astype(jnp.int32), total.astype(jnp.int32))
    return tables, route


def _slot_layout(route, counts, tm, bm):
    t = route.shape[0]
    counts = counts.astype(jnp.int32)
    padded = (counts + bm - 1) // bm * bm
    pad_end = jnp.cumsum(padded).astype(jnp.int32)
    pad_start = pad_end - padded
    idx = route[:, :TOP_K].astype(jnp.int32)
    rank = route[:, TOP_K:2 * TOP_K].astype(jnp.int32)
    onehot = (idx[:, :, None] == jnp.arange(N_EXPERTS, dtype=jnp.int32)).astype(F32)
    start = jnp.einsum('tke,e->tk', onehot, pad_start.astype(F32), precision=lax.Precision.HIGHEST)
    dest = (start.astype(jnp.int32) + rank).reshape(t // tm, tm * TOP_K)
    n_blocks = -(-t * TOP_K // bm) + N_EXPERTS
    block_start = jnp.arange(n_blocks, dtype=jnp.int32) * bm
    block_expert = jnp.minimum(jnp.sum(pad_end[None, :] <= block_start[:, None], axis=1),
                               N_EXPERTS - 1).astype(jnp.int32)
    n_used = (pad_end[-1:] // bm).astype(jnp.int32)
    following = jnp.take(pad_end, block_expert) // bm
    next_expert = jnp.where(following < n_used[0],
                            jnp.take(block_expert, jnp.minimum(following, n_blocks - 1)), -1).astype(jnp.int32)
    return padded, pad_end, dest, block_expert, next_expert, n_used, n_blocks * bm


def _rot_cols(w):
    half = w.shape[-1] // 2
    return jnp.concatenate([-w[..., half:], w[..., :half]], axis=-1)


def _swap_halves(g):
    half = g.shape[-1] // 2
    return jnp.concatenate([g[..., half:], g[..., :half]], axis=-1)


def kernel(x, attn_norm_g, w_in, a_q_norm_g, a_k_norm_g, rel_bias, q_a_norm_g, w_q_b, kv_a_norm_g,
           w_kv_b, b_q_norm_g, b_k_norm_g, w_out, ffn_norm_g, router_w, router_b, w_gate_up,
           b_gate_up, w_down, b_down):
    batch, seq, d = x.shape
    t = batch * seq
    depth = w_in.shape[0]
    tm = 512
    expert_bm = 512

    pos = jnp.arange(seq, dtype=F32)
    inv_freq = ROPE_THETA ** (-jnp.arange(0, QK_ROPE_DIM, 2, dtype=F32) / QK_ROPE_DIM)
    ang = pos[:, None] * inv_freq[None, :]
    cos, sin = jnp.cos(ang), jnp.sin(ang)
    cs = jnp.concatenate([cos, cos, sin, sin], axis=-1)

    row = jnp.arange(LANES)[:, None] // A_HEAD_DIM
    col = jnp.arange(LANES)[None, :] // A_HEAD_DIM
    head_block_ones = (row == col).astype(BF16)
    band_bias = _band_bias(rel_bias)

    x2 = x.reshape(t, d)
    for layer in range(depth):
        kpe_off = 3 * A_WIDTH + Q_LORA_RANK + KV_LORA_RANK
        w_kpe = w_in[layer][:, kpe_off:]
        w_in_r = jnp.concatenate([w_in[layer][:, :kpe_off], w_kpe, _rot_cols(w_kpe)], axis=1).astype(BF16)

        wq = w_q_b[layer].reshape(Q_LORA_RANK, B_HEADS, QK_HEAD_DIM)
        wq_rope = wq[..., QK_NOPE_DIM:]
        wq_r = jnp.concatenate([wq, _rot_cols(wq_rope)], axis=-1).reshape(Q_LORA_RANK, -1).astype(BF16)
        wkv = w_kv_b[layer].reshape(KV_LORA_RANK, B_HEADS, QK_NOPE_DIM + V_HEAD_DIM)
        wkv_r = jnp.concatenate([wkv[..., :QK_NOPE_DIM].reshape(KV_LORA_RANK, -1),
                                 wkv[..., QK_NOPE_DIM:].reshape(KV_LORA_RANK, -1)], axis=1).astype(BF16)
        gc = jnp.concatenate([q_a_norm_g[layer], kv_a_norm_g[layer]])[None, :]
        gq, gk = b_q_norm_g[layer], b_k_norm_g[layer]

        def rope_gain(gr):
            return jnp.concatenate([gr, _swap_halves(gr)])

        gqk = jnp.stack([gq[:QK_NOPE_DIM], rope_gain(gq[QK_NOPE_DIM:]),
                         gk[:QK_NOPE_DIM], rope_gain(gk[QK_NOPE_DIM:])])
        g_a = jnp.stack([jnp.tile(a_q_norm_g[layer], 2), jnp.tile(a_k_norm_g[layer], 2)])
        g_a = jnp.broadcast_to(g_a[None], (A_WIDTH // LANES, 2, LANES))

        proj = _in_proj(x2, attn_norm_g[layer][None, :], w_in_r, tm)
        qb, kb, vb = _mla_prep(proj, cs, wq_r, wkv_r, gc, gqk, seq, tm)
        out_b = _mla_attention(qb, kb, vb, batch, seq, 512)
        out_a = _dilated_attention(proj, head_block_ones, g_a, band_bias, batch, seq)

        rw = jnp.pad(router_w[layer], ((0, 0), (0, LANES - N_EXPERTS)))
        rw_hi = rw.astype(BF16)
        rw_lo = (rw - rw_hi.astype(F32)).astype(BF16)
        rb = jnp.pad(router_b[layer], (0, LANES - N_EXPERTS), constant_values=-jnp.inf)[None, :]
        x1, hp, route, counts = _out_proj(x2, out_a, out_b, w_out[layer].astype(BF16),
                                          ffn_norm_g[layer][None, :], jnp.concatenate([rw_hi, rw_lo], axis=1), rb, tm)

        padded, pad_end, dest, block_expert, next_expert, n_used, n_slots = _slot_layout(
            route, counts[0, :N_EXPERTS], tm, expert_bm)
        xs = _dispatch(padded, pad_end, n_used, dest, hp, n_slots, tm, expert_bm)
        ys = _expert_ffn(block_expert, n_used, next_expert, xs, w_gate_up[layer], b_gate_up[layer],
                         w_down[layer], b_down[layer], expert_bm)
        tables, route_local = _combine_tables(route, pad_end - padded, tm, COMBINE_LOCAL_ROWS)
        x2 = _combine(tables, route_local, x1, ys, tm, COMBINE_LOCAL_ROWS)
    return x2.reshape(batch, seq, d)
```

```python
import functools
import math

import jax
import jax.numpy as jnp
from jax import lax
from jax.experimental import pallas as pl
from jax.experimental.pallas import tpu as pltpu

A_HEADS = 8
A_HEAD_DIM = 64
A_WIDTH = A_HEADS * A_HEAD_DIM
DILATED_PATTERNS = ((128, 1), (512, 4), (2048, 16))
BAND_BLOCK = 128

B_HEADS = 4
QK_NOPE_DIM = 128
QK_ROPE_DIM = 64
QK_HEAD_DIM = QK_NOPE_DIM + QK_ROPE_DIM
V_HEAD_DIM = 128
Q_LORA_RANK = 256
KV_LORA_RANK = 256
B_WIDTH = B_HEADS * V_HEAD_DIM
ROPE_THETA = 10000.0

NUM_BUCKETS = 32
MAX_DISTANCE = 2048

N_EXPERTS = 32
TOP_K = 4
SWIGLU_LIMIT = 7.0
SWIGLU_ALPHA = 1.702
RMS_EPS = 1e-6

LANES = 128
MASK_VALUE = -1e30
LOG2E = math.log2(math.e)
PROJ_WIDTH = 3 * A_WIDTH + Q_LORA_RANK + KV_LORA_RANK + 2 * QK_ROPE_DIM
VMEM_LIMIT = 56 * 1024 * 1024

F32 = jnp.float32
BF16 = jnp.bfloat16


def _dot(a, b):
    return jnp.dot(a, b, preferred_element_type=F32)


def _dot_nt(a, b):
    return lax.dot_general(a, b, (((1,), (1,)), ((), ())), preferred_element_type=F32)


def _split_dot(x, m):
    hi = x.astype(BF16)
    lo = (x - hi.astype(F32)).astype(BF16)
    return _dot(hi, m) + _dot(lo, m)


def _params(n_parallel=1):
    return pltpu.CompilerParams(
        dimension_semantics=("arbitrary",) * n_parallel, vmem_limit_bytes=VMEM_LIMIT)


def _in_proj_kernel(x_ref, g_ref, w_ref, o_ref):
    x = x_ref[...]
    ms = jnp.mean(x * x, axis=-1, keepdims=True)
    h = (x * lax.rsqrt(ms + RMS_EPS) * g_ref[...]).astype(BF16)
    o_ref[...] = _dot(h, w_ref[...])


def _in_proj(x2, g, w, tm):
    t, d = x2.shape
    n = w.shape[1]
    return pl.pallas_call(
        _in_proj_kernel,
        grid=(t // tm,),
        in_specs=[pl.BlockSpec((tm, d), lambda i: (i, 0)),
                  pl.BlockSpec((1, d), lambda i: (0, 0)),
                  pl.BlockSpec((d, n), lambda i: (0, 0))],
        out_specs=pl.BlockSpec((tm, n), lambda i: (i, 0)),
        out_shape=jax.ShapeDtypeStruct((t, n), F32),
        compiler_params=_params(),
        name="in_proj",
    )(x2, g, w)


def _mla_prep_kernel(c_ref, kpe_ref, cs_ref, wq_ref, wkv_ref, gc_ref, gqk_ref, q_ref, k_ref, v_ref,
                     *, q_scale):
    c = c_ref[...]
    gc = gc_ref[...]

    def lora_norm(z, g):
        ms = jnp.mean(z * z, axis=-1, keepdims=True)
        return (z * lax.rsqrt(ms + RMS_EPS) * g).astype(BF16)

    cq = lora_norm(c[:, :Q_LORA_RANK], gc[:, :Q_LORA_RANK])
    ckv = lora_norm(c[:, Q_LORA_RANK:], gc[:, Q_LORA_RANK:])
    qb = _dot(cq, wq_ref[...])
    kvb = _dot(ckv, wkv_ref[...])
    kper = kpe_ref[...]
    cs = cs_ref[...]
    gqk = gqk_ref[...]
    tm = c.shape[0]

    row = lax.broadcasted_iota(jnp.int32, (LANES, LANES), 0)
    ones_all = jnp.ones((LANES, LANES), BF16)
    ones_lo = jnp.where(row < QK_ROPE_DIM, 1.0, 0.0).astype(BF16)
    lane = lax.broadcasted_iota(jnp.int32, (tm, LANES), 1)

    def rope(z, g_row):
        t = z * (g_row * cs)
        return jnp.where(lane < QK_ROPE_DIM, t + pltpu.roll(t, QK_ROPE_DIM, 1), 0.0)

    k_rope = rope(kper, gqk[3:4])
    kpe_ss = _split_dot(kper * kper, ones_lo)
    for h in range(B_HEADS):
        qn = qb[:, 2 * LANES * h: 2 * LANES * h + LANES]
        qr = qb[:, 2 * LANES * h + LANES: 2 * LANES * (h + 1)]
        ss = _split_dot(qn * qn, ones_all) + _split_dot(qr * qr, ones_lo)
        rs = lax.rsqrt(ss * (1.0 / QK_HEAD_DIM) + RMS_EPS) * q_scale
        q_ref[h, :, :LANES] = (qn * gqk[0:1] * rs).astype(BF16)
        q_ref[h, :, LANES:] = (rope(qr, gqk[1:2]) * rs).astype(BF16)
        kn = kvb[:, LANES * h: LANES * (h + 1)]
        ssk = _split_dot(kn * kn, ones_all) + kpe_ss
        rsk = lax.rsqrt(ssk * (1.0 / QK_HEAD_DIM) + RMS_EPS)
        k_ref[h, :, :LANES] = (kn * gqk[2:3] * rsk).astype(BF16)
        k_ref[h, :, LANES:] = (k_rope * rsk).astype(BF16)
    v_ref[...] = kvb[:, B_HEADS * QK_NOPE_DIM:].astype(BF16)


def _mla_prep(proj, cs, wq, wkv, gc, gqk, seq, tm):
    t = proj.shape[0]
    c_width = Q_LORA_RANK + KV_LORA_RANK
    c_block = 3 * A_WIDTH // c_width
    kpe_block = (3 * A_WIDTH + c_width) // LANES
    pos_blocks = seq // tm
    kernel = functools.partial(_mla_prep_kernel, q_scale=QK_HEAD_DIM ** -0.5 * LOG2E)
    return pl.pallas_call(
        kernel,
        grid=(t // tm,),
        in_specs=[pl.BlockSpec((tm, c_width), lambda i: (i, c_block)),
                  pl.BlockSpec((tm, LANES), lambda i: (i, kpe_block)),
                  pl.BlockSpec((tm, LANES), lambda i: (i % pos_blocks, 0)),
                  pl.BlockSpec(wq.shape, lambda i: (0, 0)),
                  pl.BlockSpec(wkv.shape, lambda i: (0, 0)),
                  pl.BlockSpec(gc.shape, lambda i: (0, 0)),
                  pl.BlockSpec(gqk.shape, lambda i: (0, 0))],
        out_specs=[pl.BlockSpec((B_HEADS, tm, 2 * LANES), lambda i: (0, i, 0)),
                   pl.BlockSpec((B_HEADS, tm, 2 * LANES), lambda i: (0, i, 0)),
                   pl.BlockSpec((tm, B_WIDTH), lambda i: (i, 0))],
        out_shape=[jax.ShapeDtypeStruct((B_HEADS, t, 2 * LANES), BF16),
                   jax.ShapeDtypeStruct((B_HEADS, t, 2 * LANES), BF16),
                   jax.ShapeDtypeStruct((t, B_WIDTH), BF16)],
        compiler_params=_params(),
        name="mla_prep",
    )(proj, proj, cs, wq, wkv, gc, gqk)


def _mla_attention_kernel(q_ref, k_ref, v_ref, o_ref, *, tq):
    seq = q_ref.shape[0]
    row = lax.broadcasted_iota(jnp.int32, (tq, tq), 0)
    col = lax.broadcasted_iota(jnp.int32, (tq, tq), 1)
    diag_mask = jnp.where(col <= row, 0.0, MASK_VALUE)
    v_ext = jnp.concatenate([v_ref[...], jnp.ones(v_ref.shape, v_ref.dtype)], axis=1)
    dv = v_ref.shape[1]
    for i in range(seq // tq):
        q = q_ref[i * tq:(i + 1) * tq, :]
        s_diag = _dot_nt(q, k_ref[i * tq:(i + 1) * tq, :]) + diag_mask
        m = jnp.max(s_diag, axis=-1, keepdims=True)
        if i > 0:
            s_past = _dot_nt(q, k_ref[:i * tq, :])
            m = jnp.maximum(m, jnp.max(s_past, axis=-1, keepdims=True))
        o = _dot(jnp.exp2(s_diag - m).astype(BF16), v_ext[i * tq:(i + 1) * tq, :])
        if i > 0:
            o = o + _dot(jnp.exp2(s_past - m).astype(BF16), v_ext[:i * tq, :])
        o_ref[i * tq:(i + 1) * tq, :] = (o[:, :dv] / o[:, dv:]).astype(o_ref.dtype)


def _mla_attention(qb, kb, vb, batch, seq, tq):
    t = vb.shape[0]
    kernel = functools.partial(_mla_attention_kernel, tq=tq)
    return pl.pallas_call(
        kernel,
        grid=(batch, B_HEADS),
        in_specs=[pl.BlockSpec((None, seq, 2 * LANES), lambda b, h: (h, b, 0)),
                  pl.BlockSpec((None, seq, 2 * LANES), lambda b, h: (h, b, 0)),
                  pl.BlockSpec((seq, V_HEAD_DIM), lambda b, h: (b, h))],
        out_specs=pl.BlockSpec((seq, V_HEAD_DIM), lambda b, h: (b, h)),
        out_shape=jax.ShapeDtypeStruct((t, B_WIDTH), BF16),
        compiler_params=_params(2),
        name="mla_attention",
    )(qb, kb, vb)


def _dilated_kernel(q_ref, k_ref, v_ref, bd_ref, g_ref, bias_ref, o_ref, qs, ks, acc_o, acc_l, acc_m,
                    *, unroll):
    seq = q_ref.shape[0]
    n_tiles = seq // BAND_BLOCK
    low = lax.broadcasted_iota(jnp.int32, (BAND_BLOCK, LANES), 1) < A_HEAD_DIM
    bd = bd_ref[...]
    g = g_ref[...]

    def head_norm(z, g_row):
        ss = _split_dot(z * z, bd)
        return z * lax.rsqrt(ss * (1.0 / A_HEAD_DIM) + RMS_EPS) * g_row

    qs[...] = head_norm(q_ref[...], g[0:1]) * (A_HEAD_DIM ** -0.5 * LOG2E)
    ks[...] = head_norm(k_ref[...], g[1:2])
    ones = jnp.ones((2 * BAND_BLOCK, LANES), BF16)

    for p, (window, dil) in enumerate(DILATED_PATTERNS):
        blocks_per_class = n_tiles // dil
        span = BAND_BLOCK * dil

        def rows_at(start, dil=dil):
            if dil == 1:
                return pl.ds(start, BAND_BLOCK)
            return pl.ds(start, BAND_BLOCK, stride=dil)

        def tile(i, carry, p=p, blocks_per_class=blocks_per_class, span=span, rows_at=rows_at):
            r = i // blocks_per_class
            j = i % blocks_per_class
            cur = rows_at(r + j * span)
            prev = rows_at(r + jnp.maximum(j - 1, 0) * span)
            first = jnp.where(j == 0, 1, 0)
            q = qs[cur, :]
            q2 = jnp.concatenate([jnp.where(low, q, 0.0), jnp.where(low, 0.0, q)], axis=0).astype(BF16)
            k_band = jnp.concatenate([ks[prev, :], ks[cur, :]], axis=0).astype(BF16)
            s = _dot_nt(q2, k_band) + bias_ref[p, first]
            m = jnp.max(s, axis=-1, keepdims=True)
            pr = jnp.exp2(s - m).astype(BF16)
            v_band = jnp.concatenate([v_ref[prev, :], v_ref[cur, :]], axis=0).astype(BF16)
            o = _dot(pr, jnp.concatenate([v_band, ones], axis=1))
            top, bot = o[:BAND_BLOCK], o[BAND_BLOCK:]
            acc_o[p, cur, :] = jnp.where(low, top[:, :LANES], bot[:, :LANES])
            acc_l[p, cur, :] = jnp.where(low, top[:, LANES:], bot[:, LANES:])
            acc_m[p, cur, :] = jnp.where(low, m[:BAND_BLOCK], m[BAND_BLOCK:])
            return carry

        lax.fori_loop(0, n_tiles, tile, 0, unroll=unroll)

    m_all = jnp.maximum(jnp.maximum(acc_m[0], acc_m[1]), acc_m[2])
    num = jnp.zeros((seq, LANES), F32)
    den = jnp.zeros((seq, LANES), F32)
    for p in range(len(DILATED_PATTERNS)):
        w = jnp.exp2(acc_m[p] - m_all)
        num = num + w * acc_o[p]
        den = den + w * acc_l[p]
    o_ref[...] = (num / den).astype(o_ref.dtype)


def _dilated_attention(proj, bd, g, bias, batch, seq):
    t = proj.shape[0]
    pairs = A_WIDTH // LANES
    n_pat = len(DILATED_PATTERNS)
    return pl.pallas_call(
        functools.partial(_dilated_kernel, unroll=16),
        grid=(batch, pairs),
        in_specs=[pl.BlockSpec((seq, LANES), lambda b, c: (b, c)),
                  pl.BlockSpec((seq, LANES), lambda b, c: (b, pairs + c)),
                  pl.BlockSpec((seq, LANES), lambda b, c: (b, 2 * pairs + c)),
                  pl.BlockSpec((LANES, LANES), lambda b, c: (0, 0)),
                  pl.BlockSpec((None, 2, LANES), lambda b, c: (c, 0, 0)),
                  pl.BlockSpec((None, n_pat, 2, 2 * BAND_BLOCK, 2 * BAND_BLOCK),
                               lambda b, c: (c, 0, 0, 0, 0))],
        out_specs=pl.BlockSpec((seq, LANES), lambda b, c: (b, c)),
        out_shape=jax.ShapeDtypeStruct((t, A_WIDTH), BF16),
        scratch_shapes=[pltpu.VMEM((seq, LANES), F32)] * 2
                       + [pltpu.VMEM((n_pat, seq, LANES), F32)] * 3,
        compiler_params=_params(2),
        name="dilated_attention",
    )(proj, proj, proj, bd, g, bias)


def _t5_bucket(dist):
    max_exact = NUM_BUCKETS // 2
    df = jnp.maximum(dist, 1).astype(F32)
    log_bucket = max_exact + (jnp.log(df / max_exact) / math.log(MAX_DISTANCE / max_exact)
                              * (NUM_BUCKETS - max_exact)).astype(jnp.int32)
    log_bucket = jnp.minimum(log_bucket, NUM_BUCKETS - 1)
    return jnp.where(dist < max_exact, dist, log_bucket)


def _band_bias(rel_bias):
    n = BAND_BLOCK
    qi = jnp.arange(n)[:, None]
    kj = jnp.arange(n)[None, :]
    buckets = jnp.arange(NUM_BUCKETS)
    tables = []
    for window, dil in DILATED_PATTERNS:
        steps = window // dil
        halves = []
        for back in (qi - kj + n, qi - kj):
            onehot = (_t5_bucket(jnp.maximum(back, 0) * dil)[:, :, None] == buckets).astype(F32)
            vals = jnp.einsum('qkb,bh->hqk', onehot, rel_bias.astype(F32),
                              precision=lax.Precision.HIGHEST) * LOG2E
            halves.append(jnp.where(((back >= 0) & (back <= steps))[None], vals, MASK_VALUE))
        prev, cur = halves
        normal = jnp.concatenate([prev, cur], axis=-1)
        first = jnp.concatenate([jnp.full_like(prev, MASK_VALUE), cur], axis=-1)
        tables.append(jnp.stack([normal, first], axis=1))
    tab = jnp.stack(tables, axis=1)
    tab = tab.reshape(A_HEADS // 2, 2, len(DILATED_PATTERNS), 2, n, 2 * n)
    return tab.transpose(0, 2, 3, 1, 4, 5).reshape(A_HEADS // 2, len(DILATED_PATTERNS), 2, 2 * n, 2 * n)


def _pack_pairs(v):
    half = v.shape[1] // 2
    bits = pltpu.bitcast(v, jnp.uint32)
    return (bits[:, :half] >> 16) | (bits[:, half:] & jnp.uint32(0xFFFF0000))


def _unpack_pairs(p):
    return (pltpu.bitcast(p << 16, F32), pltpu.bitcast(p & jnp.uint32(0xFFFF0000), F32))


def _out_proj_kernel(x_ref, a_ref, b_ref, w_ref, g_ref, rw_ref, rb_ref,
                     x1_ref, hp_ref, route_ref, route_t_ref, blk_ref, carry):
    i = pl.program_id(0)
    tm = x_ref.shape[0]

    @pl.when(i == 0)
    def _():
        carry[...] = jnp.zeros_like(carry)

    w = w_ref[...]
    x1 = x_ref[...] + _dot(a_ref[...], w[:A_WIDTH]) + _dot(b_ref[...], w[A_WIDTH:])
    x1_ref[...] = x1
    ms = jnp.mean(x1 * x1, axis=-1, keepdims=True)
    h = x1 * lax.rsqrt(ms + RMS_EPS) * g_ref[...]
    hi = h.astype(BF16)
    hi_f = hi.astype(F32)
    hp_ref[...] = _pack_pairs(hi_f)
    lo = (h - hi_f).astype(BF16)
    rw = rw_ref[...]
    hw = _dot(hi, rw)
    logits = hw[:, :LANES] + hw[:, LANES:] + _dot(lo, rw[:, :LANES]) + rb_ref[...]

    lane = lax.broadcasted_iota(jnp.int32, (tm, LANES), 1).astype(F32)
    remaining = logits
    vals, hots = [], []
    for _ in range(TOP_K):
        m = jnp.max(remaining, axis=-1, keepdims=True)
        first = jnp.min(jnp.where(remaining == m, lane, float(LANES)), axis=-1, keepdims=True)
        hot = lane == first
        remaining = jnp.where(hot, -jnp.inf, remaining)
        vals.append(m)
        hots.append(hot)
    exps = [jnp.exp(v - vals[0]) for v in vals]
    den = exps[0] + exps[1] + exps[2] + exps[3]

    chosen = jnp.zeros((tm, LANES), F32)
    for hot in hots:
        chosen = chosen + jnp.where(hot, 1.0, 0.0)
    r = lax.broadcasted_iota(jnp.int32, (tm, tm), 0)
    c = lax.broadcasted_iota(jnp.int32, (tm, tm), 1)
    earlier = jnp.where(r > c, 1.0, 0.0).astype(BF16)
    before = carry[...] + _dot(earlier, chosen.astype(BF16))
    in_block = jnp.sum(chosen, axis=0, keepdims=True)
    carry[...] = carry[...] + in_block
    blk_ref[...] = jnp.broadcast_to(in_block, blk_ref.shape)

    route = jnp.zeros((tm, LANES), F32)
    for k in range(TOP_K):
        first = jnp.sum(jnp.where(hots[k], lane, 0.0), axis=-1, keepdims=True)
        rank = jnp.sum(jnp.where(hots[k], before, 0.0), axis=-1, keepdims=True)
        route = route + jnp.where(lane == float(k), first, 0.0)
        route = route + jnp.where(lane == float(TOP_K + k), rank, 0.0)
        route = route + jnp.where(lane == float(2 * TOP_K + k), exps[k] / den, 0.0)
    route_ref[...] = route
    route_t_ref[...] = route.T[:route_t_ref.shape[0], :]


def _out_proj(x2, out_a, out_b, w, g, rw, rb, tm):
    t, d = x2.shape
    return pl.pallas_call(
        _out_proj_kernel,
        grid=(t // tm,),
        in_specs=[pl.BlockSpec((tm, d), lambda i: (i, 0)),
                  pl.BlockSpec((tm, A_WIDTH), lambda i: (i, 0)),
                  pl.BlockSpec((tm, B_WIDTH), lambda i: (i, 0)),
                  pl.BlockSpec(w.shape, lambda i: (0, 0)),
                  pl.BlockSpec((1, d), lambda i: (0, 0)),
                  pl.BlockSpec(rw.shape, lambda i: (0, 0)),
                  pl.BlockSpec((1, LANES), lambda i: (0, 0))],
        out_specs=[pl.BlockSpec((tm, d), lambda i: (i, 0)),
                   pl.BlockSpec((tm, d // 2), lambda i: (i, 0)),
                   pl.BlockSpec((tm, LANES), lambda i: (i, 0)),
                   pl.BlockSpec((2 * TOP_K, tm), lambda i: (0, i)),
                   pl.BlockSpec((None, 8, LANES), lambda i: (i, 0, 0))],
        out_shape=[jax.ShapeDtypeStruct((t, d), F32),
                   jax.ShapeDtypeStruct((t, d // 2), jnp.uint32),
                   jax.ShapeDtypeStruct((t, LANES), F32),
                   jax.ShapeDtypeStruct((2 * TOP_K, t), F32),
                   jax.ShapeDtypeStruct((t // tm, 8, LANES), F32)],
        scratch_shapes=[pltpu.VMEM((1, LANES), F32)],
        compiler_params=_params(),
        name="out_proj",
    )(x2, out_a, out_b, w, g, rw, rb)


def _dispatch_kernel(pad_ref, end_ref, nb_ref, dest_hbm, h_ref, xs_hbm, dest_s0, dest_s1, zeros,
                     sem_i, sem_z, sem_o, *, bm):
    i = pl.program_id(0)
    steps = pl.num_programs(0)
    tm = h_ref.shape[0]
    n_blocks = xs_hbm.shape[0] // bm
    dest_s = (dest_s0, dest_s1)

    def idx_copy(step, slot):
        return pltpu.make_async_copy(dest_hbm.at[step], dest_s[slot], sem_i.at[slot])

    @pl.when(i == 0)
    def _():
        idx_copy(0, 0).start()
        zeros[...] = jnp.zeros_like(zeros)

        def zero_block(start):
            return pltpu.make_async_copy(zeros, xs_hbm.at[pl.ds(pl.multiple_of(start, bm), bm), :], sem_z)

        for e in range(N_EXPERTS):
            @pl.when(pad_ref[e] > 0)
            def _():
                zero_block(end_ref[e] - bm).start()

        def start_unused(b, carry):
            zero_block(b * bm).start()
            return carry

        def wait_unused(b, carry):
            zero_block(b * bm).wait()
            return carry

        lax.fori_loop(nb_ref[0], n_blocks, start_unused, 0)
        for e in range(N_EXPERTS):
            @pl.when(pad_ref[e] > 0)
            def _():
                zero_block(end_ref[e] - bm).wait()
        lax.fori_loop(nb_ref[0], n_blocks, wait_unused, 0)

    for slot in range(2):
        @pl.when(i % 2 == slot)
        def _(slot=slot):
            @pl.when(i + 1 < steps)
            def _():
                idx_copy(i + 1, 1 - slot).start()

            idx_copy(i, slot).wait()

            def body(t, carry):
                for k in range(TOP_K):
                    pltpu.make_async_copy(h_ref.at[pl.ds(t, 1), :],
                                          xs_hbm.at[pl.ds(dest_s[slot][k * tm + t], 1), :],
                                          sem_o).start(priority=k % 2)
                return carry

            lax.fori_loop(0, tm, body, 0, unroll=8)
    for _ in range(TOP_K):
        pltpu.make_async_copy(h_ref, xs_hbm.at[pl.ds(0, tm), :], sem_o).wait()


def _dispatch(padded, pad_end, n_used, dest, hp, n_slots, tm, bm):
    t, half = hp.shape
    grid_spec = pltpu.PrefetchScalarGridSpec(
        num_scalar_prefetch=3,
        grid=(t // tm,),
        in_specs=[pl.BlockSpec(memory_space=pl.ANY),
                  pl.BlockSpec((tm, half), lambda i, p, e, n: (i, 0))],
        out_specs=pl.BlockSpec(memory_space=pl.ANY),
        scratch_shapes=[pltpu.SMEM((tm * TOP_K,), jnp.int32),
                        pltpu.SMEM((tm * TOP_K,), jnp.int32),
                        pltpu.VMEM((bm, half), jnp.uint32),
                        pltpu.SemaphoreType.DMA((2,)),
                        pltpu.SemaphoreType.DMA(()),
                        pltpu.SemaphoreType.DMA(())],
    )
    return pl.pallas_call(
        functools.partial(_dispatch_kernel, bm=bm),
        grid_spec=grid_spec,
        out_shape=jax.ShapeDtypeStruct((n_slots, half), jnp.uint32),
        compiler_params=_params(),
        name="dispatch",
    )(padded, pad_end, n_used, dest, hp)


EXPERT_SUB_ROWS = 256


def _expert_ffn_kernel(be_ref, nb_ref, next_ref, x_ref, wgu_hbm, bgu_ref, wd_hbm, bd_ref, y_ref,
                       wgu_f, wd_f, wgu_s, wd_s, sem_w):
    i = pl.program_id(0)
    d_ff = wd_f.shape[0]
    half = x_ref.shape[1]

    def weight_copies(e):
        return (pltpu.make_async_copy(wgu_hbm.at[e], wgu_f, sem_w.at[0]),
                pltpu.make_async_copy(wd_hbm.at[e], wd_f, sem_w.at[1]))

    @pl.when(jnp.logical_and(i == 0, nb_ref[0] > 0))
    def _():
        for copy in weight_copies(be_ref[0]):
            copy.start()

    @pl.when(i >= nb_ref[0])
    def _():
        y_ref[...] = jnp.zeros_like(y_ref)

    @pl.when(i < nb_ref[0])
    def _():
        changed = jnp.logical_or(i == 0, be_ref[i] != be_ref[jnp.maximum(i - 1, 0)])

        @pl.when(changed)
        def _():
            for copy in weight_copies(be_ref[i]):
                copy.wait()
            wgu_s[...] = wgu_f[...].astype(BF16)
            wd_s[...] = wd_f[...].astype(BF16)

            @pl.when(next_ref[i] >= 0)
            def _():
                for copy in weight_copies(next_ref[i]):
                    copy.start()

        for r0 in range(0, x_ref.shape[0], EXPERT_SUB_ROWS):
            rows = slice(r0, r0 + EXPERT_SUB_ROWS)
            x_lo, x_hi = _unpack_pairs(x_ref[rows, :])
            gu = (_dot(x_lo.astype(BF16), wgu_s[:half, :]) + _dot(x_hi.astype(BF16), wgu_s[half:, :])
                  + bgu_ref[...])
            gate = jnp.minimum(gu[:, :d_ff], SWIGLU_LIMIT)
            up = jnp.clip(gu[:, d_ff:], -SWIGLU_LIMIT, SWIGLU_LIMIT)
            glu = gate * jax.nn.sigmoid(SWIGLU_ALPHA * gate)
            act = ((up + 1.0) * glu).astype(BF16)
            y = _dot(act, wd_s[...]) + bd_ref[...]
            y_ref[rows, :] = _pack_pairs(y.astype(BF16).astype(F32))


def _expert_ffn(block_expert, n_used, next_expert, xs, w_gate_up, b_gate_up, w_down, b_down, bm):
    n_slots, half = xs.shape
    n_e, d, two_ff = w_gate_up.shape
    d_ff = two_ff // 2
    n_blocks = n_slots // bm

    def used(i, nb):
        return jnp.minimum(i, jnp.maximum(nb[0] - 1, 0))

    grid_spec = pltpu.PrefetchScalarGridSpec(
        num_scalar_prefetch=3,
        grid=(n_blocks,),
        in_specs=[pl.BlockSpec((bm, half), lambda i, be, nb, nx: (used(i, nb), 0)),
                  pl.BlockSpec(memory_space=pl.ANY),
                  pl.BlockSpec((None, 1, two_ff), lambda i, be, nb, nx: (be[i], 0, 0)),
                  pl.BlockSpec(memory_space=pl.ANY),
                  pl.BlockSpec((None, 1, d), lambda i, be, nb, nx: (be[i], 0, 0))],
        out_specs=pl.BlockSpec((bm, half), lambda i, be, nb, nx: (i, 0)),
        scratch_shapes=[pltpu.VMEM((d, two_ff), F32), pltpu.VMEM((d_ff, d), F32),
                        pltpu.VMEM((d, two_ff), BF16), pltpu.VMEM((d_ff, d), BF16),
                        pltpu.SemaphoreType.DMA((2,))],
    )
    return pl.pallas_call(
        _expert_ffn_kernel,
        grid_spec=grid_spec,
        out_shape=jax.ShapeDtypeStruct((n_slots, half), jnp.uint32),
        compiler_params=_params(),
        name="expert_ffn",
    )(block_expert, n_used, next_expert, xs, w_gate_up, b_gate_up.reshape(n_e, 1, two_ff),
      w_down, b_down.reshape(n_e, 1, d))


COMBINE_LOCAL_ROWS = 2560
SEG_ALIGN = 8


def _pieces(limit):
    sizes = []
    size = 1 << (limit.bit_length() - 1)
    while size >= SEG_ALIGN:
        sizes.append(size)
        size //= 2
    return sizes


def _combine_kernel(src_ref, len_ref, dst_ref, tot_ref, route_ref, local_ref, x1_ref, ys_hbm, o_ref, ybuf,
                    sem_g):
    j = pl.program_id(0)
    blocks = pl.num_programs(0) - 1
    tm = x1_ref.shape[0]
    half = x1_ref.shape[1] // 2
    local_rows = ybuf.shape[1]

    @pl.when(j == 0)
    def _():
        ybuf[...] = jnp.zeros_like(ybuf)

    @pl.when(j < blocks)
    def _():
        for s in range(2):
            @pl.when(j % 2 == s)
            def _(s=s):
                for e in range(N_EXPERTS):
                    seg = j * N_EXPERTS + e
                    n = len_ref[seg]
                    src = src_ref[seg]
                    dst = dst_ref[seg]
                    for size in _pieces(tm + 2 * SEG_ALIGN):
                        @pl.when((n & size) != 0)
                        def _(size=size, src=src, dst=dst):
                            pltpu.make_async_copy(
                                ys_hbm.at[pl.ds(pl.multiple_of(src, SEG_ALIGN), size), :],
                                ybuf.at[s, pl.ds(pl.multiple_of(dst, SEG_ALIGN), size), :],
                                sem_g.at[s]).start()
                        step = jnp.where((n & size) != 0, size, 0)
                        src = src + step
                        dst = dst + step

    @pl.when(j >= 1)
    def _():
        slot = (j - 1) % 2
        total = tot_ref[j - 1]
        for size in _pieces(local_rows):
            @pl.when((total & size) != 0)
            def _(size=size):
                pltpu.make_async_copy(ys_hbm.at[pl.ds(0, size), :], ybuf.at[slot, pl.ds(0, size), :],
                                      sem_g.at[slot]).wait()

        route = route_ref[...]
        local_t = local_ref[...]
        local = jnp.concatenate(
            [local_t, jnp.zeros((LANES - local_t.shape[0], tm), F32)], axis=0).T
        col = lax.broadcasted_iota(jnp.int32, (tm, local_rows), 1).astype(F32)
        g = jnp.zeros((tm, local_rows), F32)
        for k in range(TOP_K):
            pos = local[:, k: k + 1]
            gate = route[:, 2 * TOP_K + k: 2 * TOP_K + k + 1]
            g = jnp.where(col == pos, gate, g)
        g = g.astype(BF16)
        lo, hi = _unpack_pairs(ybuf[slot])
        o_ref[:, :half] = x1_ref[:, :half] + _dot(g, lo.astype(BF16))
        o_ref[:, half:] = x1_ref[:, half:] + _dot(g, hi.astype(BF16))


def _combine(tables, route, local_t, x1, ys, tm, local_rows):
    t, d = x1.shape
    half = d // 2

    def summed(j, *_):
        return (jnp.maximum(j - 1, 0), 0)

    grid_spec = pltpu.PrefetchScalarGridSpec(
        num_scalar_prefetch=4,
        grid=(t // tm + 1,),
        in_specs=[pl.BlockSpec((tm, LANES), summed),
                  pl.BlockSpec((local_t.shape[0], tm), lambda j, *_: (0, jnp.maximum(j - 1, 0))),
                  pl.BlockSpec((tm, d), summed),
                  pl.BlockSpec(memory_space=pl.ANY)],
        out_specs=pl.BlockSpec((tm, d), summed),
        scratch_shapes=[pltpu.VMEM((2, local_rows, half), jnp.uint32),
                        pltpu.SemaphoreType.DMA((2,))],
    )
    return pl.pallas_call(
        _combine_kernel,
        grid_spec=grid_spec,
        out_shape=jax.ShapeDtypeStruct((t, d), F32),
        compiler_params=_params(),
        name="combine",
    )(*tables, route, local_t, x1, ys)


def _per_expert(table, idx):
    out = jnp.zeros(idx.shape, table.dtype)
    for e in range(N_EXPERTS):
        out = jnp.where(idx == e, table[:, e][None, :, None], out)
    return out


def _combine_tables(idx, rank, n, pad_start, tm, local_rows):
    before = jnp.cumsum(n, axis=0) - n
    first = pad_start[None, :] + before
    src = first // SEG_ALIGN * SEG_ALIGN
    length = jnp.where(n > 0, (first + n + SEG_ALIGN - 1) // SEG_ALIGN * SEG_ALIGN - src, 0)
    dst = jnp.cumsum(length, axis=1) - length
    total = jnp.sum(length, axis=1)
    assert local_rows >= tm * TOP_K + N_EXPERTS * 2 * (SEG_ALIGN - 1)
    local = (_per_expert(dst + first - src - before, idx) + rank).reshape(TOP_K, -1).astype(F32)
    tables = (src.reshape(-1).astype(jnp.int32), length.reshape(-1).astype(jnp.int32),
              dst.reshape(-1).astype(jnp.int32), total.astype(jnp.int32))
    return tables, local


def _slot_layout(route_t, n, tm, bm):
    t = route_t.shape[1]
    blocks = t // tm
    idx = route_t[:TOP_K].astype(jnp.int32).reshape(TOP_K, blocks, tm)
    rank = route_t[TOP_K:].astype(jnp.int32).reshape(TOP_K, blocks, tm)
    counts = jnp.sum(n, axis=0)
    padded = (counts + bm - 1) // bm * bm
    pad_end = jnp.cumsum(padded).astype(jnp.int32)
    pad_start = pad_end - padded
    dest = _per_expert(jnp.broadcast_to(pad_start[None, :], n.shape), idx) + rank
    dest = dest.transpose(1, 0, 2).reshape(blocks, TOP_K * tm)
    n_blocks = -(-t * TOP_K // bm) + N_EXPERTS
    block_start = jnp.arange(n_blocks, dtype=jnp.int32) * bm
    block_expert = jnp.minimum(jnp.sum(pad_end[None, :] <= block_start[:, None], axis=1),
                               N_EXPERTS - 1).astype(jnp.int32)
    n_used = (pad_end[-1:] // bm).astype(jnp.int32)
    following = jnp.take(pad_end, block_expert) // bm
    next_expert = jnp.where(following < n_used[0],
                            jnp.take(block_expert, jnp.minimum(following, n_blocks - 1)), -1).astype(jnp.int32)
    return idx, rank, padded, pad_end, dest, block_expert, next_expert, n_used, n_blocks * bm


def _rot_cols(w):
    half = w.shape[-1] // 2
    return jnp.concatenate([-w[..., half:], w[..., :half]], axis=-1)


def _swap_halves(g):
    half = g.shape[-1] // 2
    return jnp.concatenate([g[..., half:], g[..., :half]], axis=-1)


def kernel(x, attn_norm_g, w_in, a_q_norm_g, a_k_norm_g, rel_bias, q_a_norm_g, w_q_b, kv_a_norm_g,
           w_kv_b, b_q_norm_g, b_k_norm_g, w_out, ffn_norm_g, router_w, router_b, w_gate_up,
           b_gate_up, w_down, b_down):
    batch, seq, d = x.shape
    t = batch * seq
    depth = w_in.shape[0]
    tm = 512
    expert_bm = 512

    pos = jnp.arange(seq, dtype=F32)
    inv_freq = ROPE_THETA ** (-jnp.arange(0, QK_ROPE_DIM, 2, dtype=F32) / QK_ROPE_DIM)
    ang = pos[:, None] * inv_freq[None, :]
    cos, sin = jnp.cos(ang), jnp.sin(ang)
    cs = jnp.concatenate([cos, cos, sin, sin], axis=-1)

    row = jnp.arange(LANES)[:, None] // A_HEAD_DIM
    col = jnp.arange(LANES)[None, :] // A_HEAD_DIM
    head_block_ones = (row == col).astype(BF16)
    band_bias = _band_bias(rel_bias)

    x2 = x.reshape(t, d)
    for layer in range(depth):
        kpe_off = 3 * A_WIDTH + Q_LORA_RANK + KV_LORA_RANK
        w_kpe = w_in[layer][:, kpe_off:]
        w_in_r = jnp.concatenate([w_in[layer][:, :kpe_off], w_kpe, _rot_cols(w_kpe)], axis=1).astype(BF16)

        wq = w_q_b[layer].reshape(Q_LORA_RANK, B_HEADS, QK_HEAD_DIM)
        wq_rope = wq[..., QK_NOPE_DIM:]
        wq_r = jnp.concatenate([wq, _rot_cols(wq_rope)], axis=-1).reshape(Q_LORA_RANK, -1).astype(BF16)
        wkv = w_kv_b[layer].reshape(KV_LORA_RANK, B_HEADS, QK_NOPE_DIM + V_HEAD_DIM)
        wkv_r = jnp.concatenate([wkv[..., :QK_NOPE_DIM].reshape(KV_LORA_RANK, -1),
                                 wkv[..., QK_NOPE_DIM:].reshape(KV_LORA_RANK, -1)], axis=1).astype(BF16)
        gc = jnp.concatenate([q_a_norm_g[layer], kv_a_norm_g[layer]])[None, :]
        gq, gk = b_q_norm_g[layer], b_k_norm_g[layer]

        def rope_gain(gr):
            return jnp.concatenate([gr, _swap_halves(gr)])

        gqk = jnp.stack([gq[:QK_NOPE_DIM], rope_gain(gq[QK_NOPE_DIM:]),
                         gk[:QK_NOPE_DIM], rope_gain(gk[QK_NOPE_DIM:])])
        g_a = jnp.stack([jnp.tile(a_q_norm_g[layer], 2), jnp.tile(a_k_norm_g[layer], 2)])
        g_a = jnp.broadcast_to(g_a[None], (A_WIDTH // LANES, 2, LANES))

        proj = _in_proj(x2, attn_norm_g[layer][None, :], w_in_r, tm)
        qb, kb, vb = _mla_prep(proj, cs, wq_r, wkv_r, gc, gqk, seq, tm)
        out_b = _mla_attention(qb, kb, vb, batch, seq, 512)
        out_a = _dilated_attention(proj, head_block_ones, g_a, band_bias, batch, seq)

        rw = jnp.pad(router_w[layer], ((0, 0), (0, LANES - N_EXPERTS)))
        rw_hi = rw.astype(BF16)
        rw_lo = (rw - rw_hi.astype(F32)).astype(BF16)
        rb = jnp.pad(router_b[layer], (0, LANES - N_EXPERTS), constant_values=-jnp.inf)[None, :]
        x1, hp, route, route_t, in_block = _out_proj(
            x2, out_a, out_b, w_out[layer].astype(BF16), ffn_norm_g[layer][None, :],
            jnp.concatenate([rw_hi, rw_lo], axis=1), rb, tm)

        n = in_block[:, 0, :N_EXPERTS].astype(jnp.int32)
        idx, rank, padded, pad_end, dest, block_expert, next_expert, n_used, n_slots = _slot_layout(
            route_t, n, tm, expert_bm)
        xs = _dispatch(padded, pad_end, n_used, dest, hp, n_slots, tm, expert_bm)
        ys = _expert_ffn(block_expert, n_used, next_expert, xs, w_gate_up[layer], b_gate_up[layer],
                         w_down[layer], b_down[layer], expert_bm)
        tables, local_t = _combine_tables(idx, rank, n, pad_end - padded, tm, COMBINE_LOCAL_ROWS)
        x2 = _combine(tables, route, local_t, x1, ys, tm, COMBINE_LOCAL_ROWS)
    return x2.reshape(batch, seq, d)
```

```python
import functools
import math

import jax
import jax.numpy as jnp
from jax import lax
from jax.experimental import pallas as pl
from jax.experimental.pallas import tpu as pltpu

A_HEADS = 8
A_HEAD_DIM = 64
A_WIDTH = A_HEADS * A_HEAD_DIM
DILATED_PATTERNS = ((128, 1), (512, 4), (2048, 16))
BAND_BLOCK = 128

B_HEADS = 4
QK_NOPE_DIM = 128
QK_ROPE_DIM = 64
QK_HEAD_DIM = QK_NOPE_DIM + QK_ROPE_DIM
V_HEAD_DIM = 128
Q_LORA_RANK = 256
KV_LORA_RANK = 256
B_WIDTH = B_HEADS * V_HEAD_DIM
ROPE_THETA = 10000.0

NUM_BUCKETS = 32
MAX_DISTANCE = 2048

N_EXPERTS = 32
TOP_K = 4
SWIGLU_LIMIT = 7.0
SWIGLU_ALPHA = 1.702
RMS_EPS = 1e-6

LANES = 128
MASK_VALUE = -1e30
LOG2E = math.log2(math.e)
PROJ_WIDTH = 3 * A_WIDTH + Q_LORA_RANK + KV_LORA_RANK + 2 * QK_ROPE_DIM
VMEM_LIMIT = 56 * 1024 * 1024

F32 = jnp.float32
BF16 = jnp.bfloat16


def _dot(a, b):
    return jnp.dot(a, b, preferred_element_type=F32)


def _dot_nt(a, b):
    return lax.dot_general(a, b, (((1,), (1,)), ((), ())), preferred_element_type=F32)


def _split_dot(x, m):
    return _dot(x.astype(BF16), m)


def _params(n_parallel=1):
    return pltpu.CompilerParams(
        dimension_semantics=("arbitrary",) * n_parallel, vmem_limit_bytes=VMEM_LIMIT)


def _in_proj_kernel(x_ref, g_ref, w_ref, o_ref):
    x = x_ref[...]
    ms = jnp.mean(x * x, axis=-1, keepdims=True)
    h = (x * lax.rsqrt(ms + RMS_EPS) * g_ref[...]).astype(BF16)
    o_ref[...] = _dot(h, w_ref[...])


def _in_proj(x2, g, w, tm):
    t, d = x2.shape
    n = w.shape[1]
    return pl.pallas_call(
        _in_proj_kernel,
        grid=(t // tm,),
        in_specs=[pl.BlockSpec((tm, d), lambda i: (i, 0)),
                  pl.BlockSpec((1, d), lambda i: (0, 0)),
                  pl.BlockSpec((d, n), lambda i: (0, 0))],
        out_specs=pl.BlockSpec((tm, n), lambda i: (i, 0)),
        out_shape=jax.ShapeDtypeStruct((t, n), F32),
        compiler_params=_params(),
        name="in_proj",
    )(x2, g, w)


def _mla_prep_kernel(c_ref, kpe_ref, cs_ref, wq_ref, wkv_ref, gc_ref, gqk_ref, q_ref, k_ref, v_ref,
                     *, q_scale):
    c = c_ref[...]
    gc = gc_ref[...]

    def lora_norm(z, g):
        ms = jnp.mean(z * z, axis=-1, keepdims=True)
        return (z * lax.rsqrt(ms + RMS_EPS) * g).astype(BF16)

    cq = lora_norm(c[:, :Q_LORA_RANK], gc[:, :Q_LORA_RANK])
    ckv = lora_norm(c[:, Q_LORA_RANK:], gc[:, Q_LORA_RANK:])
    qb = _dot(cq, wq_ref[...])
    kvb = _dot(ckv, wkv_ref[...])
    kper = kpe_ref[...]
    cs = cs_ref[...]
    gqk = gqk_ref[...]
    tm = c.shape[0]

    row = lax.broadcasted_iota(jnp.int32, (LANES, LANES), 0)
    ones_all = jnp.ones((LANES, LANES), BF16)
    ones_lo = jnp.where(row < QK_ROPE_DIM, 1.0, 0.0).astype(BF16)
    lane = lax.broadcasted_iota(jnp.int32, (tm, LANES), 1)

    def rope(z, g_row):
        t = z * (g_row * cs)
        return jnp.where(lane < QK_ROPE_DIM, t + pltpu.roll(t, QK_ROPE_DIM, 1), 0.0)

    k_rope = rope(kper, gqk[3:4])
    kpe_ss = _split_dot(kper * kper, ones_lo)
    for h in range(B_HEADS):
        qn = qb[:, 2 * LANES * h: 2 * LANES * h + LANES]
        qr = qb[:, 2 * LANES * h + LANES: 2 * LANES * (h + 1)]
        ss = _split_dot(qn * qn, ones_all) + _split_dot(qr * qr, ones_lo)
        rs = lax.rsqrt(ss * (1.0 / QK_HEAD_DIM) + RMS_EPS) * q_scale
        q_ref[h, :, :LANES] = (qn * gqk[0:1] * rs).astype(BF16)
        q_ref[h, :, LANES:] = (rope(qr, gqk[1:2]) * rs).astype(BF16)
        kn = kvb[:, LANES * h: LANES * (h + 1)]
        ssk = _split_dot(kn * kn, ones_all) + kpe_ss
        rsk = lax.rsqrt(ssk * (1.0 / QK_HEAD_DIM) + RMS_EPS)
        k_ref[h, :, :LANES] = (kn * gqk[2:3] * rsk).astype(BF16)
        k_ref[h, :, LANES:] = (k_rope * rsk).astype(BF16)
    v_ref[...] = kvb[:, B_HEADS * QK_NOPE_DIM:].astype(BF16)


def _mla_prep(proj, cs, wq, wkv, gc, gqk, seq, tm):
    t = proj.shape[0]
    c_width = Q_LORA_RANK + KV_LORA_RANK
    c_block = 3 * A_WIDTH // c_width
    kpe_block = (3 * A_WIDTH + c_width) // LANES
    pos_blocks = seq // tm
    kernel = functools.partial(_mla_prep_kernel, q_scale=QK_HEAD_DIM ** -0.5 * LOG2E)
    return pl.pallas_call(
        kernel,
        grid=(t // tm,),
        in_specs=[pl.BlockSpec((tm, c_width), lambda i: (i, c_block)),
                  pl.BlockSpec((tm, LANES), lambda i: (i, kpe_block)),
                  pl.BlockSpec((tm, LANES), lambda i: (i % pos_blocks, 0)),
                  pl.BlockSpec(wq.shape, lambda i: (0, 0)),
                  pl.BlockSpec(wkv.shape, lambda i: (0, 0)),
                  pl.BlockSpec(gc.shape, lambda i: (0, 0)),
                  pl.BlockSpec(gqk.shape, lambda i: (0, 0))],
        out_specs=[pl.BlockSpec((B_HEADS, tm, 2 * LANES), lambda i: (0, i, 0)),
                   pl.BlockSpec((B_HEADS, tm, 2 * LANES), lambda i: (0, i, 0)),
                   pl.BlockSpec((tm, B_WIDTH), lambda i: (i, 0))],
        out_shape=[jax.ShapeDtypeStruct((B_HEADS, t, 2 * LANES), BF16),
                   jax.ShapeDtypeStruct((B_HEADS, t, 2 * LANES), BF16),
                   jax.ShapeDtypeStruct((t, B_WIDTH), BF16)],
        compiler_params=_params(),
        name="mla_prep",
    )(proj, proj, cs, wq, wkv, gc, gqk)


def _mla_attention_kernel(q_ref, k_ref, v_ref, o_ref, *, tq):
    seq = q_ref.shape[0]
    row = lax.broadcasted_iota(jnp.int32, (tq, tq), 0)
    col = lax.broadcasted_iota(jnp.int32, (tq, tq), 1)
    diag_mask = jnp.where(col <= row, 0.0, MASK_VALUE)
    v_ext = jnp.concatenate([v_ref[...], jnp.ones(v_ref.shape, v_ref.dtype)], axis=1)
    dv = v_ref.shape[1]
    for i in range(seq // tq):
        q = q_ref[i * tq:(i + 1) * tq, :]
        s_diag = _dot_nt(q, k_ref[i * tq:(i + 1) * tq, :]) + diag_mask
        m = jnp.max(s_diag, axis=-1, keepdims=True)
        if i > 0:
            s_past = _dot_nt(q, k_ref[:i * tq, :])
            m = jnp.maximum(m, jnp.max(s_past, axis=-1, keepdims=True))
        o = _dot(jnp.exp2(s_diag - m).astype(BF16), v_ext[i * tq:(i + 1) * tq, :])
        if i > 0:
            o = o + _dot(jnp.exp2(s_past - m).astype(BF16), v_ext[:i * tq, :])
        o_ref[i * tq:(i + 1) * tq, :] = (o[:, :dv] / o[:, dv:]).astype(o_ref.dtype)


def _mla_attention(qb, kb, vb, batch, seq, tq):
    t = vb.shape[0]
    kernel = functools.partial(_mla_attention_kernel, tq=tq)
    return pl.pallas_call(
        kernel,
        grid=(batch, B_HEADS),
        in_specs=[pl.BlockSpec((None, seq, 2 * LANES), lambda b, h: (h, b, 0)),
                  pl.BlockSpec((None, seq, 2 * LANES), lambda b, h: (h, b, 0)),
                  pl.BlockSpec((seq, V_HEAD_DIM), lambda b, h: (b, h))],
        out_specs=pl.BlockSpec((seq, V_HEAD_DIM), lambda b, h: (b, h)),
        out_shape=jax.ShapeDtypeStruct((t, B_WIDTH), BF16),
        compiler_params=_params(2),
        name="mla_attention",
    )(qb, kb, vb)


def _dilated_kernel(q_ref, k_ref, v_ref, bd_ref, g_ref, bias_ref, o_ref, qs, ks, acc_o, acc_l, acc_m,
                    *, unroll):
    seq = q_ref.shape[0]
    n_tiles = seq // BAND_BLOCK
    low = lax.broadcasted_iota(jnp.int32, (BAND_BLOCK, LANES), 1) < A_HEAD_DIM
    bd = bd_ref[...]
    g = g_ref[...]

    def head_norm(z, g_row):
        ss = _split_dot(z * z, bd)
        return z * lax.rsqrt(ss * (1.0 / A_HEAD_DIM) + RMS_EPS) * g_row

    qs[...] = head_norm(q_ref[...], g[0:1]) * (A_HEAD_DIM ** -0.5 * LOG2E)
    ks[...] = head_norm(k_ref[...], g[1:2])
    ones = jnp.ones((2 * BAND_BLOCK, LANES), BF16)

    for p, (window, dil) in enumerate(DILATED_PATTERNS):
        blocks_per_class = n_tiles // dil
        span = BAND_BLOCK * dil

        def rows_at(start, dil=dil):
            if dil == 1:
                return pl.ds(start, BAND_BLOCK)
            return pl.ds(start, BAND_BLOCK, stride=dil)

        def tile(i, carry, p=p, blocks_per_class=blocks_per_class, span=span, rows_at=rows_at):
            r = i // blocks_per_class
            j = i % blocks_per_class
            cur = rows_at(r + j * span)
            prev = rows_at(r + jnp.maximum(j - 1, 0) * span)
            first = jnp.where(j == 0, 1, 0)
            q = qs[cur, :]
            q2 = jnp.concatenate([jnp.where(low, q, 0.0), jnp.where(low, 0.0, q)], axis=0).astype(BF16)
            k_band = jnp.concatenate([ks[prev, :], ks[cur, :]], axis=0).astype(BF16)
            s = _dot_nt(q2, k_band) + bias_ref[p, first]
            m = jnp.max(s, axis=-1, keepdims=True)
            pr = jnp.exp2(s - m).astype(BF16)
            v_band = jnp.concatenate([v_ref[prev, :], v_ref[cur, :]], axis=0).astype(BF16)
            o = _dot(pr, jnp.concatenate([v_band, ones], axis=1))
            top, bot = o[:BAND_BLOCK], o[BAND_BLOCK:]
            acc_o[p, cur, :] = jnp.where(low, top[:, :LANES], bot[:, :LANES])
            acc_l[p, cur, :] = jnp.where(low, top[:, LANES:], bot[:, LANES:])
            acc_m[p, cur, :] = jnp.where(low, m[:BAND_BLOCK], m[BAND_BLOCK:])
            return carry

        lax.fori_loop(0, n_tiles, tile, 0, unroll=unroll)

    m_all = jnp.maximum(jnp.maximum(acc_m[0], acc_m[1]), acc_m[2])
    num = jnp.zeros((seq, LANES), F32)
    den = jnp.zeros((seq, LANES), F32)
    for p in range(len(DILATED_PATTERNS)):
        w = jnp.exp2(acc_m[p] - m_all)
        num = num + w * acc_o[p]
        den = den + w * acc_l[p]
    o_ref[...] = (num / den).astype(o_ref.dtype)


def _dilated_attention(proj, bd, g, bias, batch, seq):
    t = proj.shape[0]
    pairs = A_WIDTH // LANES
    n_pat = len(DILATED_PATTERNS)
    return pl.pallas_call(
        functools.partial(_dilated_kernel, unroll=16),
        grid=(batch, pairs),
        in_specs=[pl.BlockSpec((seq, LANES), lambda b, c: (b, c)),
                  pl.BlockSpec((seq, LANES), lambda b, c: (b, pairs + c)),
                  pl.BlockSpec((seq, LANES), lambda b, c: (b, 2 * pairs + c)),
                  pl.BlockSpec((LANES, LANES), lambda b, c: (0, 0)),
                  pl.BlockSpec((None, 2, LANES), lambda b, c: (c, 0, 0)),
                  pl.BlockSpec((None, n_pat, 2, 2 * BAND_BLOCK, 2 * BAND_BLOCK),
                               lambda b, c: (c, 0, 0, 0, 0))],
        out_specs=pl.BlockSpec((seq, LANES), lambda b, c: (b, c)),
        out_shape=jax.ShapeDtypeStruct((t, A_WIDTH), BF16),
        scratch_shapes=[pltpu.VMEM((seq, LANES), F32)] * 2
                       + [pltpu.VMEM((n_pat, seq, LANES), F32)] * 3,
        compiler_params=_params(2),
        name="dilated_attention",
    )(proj, proj, proj, bd, g, bias)


def _t5_bucket(dist):
    max_exact = NUM_BUCKETS // 2
    df = jnp.maximum(dist, 1).astype(F32)
    log_bucket = max_exact + (jnp.log(df / max_exact) / math.log(MAX_DISTANCE / max_exact)
                              * (NUM_BUCKETS - max_exact)).astype(jnp.int32)
    log_bucket = jnp.minimum(log_bucket, NUM_BUCKETS - 1)
    return jnp.where(dist < max_exact, dist, log_bucket)


def _band_bias(rel_bias):
    n = BAND_BLOCK
    qi = jnp.arange(n)[:, None]
    kj = jnp.arange(n)[None, :]
    buckets = jnp.arange(NUM_BUCKETS)
    tables = []
    for window, dil in DILATED_PATTERNS:
        steps = window // dil
        halves = []
        for back in (qi - kj + n, qi - kj):
            onehot = (_t5_bucket(jnp.maximum(back, 0) * dil)[:, :, None] == buckets).astype(F32)
            vals = jnp.einsum('qkb,bh->hqk', onehot, rel_bias.astype(F32),
                              precision=lax.Precision.HIGHEST) * LOG2E
            halves.append(jnp.where(((back >= 0) & (back <= steps))[None], vals, MASK_VALUE))
        prev, cur = halves
        normal = jnp.concatenate([prev, cur], axis=-1)
        first = jnp.concatenate([jnp.full_like(prev, MASK_VALUE), cur], axis=-1)
        tables.append(jnp.stack([normal, first], axis=1))
    tab = jnp.stack(tables, axis=1)
    tab = tab.reshape(A_HEADS // 2, 2, len(DILATED_PATTERNS), 2, n, 2 * n)
    return tab.transpose(0, 2, 3, 1, 4, 5).reshape(A_HEADS // 2, len(DILATED_PATTERNS), 2, 2 * n, 2 * n)


def _pack_pairs(v):
    half = v.shape[1] // 2
    bits = pltpu.bitcast(v, jnp.uint32)
    return (bits[:, :half] >> 16) | (bits[:, half:] & jnp.uint32(0xFFFF0000))


def _unpack_pairs(p):
    return (pltpu.bitcast(p << 16, F32), pltpu.bitcast(p & jnp.uint32(0xFFFF0000), F32))


def _out_proj_kernel(x_ref, a_ref, b_ref, w_ref, g_ref, rw_ref, rb_ref,
                     x1_ref, hp_ref, route_ref, route_t_ref, blk_ref, carry):
    i = pl.program_id(0)
    tm = x_ref.shape[0]

    @pl.when(i == 0)
    def _():
        carry[...] = jnp.zeros_like(carry)

    w = w_ref[...]
    x1 = x_ref[...] + _dot(a_ref[...], w[:A_WIDTH]) + _dot(b_ref[...], w[A_WIDTH:])
    x1_ref[...] = x1
    ms = jnp.mean(x1 * x1, axis=-1, keepdims=True)
    h = x1 * lax.rsqrt(ms + RMS_EPS) * g_ref[...]
    hi = h.astype(BF16)
    hi_f = hi.astype(F32)
    hp_ref[...] = _pack_pairs(hi_f)
    lo = (h - hi_f).astype(BF16)
    rw = rw_ref[...]
    hw = _dot(hi, rw)
    logits = hw[:, :LANES] + hw[:, LANES:] + _dot(lo, rw[:, :LANES]) + rb_ref[...]

    lane = lax.broadcasted_iota(jnp.int32, (tm, LANES), 1).astype(F32)
    remaining = logits
    vals, hots = [], []
    for _ in range(TOP_K):
        m = jnp.max(remaining, axis=-1, keepdims=True)
        first = jnp.min(jnp.where(remaining == m, lane, float(LANES)), axis=-1, keepdims=True)
        hot = lane == first
        remaining = jnp.where(hot, -jnp.inf, remaining)
        vals.append(m)
        hots.append(hot)
    exps = [jnp.exp(v - vals[0]) for v in vals]
    den = exps[0] + exps[1] + exps[2] + exps[3]

    chosen = jnp.zeros((tm, LANES), F32)
    for hot in hots:
        chosen = chosen + jnp.where(hot, 1.0, 0.0)
    r = lax.broadcasted_iota(jnp.int32, (tm, tm), 0)
    c = lax.broadcasted_iota(jnp.int32, (tm, tm), 1)
    earlier = jnp.where(r > c, 1.0, 0.0).astype(BF16)
    before = carry[...] + _dot(earlier, chosen.astype(BF16))
    in_block = jnp.sum(chosen, axis=0, keepdims=True)
    carry[...] = carry[...] + in_block
    blk_ref[...] = jnp.broadcast_to(in_block, blk_ref.shape)

    route = jnp.zeros((tm, LANES), F32)
    for k in range(TOP_K):
        first = jnp.sum(jnp.where(hots[k], lane, 0.0), axis=-1, keepdims=True)
        rank = jnp.sum(jnp.where(hots[k], before, 0.0), axis=-1, keepdims=True)
        route = route + jnp.where(lane == float(k), first, 0.0)
        route = route + jnp.where(lane == float(TOP_K + k), rank, 0.0)
        route = route + jnp.where(lane == float(2 * TOP_K + k), exps[k] / den, 0.0)
    route_ref[...] = route
    route_t_ref[...] = route.T[:route_t_ref.shape[0], :]


def _out_proj(x2, out_a, out_b, w, g, rw, rb, tm):
    t, d = x2.shape
    return pl.pallas_call(
        _out_proj_kernel,
        grid=(t // tm,),
        in_specs=[pl.BlockSpec((tm, d), lambda i: (i, 0)),
                  pl.BlockSpec((tm, A_WIDTH), lambda i: (i, 0)),
                  pl.BlockSpec((tm, B_WIDTH), lambda i: (i, 0)),
                  pl.BlockSpec(w.shape, lambda i: (0, 0)),
                  pl.BlockSpec((1, d), lambda i: (0, 0)),
                  pl.BlockSpec(rw.shape, lambda i: (0, 0)),
                  pl.BlockSpec((1, LANES), lambda i: (0, 0))],
        out_specs=[pl.BlockSpec((tm, d), lambda i: (i, 0)),
                   pl.BlockSpec((tm, d // 2), lambda i: (i, 0)),
                   pl.BlockSpec((tm, LANES), lambda i: (i, 0)),
                   pl.BlockSpec((2 * TOP_K, tm), lambda i: (0, i)),
                   pl.BlockSpec((None, 8, LANES), lambda i: (i, 0, 0))],
        out_shape=[jax.ShapeDtypeStruct((t, d), F32),
                   jax.ShapeDtypeStruct((t, d // 2), jnp.uint32),
                   jax.ShapeDtypeStruct((t, LANES), F32),
                   jax.ShapeDtypeStruct((2 * TOP_K, t), F32),
                   jax.ShapeDtypeStruct((t // tm, 8, LANES), F32)],
        scratch_shapes=[pltpu.VMEM((1, LANES), F32)],
        compiler_params=_params(),
        name="out_proj",
    )(x2, out_a, out_b, w, g, rw, rb)


def _dispatch_kernel(pad_ref, end_ref, nb_ref, dest_hbm, h_ref, xs_hbm, dest_s0, dest_s1, zeros,
                     sem_i, sem_z, sem_o, *, bm):
    i = pl.program_id(0)
    steps = pl.num_programs(0)
    tm = h_ref.shape[0]
    n_blocks = xs_hbm.shape[0] // bm
    dest_s = (dest_s0, dest_s1)

    def idx_copy(step, slot):
        return pltpu.make_async_copy(dest_hbm.at[step], dest_s[slot], sem_i.at[slot])

    @pl.when(i == 0)
    def _():
        idx_copy(0, 0).start()
        zeros[...] = jnp.zeros_like(zeros)

        def zero_block(start):
            return pltpu.make_async_copy(zeros, xs_hbm.at[pl.ds(pl.multiple_of(start, bm), bm), :], sem_z)

        for e in range(N_EXPERTS):
            @pl.when(pad_ref[e] > 0)
            def _():
                zero_block(end_ref[e] - bm).start()

        def start_unused(b, carry):
            zero_block(b * bm).start()
            return carry

        def wait_unused(b, carry):
            zero_block(b * bm).wait()
            return carry

        lax.fori_loop(nb_ref[0], n_blocks, start_unused, 0)
        for e in range(N_EXPERTS):
            @pl.when(pad_ref[e] > 0)
            def _():
                zero_block(end_ref[e] - bm).wait()
        lax.fori_loop(nb_ref[0], n_blocks, wait_unused, 0)

    for slot in range(2):
        @pl.when(i % 2 == slot)
        def _(slot=slot):
            @pl.when(i + 1 < steps)
            def _():
                idx_copy(i + 1, 1 - slot).start()

            idx_copy(i, slot).wait()

            def body(t, carry):
                for k in range(TOP_K):
                    pltpu.make_async_copy(h_ref.at[pl.ds(t, 1), :],
                                          xs_hbm.at[pl.ds(dest_s[slot][k * tm + t], 1), :],
                                          sem_o).start(priority=k % 2)
                return carry

            lax.fori_loop(0, tm, body, 0, unroll=8)
    for _ in range(TOP_K):
        pltpu.make_async_copy(h_ref, xs_hbm.at[pl.ds(0, tm), :], sem_o).wait()


def _dispatch(padded, pad_end, n_used, dest, hp, n_slots, tm, bm):
    t, half = hp.shape
    grid_spec = pltpu.PrefetchScalarGridSpec(
        num_scalar_prefetch=3,
        grid=(t // tm,),
        in_specs=[pl.BlockSpec(memory_space=pl.ANY),
                  pl.BlockSpec((tm, half), lambda i, p, e, n: (i, 0))],
        out_specs=pl.BlockSpec(memory_space=pl.ANY),
        scratch_shapes=[pltpu.SMEM((tm * TOP_K,), jnp.int32),
                        pltpu.SMEM((tm * TOP_K,), jnp.int32),
                        pltpu.VMEM((bm, half), jnp.uint32),
                        pltpu.SemaphoreType.DMA((2,)),
                        pltpu.SemaphoreType.DMA(()),
                        pltpu.SemaphoreType.DMA(())],
    )
    return pl.pallas_call(
        functools.partial(_dispatch_kernel, bm=bm),
        grid_spec=grid_spec,
        out_shape=jax.ShapeDtypeStruct((n_slots, half), jnp.uint32),
        compiler_params=_params(),
        name="dispatch",
    )(padded, pad_end, n_used, dest, hp)


EXPERT_SUB_ROWS = 256


def _expert_ffn_kernel(be_ref, nb_ref, next_ref, x_ref, wgu_hbm, bgu_ref, wd_hbm, bd_ref, y_ref,
                       wgu_f, wd_f, wgu_s, wd_s, sem_w):
    i = pl.program_id(0)
    d_ff = wd_f.shape[0]
    half = x_ref.shape[1]

    def weight_copies(e):
        return (pltpu.make_async_copy(wgu_hbm.at[e], wgu_f, sem_w.at[0]),
                pltpu.make_async_copy(wd_hbm.at[e], wd_f, sem_w.at[1]))

    @pl.when(jnp.logical_and(i == 0, nb_ref[0] > 0))
    def _():
        for copy in weight_copies(be_ref[0]):
            copy.start()

    @pl.when(i >= nb_ref[0])
    def _():
        y_ref[...] = jnp.zeros_like(y_ref)

    @pl.when(i < nb_ref[0])
    def _():
        changed = jnp.logical_or(i == 0, be_ref[i] != be_ref[jnp.maximum(i - 1, 0)])

        @pl.when(changed)
        def _():
            for copy in weight_copies(be_ref[i]):
                copy.wait()
            wgu_s[...] = wgu_f[...].astype(BF16)
            wd_s[...] = wd_f[...].astype(BF16)

            @pl.when(next_ref[i] >= 0)
            def _():
                for copy in weight_copies(next_ref[i]):
                    copy.start()

        for r0 in range(0, x_ref.shape[0], EXPERT_SUB_ROWS):
            rows = slice(r0, r0 + EXPERT_SUB_ROWS)
            x_lo, x_hi = _unpack_pairs(x_ref[rows, :])
            gu = (_dot(x_lo.astype(BF16), wgu_s[:half, :]) + _dot(x_hi.astype(BF16), wgu_s[half:, :])
                  + bgu_ref[...])
            gate = jnp.minimum(gu[:, :d_ff], SWIGLU_LIMIT)
            up = jnp.clip(gu[:, d_ff:], -SWIGLU_LIMIT, SWIGLU_LIMIT)
            glu = gate * jax.nn.sigmoid(SWIGLU_ALPHA * gate)
            act = ((up + 1.0) * glu).astype(BF16)
            y = _dot(act, wd_s[...]) + bd_ref[...]
            y_ref[rows, :] = _pack_pairs(y.astype(BF16).astype(F32))


def _expert_ffn(block_expert, n_used, next_expert, xs, w_gate_up, b_gate_up, w_down, b_down, bm):
    n_slots, half = xs.shape
    n_e, d, two_ff = w_gate_up.shape
    d_ff = two_ff // 2
    n_blocks = n_slots // bm

    def used(i, nb):
        return jnp.minimum(i, jnp.maximum(nb[0] - 1, 0))

    grid_spec = pltpu.PrefetchScalarGridSpec(
        num_scalar_prefetch=3,
        grid=(n_blocks,),
        in_specs=[pl.BlockSpec((bm, half), lambda i, be, nb, nx: (used(i, nb), 0)),
                  pl.BlockSpec(memory_space=pl.ANY),
                  pl.BlockSpec((None, 1, two_ff), lambda i, be, nb, nx: (be[i], 0, 0)),
                  pl.BlockSpec(memory_space=pl.ANY),
                  pl.BlockSpec((None, 1, d), lambda i, be, nb, nx: (be[i], 0, 0))],
        out_specs=pl.BlockSpec((bm, half), lambda i, be, nb, nx: (i, 0)),
        scratch_shapes=[pltpu.VMEM((d, two_ff), F32), pltpu.VMEM((d_ff, d), F32),
                        pltpu.VMEM((d, two_ff), BF16), pltpu.VMEM((d_ff, d), BF16),
                        pltpu.SemaphoreType.DMA((2,))],
    )
    return pl.pallas_call(
        _expert_ffn_kernel,
        grid_spec=grid_spec,
        out_shape=jax.ShapeDtypeStruct((n_slots, half), jnp.uint32),
        compiler_params=_params(),
        name="expert_ffn",
    )(block_expert, n_used, next_expert, xs, w_gate_up, b_gate_up.reshape(n_e, 1, two_ff),
      w_down, b_down.reshape(n_e, 1, d))


COMBINE_LOCAL_ROWS = 2560
SEG_ALIGN = 8


def _pieces(limit):
    sizes = []
    size = 1 << (limit.bit_length() - 1)
    while size >= SEG_ALIGN:
        sizes.append(size)
        size //= 2
    return sizes


def _combine_kernel(src_ref, len_ref, dst_ref, tot_ref, route_ref, local_ref, x1_ref, ys_hbm, o_ref, ybuf,
                    sem_g):
    j = pl.program_id(0)
    blocks = pl.num_programs(0) - 1
    tm = x1_ref.shape[0]
    half = x1_ref.shape[1] // 2
    local_rows = ybuf.shape[1]

    @pl.when(j == 0)
    def _():
        ybuf[...] = jnp.zeros_like(ybuf)

    @pl.when(j < blocks)
    def _():
        for s in range(2):
            @pl.when(j % 2 == s)
            def _(s=s):
                for e in range(N_EXPERTS):
                    seg = j * N_EXPERTS + e
                    n = len_ref[seg]
                    src = src_ref[seg]
                    dst = dst_ref[seg]
                    for size in _pieces(tm + 2 * SEG_ALIGN):
                        @pl.when((n & size) != 0)
                        def _(size=size, src=src, dst=dst):
                            pltpu.make_async_copy(
                                ys_hbm.at[pl.ds(pl.multiple_of(src, SEG_ALIGN), size), :],
                                ybuf.at[s, pl.ds(pl.multiple_of(dst, SEG_ALIGN), size), :],
                                sem_g.at[s]).start()
                        step = jnp.where((n & size) != 0, size, 0)
                        src = src + step
                        dst = dst + step

    @pl.when(j >= 1)
    def _():
        slot = (j - 1) % 2
        total = tot_ref[j - 1]
        for size in _pieces(local_rows):
            @pl.when((total & size) != 0)
            def _(size=size):
                pltpu.make_async_copy(ys_hbm.at[pl.ds(0, size), :], ybuf.at[slot, pl.ds(0, size), :],
                                      sem_g.at[slot]).wait()

        route = route_ref[...]
        local_t = local_ref[...]
        local = jnp.concatenate(
            [local_t, jnp.zeros((LANES - local_t.shape[0], tm), F32)], axis=0).T
        col = lax.broadcasted_iota(jnp.int32, (tm, local_rows), 1).astype(F32)
        g = jnp.zeros((tm, local_rows), F32)
        for k in range(TOP_K):
            pos = local[:, k: k + 1]
            gate = route[:, 2 * TOP_K + k: 2 * TOP_K + k + 1]
            g = jnp.where(col == pos, gate, g)
        g = g.astype(BF16)
        lo, hi = _unpack_pairs(ybuf[slot])
        o_ref[:, :half] = x1_ref[:, :half] + _dot(g, lo.astype(BF16))
        o_ref[:, half:] = x1_ref[:, half:] + _dot(g, hi.astype(BF16))


def _combine(tables, route, local_t, x1, ys, tm, local_rows):
    t, d = x1.shape
    half = d // 2

    def summed(j, *_):
        return (jnp.maximum(j - 1, 0), 0)

    grid_spec = pltpu.PrefetchScalarGridSpec(
        num_scalar_prefetch=4,
        grid=(t // tm + 1,),
        in_specs=[pl.BlockSpec((tm, LANES), summed),
                  pl.BlockSpec((local_t.shape[0], tm), lambda j, *_: (0, jnp.maximum(j - 1, 0))),
                  pl.BlockSpec((tm, d), summed),
                  pl.BlockSpec(memory_space=pl.ANY)],
        out_specs=pl.BlockSpec((tm, d), summed),
        scratch_shapes=[pltpu.VMEM((2, local_rows, half), jnp.uint32),
                        pltpu.SemaphoreType.DMA((2,))],
    )
    return pl.pallas_call(
        _combine_kernel,
        grid_spec=grid_spec,
        out_shape=jax.ShapeDtypeStruct((t, d), F32),
        compiler_params=_params(),
        name="combine",
    )(*tables, route, local_t, x1, ys)


def _per_expert(table, idx):
    out = jnp.zeros(idx.shape, table.dtype)
    for e in range(N_EXPERTS):
        out = jnp.where(idx == e, table[:, e][None, :, None], out)
    return out


def _combine_tables(idx, rank, n, pad_start, tm, local_rows):
    before = jnp.cumsum(n, axis=0) - n
    first = pad_start[None, :] + before
    src = first // SEG_ALIGN * SEG_ALIGN
    length = jnp.where(n > 0, (first + n + SEG_ALIGN - 1) // SEG_ALIGN * SEG_ALIGN - src, 0)
    dst = jnp.cumsum(length, axis=1) - length
    total = jnp.sum(length, axis=1)
    assert local_rows >= tm * TOP_K + N_EXPERTS * 2 * (SEG_ALIGN - 1)
    local = (_per_expert(dst + first - src - before, idx) + rank).reshape(TOP_K, -1).astype(F32)
    tables = (src.reshape(-1).astype(jnp.int32), length.reshape(-1).astype(jnp.int32),
              dst.reshape(-1).astype(jnp.int32), total.astype(jnp.int32))
    return tables, local


def _slot_layout(route_t, n, tm, bm):
    t = route_t.shape[1]
    blocks = t // tm
    idx = route_t[:TOP_K].astype(jnp.int32).reshape(TOP_K, blocks, tm)
    rank = route_t[TOP_K:].astype(jnp.int32).reshape(TOP_K, blocks, tm)
    counts = jnp.sum(n, axis=0)
    padded = (counts + bm - 1) // bm * bm
    pad_end = jnp.cumsum(padded).astype(jnp.int32)
    pad_start = pad_end - padded
    dest = _per_expert(jnp.broadcast_to(pad_start[None, :], n.shape), idx) + rank
    dest = dest.transpose(1, 0, 2).reshape(blocks, TOP_K * tm)
    n_blocks = -(-t * TOP_K // bm) + N_EXPERTS
    block_start = jnp.arange(n_blocks, dtype=jnp.int32) * bm
    block_expert = jnp.minimum(jnp.sum(pad_end[None, :] <= block_start[:, None], axis=1),
                               N_EXPERTS - 1).astype(jnp.int32)
    n_used = (pad_end[-1:] // bm).astype(jnp.int32)
    following = jnp.take(pad_end, block_expert) // bm
    next_expert = jnp.where(following < n_used[0],
                            jnp.take(block_expert, jnp.minimum(following, n_blocks - 1)), -1).astype(jnp.int32)
    return idx, rank, padded, pad_end, dest, block_expert, next_expert, n_used, n_blocks * bm


def _rot_cols(w):
    half = w.shape[-1] // 2
    return jnp.concatenate([-w[..., half:], w[..., :half]], axis=-1)


def _swap_halves(g):
    half = g.shape[-1] // 2
    return jnp.concatenate([g[..., half:], g[..., :half]], axis=-1)


def kernel(x, attn_norm_g, w_in, a_q_norm_g, a_k_norm_g, rel_bias, q_a_norm_g, w_q_b, kv_a_norm_g,
           w_kv_b, b_q_norm_g, b_k_norm_g, w_out, ffn_norm_g, router_w, router_b, w_gate_up,
           b_gate_up, w_down, b_down):
    batch, seq, d = x.shape
    t = batch * seq
    depth = w_in.shape[0]
    tm = 512
    expert_bm = 512

    pos = jnp.arange(seq, dtype=F32)
    inv_freq = ROPE_THETA ** (-jnp.arange(0, QK_ROPE_DIM, 2, dtype=F32) / QK_ROPE_DIM)
    ang = pos[:, None] * inv_freq[None, :]
    cos, sin = jnp.cos(ang), jnp.sin(ang)
    cs = jnp.concatenate([cos, cos, sin, sin], axis=-1)

    row = jnp.arange(LANES)[:, None] // A_HEAD_DIM
    col = jnp.arange(LANES)[None, :] // A_HEAD_DIM
    head_block_ones = (row == col).astype(BF16)
    band_bias = _band_bias(rel_bias)

    x2 = x.reshape(t, d)
    for layer in range(depth):
        kpe_off = 3 * A_WIDTH + Q_LORA_RANK + KV_LORA_RANK
        w_kpe = w_in[layer][:, kpe_off:]
        w_in_r = jnp.concatenate([w_in[layer][:, :kpe_off], w_kpe, _rot_cols(w_kpe)], axis=1).astype(BF16)

        wq = w_q_b[layer].reshape(Q_LORA_RANK, B_HEADS, QK_HEAD_DIM)
        wq_rope = wq[..., QK_NOPE_DIM:]
        wq_r = jnp.concatenate([wq, _rot_cols(wq_rope)], axis=-1).reshape(Q_LORA_RANK, -1).astype(BF16)
        wkv = w_kv_b[layer].reshape(KV_LORA_RANK, B_HEADS, QK_NOPE_DIM + V_HEAD_DIM)
        wkv_r = jnp.concatenate([wkv[..., :QK_NOPE_DIM].reshape(KV_LORA_RANK, -1),
                                 wkv[..., QK_NOPE_DIM:].reshape(KV_LORA_RANK, -1)], axis=1).astype(BF16)
        gc = jnp.concatenate([q_a_norm_g[layer], kv_a_norm_g[layer]])[None, :]
        gq, gk = b_q_norm_g[layer], b_k_norm_g[layer]

        def rope_gain(gr):
            return jnp.concatenate([gr, _swap_halves(gr)])

        gqk = jnp.stack([gq[:QK_NOPE_DIM], rope_gain(gq[QK_NOPE_DIM:]),
                         gk[:QK_NOPE_DIM], rope_gain(gk[QK_NOPE_DIM:])])
        g_a = jnp.stack([jnp.tile(a_q_norm_g[layer], 2), jnp.tile(a_k_norm_g[layer], 2)])
        g_a = jnp.broadcast_to(g_a[None], (A_WIDTH // LANES, 2, LANES))

        proj = _in_proj(x2, attn_norm_g[layer][None, :], w_in_r, tm)
        qb, kb, vb = _mla_prep(proj, cs, wq_r, wkv_r, gc, gqk, seq, tm)
        out_b = _mla_attention(qb, kb, vb, batch, seq, 512)
        out_a = _dilated_attention(proj, head_block_ones, g_a, band_bias, batch, seq)

        rw = jnp.pad(router_w[layer], ((0, 0), (0, LANES - N_EXPERTS)))
        rw_hi = rw.astype(BF16)
        rw_lo = (rw - rw_hi.astype(F32)).astype(BF16)
        rb = jnp.pad(router_b[layer], (0, LANES - N_EXPERTS), constant_values=-jnp.inf)[None, :]
        x1, hp, route, route_t, in_block = _out_proj(
            x2, out_a, out_b, w_out[layer].astype(BF16), ffn_norm_g[layer][None, :],
            jnp.concatenate([rw_hi, rw_lo], axis=1), rb, tm)

        n = in_block[:, 0, :N_EXPERTS].astype(jnp.int32)
        idx, rank, padded, pad_end, dest, block_expert, next_expert, n_used, n_slots = _slot_layout(
            route_t, n, tm, expert_bm)
        xs = _dispatch(padded, pad_end, n_used, dest, hp, n_slots, tm, expert_bm)
        ys = _expert_ffn(block_expert, n_used, next_expert, xs, w_gate_up[layer], b_gate_up[layer],
                         w_down[layer], b_down[layer], expert_bm)
        tables, local_t = _combine_tables(idx, rank, n, pad_end - padded, tm, COMBINE_LOCAL_ROWS)
        x2 = _combine(tables, route, local_t, x1, ys, tm, COMBINE_LOCAL_ROWS)
    return x2.reshape(batch, seq, d)
```

```python
import functools
import math

import jax
import jax.numpy as jnp
from jax import lax
from jax.experimental import pallas as pl
from jax.experimental.pallas import tpu as pltpu

A_HEADS = 8
A_HEAD_DIM = 64
A_WIDTH = A_HEADS * A_HEAD_DIM
DILATED_PATTERNS = ((128, 1), (512, 4), (2048, 16))
BAND_BLOCK = 128

B_HEADS = 4
QK_NOPE_DIM = 128
QK_ROPE_DIM = 64
QK_HEAD_DIM = QK_NOPE_DIM + QK_ROPE_DIM
V_HEAD_DIM = 128
Q_LORA_RANK = 256
KV_LORA_RANK = 256
B_WIDTH = B_HEADS * V_HEAD_DIM
ROPE_THETA = 10000.0

NUM_BUCKETS = 32
MAX_DISTANCE = 2048

N_EXPERTS = 32
TOP_K = 4
SWIGLU_LIMIT = 7.0
SWIGLU_ALPHA = 1.702
RMS_EPS = 1e-6

LANES = 128
MASK_VALUE = -1e30
LOG2E = math.log2(math.e)
PROJ_WIDTH = 3 * A_WIDTH + Q_LORA_RANK + KV_LORA_RANK + 2 * QK_ROPE_DIM
VMEM_LIMIT = 56 * 1024 * 1024

F32 = jnp.float32
BF16 = jnp.bfloat16


def _dot(a, b):
    return jnp.dot(a, b, preferred_element_type=F32)


def _dot_nt(a, b):
    return lax.dot_general(a, b, (((1,), (1,)), ((), ())), preferred_element_type=F32)


def _group_sum(x, m):
    return _dot(x.astype(BF16), m)


def _params(n_parallel=1):
    return pltpu.CompilerParams(
        dimension_semantics=("arbitrary",) * n_parallel, vmem_limit_bytes=VMEM_LIMIT)


IN_PROJ_SUB_ROWS = 256


def _mla_qkv(c, kper, cs, wq_ref, wkv_ref, gc, gqk, q_ref, k_ref, v_ref, rows, q_scale):
    def lora_norm(z, g):
        ms = jnp.mean(z * z, axis=-1, keepdims=True)
        return (z * lax.rsqrt(ms + RMS_EPS) * g).astype(BF16)

    cq = lora_norm(c[:, :Q_LORA_RANK], gc[:, :Q_LORA_RANK])
    ckv = lora_norm(c[:, Q_LORA_RANK:], gc[:, Q_LORA_RANK:])
    qb = _dot(cq, wq_ref[...])
    kvb = _dot(ckv, wkv_ref[...])
    n = c.shape[0]

    row = lax.broadcasted_iota(jnp.int32, (LANES, LANES), 0)
    ones_all = jnp.ones((LANES, LANES), BF16)
    ones_lo = jnp.where(row < QK_ROPE_DIM, 1.0, 0.0).astype(BF16)
    lane = lax.broadcasted_iota(jnp.int32, (n, LANES), 1)

    def rope(z, g_row):
        t = z * (g_row * cs)
        return jnp.where(lane < QK_ROPE_DIM, t + pltpu.roll(t, QK_ROPE_DIM, 1), 0.0)

    k_rope = rope(kper, gqk[3:4])
    kpe_ss = _group_sum(kper * kper, ones_lo)
    for h in range(B_HEADS):
        qn = qb[:, 2 * LANES * h: 2 * LANES * h + LANES]
        qr = qb[:, 2 * LANES * h + LANES: 2 * LANES * (h + 1)]
        ss = _group_sum(qn * qn, ones_all) + _group_sum(qr * qr, ones_lo)
        rs = lax.rsqrt(ss * (1.0 / QK_HEAD_DIM) + RMS_EPS) * q_scale
        q_ref[h, rows, :LANES] = (qn * gqk[0:1] * rs).astype(BF16)
        q_ref[h, rows, LANES:] = (rope(qr, gqk[1:2]) * rs).astype(BF16)
        kn = kvb[:, LANES * h: LANES * (h + 1)]
        ssk = _group_sum(kn * kn, ones_all) + kpe_ss
        rsk = lax.rsqrt(ssk * (1.0 / QK_HEAD_DIM) + RMS_EPS)
        k_ref[h, rows, :LANES] = (kn * gqk[2:3] * rsk).astype(BF16)
        k_ref[h, rows, LANES:] = (k_rope * rsk).astype(BF16)
    v_ref[rows, :] = kvb[:, B_HEADS * QK_NOPE_DIM:].astype(BF16)


def _in_proj_kernel(x_ref, g_ref, w_ref, cs_ref, wq_ref, wkv_ref, gc_ref, gqk_ref,
                    a_ref, q_ref, k_ref, v_ref, *, q_scale):
    a_width = a_ref.shape[1]
    c_width = Q_LORA_RANK + KV_LORA_RANK
    for r0 in range(0, x_ref.shape[0], IN_PROJ_SUB_ROWS):
        rows = slice(r0, r0 + IN_PROJ_SUB_ROWS)
        x = x_ref[rows, :]
        ms = jnp.mean(x * x, axis=-1, keepdims=True)
        h = (x * lax.rsqrt(ms + RMS_EPS) * g_ref[...]).astype(BF16)
        proj = _dot(h, w_ref[...])
        a_ref[rows, :] = proj[:, :a_width]
        _mla_qkv(proj[:, a_width:a_width + c_width], proj[:, a_width + c_width:], cs_ref[rows, :],
                 wq_ref, wkv_ref, gc_ref[...], gqk_ref[...], q_ref, k_ref, v_ref, rows, q_scale)


def _in_proj(x2, g, w, cs, wq, wkv, gc, gqk, seq, tm):
    t, d = x2.shape
    a_width = 3 * A_WIDTH
    pos_blocks = seq // tm
    kernel = functools.partial(_in_proj_kernel, q_scale=QK_HEAD_DIM ** -0.5 * LOG2E)
    return pl.pallas_call(
        kernel,
        grid=(t // tm,),
        in_specs=[pl.BlockSpec((tm, d), lambda i: (i, 0)),
                  pl.BlockSpec((1, d), lambda i: (0, 0)),
                  pl.BlockSpec(w.shape, lambda i: (0, 0)),
                  pl.BlockSpec((tm, LANES), lambda i: (i % pos_blocks, 0)),
                  pl.BlockSpec(wq.shape, lambda i: (0, 0)),
                  pl.BlockSpec(wkv.shape, lambda i: (0, 0)),
                  pl.BlockSpec(gc.shape, lambda i: (0, 0)),
                  pl.BlockSpec(gqk.shape, lambda i: (0, 0))],
        out_specs=[pl.BlockSpec((tm, a_width), lambda i: (i, 0)),
                   pl.BlockSpec((B_HEADS, tm, 2 * LANES), lambda i: (0, i, 0)),
                   pl.BlockSpec((B_HEADS, tm, 2 * LANES), lambda i: (0, i, 0)),
                   pl.BlockSpec((tm, B_WIDTH), lambda i: (i, 0))],
        out_shape=[jax.ShapeDtypeStruct((t, a_width), F32),
                   jax.ShapeDtypeStruct((B_HEADS, t, 2 * LANES), BF16),
                   jax.ShapeDtypeStruct((B_HEADS, t, 2 * LANES), BF16),
                   jax.ShapeDtypeStruct((t, B_WIDTH), BF16)],
        compiler_params=_params(),
        name="in_proj",
    )(x2, g, w, cs, wq, wkv, gc, gqk)


def _mla_attention_kernel(q_ref, k_ref, v_ref, o_ref, *, tq):
    seq = q_ref.shape[0]
    row = lax.broadcasted_iota(jnp.int32, (tq, tq), 0)
    col = lax.broadcasted_iota(jnp.int32, (tq, tq), 1)
    diag_mask = jnp.where(col <= row, 0.0, MASK_VALUE)
    v_ext = jnp.concatenate([v_ref[...], jnp.ones(v_ref.shape, v_ref.dtype)], axis=1)
    dv = v_ref.shape[1]
    for i in range(seq // tq):
        q = q_ref[i * tq:(i + 1) * tq, :]
        s_diag = _dot_nt(q, k_ref[i * tq:(i + 1) * tq, :]) + diag_mask
        m = jnp.max(s_diag, axis=-1, keepdims=True)
        if i > 0:
            s_past = _dot_nt(q, k_ref[:i * tq, :])
            m = jnp.maximum(m, jnp.max(s_past, axis=-1, keepdims=True))
        o = _dot(jnp.exp2(s_diag - m).astype(BF16), v_ext[i * tq:(i + 1) * tq, :])
        if i > 0:
            o = o + _dot(jnp.exp2(s_past - m).astype(BF16), v_ext[:i * tq, :])
        o_ref[i * tq:(i + 1) * tq, :] = (o[:, :dv] / o[:, dv:]).astype(o_ref.dtype)


def _mla_attention(qb, kb, vb, batch, seq, tq):
    t = vb.shape[0]
    kernel = functools.partial(_mla_attention_kernel, tq=tq)
    return pl.pallas_call(
        kernel,
        grid=(batch, B_HEADS),
        in_specs=[pl.BlockSpec((None, seq, 2 * LANES), lambda b, h: (h, b, 0)),
                  pl.BlockSpec((None, seq, 2 * LANES), lambda b, h: (h, b, 0)),
                  pl.BlockSpec((seq, V_HEAD_DIM), lambda b, h: (b, h))],
        out_specs=pl.BlockSpec((seq, V_HEAD_DIM), lambda b, h: (b, h)),
        out_shape=jax.ShapeDtypeStruct((t, B_WIDTH), BF16),
        compiler_params=_params(2),
        name="mla_attention",
    )(qb, kb, vb)


def _dilated_kernel(q_ref, k_ref, v_ref, bd_ref, g_ref, bias_ref, o_ref, qs, ks, acc_o, acc_l, acc_m,
                    *, unroll):
    seq = q_ref.shape[0]
    n_tiles = seq // BAND_BLOCK
    low = lax.broadcasted_iota(jnp.int32, (BAND_BLOCK, LANES), 1) < A_HEAD_DIM
    bd = bd_ref[...]
    g = g_ref[...]

    def head_norm(z, g_row):
        ss = _group_sum(z * z, bd)
        return z * lax.rsqrt(ss * (1.0 / A_HEAD_DIM) + RMS_EPS) * g_row

    qs[...] = head_norm(q_ref[...], g[0:1]) * (A_HEAD_DIM ** -0.5 * LOG2E)
    ks[...] = head_norm(k_ref[...], g[1:2])
    ones = jnp.ones((2 * BAND_BLOCK, LANES), BF16)

    for p, (window, dil) in enumerate(DILATED_PATTERNS):
        blocks_per_class = n_tiles // dil
        span = BAND_BLOCK * dil

        def rows_at(start, dil=dil):
            if dil == 1:
                return pl.ds(start, BAND_BLOCK)
            return pl.ds(start, BAND_BLOCK, stride=dil)

        def tile(i, carry, p=p, blocks_per_class=blocks_per_class, span=span, rows_at=rows_at):
            r = i // blocks_per_class
            j = i % blocks_per_class
            cur = rows_at(r + j * span)
            prev = rows_at(r + jnp.maximum(j - 1, 0) * span)
            first = jnp.where(j == 0, 1, 0)
            q = qs[cur, :]
            q2 = jnp.concatenate([jnp.where(low, q, 0.0), jnp.where(low, 0.0, q)], axis=0).astype(BF16)
            k_band = jnp.concatenate([ks[prev, :], ks[cur, :]], axis=0).astype(BF16)
            s = _dot_nt(q2, k_band) + bias_ref[p, first]
            m = jnp.max(s, axis=-1, keepdims=True)
            pr = jnp.exp2(s - m).astype(BF16)
            v_band = jnp.concatenate([v_ref[prev, :], v_ref[cur, :]], axis=0).astype(BF16)
            o = _dot(pr, jnp.concatenate([v_band, ones], axis=1))
            top, bot = o[:BAND_BLOCK], o[BAND_BLOCK:]
            acc_o[p, cur, :] = jnp.where(low, top[:, :LANES], bot[:, :LANES])
            acc_l[p, cur, :] = jnp.where(low, top[:, LANES:], bot[:, LANES:])
            acc_m[p, cur, :] = jnp.where(low, m[:BAND_BLOCK], m[BAND_BLOCK:])
            return carry

        lax.fori_loop(0, n_tiles, tile, 0, unroll=unroll)

    m_all = jnp.maximum(jnp.maximum(acc_m[0], acc_m[1]), acc_m[2])
    num = jnp.zeros((seq, LANES), F32)
    den = jnp.zeros((seq, LANES), F32)
    for p in range(len(DILATED_PATTERNS)):
        w = jnp.exp2(acc_m[p] - m_all)
        num = num + w * acc_o[p]
        den = den + w * acc_l[p]
    o_ref[...] = (num / den).astype(o_ref.dtype)


def _dilated_attention(proj, bd, g, bias, batch, seq):
    t = proj.shape[0]
    pairs = A_WIDTH // LANES
    n_pat = len(DILATED_PATTERNS)
    return pl.pallas_call(
        functools.partial(_dilated_kernel, unroll=16),
        grid=(batch, pairs),
        in_specs=[pl.BlockSpec((seq, LANES), lambda b, c: (b, c)),
                  pl.BlockSpec((seq, LANES), lambda b, c: (b, pairs + c)),
                  pl.BlockSpec((seq, LANES), lambda b, c: (b, 2 * pairs + c)),
                  pl.BlockSpec((LANES, LANES), lambda b, c: (0, 0)),
                  pl.BlockSpec((None, 2, LANES), lambda b, c: (c, 0, 0)),
                  pl.BlockSpec((None, n_pat, 2, 2 * BAND_BLOCK, 2 * BAND_BLOCK),
                               lambda b, c: (c, 0, 0, 0, 0))],
        out_specs=pl.BlockSpec((seq, LANES), lambda b, c: (b, c)),
        out_shape=jax.ShapeDtypeStruct((t, A_WIDTH), BF16),
        scratch_shapes=[pltpu.VMEM((seq, LANES), F32)] * 2
                       + [pltpu.VMEM((n_pat, seq, LANES), F32)] * 3,
        compiler_params=_params(2),
        name="dilated_attention",
    )(proj, proj, proj, bd, g, bias)


def _t5_bucket(dist):
    max_exact = NUM_BUCKETS // 2
    df = jnp.maximum(dist, 1).astype(F32)
    log_bucket = max_exact + (jnp.log(df / max_exact) / math.log(MAX_DISTANCE / max_exact)
                              * (NUM_BUCKETS - max_exact)).astype(jnp.int32)
    log_bucket = jnp.minimum(log_bucket, NUM_BUCKETS - 1)
    return jnp.where(dist < max_exact, dist, log_bucket)


def _band_bias(rel_bias):
    n = BAND_BLOCK
    qi = jnp.arange(n)[:, None]
    kj = jnp.arange(n)[None, :]
    buckets = jnp.arange(NUM_BUCKETS)
    tables = []
    for window, dil in DILATED_PATTERNS:
        steps = window // dil
        halves = []
        for back in (qi - kj + n, qi - kj):
            onehot = (_t5_bucket(jnp.maximum(back, 0) * dil)[:, :, None] == buckets).astype(F32)
            vals = jnp.einsum('qkb,bh->hqk', onehot, rel_bias.astype(F32),
                              precision=lax.Precision.HIGHEST) * LOG2E
            halves.append(jnp.where(((back >= 0) & (back <= steps))[None], vals, MASK_VALUE))
        prev, cur = halves
        normal = jnp.concatenate([prev, cur], axis=-1)
        first = jnp.concatenate([jnp.full_like(prev, MASK_VALUE), cur], axis=-1)
        tables.append(jnp.stack([normal, first], axis=1))
    tab = jnp.stack(tables, axis=1)
    tab = tab.reshape(A_HEADS // 2, 2, len(DILATED_PATTERNS), 2, n, 2 * n)
    return tab.transpose(0, 2, 3, 1, 4, 5).reshape(A_HEADS // 2, len(DILATED_PATTERNS), 2, 2 * n, 2 * n)


def _pack_pairs(v):
    half = v.shape[1] // 2
    bits = pltpu.bitcast(v, jnp.uint32)
    return (bits[:, :half] >> 16) | (bits[:, half:] & jnp.uint32(0xFFFF0000))


def _unpack_pairs(p):
    return (pltpu.bitcast(p << 16, F32), pltpu.bitcast(p & jnp.uint32(0xFFFF0000), F32))


def _out_proj_kernel(x_ref, a_ref, b_ref, w_ref, g_ref, rw_ref, rb_ref,
                     x1_ref, hp_ref, route_ref, route_t_ref, blk_ref, carry):
    i = pl.program_id(0)
    tm = x_ref.shape[0]

    @pl.when(i == 0)
    def _():
        carry[...] = jnp.zeros_like(carry)

    w = w_ref[...]
    x1 = x_ref[...] + _dot(a_ref[...], w[:A_WIDTH]) + _dot(b_ref[...], w[A_WIDTH:])
    x1_ref[...] = x1
    ms = jnp.mean(x1 * x1, axis=-1, keepdims=True)
    h = x1 * lax.rsqrt(ms + RMS_EPS) * g_ref[...]
    hi = h.astype(BF16)
    hi_f = hi.astype(F32)
    hp_ref[...] = _pack_pairs(hi_f)
    lo = (h - hi_f).astype(BF16)
    rw = rw_ref[...]
    hw = _dot(hi, rw)
    logits = hw[:, :LANES] + hw[:, LANES:] + _dot(lo, rw[:, :LANES]) + rb_ref[...]

    lane = lax.broadcasted_iota(jnp.int32, (tm, LANES), 1).astype(F32)
    remaining = logits
    vals, hots = [], []
    for _ in range(TOP_K):
        m = jnp.max(remaining, axis=-1, keepdims=True)
        first = jnp.min(jnp.where(remaining == m, lane, float(LANES)), axis=-1, keepdims=True)
        hot = lane == first
        remaining = jnp.where(hot, -jnp.inf, remaining)
        vals.append(m)
        hots.append(hot)
    exps = [jnp.exp(v - vals[0]) for v in vals]
    den = exps[0] + exps[1] + exps[2] + exps[3]

    chosen = jnp.zeros((tm, LANES), F32)
    for hot in hots:
        chosen = chosen + jnp.where(hot, 1.0, 0.0)
    r = lax.broadcasted_iota(jnp.int32, (tm, tm), 0)
    c = lax.broadcasted_iota(jnp.int32, (tm, tm), 1)
    earlier = jnp.where(r > c, 1.0, 0.0).astype(BF16)
    before = carry[...] + _dot(earlier, chosen.astype(BF16))
    in_block = jnp.sum(chosen, axis=0, keepdims=True)
    carry[...] = carry[...] + in_block
    blk_ref[...] = jnp.broadcast_to(in_block, blk_ref.shape)

    route = jnp.zeros((tm, LANES), F32)
    for k in range(TOP_K):
        first = jnp.sum(jnp.where(hots[k], lane, 0.0), axis=-1, keepdims=True)
        rank = jnp.sum(jnp.where(hots[k], before, 0.0), axis=-1, keepdims=True)
        route = route + jnp.where(lane == float(k), first, 0.0)
        route = route + jnp.where(lane == float(TOP_K + k), rank, 0.0)
        route = route + jnp.where(lane == float(2 * TOP_K + k), exps[k] / den, 0.0)
    route_ref[...] = route
    route_t_ref[...] = route.T[:route_t_ref.shape[0], :]


def _out_proj(x2, out_a, out_b, w, g, rw, rb, tm):
    t, d = x2.shape
    return pl.pallas_call(
        _out_proj_kernel,
        grid=(t // tm,),
        in_specs=[pl.BlockSpec((tm, d), lambda i: (i, 0)),
                  pl.BlockSpec((tm, A_WIDTH), lambda i: (i, 0)),
                  pl.BlockSpec((tm, B_WIDTH), lambda i: (i, 0)),
                  pl.BlockSpec(w.shape, lambda i: (0, 0)),
                  pl.BlockSpec((1, d), lambda i: (0, 0)),
                  pl.BlockSpec(rw.shape, lambda i: (0, 0)),
                  pl.BlockSpec((1, LANES), lambda i: (0, 0))],
        out_specs=[pl.BlockSpec((tm, d), lambda i: (i, 0)),
                   pl.BlockSpec((tm, d // 2), lambda i: (i, 0)),
                   pl.BlockSpec((tm, LANES), lambda i: (i, 0)),
                   pl.BlockSpec((2 * TOP_K, tm), lambda i: (0, i)),
                   pl.BlockSpec((None, 8, LANES), lambda i: (i, 0, 0))],
        out_shape=[jax.ShapeDtypeStruct((t, d), F32),
                   jax.ShapeDtypeStruct((t, d // 2), jnp.uint32),
                   jax.ShapeDtypeStruct((t, LANES), F32),
                   jax.ShapeDtypeStruct((2 * TOP_K, t), F32),
                   jax.ShapeDtypeStruct((t // tm, 8, LANES), F32)],
        scratch_shapes=[pltpu.VMEM((1, LANES), F32)],
        compiler_params=_params(),
        name="out_proj",
    )(x2, out_a, out_b, w, g, rw, rb)


def _dispatch_kernel(pad_ref, end_ref, nb_ref, dest_hbm, h_ref, xs_hbm, dest_s0, dest_s1, zeros,
                     sem_i, sem_z, sem_o, *, bm):
    i = pl.program_id(0)
    steps = pl.num_programs(0)
    tm = h_ref.shape[0]
    n_blocks = xs_hbm.shape[0] // bm
    dest_s = (dest_s0, dest_s1)

    def idx_copy(step, slot):
        return pltpu.make_async_copy(dest_hbm.at[step], dest_s[slot], sem_i.at[slot])

    @pl.when(i == 0)
    def _():
        idx_copy(0, 0).start()
        zeros[...] = jnp.zeros_like(zeros)

        def zero_block(start):
            return pltpu.make_async_copy(zeros, xs_hbm.at[pl.ds(pl.multiple_of(start, bm), bm), :], sem_z)

        for e in range(N_EXPERTS):
            @pl.when(pad_ref[e] > 0)
            def _():
                zero_block(end_ref[e] - bm).start()

        def start_unused(b, carry):
            zero_block(b * bm).start()
            return carry

        def wait_unused(b, carry):
            zero_block(b * bm).wait()
            return carry

        lax.fori_loop(nb_ref[0], n_blocks, start_unused, 0)
        for e in range(N_EXPERTS):
            @pl.when(pad_ref[e] > 0)
            def _():
                zero_block(end_ref[e] - bm).wait()
        lax.fori_loop(nb_ref[0], n_blocks, wait_unused, 0)

    for slot in range(2):
        @pl.when(i % 2 == slot)
        def _(slot=slot):
            @pl.when(i + 1 < steps)
            def _():
                idx_copy(i + 1, 1 - slot).start()

            idx_copy(i, slot).wait()

            def body(t, carry):
                for k in range(TOP_K):
                    pltpu.make_async_copy(h_ref.at[pl.ds(t, 1), :],
                                          xs_hbm.at[pl.ds(dest_s[slot][k * tm + t], 1), :],
                                          sem_o).start(priority=k % 2)
                return carry

            lax.fori_loop(0, tm, body, 0, unroll=8)
    for _ in range(TOP_K):
        pltpu.make_async_copy(h_ref, xs_hbm.at[pl.ds(0, tm), :], sem_o).wait()


def _dispatch(padded, pad_end, n_used, dest, hp, n_slots, tm, bm):
    t, half = hp.shape
    grid_spec = pltpu.PrefetchScalarGridSpec(
        num_scalar_prefetch=3,
        grid=(t // tm,),
        in_specs=[pl.BlockSpec(memory_space=pl.ANY),
                  pl.BlockSpec((tm, half), lambda i, p, e, n: (i, 0))],
        out_specs=pl.BlockSpec(memory_space=pl.ANY),
        scratch_shapes=[pltpu.SMEM((tm * TOP_K,), jnp.int32),
                        pltpu.SMEM((tm * TOP_K,), jnp.int32),
                        pltpu.VMEM((bm, half), jnp.uint32),
                        pltpu.SemaphoreType.DMA((2,)),
                        pltpu.SemaphoreType.DMA(()),
                        pltpu.SemaphoreType.DMA(())],
    )
    return pl.pallas_call(
        functools.partial(_dispatch_kernel, bm=bm),
        grid_spec=grid_spec,
        out_shape=jax.ShapeDtypeStruct((n_slots, half), jnp.uint32),
        compiler_params=_params(),
        name="dispatch",
    )(padded, pad_end, n_used, dest, hp)


EXPERT_SUB_ROWS = 256


def _expert_ffn_kernel(be_ref, nb_ref, next_ref, x_ref, wgu_hbm, bgu_ref, wd_hbm, bd_ref, y_ref,
                       wgu_f, wd_f, wgu_s, wd_s, sem_w):
    i = pl.program_id(0)
    d_ff = wd_f.shape[0]
    half = x_ref.shape[1]

    def weight_copies(e):
        return (pltpu.make_async_copy(wgu_hbm.at[e], wgu_f, sem_w.at[0]),
                pltpu.make_async_copy(wd_hbm.at[e], wd_f, sem_w.at[1]))

    @pl.when(jnp.logical_and(i == 0, nb_ref[0] > 0))
    def _():
        for copy in weight_copies(be_ref[0]):
            copy.start()

    @pl.when(i >= nb_ref[0])
    def _():
        y_ref[...] = jnp.zeros_like(y_ref)

    @pl.when(i < nb_ref[0])
    def _():
        changed = jnp.logical_or(i == 0, be_ref[i] != be_ref[jnp.maximum(i - 1, 0)])

        @pl.when(changed)
        def _():
            for copy in weight_copies(be_ref[i]):
                copy.wait()
            wgu_s[...] = wgu_f[...].astype(BF16)
            wd_s[...] = wd_f[...].astype(BF16)

            @pl.when(next_ref[i] >= 0)
            def _():
                for copy in weight_copies(next_ref[i]):
                    copy.start()

        for r0 in range(0, x_ref.shape[0], EXPERT_SUB_ROWS):
            rows = slice(r0, r0 + EXPERT_SUB_ROWS)
            x_lo, x_hi = _unpack_pairs(x_ref[rows, :])
            gu = (_dot(x_lo.astype(BF16), wgu_s[:half, :]) + _dot(x_hi.astype(BF16), wgu_s[half:, :])
                  + bgu_ref[...])
            gate = jnp.minimum(gu[:, :d_ff], SWIGLU_LIMIT)
            up = jnp.clip(gu[:, d_ff:], -SWIGLU_LIMIT, SWIGLU_LIMIT)
            glu = gate * jax.nn.sigmoid(SWIGLU_ALPHA * gate)
            act = ((up + 1.0) * glu).astype(BF16)
            y = _dot(act, wd_s[...]) + bd_ref[...]
            y_ref[rows, :] = _pack_pairs(y.astype(BF16).astype(F32))


def _expert_ffn(block_expert, n_used, next_expert, xs, w_gate_up, b_gate_up, w_down, b_down, bm):
    n_slots, half = xs.shape
    n_e, d, two_ff = w_gate_up.shape
    d_ff = two_ff // 2
    n_blocks = n_slots // bm

    def used(i, nb):
        return jnp.minimum(i, jnp.maximum(nb[0] - 1, 0))

    grid_spec = pltpu.PrefetchScalarGridSpec(
        num_scalar_prefetch=3,
        grid=(n_blocks,),
        in_specs=[pl.BlockSpec((bm, half), lambda i, be, nb, nx: (used(i, nb), 0)),
                  pl.BlockSpec(memory_space=pl.ANY),
                  pl.BlockSpec((None, 1, two_ff), lambda i, be, nb, nx: (be[i], 0, 0)),
                  pl.BlockSpec(memory_space=pl.ANY),
                  pl.BlockSpec((None, 1, d), lambda i, be, nb, nx: (be[i], 0, 0))],
        out_specs=pl.BlockSpec((bm, half), lambda i, be, nb, nx: (i, 0)),
        scratch_shapes=[pltpu.VMEM((d, two_ff), F32), pltpu.VMEM((d_ff, d), F32),
                        pltpu.VMEM((d, two_ff), BF16), pltpu.VMEM((d_ff, d), BF16),
                        pltpu.SemaphoreType.DMA((2,))],
    )
    return pl.pallas_call(
        _expert_ffn_kernel,
        grid_spec=grid_spec,
        out_shape=jax.ShapeDtypeStruct((n_slots, half), jnp.uint32),
        compiler_params=_params(),
        name="expert_ffn",
    )(block_expert, n_used, next_expert, xs, w_gate_up, b_gate_up.reshape(n_e, 1, two_ff),
      w_down, b_down.reshape(n_e, 1, d))


COMBINE_LOCAL_ROWS = 2560
SEG_ALIGN = 8


def _pieces(limit):
    sizes = []
    size = 1 << (limit.bit_length() - 1)
    while size >= SEG_ALIGN:
        sizes.append(size)
        size //= 2
    return sizes


def _combine_kernel(src_ref, len_ref, dst_ref, tot_ref, route_ref, local_ref, x1_ref, ys_hbm, o_ref, ybuf,
                    sem_g):
    j = pl.program_id(0)
    blocks = pl.num_programs(0) - 1
    tm = x1_ref.shape[0]
    half = x1_ref.shape[1] // 2
    local_rows = ybuf.shape[1]

    @pl.when(j == 0)
    def _():
        ybuf[...] = jnp.zeros_like(ybuf)

    @pl.when(j < blocks)
    def _():
        for s in range(2):
            @pl.when(j % 2 == s)
            def _(s=s):
                for e in range(N_EXPERTS):
                    seg = j * N_EXPERTS + e
                    n = len_ref[seg]
                    src = src_ref[seg]
                    dst = dst_ref[seg]
                    for size in _pieces(tm + 2 * SEG_ALIGN):
                        @pl.when((n & size) != 0)
                        def _(size=size, src=src, dst=dst):
                            pltpu.make_async_copy(
                                ys_hbm.at[pl.ds(pl.multiple_of(src, SEG_ALIGN), size), :],
                                ybuf.at[s, pl.ds(pl.multiple_of(dst, SEG_ALIGN), size), :],
                                sem_g.at[s]).start()
                        step = jnp.where((n & size) != 0, size, 0)
                        src = src + step
                        dst = dst + step

    @pl.when(j >= 1)
    def _():
        slot = (j - 1) % 2
        total = tot_ref[j - 1]
        for size in _pieces(local_rows):
            @pl.when((total & size) != 0)
            def _(size=size):
                pltpu.make_async_copy(ys_hbm.at[pl.ds(0, size), :], ybuf.at[slot, pl.ds(0, size), :],
                                      sem_g.at[slot]).wait()

        route = route_ref[...]
        local_t = local_ref[...]
        local = jnp.concatenate(
            [local_t, jnp.zeros((LANES - local_t.shape[0], tm), F32)], axis=0).T
        col = lax.broadcasted_iota(jnp.int32, (tm, local_rows), 1).astype(F32)
        g = jnp.zeros((tm, local_rows), F32)
        for k in range(TOP_K):
            pos = local[:, k: k + 1]
            gate = route[:, 2 * TOP_K + k: 2 * TOP_K + k + 1]
            g = jnp.where(col == pos, gate, g)
        g = g.astype(BF16)
        lo, hi = _unpack_pairs(ybuf[slot])
        o_ref[:, :half] = x1_ref[:, :half] + _dot(g, lo.astype(BF16))
        o_ref[:, half:] = x1_ref[:, half:] + _dot(g, hi.astype(BF16))


def _combine(tables, route, local_t, x1, ys, tm, local_rows):
    t, d = x1.shape
    half = d // 2

    def summed(j, *_):
        return (jnp.maximum(j - 1, 0), 0)

    grid_spec = pltpu.PrefetchScalarGridSpec(
        num_scalar_prefetch=4,
        grid=(t // tm + 1,),
        in_specs=[pl.BlockSpec((tm, LANES), summed),
                  pl.BlockSpec((local_t.shape[0], tm), lambda j, *_: (0, jnp.maximum(j - 1, 0))),
                  pl.BlockSpec((tm, d), summed),
                  pl.BlockSpec(memory_space=pl.ANY)],
        out_specs=pl.BlockSpec((tm, d), summed),
        scratch_shapes=[pltpu.VMEM((2, local_rows, half), jnp.uint32),
                        pltpu.SemaphoreType.DMA((2,))],
    )
    return pl.pallas_call(
        _combine_kernel,
        grid_spec=grid_spec,
        out_shape=jax.ShapeDtypeStruct((t, d), F32),
        compiler_params=_params(),
        name="combine",
    )(*tables, route, local_t, x1, ys)


def _per_expert(table, idx):
    out = jnp.zeros(idx.shape, table.dtype)
    for e in range(N_EXPERTS):
        out = jnp.where(idx == e, table[:, e][None, :, None], out)
    return out


def _combine_tables(idx, rank, n, pad_start, tm, local_rows):
    before = jnp.cumsum(n, axis=0) - n
    first = pad_start[None, :] + before
    src = first // SEG_ALIGN * SEG_ALIGN
    length = jnp.where(n > 0, (first + n + SEG_ALIGN - 1) // SEG_ALIGN * SEG_ALIGN - src, 0)
    dst = jnp.cumsum(length, axis=1) - length
    total = jnp.sum(length, axis=1)
    assert local_rows >= tm * TOP_K + N_EXPERTS * 2 * (SEG_ALIGN - 1)
    local = (_per_expert(dst + first - src - before, idx) + rank).reshape(TOP_K, -1).astype(F32)
    tables = (src.reshape(-1).astype(jnp.int32), length.reshape(-1).astype(jnp.int32),
              dst.reshape(-1).astype(jnp.int32), total.astype(jnp.int32))
    return tables, local


def _slot_layout(route_t, n, tm, bm):
    t = route_t.shape[1]
    blocks = t // tm
    idx = route_t[:TOP_K].astype(jnp.int32).reshape(TOP_K, blocks, tm)
    rank = route_t[TOP_K:].astype(jnp.int32).reshape(TOP_K, blocks, tm)
    counts = jnp.sum(n, axis=0)
    padded = (counts + bm - 1) // bm * bm
    pad_end = jnp.cumsum(padded).astype(jnp.int32)
    pad_start = pad_end - padded
    dest = _per_expert(jnp.broadcast_to(pad_start[None, :], n.shape), idx) + rank
    dest = dest.transpose(1, 0, 2).reshape(blocks, TOP_K * tm)
    n_blocks = -(-t * TOP_K // bm) + N_EXPERTS
    block_start = jnp.arange(n_blocks, dtype=jnp.int32) * bm
    block_expert = jnp.minimum(jnp.sum(pad_end[None, :] <= block_start[:, None], axis=1),
                               N_EXPERTS - 1).astype(jnp.int32)
    n_used = (pad_end[-1:] // bm).astype(jnp.int32)
    following = jnp.take(pad_end, block_expert) // bm
    next_expert = jnp.where(following < n_used[0],
                            jnp.take(block_expert, jnp.minimum(following, n_blocks - 1)), -1).astype(jnp.int32)
    return idx, rank, padded, pad_end, dest, block_expert, next_expert, n_used, n_blocks * bm


def _rot_cols(w):
    half = w.shape[-1] // 2
    return jnp.concatenate([-w[..., half:], w[..., :half]], axis=-1)


def _swap_halves(g):
    half = g.shape[-1] // 2
    return jnp.concatenate([g[..., half:], g[..., :half]], axis=-1)


def kernel(x, attn_norm_g, w_in, a_q_norm_g, a_k_norm_g, rel_bias, q_a_norm_g, w_q_b, kv_a_norm_g,
           w_kv_b, b_q_norm_g, b_k_norm_g, w_out, ffn_norm_g, router_w, router_b, w_gate_up,
           b_gate_up, w_down, b_down):
    batch, seq, d = x.shape
    t = batch * seq
    depth = w_in.shape[0]
    tm = 512
    expert_bm = 512

    pos = jnp.arange(seq, dtype=F32)
    inv_freq = ROPE_THETA ** (-jnp.arange(0, QK_ROPE_DIM, 2, dtype=F32) / QK_ROPE_DIM)
    ang = pos[:, None] * inv_freq[None, :]
    cos, sin = jnp.cos(ang), jnp.sin(ang)
    cs = jnp.concatenate([cos, cos, sin, sin], axis=-1)

    row = jnp.arange(LANES)[:, None] // A_HEAD_DIM
    col = jnp.arange(LANES)[None, :] // A_HEAD_DIM
    head_block_ones = (row == col).astype(BF16)
    band_bias = _band_bias(rel_bias)

    x2 = x.reshape(t, d)
    for layer in range(depth):
        kpe_off = 3 * A_WIDTH + Q_LORA_RANK + KV_LORA_RANK
        w_kpe = w_in[layer][:, kpe_off:]
        w_in_r = jnp.concatenate([w_in[layer][:, :kpe_off], w_kpe, _rot_cols(w_kpe)], axis=1).astype(BF16)

        wq = w_q_b[layer].reshape(Q_LORA_RANK, B_HEADS, QK_HEAD_DIM)
        wq_rope = wq[..., QK_NOPE_DIM:]
        wq_r = jnp.concatenate([wq, _rot_cols(wq_rope)], axis=-1).reshape(Q_LORA_RANK, -1).astype(BF16)
        wkv = w_kv_b[layer].reshape(KV_LORA_RANK, B_HEADS, QK_NOPE_DIM + V_HEAD_DIM)
        wkv_r = jnp.concatenate([wkv[..., :QK_NOPE_DIM].reshape(KV_LORA_RANK, -1),
                                 wkv[..., QK_NOPE_DIM:].reshape(KV_LORA_RANK, -1)], axis=1).astype(BF16)
        gc = jnp.concatenate([q_a_norm_g[layer], kv_a_norm_g[layer]])[None, :]
        gq, gk = b_q_norm_g[layer], b_k_norm_g[layer]

        def rope_gain(gr):
            return jnp.concatenate([gr, _swap_halves(gr)])

        gqk = jnp.stack([gq[:QK_NOPE_DIM], rope_gain(gq[QK_NOPE_DIM:]),
                         gk[:QK_NOPE_DIM], rope_gain(gk[QK_NOPE_DIM:])])
        g_a = jnp.stack([jnp.tile(a_q_norm_g[layer], 2), jnp.tile(a_k_norm_g[layer], 2)])
        g_a = jnp.broadcast_to(g_a[None], (A_WIDTH // LANES, 2, LANES))

        proj, qb, kb, vb = _in_proj(x2, attn_norm_g[layer][None, :], w_in_r, cs, wq_r, wkv_r, gc, gqk, seq, tm)
        out_b = _mla_attention(qb, kb, vb, batch, seq, 512)
        out_a = _dilated_attention(proj, head_block_ones, g_a, band_bias, batch, seq)

        rw = jnp.pad(router_w[layer], ((0, 0), (0, LANES - N_EXPERTS)))
        rw_hi = rw.astype(BF16)
        rw_lo = (rw - rw_hi.astype(F32)).astype(BF16)
        rb = jnp.pad(router_b[layer], (0, LANES - N_EXPERTS), constant_values=-jnp.inf)[None, :]
        x1, hp, route, route_t, in_block = _out_proj(
            x2, out_a, out_b, w_out[layer].astype(BF16), ffn_norm_g[layer][None, :],
            jnp.concatenate([rw_hi, rw_lo], axis=1), rb, tm)

        n = in_block[:, 0, :N_EXPERTS].astype(jnp.int32)
        idx, rank, padded, pad_end, dest, block_expert, next_expert, n_used, n_slots = _slot_layout(
            route_t, n, tm, expert_bm)
        xs = _dispatch(padded, pad_end, n_used, dest, hp, n_slots, tm, expert_bm)
        ys = _expert_ffn(block_expert, n_used, next_expert, xs, w_gate_up[layer], b_gate_up[layer],
                         w_down[layer], b_down[layer], expert_bm)
        tables, local_t = _combine_tables(idx, rank, n, pad_end - padded, tm, COMBINE_LOCAL_ROWS)
        x2 = _combine(tables, route, local_t, x1, ys, tm, COMBINE_LOCAL_ROWS)
    return x2.reshape(batch, seq, d)
```

```python
import functools
import math

import jax
import jax.numpy as jnp
from jax import lax
from jax.experimental import pallas as pl
from jax.experimental.pallas import tpu as pltpu

A_HEADS = 8
A_HEAD_DIM = 64
A_WIDTH = A_HEADS * A_HEAD_DIM
DILATED_PATTERNS = ((128, 1), (512, 4), (2048, 16))
BAND_BLOCK = 128

B_HEADS = 4
QK_NOPE_DIM = 128
QK_ROPE_DIM = 64
QK_HEAD_DIM = QK_NOPE_DIM + QK_ROPE_DIM
V_HEAD_DIM = 128
Q_LORA_RANK = 256
KV_LORA_RANK = 256
B_WIDTH = B_HEADS * V_HEAD_DIM
ROPE_THETA = 10000.0

NUM_BUCKETS = 32
MAX_DISTANCE = 2048

N_EXPERTS = 32
TOP_K = 4
SWIGLU_LIMIT = 7.0
SWIGLU_ALPHA = 1.702
RMS_EPS = 1e-6

LANES = 128
MASK_VALUE = -1e30
LOG2E = math.log2(math.e)
PROJ_WIDTH = 3 * A_WIDTH + Q_LORA_RANK + KV_LORA_RANK + 2 * QK_ROPE_DIM
VMEM_LIMIT = 56 * 1024 * 1024

F32 = jnp.float32
BF16 = jnp.bfloat16


def _dot(a, b):
    return jnp.dot(a, b, preferred_element_type=F32)


def _dot_nt(a, b):
    return lax.dot_general(a, b, (((1,), (1,)), ((), ())), preferred_element_type=F32)


def _group_sum(x, m):
    return _dot(x.astype(BF16), m)


def _params(n_parallel=1):
    return pltpu.CompilerParams(
        dimension_semantics=("arbitrary",) * n_parallel, vmem_limit_bytes=VMEM_LIMIT)


IN_PROJ_SUB_ROWS = 256


def _mla_qkv(c, kper, cs, wq_ref, wkv_ref, gc, gqk, q_ref, k_ref, v_ref, rows, q_scale):
    def lora_norm(z, g):
        ms = jnp.mean(z * z, axis=-1, keepdims=True)
        return (z * lax.rsqrt(ms + RMS_EPS) * g).astype(BF16)

    cq = lora_norm(c[:, :Q_LORA_RANK], gc[:, :Q_LORA_RANK])
    ckv = lora_norm(c[:, Q_LORA_RANK:], gc[:, Q_LORA_RANK:])
    qb = _dot(cq, wq_ref[...])
    kvb = _dot(ckv, wkv_ref[...])
    n = c.shape[0]

    row = lax.broadcasted_iota(jnp.int32, (LANES, LANES), 0)
    ones_all = jnp.ones((LANES, LANES), BF16)
    ones_lo = jnp.where(row < QK_ROPE_DIM, 1.0, 0.0).astype(BF16)
    lane = lax.broadcasted_iota(jnp.int32, (n, LANES), 1)

    def rope(z, g_row):
        t = z * (g_row * cs)
        return jnp.where(lane < QK_ROPE_DIM, t + pltpu.roll(t, QK_ROPE_DIM, 1), 0.0)

    k_rope = rope(kper, gqk[3:4])
    kpe_ss = _group_sum(kper * kper, ones_lo)
    for h in range(B_HEADS):
        qn = qb[:, 2 * LANES * h: 2 * LANES * h + LANES]
        qr = qb[:, 2 * LANES * h + LANES: 2 * LANES * (h + 1)]
        ss = _group_sum(qn * qn, ones_all) + _group_sum(qr * qr, ones_lo)
        rs = lax.rsqrt(ss * (1.0 / QK_HEAD_DIM) + RMS_EPS) * q_scale
        q_ref[h, rows, :LANES] = (qn * gqk[0:1] * rs).astype(BF16)
        q_ref[h, rows, LANES:] = (rope(qr, gqk[1:2]) * rs).astype(BF16)
        kn = kvb[:, LANES * h: LANES * (h + 1)]
        ssk = _group_sum(kn * kn, ones_all) + kpe_ss
        rsk = lax.rsqrt(ssk * (1.0 / QK_HEAD_DIM) + RMS_EPS)
        k_ref[h, rows, :LANES] = (kn * gqk[2:3] * rsk).astype(BF16)
        k_ref[h, rows, LANES:] = (k_rope * rsk).astype(BF16)
    v_ref[rows, :] = kvb[:, B_HEADS * QK_NOPE_DIM:].astype(BF16)


def _in_proj_kernel(x_ref, g_ref, w_ref, cs_ref, wq_ref, wkv_ref, gc_ref, gqk_ref,
                    a_ref, q_ref, k_ref, v_ref, *, q_scale):
    a_width = a_ref.shape[1]
    c_width = Q_LORA_RANK + KV_LORA_RANK
    for r0 in range(0, x_ref.shape[0], IN_PROJ_SUB_ROWS):
        rows = slice(r0, r0 + IN_PROJ_SUB_ROWS)
        x = x_ref[rows, :]
        ms = jnp.mean(x * x, axis=-1, keepdims=True)
        h = (x * lax.rsqrt(ms + RMS_EPS) * g_ref[...]).astype(BF16)
        proj = _dot(h, w_ref[...])
        a_ref[rows, :] = proj[:, :a_width]
        _mla_qkv(proj[:, a_width:a_width + c_width], proj[:, a_width + c_width:], cs_ref[rows, :],
                 wq_ref, wkv_ref, gc_ref[...], gqk_ref[...], q_ref, k_ref, v_ref, rows, q_scale)


def _in_proj(x2, g, w, cs, wq, wkv, gc, gqk, seq, tm):
    t, d = x2.shape
    a_width = 3 * A_WIDTH
    pos_blocks = seq // tm
    kernel = functools.partial(_in_proj_kernel, q_scale=QK_HEAD_DIM ** -0.5 * LOG2E)
    return pl.pallas_call(
        kernel,
        grid=(t // tm,),
        in_specs=[pl.BlockSpec((tm, d), lambda i: (i, 0)),
                  pl.BlockSpec((1, d), lambda i: (0, 0)),
                  pl.BlockSpec(w.shape, lambda i: (0, 0)),
                  pl.BlockSpec((tm, LANES), lambda i: (i % pos_blocks, 0)),
                  pl.BlockSpec(wq.shape, lambda i: (0, 0)),
                  pl.BlockSpec(wkv.shape, lambda i: (0, 0)),
                  pl.BlockSpec(gc.shape, lambda i: (0, 0)),
                  pl.BlockSpec(gqk.shape, lambda i: (0, 0))],
        out_specs=[pl.BlockSpec((tm, a_width), lambda i: (i, 0)),
                   pl.BlockSpec((B_HEADS, tm, 2 * LANES), lambda i: (0, i, 0)),
                   pl.BlockSpec((B_HEADS, tm, 2 * LANES), lambda i: (0, i, 0)),
                   pl.BlockSpec((tm, B_WIDTH), lambda i: (i, 0))],
        out_shape=[jax.ShapeDtypeStruct((t, a_width), F32),
                   jax.ShapeDtypeStruct((B_HEADS, t, 2 * LANES), BF16),
                   jax.ShapeDtypeStruct((B_HEADS, t, 2 * LANES), BF16),
                   jax.ShapeDtypeStruct((t, B_WIDTH), BF16)],
        compiler_params=_params(),
        name="in_proj",
    )(x2, g, w, cs, wq, wkv, gc, gqk)


def _mla_attention_kernel(q_ref, k_ref, v_ref, o_ref, *, tq):
    seq = q_ref.shape[0]
    row = lax.broadcasted_iota(jnp.int32, (tq, tq), 0)
    col = lax.broadcasted_iota(jnp.int32, (tq, tq), 1)
    diag_mask = jnp.where(col <= row, 0.0, MASK_VALUE)
    v_ext = jnp.concatenate([v_ref[...], jnp.ones(v_ref.shape, v_ref.dtype)], axis=1)
    dv = v_ref.shape[1]
    for i in range(seq // tq):
        q = q_ref[i * tq:(i + 1) * tq, :]
        s_diag = _dot_nt(q, k_ref[i * tq:(i + 1) * tq, :]) + diag_mask
        m = jnp.max(s_diag, axis=-1, keepdims=True)
        if i > 0:
            s_past = _dot_nt(q, k_ref[:i * tq, :])
            m = jnp.maximum(m, jnp.max(s_past, axis=-1, keepdims=True))
        o = _dot(jnp.exp2(s_diag - m).astype(BF16), v_ext[i * tq:(i + 1) * tq, :])
        if i > 0:
            o = o + _dot(jnp.exp2(s_past - m).astype(BF16), v_ext[:i * tq, :])
        o_ref[i * tq:(i + 1) * tq, :] = (o[:, :dv] / o[:, dv:]).astype(o_ref.dtype)


def _mla_attention(qb, kb, vb, batch, seq, tq):
    t = vb.shape[0]
    kernel = functools.partial(_mla_attention_kernel, tq=tq)
    return pl.pallas_call(
        kernel,
        grid=(batch, B_HEADS),
        in_specs=[pl.BlockSpec((None, seq, 2 * LANES), lambda b, h: (h, b, 0)),
                  pl.BlockSpec((None, seq, 2 * LANES), lambda b, h: (h, b, 0)),
                  pl.BlockSpec((seq, V_HEAD_DIM), lambda b, h: (b, h))],
        out_specs=pl.BlockSpec((seq, V_HEAD_DIM), lambda b, h: (b, h)),
        out_shape=jax.ShapeDtypeStruct((t, B_WIDTH), BF16),
        compiler_params=_params(2),
        name="mla_attention",
    )(qb, kb, vb)


def _dilated_kernel(q_ref, k_ref, v_ref, bd_ref, g_ref, bias_ref, o_ref, qs, ks, acc_o, acc_l, acc_m,
                    *, unroll):
    seq = q_ref.shape[0]
    n_tiles = seq // BAND_BLOCK
    low = lax.broadcasted_iota(jnp.int32, (BAND_BLOCK, LANES), 1) < A_HEAD_DIM
    bd = bd_ref[...]
    g = g_ref[...]

    def head_norm(z, g_row):
        ss = _group_sum(z * z, bd)
        return z * lax.rsqrt(ss * (1.0 / A_HEAD_DIM) + RMS_EPS) * g_row

    qs[...] = head_norm(q_ref[...], g[0:1]) * (A_HEAD_DIM ** -0.5 * LOG2E)
    ks[...] = head_norm(k_ref[...], g[1:2])
    ones = jnp.ones((2 * BAND_BLOCK, LANES), BF16)

    for p, (window, dil) in enumerate(DILATED_PATTERNS):
        blocks_per_class = n_tiles // dil
        span = BAND_BLOCK * dil

        def rows_at(start, dil=dil):
            if dil == 1:
                return pl.ds(start, BAND_BLOCK)
            return pl.ds(start, BAND_BLOCK, stride=dil)

        def tile(i, carry, p=p, blocks_per_class=blocks_per_class, span=span, rows_at=rows_at):
            r = i // blocks_per_class
            j = i % blocks_per_class
            cur = rows_at(r + j * span)
            prev = rows_at(r + jnp.maximum(j - 1, 0) * span)
            first = jnp.where(j == 0, 1, 0)
            q = qs[cur, :]
            q2 = jnp.concatenate([jnp.where(low, q, 0.0), jnp.where(low, 0.0, q)], axis=0).astype(BF16)
            k_band = jnp.concatenate([ks[prev, :], ks[cur, :]], axis=0).astype(BF16)
            s = _dot_nt(q2, k_band) + bias_ref[p, first]
            m = jnp.max(s, axis=-1, keepdims=True)
            pr = jnp.exp2(s - m).astype(BF16)
            v_band = jnp.concatenate([v_ref[prev, :], v_ref[cur, :]], axis=0).astype(BF16)
            o = _dot(pr, jnp.concatenate([v_band, ones], axis=1))
            top, bot = o[:BAND_BLOCK], o[BAND_BLOCK:]
            acc_o[p, cur, :] = jnp.where(low, top[:, :LANES], bot[:, :LANES])
            acc_l[p, cur, :] = jnp.where(low, top[:, LANES:], bot[:, LANES:])
            acc_m[p, cur, :] = jnp.where(low, m[:BAND_BLOCK], m[BAND_BLOCK:])
            return carry

        lax.fori_loop(0, n_tiles, tile, 0, unroll=unroll)

    m_all = jnp.maximum(jnp.maximum(acc_m[0], acc_m[1]), acc_m[2])
    num = jnp.zeros((seq, LANES), F32)
    den = jnp.zeros((seq, LANES), F32)
    for p in range(len(DILATED_PATTERNS)):
        w = jnp.exp2(acc_m[p] - m_all)
        num = num + w * acc_o[p]
        den = den + w * acc_l[p]
    o_ref[...] = (num / den).astype(o_ref.dtype)


def _dilated_attention(proj, bd, g, bias, batch, seq):
    t = proj.shape[0]
    pairs = A_WIDTH // LANES
    n_pat = len(DILATED_PATTERNS)
    return pl.pallas_call(
        functools.partial(_dilated_kernel, unroll=16),
        grid=(batch, pairs),
        in_specs=[pl.BlockSpec((seq, LANES), lambda b, c: (b, c)),
                  pl.BlockSpec((seq, LANES), lambda b, c: (b, pairs + c)),
                  pl.BlockSpec((seq, LANES), lambda b, c: (b, 2 * pairs + c)),
                  pl.BlockSpec((LANES, LANES), lambda b, c: (0, 0)),
                  pl.BlockSpec((None, 2, LANES), lambda b, c: (c, 0, 0)),
                  pl.BlockSpec((None, n_pat, 2, 2 * BAND_BLOCK, 2 * BAND_BLOCK),
                               lambda b, c: (c, 0, 0, 0, 0))],
        out_specs=pl.BlockSpec((seq, LANES), lambda b, c: (b, c)),
        out_shape=jax.ShapeDtypeStruct((t, A_WIDTH), BF16),
        scratch_shapes=[pltpu.VMEM((seq, LANES), F32)] * 2
                       + [pltpu.VMEM((n_pat, seq, LANES), F32)] * 3,
        compiler_params=_params(2),
        name="dilated_attention",
    )(proj, proj, proj, bd, g, bias)


def _attention_kernel(qa_ref, ka_ref, va_ref, bd_ref, g_ref, bias_ref, qb_ref, kb_ref, vb_ref,
                      oa_ref, ob_ref, qs, ks, acc_o, acc_l, acc_m, *, unroll, tq):
    _mla_attention_kernel(qb_ref, kb_ref, vb_ref, ob_ref, tq=tq)
    _dilated_kernel(qa_ref, ka_ref, va_ref, bd_ref, g_ref, bias_ref, oa_ref, qs, ks, acc_o, acc_l, acc_m,
                    unroll=unroll)


def _attention(proj, bd, g, bias, qb, kb, vb, batch, seq, tq):
    t = proj.shape[0]
    pairs = A_WIDTH // LANES
    assert pairs == B_HEADS
    n_pat = len(DILATED_PATTERNS)
    return pl.pallas_call(
        functools.partial(_attention_kernel, unroll=16, tq=tq),
        grid=(batch, pairs),
        in_specs=[pl.BlockSpec((seq, LANES), lambda b, c: (b, c)),
                  pl.BlockSpec((seq, LANES), lambda b, c: (b, pairs + c)),
                  pl.BlockSpec((seq, LANES), lambda b, c: (b, 2 * pairs + c)),
                  pl.BlockSpec((LANES, LANES), lambda b, c: (0, 0)),
                  pl.BlockSpec((None, 2, LANES), lambda b, c: (c, 0, 0)),
                  pl.BlockSpec((None, n_pat, 2, 2 * BAND_BLOCK, 2 * BAND_BLOCK),
                               lambda b, c: (c, 0, 0, 0, 0)),
                  pl.BlockSpec((None, seq, 2 * LANES), lambda b, h: (h, b, 0)),
                  pl.BlockSpec((None, seq, 2 * LANES), lambda b, h: (h, b, 0)),
                  pl.BlockSpec((seq, V_HEAD_DIM), lambda b, h: (b, h))],
        out_specs=[pl.BlockSpec((seq, LANES), lambda b, c: (b, c)),
                   pl.BlockSpec((seq, V_HEAD_DIM), lambda b, h: (b, h))],
        out_shape=[jax.ShapeDtypeStruct((t, A_WIDTH), BF16),
                   jax.ShapeDtypeStruct((t, B_WIDTH), BF16)],
        scratch_shapes=[pltpu.VMEM((seq, LANES), F32)] * 2
                       + [pltpu.VMEM((n_pat, seq, LANES), F32)] * 3,
        compiler_params=_params(2),
        name="attention",
    )(proj, proj, proj, bd, g, bias, qb, kb, vb)


def _t5_bucket(dist):
    max_exact = NUM_BUCKETS // 2
    df = jnp.maximum(dist, 1).astype(F32)
    log_bucket = max_exact + (jnp.log(df / max_exact) / math.log(MAX_DISTANCE / max_exact)
                              * (NUM_BUCKETS - max_exact)).astype(jnp.int32)
    log_bucket = jnp.minimum(log_bucket, NUM_BUCKETS - 1)
    return jnp.where(dist < max_exact, dist, log_bucket)


def _band_bias(rel_bias):
    n = BAND_BLOCK
    qi = jnp.arange(n)[:, None]
    kj = jnp.arange(n)[None, :]
    buckets = jnp.arange(NUM_BUCKETS)
    tables = []
    for window, dil in DILATED_PATTERNS:
        steps = window // dil
        halves = []
        for back in (qi - kj + n, qi - kj):
            onehot = (_t5_bucket(jnp.maximum(back, 0) * dil)[:, :, None] == buckets).astype(F32)
            vals = jnp.einsum('qkb,bh->hqk', onehot, rel_bias.astype(F32),
                              precision=lax.Precision.HIGHEST) * LOG2E
            halves.append(jnp.where(((back >= 0) & (back <= steps))[None], vals, MASK_VALUE))
        prev, cur = halves
        normal = jnp.concatenate([prev, cur], axis=-1)
        first = jnp.concatenate([jnp.full_like(prev, MASK_VALUE), cur], axis=-1)
        tables.append(jnp.stack([normal, first], axis=1))
    tab = jnp.stack(tables, axis=1)
    tab = tab.reshape(A_HEADS // 2, 2, len(DILATED_PATTERNS), 2, n, 2 * n)
    return tab.transpose(0, 2, 3, 1, 4, 5).reshape(A_HEADS // 2, len(DILATED_PATTERNS), 2, 2 * n, 2 * n)


def _pack_pairs(v):
    half = v.shape[1] // 2
    bits = pltpu.bitcast(v, jnp.uint32)
    return (bits[:, :half] >> 16) | (bits[:, half:] & jnp.uint32(0xFFFF0000))


def _unpack_pairs(p):
    return (pltpu.bitcast(p << 16, F32), pltpu.bitcast(p & jnp.uint32(0xFFFF0000), F32))


def _out_proj_kernel(x_ref, a_ref, b_ref, w_ref, g_ref, rw_ref, rb_ref,
                     x1_ref, hp_ref, route_ref, route_t_ref, blk_ref, carry):
    i = pl.program_id(0)
    tm = x_ref.shape[0]

    @pl.when(i == 0)
    def _():
        carry[...] = jnp.zeros_like(carry)

    w = w_ref[...]
    x1 = x_ref[...] + _dot(a_ref[...], w[:A_WIDTH]) + _dot(b_ref[...], w[A_WIDTH:])
    x1_ref[...] = x1
    ms = jnp.mean(x1 * x1, axis=-1, keepdims=True)
    h = x1 * lax.rsqrt(ms + RMS_EPS) * g_ref[...]
    hi = h.astype(BF16)
    hi_f = hi.astype(F32)
    hp_ref[...] = _pack_pairs(hi_f)
    lo = (h - hi_f).astype(BF16)
    rw = rw_ref[...]
    hw = _dot(hi, rw)
    logits = hw[:, :LANES] + hw[:, LANES:] + _dot(lo, rw[:, :LANES]) + rb_ref[...]

    lane = lax.broadcasted_iota(jnp.int32, (tm, LANES), 1).astype(F32)
    remaining = logits
    vals, hots = [], []
    for _ in range(TOP_K):
        m = jnp.max(remaining, axis=-1, keepdims=True)
        first = jnp.min(jnp.where(remaining == m, lane, float(LANES)), axis=-1, keepdims=True)
        hot = lane == first
        remaining = jnp.where(hot, -jnp.inf, remaining)
        vals.append(m)
        hots.append(hot)
    exps = [jnp.exp(v - vals[0]) for v in vals]
    den = exps[0] + exps[1] + exps[2] + exps[3]

    chosen = jnp.zeros((tm, LANES), F32)
    for hot in hots:
        chosen = chosen + jnp.where(hot, 1.0, 0.0)
    r = lax.broadcasted_iota(jnp.int32, (tm, tm), 0)
    c = lax.broadcasted_iota(jnp.int32, (tm, tm), 1)
    earlier = jnp.where(r > c, 1.0, 0.0).astype(BF16)
    before = carry[...] + _dot(earlier, chosen.astype(BF16))
    in_block = jnp.sum(chosen, axis=0, keepdims=True)
    carry[...] = carry[...] + in_block
    blk_ref[...] = jnp.broadcast_to(in_block, blk_ref.shape)

    route = jnp.zeros((tm, LANES), F32)
    for k in range(TOP_K):
        first = jnp.sum(jnp.where(hots[k], lane, 0.0), axis=-1, keepdims=True)
        rank = jnp.sum(jnp.where(hots[k], before, 0.0), axis=-1, keepdims=True)
        route = route + jnp.where(lane == float(k), first, 0.0)
        route = route + jnp.where(lane == float(TOP_K + k), rank, 0.0)
        route = route + jnp.where(lane == float(2 * TOP_K + k), exps[k] / den, 0.0)
    route_ref[...] = route
    route_t_ref[...] = route.T[:route_t_ref.shape[0], :]


def _out_proj(x2, out_a, out_b, w, g, rw, rb, tm):
    t, d = x2.shape
    return pl.pallas_call(
        _out_proj_kernel,
        grid=(t // tm,),
        in_specs=[pl.BlockSpec((tm, d), lambda i: (i, 0)),
                  pl.BlockSpec((tm, A_WIDTH), lambda i: (i, 0)),
                  pl.BlockSpec((tm, B_WIDTH), lambda i: (i, 0)),
                  pl.BlockSpec(w.shape, lambda i: (0, 0)),
                  pl.BlockSpec((1, d), lambda i: (0, 0)),
                  pl.BlockSpec(rw.shape, lambda i: (0, 0)),
                  pl.BlockSpec((1, LANES), lambda i: (0, 0))],
        out_specs=[pl.BlockSpec((tm, d), lambda i: (i, 0)),
                   pl.BlockSpec((tm, d // 2), lambda i: (i, 0)),
                   pl.BlockSpec((tm, LANES), lambda i: (i, 0)),
                   pl.BlockSpec((2 * TOP_K, tm), lambda i: (0, i)),
                   pl.BlockSpec((None, 8, LANES), lambda i: (i, 0, 0))],
        out_shape=[jax.ShapeDtypeStruct((t, d), F32),
                   jax.ShapeDtypeStruct((t, d // 2), jnp.uint32),
                   jax.ShapeDtypeStruct((t, LANES), F32),
                   jax.ShapeDtypeStruct((2 * TOP_K, t), F32),
                   jax.ShapeDtypeStruct((t // tm, 8, LANES), F32)],
        scratch_shapes=[pltpu.VMEM((1, LANES), F32)],
        compiler_params=_params(),
        name="out_proj",
    )(x2, out_a, out_b, w, g, rw, rb)


def _dispatch_kernel(pad_ref, end_ref, nb_ref, dest_hbm, h_ref, xs_hbm, dest_s0, dest_s1, zeros,
                     sem_i, sem_z, sem_o, *, bm):
    i = pl.program_id(0)
    steps = pl.num_programs(0)
    tm = h_ref.shape[0]
    n_blocks = xs_hbm.shape[0] // bm
    dest_s = (dest_s0, dest_s1)

    def idx_copy(step, slot):
        return pltpu.make_async_copy(dest_hbm.at[step], dest_s[slot], sem_i.at[slot])

    @pl.when(i == 0)
    def _():
        idx_copy(0, 0).start()
        zeros[...] = jnp.zeros_like(zeros)

        def zero_block(start):
            return pltpu.make_async_copy(zeros, xs_hbm.at[pl.ds(pl.multiple_of(start, bm), bm), :], sem_z)

        for e in range(N_EXPERTS):
            @pl.when(pad_ref[e] > 0)
            def _():
                zero_block(end_ref[e] - bm).start()

        def start_unused(b, carry):
            zero_block(b * bm).start()
            return carry

        def wait_unused(b, carry):
            zero_block(b * bm).wait()
            return carry

        lax.fori_loop(nb_ref[0], n_blocks, start_unused, 0)
        for e in range(N_EXPERTS):
            @pl.when(pad_ref[e] > 0)
            def _():
                zero_block(end_ref[e] - bm).wait()
        lax.fori_loop(nb_ref[0], n_blocks, wait_unused, 0)

    for slot in range(2):
        @pl.when(i % 2 == slot)
        def _(slot=slot):
            @pl.when(i + 1 < steps)
            def _():
                idx_copy(i + 1, 1 - slot).start()

            idx_copy(i, slot).wait()

            def body(t, carry):
                for k in range(TOP_K):
                    pltpu.make_async_copy(h_ref.at[pl.ds(t, 1), :],
                                          xs_hbm.at[pl.ds(dest_s[slot][k * tm + t], 1), :],
                                          sem_o).start(priority=k % 2)
                return carry

            lax.fori_loop(0, tm, body, 0, unroll=8)
    for _ in range(TOP_K):
        pltpu.make_async_copy(h_ref, xs_hbm.at[pl.ds(0, tm), :], sem_o).wait()


def _dispatch(padded, pad_end, n_used, dest, hp, n_slots, tm, bm):
    t, half = hp.shape
    grid_spec = pltpu.PrefetchScalarGridSpec(
        num_scalar_prefetch=3,
        grid=(t // tm,),
        in_specs=[pl.BlockSpec(memory_space=pl.ANY),
                  pl.BlockSpec((tm, half), lambda i, p, e, n: (i, 0))],
        out_specs=pl.BlockSpec(memory_space=pl.ANY),
        scratch_shapes=[pltpu.SMEM((tm * TOP_K,), jnp.int32),
                        pltpu.SMEM((tm * TOP_K,), jnp.int32),
                        pltpu.VMEM((bm, half), jnp.uint32),
                        pltpu.SemaphoreType.DMA((2,)),
                        pltpu.SemaphoreType.DMA(()),
                        pltpu.SemaphoreType.DMA(())],
    )
    return pl.pallas_call(
        functools.partial(_dispatch_kernel, bm=bm),
        grid_spec=grid_spec,
        out_shape=jax.ShapeDtypeStruct((n_slots, half), jnp.uint32),
        compiler_params=_params(),
        name="dispatch",
    )(padded, pad_end, n_used, dest, hp)


EXPERT_SUB_ROWS = 256


def _expert_ffn_kernel(be_ref, nb_ref, next_ref, x_ref, wgu_hbm, bgu_ref, wd_hbm, bd_ref, y_ref,
                       wgu_f, wd_f, wgu_s, wd_s, sem_w):
    i = pl.program_id(0)
    d_ff = wd_f.shape[0]
    half = x_ref.shape[1]

    def weight_copies(e):
        return (pltpu.make_async_copy(wgu_hbm.at[e], wgu_f, sem_w.at[0]),
                pltpu.make_async_copy(wd_hbm.at[e], wd_f, sem_w.at[1]))

    @pl.when(jnp.logical_and(i == 0, nb_ref[0] > 0))
    def _():
        for copy in weight_copies(be_ref[0]):
            copy.start()

    @pl.when(i >= nb_ref[0])
    def _():
        y_ref[...] = jnp.zeros_like(y_ref)

    @pl.when(i < nb_ref[0])
    def _():
        changed = jnp.logical_or(i == 0, be_ref[i] != be_ref[jnp.maximum(i - 1, 0)])

        @pl.when(changed)
        def _():
            for copy in weight_copies(be_ref[i]):
                copy.wait()
            wgu_s[...] = wgu_f[...].astype(BF16)
            wd_s[...] = wd_f[...].astype(BF16)

            @pl.when(next_ref[i] >= 0)
            def _():
                for copy in weight_copies(next_ref[i]):
                    copy.start()

        for r0 in range(0, x_ref.shape[0], EXPERT_SUB_ROWS):
            rows = slice(r0, r0 + EXPERT_SUB_ROWS)
            x_lo, x_hi = _unpack_pairs(x_ref[rows, :])
            gu = (_dot(x_lo.astype(BF16), wgu_s[:half, :]) + _dot(x_hi.astype(BF16), wgu_s[half:, :])
                  + bgu_ref[...])
            gate = jnp.minimum(gu[:, :d_ff], SWIGLU_LIMIT)
            up = jnp.clip(gu[:, d_ff:], -SWIGLU_LIMIT, SWIGLU_LIMIT)
            glu = gate * jax.nn.sigmoid(SWIGLU_ALPHA * gate)
            act = ((up + 1.0) * glu).astype(BF16)
            y = _dot(act, wd_s[...]) + bd_ref[...]
            y_ref[rows, :] = _pack_pairs(y.astype(BF16).astype(F32))


def _expert_ffn(block_expert, n_used, next_expert, xs, w_gate_up, b_gate_up, w_down, b_down, bm):
    n_slots, half = xs.shape
    n_e, d, two_ff = w_gate_up.shape
    d_ff = two_ff // 2
    n_blocks = n_slots // bm

    def used(i, nb):
        return jnp.minimum(i, jnp.maximum(nb[0] - 1, 0))

    grid_spec = pltpu.PrefetchScalarGridSpec(
        num_scalar_prefetch=3,
        grid=(n_blocks,),
        in_specs=[pl.BlockSpec((bm, half), lambda i, be, nb, nx: (used(i, nb), 0)),
                  pl.BlockSpec(memory_space=pl.ANY),
                  pl.BlockSpec((None, 1, two_ff), lambda i, be, nb, nx: (be[i], 0, 0)),
                  pl.BlockSpec(memory_space=pl.ANY),
                  pl.BlockSpec((None, 1, d), lambda i, be, nb, nx: (be[i], 0, 0))],
        out_specs=pl.BlockSpec((bm, half), lambda i, be, nb, nx: (i, 0)),
        scratch_shapes=[pltpu.VMEM((d, two_ff), F32), pltpu.VMEM((d_ff, d), F32),
                        pltpu.VMEM((d, two_ff), BF16), pltpu.VMEM((d_ff, d), BF16),
                        pltpu.SemaphoreType.DMA((2,))],
    )
    return pl.pallas_call(
        _expert_ffn_kernel,
        grid_spec=grid_spec,
        out_shape=jax.ShapeDtypeStruct((n_slots, half), jnp.uint32),
        compiler_params=_params(),
        name="expert_ffn",
    )(block_expert, n_used, next_expert, xs, w_gate_up, b_gate_up.reshape(n_e, 1, two_ff),
      w_down, b_down.reshape(n_e, 1, d))


COMBINE_LOCAL_ROWS = 2560
SEG_ALIGN = 8


def _pieces(limit):
    sizes = []
    size = 1 << (limit.bit_length() - 1)
    while size >= SEG_ALIGN:
        sizes.append(size)
        size //= 2
    return sizes


def _combine_kernel(src_ref, len_ref, dst_ref, tot_ref, route_ref, local_ref, x1_ref, ys_hbm, o_ref, ybuf,
                    sem_g):
    j = pl.program_id(0)
    blocks = pl.num_programs(0) - 1
    tm = x1_ref.shape[0]
    half = x1_ref.shape[1] // 2
    local_rows = ybuf.shape[1]

    @pl.when(j == 0)
    def _():
        ybuf[...] = jnp.zeros_like(ybuf)

    @pl.when(j < blocks)
    def _():
        for s in range(2):
            @pl.when(j % 2 == s)
            def _(s=s):
                for e in range(N_EXPERTS):
                    seg = j * N_EXPERTS + e
                    n = len_ref[seg]
                    src = src_ref[seg]
                    dst = dst_ref[seg]
                    for size in _pieces(tm + 2 * SEG_ALIGN):
                        @pl.when((n & size) != 0)
                        def _(size=size, src=src, dst=dst):
                            pltpu.make_async_copy(
                                ys_hbm.at[pl.ds(pl.multiple_of(src, SEG_ALIGN), size), :],
                                ybuf.at[s, pl.ds(pl.multiple_of(dst, SEG_ALIGN), size), :],
                                sem_g.at[s]).start()
                        step = jnp.where((n & size) != 0, size, 0)
                        src = src + step
                        dst = dst + step

    @pl.when(j >= 1)
    def _():
        slot = (j - 1) % 2
        total = tot_ref[j - 1]
        for size in _pieces(local_rows):
            @pl.when((total & size) != 0)
            def _(size=size):
                pltpu.make_async_copy(ys_hbm.at[pl.ds(0, size), :], ybuf.at[slot, pl.ds(0, size), :],
                                      sem_g.at[slot]).wait()

        route = route_ref[...]
        local_t = local_ref[...]
        local = jnp.concatenate(
            [local_t, jnp.zeros((LANES - local_t.shape[0], tm), F32)], axis=0).T
        col = lax.broadcasted_iota(jnp.int32, (tm, local_rows), 1).astype(F32)
        g = jnp.zeros((tm, local_rows), F32)
        for k in range(TOP_K):
            pos = local[:, k: k + 1]
            gate = route[:, 2 * TOP_K + k: 2 * TOP_K + k + 1]
            g = jnp.where(col == pos, gate, g)
        g = g.astype(BF16)
        lo, hi = _unpack_pairs(ybuf[slot])
        o_ref[:, :half] = x1_ref[:, :half] + _dot(g, lo.astype(BF16))
        o_ref[:, half:] = x1_ref[:, half:] + _dot(g, hi.astype(BF16))


def _combine(tables, route, local_t, x1, ys, tm, local_rows):
    t, d = x1.shape
    half = d // 2

    def summed(j, *_):
        return (jnp.maximum(j - 1, 0), 0)

    grid_spec = pltpu.PrefetchScalarGridSpec(
        num_scalar_prefetch=4,
        grid=(t // tm + 1,),
        in_specs=[pl.BlockSpec((tm, LANES), summed),
                  pl.BlockSpec((local_t.shape[0], tm), lambda j, *_: (0, jnp.maximum(j - 1, 0))),
                  pl.BlockSpec((tm, d), summed),
                  pl.BlockSpec(memory_space=pl.ANY)],
        out_specs=pl.BlockSpec((tm, d), summed),
        scratch_shapes=[pltpu.VMEM((2, local_rows, half), jnp.uint32),
                        pltpu.SemaphoreType.DMA((2,))],
    )
    return pl.pallas_call(
        _combine_kernel,
        grid_spec=grid_spec,
        out_shape=jax.ShapeDtypeStruct((t, d), F32),
        compiler_params=_params(),
        name="combine",
    )(*tables, route, local_t, x1, ys)


def _per_expert(table, idx):
    out = jnp.zeros(idx.shape, table.dtype)
    for e in range(N_EXPERTS):
        out = jnp.where(idx == e, table[:, e][None, :, None], out)
    return out


def _combine_tables(idx, rank, n, pad_start, tm, local_rows):
    before = jnp.cumsum(n, axis=0) - n
    first = pad_start[None, :] + before
    src = first // SEG_ALIGN * SEG_ALIGN
    length = jnp.where(n > 0, (first + n + SEG_ALIGN - 1) // SEG_ALIGN * SEG_ALIGN - src, 0)
    dst = jnp.cumsum(length, axis=1) - length
    total = jnp.sum(length, axis=1)
    assert local_rows >= tm * TOP_K + N_EXPERTS * 2 * (SEG_ALIGN - 1)
    local = (_per_expert(dst + first - src - before, idx) + rank).reshape(TOP_K, -1).astype(F32)
    tables = (src.reshape(-1).astype(jnp.int32), length.reshape(-1).astype(jnp.int32),
              dst.reshape(-1).astype(jnp.int32), total.astype(jnp.int32))
    return tables, local


def _slot_layout(route_t, n, tm, bm):
    t = route_t.shape[1]
    blocks = t // tm
    idx = route_t[:TOP_K].astype(jnp.int32).reshape(TOP_K, blocks, tm)
    rank = route_t[TOP_K:].astype(jnp.int32).reshape(TOP_K, blocks, tm)
    counts = jnp.sum(n, axis=0)
    padded = (counts + bm - 1) // bm * bm
    pad_end = jnp.cumsum(padded).astype(jnp.int32)
    pad_start = pad_end - padded
    dest = _per_expert(jnp.broadcast_to(pad_start[None, :], n.shape), idx) + rank
    dest = dest.transpose(1, 0, 2).reshape(blocks, TOP_K * tm)
    n_blocks = -(-t * TOP_K // bm) + N_EXPERTS
    block_start = jnp.arange(n_blocks, dtype=jnp.int32) * bm
    block_expert = jnp.minimum(jnp.sum(pad_end[None, :] <= block_start[:, None], axis=1),
                               N_EXPERTS - 1).astype(jnp.int32)
    n_used = (pad_end[-1:] // bm).astype(jnp.int32)
    following = jnp.take(pad_end, block_expert) // bm
    next_expert = jnp.where(following < n_used[0],
                            jnp.take(block_expert, jnp.minimum(following, n_blocks - 1)), -1).astype(jnp.int32)
    return idx, rank, padded, pad_end, dest, block_expert, next_expert, n_used, n_blocks * bm


def _rot_cols(w):
    half = w.shape[-1] // 2
    return jnp.concatenate([-w[..., half:], w[..., :half]], axis=-1)


def _swap_halves(g):
    half = g.shape[-1] // 2
    return jnp.concatenate([g[..., half:], g[..., :half]], axis=-1)


def kernel(x, attn_norm_g, w_in, a_q_norm_g, a_k_norm_g, rel_bias, q_a_norm_g, w_q_b, kv_a_norm_g,
           w_kv_b, b_q_norm_g, b_k_norm_g, w_out, ffn_norm_g, router_w, router_b, w_gate_up,
           b_gate_up, w_down, b_down):
    batch, seq, d = x.shape
    t = batch * seq
    depth = w_in.shape[0]
    tm = 512
    expert_bm = 512

    pos = jnp.arange(seq, dtype=F32)
    inv_freq = ROPE_THETA ** (-jnp.arange(0, QK_ROPE_DIM, 2, dtype=F32) / QK_ROPE_DIM)
    ang = pos[:, None] * inv_freq[None, :]
    cos, sin = jnp.cos(ang), jnp.sin(ang)
    cs = jnp.concatenate([cos, cos, sin, sin], axis=-1)

    row = jnp.arange(LANES)[:, None] // A_HEAD_DIM
    col = jnp.arange(LANES)[None, :] // A_HEAD_DIM
    head_block_ones = (row == col).astype(BF16)
    band_bias = _band_bias(rel_bias)

    x2 = x.reshape(t, d)
    for layer in range(depth):
        kpe_off = 3 * A_WIDTH + Q_LORA_RANK + KV_LORA_RANK
        w_kpe = w_in[layer][:, kpe_off:]
        w_in_r = jnp.concatenate([w_in[layer][:, :kpe_off], w_kpe, _rot_cols(w_kpe)], axis=1).astype(BF16)

        wq = w_q_b[layer].reshape(Q_LORA_RANK, B_HEADS, QK_HEAD_DIM)
        wq_rope = wq[..., QK_NOPE_DIM:]
        wq_r = jnp.concatenate([wq, _rot_cols(wq_rope)], axis=-1).reshape(Q_LORA_RANK, -1).astype(BF16)
        wkv = w_kv_b[layer].reshape(KV_LORA_RANK, B_HEADS, QK_NOPE_DIM + V_HEAD_DIM)
        wkv_r = jnp.concatenate([wkv[..., :QK_NOPE_DIM].reshape(KV_LORA_RANK, -1),
                                 wkv[..., QK_NOPE_DIM:].reshape(KV_LORA_RANK, -1)], axis=1).astype(BF16)
        gc = jnp.concatenate([q_a_norm_g[layer], kv_a_norm_g[layer]])[None, :]
        gq, gk = b_q_norm_g[layer], b_k_norm_g[layer]

        def rope_gain(gr):
            return jnp.concatenate([gr, _swap_halves(gr)])

        gqk = jnp.stack([gq[:QK_NOPE_DIM], rope_gain(gq[QK_NOPE_DIM:]),
                         gk[:QK_NOPE_DIM], rope_gain(gk[QK_NOPE_DIM:])])
        g_a = jnp.stack([jnp.tile(a_q_norm_g[layer], 2), jnp.tile(a_k_norm_g[layer], 2)])
        g_a = jnp.broadcast_to(g_a[None], (A_WIDTH // LANES, 2, LANES))

        proj, qb, kb, vb = _in_proj(x2, attn_norm_g[layer][None, :], w_in_r, cs, wq_r, wkv_r, gc, gqk, seq, tm)
        out_a, out_b = _attention(proj, head_block_ones, g_a, band_bias, qb, kb, vb, batch, seq, 512)

        rw = jnp.pad(router_w[layer], ((0, 0), (0, LANES - N_EXPERTS)))
        rw_hi = rw.astype(BF16)
        rw_lo = (rw - rw_hi.astype(F32)).astype(BF16)
        rb = jnp.pad(router_b[layer], (0, LANES - N_EXPERTS), constant_values=-jnp.inf)[None, :]
        x1, hp, route, route_t, in_block = _out_proj(
            x2, out_a, out_b, w_out[layer].astype(BF16), ffn_norm_g[layer][None, :],
            jnp.concatenate([rw_hi, rw_lo], axis=1), rb, tm)

        n = in_block[:, 0, :N_EXPERTS].astype(jnp.int32)
        idx, rank, padded, pad_end, dest, block_expert, next_expert, n_used, n_slots = _slot_layout(
            route_t, n, tm, expert_bm)
        xs = _dispatch(padded, pad_end, n_used, dest, hp, n_slots, tm, expert_bm)
        ys = _expert_ffn(block_expert, n_used, next_expert, xs, w_gate_up[layer], b_gate_up[layer],
                         w_down[layer], b_down[layer], expert_bm)
        tables, local_t = _combine_tables(idx, rank, n, pad_end - padded, tm, COMBINE_LOCAL_ROWS)
        x2 = _combine(tables, route, local_t, x1, ys, tm, COMBINE_LOCAL_ROWS)
    return x2.reshape(batch, seq, d)
```

```python
import functools
import math

import jax
import jax.numpy as jnp
from jax import lax
from jax.experimental import pallas as pl
from jax.experimental.pallas import tpu as pltpu

A_HEADS = 8
A_HEAD_DIM = 64
A_WIDTH = A_HEADS * A_HEAD_DIM
DILATED_PATTERNS = ((128, 1), (512, 4), (2048, 16))
BAND_BLOCK = 128

B_HEADS = 4
QK_NOPE_DIM = 128
QK_ROPE_DIM = 64
QK_HEAD_DIM = QK_NOPE_DIM + QK_ROPE_DIM
V_HEAD_DIM = 128
Q_LORA_RANK = 256
KV_LORA_RANK = 256
B_WIDTH = B_HEADS * V_HEAD_DIM
ROPE_THETA = 10000.0

NUM_BUCKETS = 32
MAX_DISTANCE = 2048

N_EXPERTS = 32
TOP_K = 4
SWIGLU_LIMIT = 7.0
SWIGLU_ALPHA = 1.702
RMS_EPS = 1e-6

LANES = 128
MASK_VALUE = -1e30
LOG2E = math.log2(math.e)
PROJ_WIDTH = 3 * A_WIDTH + Q_LORA_RANK + KV_LORA_RANK + 2 * QK_ROPE_DIM
VMEM_LIMIT = 56 * 1024 * 1024

F32 = jnp.float32
BF16 = jnp.bfloat16


def _dot(a, b):
    return jnp.dot(a, b, preferred_element_type=F32)


def _dot_nt(a, b):
    return lax.dot_general(a, b, (((1,), (1,)), ((), ())), preferred_element_type=F32)


def _group_sum(x, m):
    return _dot(x.astype(BF16), m)


def _params(n_parallel=1):
    return pltpu.CompilerParams(
        dimension_semantics=("arbitrary",) * n_parallel, vmem_limit_bytes=VMEM_LIMIT)


IN_PROJ_SUB_ROWS = 256


def _mla_qkv(c, kper, cs, wq_ref, wkv_ref, gc, gqk, q_ref, k_ref, v_ref, rows, q_scale):
    def lora_norm(z, g):
        ms = jnp.mean(z * z, axis=-1, keepdims=True)
        return (z * lax.rsqrt(ms + RMS_EPS) * g).astype(BF16)

    cq = lora_norm(c[:, :Q_LORA_RANK], gc[:, :Q_LORA_RANK])
    ckv = lora_norm(c[:, Q_LORA_RANK:], gc[:, Q_LORA_RANK:])
    qb = _dot(cq, wq_ref[...])
    kvb = _dot(ckv, wkv_ref[...])
    n = c.shape[0]

    row = lax.broadcasted_iota(jnp.int32, (LANES, LANES), 0)
    ones_all = jnp.ones((LANES, LANES), BF16)
    ones_lo = jnp.where(row < QK_ROPE_DIM, 1.0, 0.0).astype(BF16)
    lane = lax.broadcasted_iota(jnp.int32, (n, LANES), 1)

    def rope(z, g_row):
        t = z * (g_row * cs)
        return jnp.where(lane < QK_ROPE_DIM, t + pltpu.roll(t, QK_ROPE_DIM, 1), 0.0)

    k_rope = rope(kper, gqk[3:4])
    kpe_ss = _group_sum(kper * kper, ones_lo)
    for h in range(B_HEADS):
        qn = qb[:, 2 * LANES * h: 2 * LANES * h + LANES]
        qr = qb[:, 2 * LANES * h + LANES: 2 * LANES * (h + 1)]
        ss = _group_sum(qn * qn, ones_all) + _group_sum(qr * qr, ones_lo)
        rs = lax.rsqrt(ss * (1.0 / QK_HEAD_DIM) + RMS_EPS) * q_scale
        q_ref[h, rows, :LANES] = (qn * gqk[0:1] * rs).astype(BF16)
        q_ref[h, rows, LANES:] = (rope(qr, gqk[1:2]) * rs).astype(BF16)
        kn = kvb[:, LANES * h: LANES * (h + 1)]
        ssk = _group_sum(kn * kn, ones_all) + kpe_ss
        rsk = lax.rsqrt(ssk * (1.0 / QK_HEAD_DIM) + RMS_EPS)
        k_ref[h, rows, :LANES] = (kn * gqk[2:3] * rsk).astype(BF16)
        k_ref[h, rows, LANES:] = (k_rope * rsk).astype(BF16)
    v_ref[rows, :] = kvb[:, B_HEADS * QK_NOPE_DIM:].astype(BF16)


def _in_proj_kernel(x_ref, g_ref, w_ref, cs_ref, wq_ref, wkv_ref, gc_ref, gqk_ref,
                    a_ref, q_ref, k_ref, v_ref, *, q_scale):
    a_width = a_ref.shape[1]
    c_width = Q_LORA_RANK + KV_LORA_RANK
    for r0 in range(0, x_ref.shape[0], IN_PROJ_SUB_ROWS):
        rows = slice(r0, r0 + IN_PROJ_SUB_ROWS)
        x = x_ref[rows, :]
        ms = jnp.mean(x * x, axis=-1, keepdims=True)
        h = (x * lax.rsqrt(ms + RMS_EPS) * g_ref[...]).astype(BF16)
        proj = _dot(h, w_ref[...])
        a_ref[rows, :] = proj[:, :a_width]
        _mla_qkv(proj[:, a_width:a_width + c_width], proj[:, a_width + c_width:], cs_ref[rows, :],
                 wq_ref, wkv_ref, gc_ref[...], gqk_ref[...], q_ref, k_ref, v_ref, rows, q_scale)


def _in_proj(x2, g, w, cs, wq, wkv, gc, gqk, seq, tm):
    t, d = x2.shape
    a_width = 3 * A_WIDTH
    pos_blocks = seq // tm
    kernel = functools.partial(_in_proj_kernel, q_scale=QK_HEAD_DIM ** -0.5 * LOG2E)
    return pl.pallas_call(
        kernel,
        grid=(t // tm,),
        in_specs=[pl.BlockSpec((tm, d), lambda i: (i, 0)),
                  pl.BlockSpec((1, d), lambda i: (0, 0)),
                  pl.BlockSpec(w.shape, lambda i: (0, 0)),
                  pl.BlockSpec((tm, LANES), lambda i: (i % pos_blocks, 0)),
                  pl.BlockSpec(wq.shape, lambda i: (0, 0)),
                  pl.BlockSpec(wkv.shape, lambda i: (0, 0)),
                  pl.BlockSpec(gc.shape, lambda i: (0, 0)),
                  pl.BlockSpec(gqk.shape, lambda i: (0, 0))],
        out_specs=[pl.BlockSpec((tm, a_width), lambda i: (i, 0)),
                   pl.BlockSpec((B_HEADS, tm, 2 * LANES), lambda i: (0, i, 0)),
                   pl.BlockSpec((B_HEADS, tm, 2 * LANES), lambda i: (0, i, 0)),
                   pl.BlockSpec((tm, B_WIDTH), lambda i: (i, 0))],
        out_shape=[jax.ShapeDtypeStruct((t, a_width), F32),
                   jax.ShapeDtypeStruct((B_HEADS, t, 2 * LANES), BF16),
                   jax.ShapeDtypeStruct((B_HEADS, t, 2 * LANES), BF16),
                   jax.ShapeDtypeStruct((t, B_WIDTH), BF16)],
        compiler_params=_params(),
        name="in_proj",
    )(x2, g, w, cs, wq, wkv, gc, gqk)


def _mla_attention_kernel(q_ref, k_ref, v_ref, o_ref, *, tq):
    seq = q_ref.shape[0]
    row = lax.broadcasted_iota(jnp.int32, (tq, tq), 0)
    col = lax.broadcasted_iota(jnp.int32, (tq, tq), 1)
    diag_mask = jnp.where(col <= row, 0.0, MASK_VALUE)
    v_ext = jnp.concatenate([v_ref[...], jnp.ones(v_ref.shape, v_ref.dtype)], axis=1)
    dv = v_ref.shape[1]
    for i in range(seq // tq):
        q = q_ref[i * tq:(i + 1) * tq, :]
        s_diag = _dot_nt(q, k_ref[i * tq:(i + 1) * tq, :]) + diag_mask
        m = jnp.max(s_diag, axis=-1, keepdims=True)
        if i > 0:
            s_past = _dot_nt(q, k_ref[:i * tq, :])
            m = jnp.maximum(m, jnp.max(s_past, axis=-1, keepdims=True))
        o = _dot(jnp.exp2(s_diag - m).astype(BF16), v_ext[i * tq:(i + 1) * tq, :])
        if i > 0:
            o = o + _dot(jnp.exp2(s_past - m).astype(BF16), v_ext[:i * tq, :])
        o_ref[i * tq:(i + 1) * tq, :] = (o[:, :dv] / o[:, dv:]).astype(o_ref.dtype)


def _mla_attention(qb, kb, vb, batch, seq, tq):
    t = vb.shape[0]
    kernel = functools.partial(_mla_attention_kernel, tq=tq)
    return pl.pallas_call(
        kernel,
        grid=(batch, B_HEADS),
        in_specs=[pl.BlockSpec((None, seq, 2 * LANES), lambda b, h: (h, b, 0)),
                  pl.BlockSpec((None, seq, 2 * LANES), lambda b, h: (h, b, 0)),
                  pl.BlockSpec((seq, V_HEAD_DIM), lambda b, h: (b, h))],
        out_specs=pl.BlockSpec((seq, V_HEAD_DIM), lambda b, h: (b, h)),
        out_shape=jax.ShapeDtypeStruct((t, B_WIDTH), BF16),
        compiler_params=_params(2),
        name="mla_attention",
    )(qb, kb, vb)


def _dilated_kernel(q_ref, k_ref, v_ref, bd_ref, g_ref, bias_ref, o_ref, qs, ks, acc_o, acc_l, acc_m,
                    *, unroll):
    seq = q_ref.shape[0]
    n_tiles = seq // BAND_BLOCK
    low = lax.broadcasted_iota(jnp.int32, (BAND_BLOCK, LANES), 1) < A_HEAD_DIM
    bd = bd_ref[...]
    g = g_ref[...]

    def head_norm(z, g_row):
        ss = _group_sum(z * z, bd)
        return z * lax.rsqrt(ss * (1.0 / A_HEAD_DIM) + RMS_EPS) * g_row

    qs[...] = head_norm(q_ref[...], g[0:1]) * (A_HEAD_DIM ** -0.5 * LOG2E)
    ks[...] = head_norm(k_ref[...], g[1:2])
    ones = jnp.ones((2 * BAND_BLOCK, LANES), BF16)

    for p, (window, dil) in enumerate(DILATED_PATTERNS):
        blocks_per_class = n_tiles // dil
        span = BAND_BLOCK * dil

        def rows_at(start, dil=dil):
            if dil == 1:
                return pl.ds(start, BAND_BLOCK)
            return pl.ds(start, BAND_BLOCK, stride=dil)

        def tile(i, carry, p=p, blocks_per_class=blocks_per_class, span=span, rows_at=rows_at):
            r = i // blocks_per_class
            j = i % blocks_per_class
            cur = rows_at(r + j * span)
            prev = rows_at(r + jnp.maximum(j - 1, 0) * span)
            first = jnp.where(j == 0, 1, 0)
            q = qs[cur, :]
            q2 = jnp.concatenate([jnp.where(low, q, 0.0), jnp.where(low, 0.0, q)], axis=0).astype(BF16)
            k_band = jnp.concatenate([ks[prev, :], ks[cur, :]], axis=0).astype(BF16)
            s = _dot_nt(q2, k_band) + bias_ref[p, first]
            m = jnp.max(s, axis=-1, keepdims=True)
            pr = jnp.exp2(s - m).astype(BF16)
            v_band = jnp.concatenate([v_ref[prev, :], v_ref[cur, :]], axis=0).astype(BF16)
            o = _dot(pr, jnp.concatenate([v_band, ones], axis=1))
            top, bot = o[:BAND_BLOCK], o[BAND_BLOCK:]
            acc_o[p, cur, :] = jnp.where(low, top[:, :LANES], bot[:, :LANES])
            acc_l[p, cur, :] = jnp.where(low, top[:, LANES:], bot[:, LANES:])
            acc_m[p, cur, :] = jnp.where(low, m[:BAND_BLOCK], m[BAND_BLOCK:])
            return carry

        lax.fori_loop(0, n_tiles, tile, 0, unroll=unroll)

    m_all = jnp.maximum(jnp.maximum(acc_m[0], acc_m[1]), acc_m[2])
    num = jnp.zeros((seq, LANES), F32)
    den = jnp.zeros((seq, LANES), F32)
    for p in range(len(DILATED_PATTERNS)):
        w = jnp.exp2(acc_m[p] - m_all)
        num = num + w * acc_o[p]
        den = den + w * acc_l[p]
    o_ref[...] = (num / den).astype(o_ref.dtype)


def _dilated_attention(proj, bd, g, bias, batch, seq):
    t = proj.shape[0]
    pairs = A_WIDTH // LANES
    n_pat = len(DILATED_PATTERNS)
    return pl.pallas_call(
        functools.partial(_dilated_kernel, unroll=16),
        grid=(batch, pairs),
        in_specs=[pl.BlockSpec((seq, LANES), lambda b, c: (b, c)),
                  pl.BlockSpec((seq, LANES), lambda b, c: (b, pairs + c)),
                  pl.BlockSpec((seq, LANES), lambda b, c: (b, 2 * pairs + c)),
                  pl.BlockSpec((LANES, LANES), lambda b, c: (0, 0)),
                  pl.BlockSpec((None, 2, LANES), lambda b, c: (c, 0, 0)),
                  pl.BlockSpec((None, n_pat, 2, 2 * BAND_BLOCK, 2 * BAND_BLOCK),
                               lambda b, c: (c, 0, 0, 0, 0))],
        out_specs=pl.BlockSpec((seq, LANES), lambda b, c: (b, c)),
        out_shape=jax.ShapeDtypeStruct((t, A_WIDTH), BF16),
        scratch_shapes=[pltpu.VMEM((seq, LANES), F32)] * 2
                       + [pltpu.VMEM((n_pat, seq, LANES), F32)] * 3,
        compiler_params=_params(2),
        name="dilated_attention",
    )(proj, proj, proj, bd, g, bias)


def _attention_kernel(qa_ref, ka_ref, va_ref, bd_ref, g_ref, bias_ref, qb_ref, kb_ref, vb_ref,
                      oa_ref, ob_ref, qs, ks, acc_o, acc_l, acc_m, *, unroll, tq):
    _mla_attention_kernel(qb_ref, kb_ref, vb_ref, ob_ref, tq=tq)
    _dilated_kernel(qa_ref, ka_ref, va_ref, bd_ref, g_ref, bias_ref, oa_ref, qs, ks, acc_o, acc_l, acc_m,
                    unroll=unroll)


def _attention(proj, bd, g, bias, qb, kb, vb, batch, seq, tq):
    t = proj.shape[0]
    pairs = A_WIDTH // LANES
    assert pairs == B_HEADS
    n_pat = len(DILATED_PATTERNS)
    return pl.pallas_call(
        functools.partial(_attention_kernel, unroll=16, tq=tq),
        grid=(batch, pairs),
        in_specs=[pl.BlockSpec((seq, LANES), lambda b, c: (b, c)),
                  pl.BlockSpec((seq, LANES), lambda b, c: (b, pairs + c)),
                  pl.BlockSpec((seq, LANES), lambda b, c: (b, 2 * pairs + c)),
                  pl.BlockSpec((LANES, LANES), lambda b, c: (0, 0)),
                  pl.BlockSpec((None, 2, LANES), lambda b, c: (c, 0, 0)),
                  pl.BlockSpec((None, n_pat, 2, 2 * BAND_BLOCK, 2 * BAND_BLOCK),
                               lambda b, c: (c, 0, 0, 0, 0)),
                  pl.BlockSpec((None, seq, 2 * LANES), lambda b, h: (h, b, 0)),
                  pl.BlockSpec((None, seq, 2 * LANES), lambda b, h: (h, b, 0)),
                  pl.BlockSpec((seq, V_HEAD_DIM), lambda b, h: (b, h))],
        out_specs=[pl.BlockSpec((seq, LANES), lambda b, c: (b, c)),
                   pl.BlockSpec((seq, V_HEAD_DIM), lambda b, h: (b, h))],
        out_shape=[jax.ShapeDtypeStruct((t, A_WIDTH), BF16),
                   jax.ShapeDtypeStruct((t, B_WIDTH), BF16)],
        scratch_shapes=[pltpu.VMEM((seq, LANES), F32)] * 2
                       + [pltpu.VMEM((n_pat, seq, LANES), F32)] * 3,
        compiler_params=_params(2),
        name="attention",
    )(proj, proj, proj, bd, g, bias, qb, kb, vb)


def _t5_bucket(dist):
    max_exact = NUM_BUCKETS // 2
    df = jnp.maximum(dist, 1).astype(F32)
    log_bucket = max_exact + (jnp.log(df / max_exact) / math.log(MAX_DISTANCE / max_exact)
                              * (NUM_BUCKETS - max_exact)).astype(jnp.int32)
    log_bucket = jnp.minimum(log_bucket, NUM_BUCKETS - 1)
    return jnp.where(dist < max_exact, dist, log_bucket)


def _band_bias(rel_bias):
    n = BAND_BLOCK
    qi = jnp.arange(n)[:, None]
    kj = jnp.arange(n)[None, :]
    buckets = jnp.arange(NUM_BUCKETS)
    tables = []
    for window, dil in DILATED_PATTERNS:
        steps = window // dil
        halves = []
        for back in (qi - kj + n, qi - kj):
            onehot = (_t5_bucket(jnp.maximum(back, 0) * dil)[:, :, None] == buckets).astype(F32)
            vals = jnp.einsum('qkb,bh->hqk', onehot, rel_bias.astype(F32),
                              precision=lax.Precision.HIGHEST) * LOG2E
            halves.append(jnp.where(((back >= 0) & (back <= steps))[None], vals, MASK_VALUE))
        prev, cur = halves
        normal = jnp.concatenate([prev, cur], axis=-1)
        first = jnp.concatenate([jnp.full_like(prev, MASK_VALUE), cur], axis=-1)
        tables.append(jnp.stack([normal, first], axis=1))
    tab = jnp.stack(tables, axis=1)
    tab = tab.reshape(A_HEADS // 2, 2, len(DILATED_PATTERNS), 2, n, 2 * n)
    return tab.transpose(0, 2, 3, 1, 4, 5).reshape(A_HEADS // 2, len(DILATED_PATTERNS), 2, 2 * n, 2 * n)


def _pack_pairs(v):
    half = v.shape[1] // 2
    bits = pltpu.bitcast(v, jnp.uint32)
    return (bits[:, :half] >> 16) | (bits[:, half:] & jnp.uint32(0xFFFF0000))


def _unpack_pairs(p):
    return (pltpu.bitcast(p << 16, F32), pltpu.bitcast(p & jnp.uint32(0xFFFF0000), F32))


def _out_proj_kernel(x_ref, a_ref, b_ref, w_ref, g_ref, rw_ref, rb_ref,
                     x1_ref, hp_ref, route_ref, route_t_ref, blk_ref, carry):
    i = pl.program_id(0)
    tm = x_ref.shape[0]

    @pl.when(i == 0)
    def _():
        carry[...] = jnp.zeros_like(carry)

    w = w_ref[...]
    x1 = x_ref[...] + _dot(a_ref[...], w[:A_WIDTH]) + _dot(b_ref[...], w[A_WIDTH:])
    x1_ref[...] = x1
    ms = jnp.mean(x1 * x1, axis=-1, keepdims=True)
    h = x1 * lax.rsqrt(ms + RMS_EPS) * g_ref[...]
    hi = h.astype(BF16)
    hi_f = hi.astype(F32)
    hp_ref[...] = _pack_pairs(hi_f)
    lo = (h - hi_f).astype(BF16)
    rw = rw_ref[...]
    hw = _dot(hi, rw)
    logits = hw[:, :LANES] + hw[:, LANES:] + _dot(lo, rw[:, :LANES]) + rb_ref[...]

    lane = lax.broadcasted_iota(jnp.int32, (tm, LANES), 1).astype(F32)
    remaining = logits
    vals, hots = [], []
    for _ in range(TOP_K):
        m = jnp.max(remaining, axis=-1, keepdims=True)
        first = jnp.min(jnp.where(remaining == m, lane, float(LANES)), axis=-1, keepdims=True)
        hot = lane == first
        remaining = jnp.where(hot, -jnp.inf, remaining)
        vals.append(m)
        hots.append(hot)
    exps = [jnp.exp(v - vals[0]) for v in vals]
    den = exps[0] + exps[1] + exps[2] + exps[3]

    chosen = jnp.zeros((tm, LANES), F32)
    for hot in hots:
        chosen = chosen + jnp.where(hot, 1.0, 0.0)
    r = lax.broadcasted_iota(jnp.int32, (tm, tm), 0)
    c = lax.broadcasted_iota(jnp.int32, (tm, tm), 1)
    earlier = jnp.where(r > c, 1.0, 0.0).astype(BF16)
    before = carry[...] + _dot(earlier, chosen.astype(BF16))
    in_block = jnp.sum(chosen, axis=0, keepdims=True)
    carry[...] = carry[...] + in_block
    blk_ref[...] = jnp.broadcast_to(in_block, blk_ref.shape)

    route = jnp.zeros((tm, LANES), F32)
    for k in range(TOP_K):
        first = jnp.sum(jnp.where(hots[k], lane, 0.0), axis=-1, keepdims=True)
        rank = jnp.sum(jnp.where(hots[k], before, 0.0), axis=-1, keepdims=True)
        route = route + jnp.where(lane == float(k), first, 0.0)
        route = route + jnp.where(lane == float(TOP_K + k), rank, 0.0)
        route = route + jnp.where(lane == float(2 * TOP_K + k), exps[k] / den, 0.0)
    route_ref[...] = route
    route_t_ref[...] = route.T[:route_t_ref.shape[0], :]


def _out_proj(x2, out_a, out_b, w, g, rw, rb, tm):
    t, d = x2.shape
    return pl.pallas_call(
        _out_proj_kernel,
        grid=(t // tm,),
        in_specs=[pl.BlockSpec((tm, d), lambda i: (i, 0)),
                  pl.BlockSpec((tm, A_WIDTH), lambda i: (i, 0)),
                  pl.BlockSpec((tm, B_WIDTH), lambda i: (i, 0)),
                  pl.BlockSpec(w.shape, lambda i: (0, 0)),
                  pl.BlockSpec((1, d), lambda i: (0, 0)),
                  pl.BlockSpec(rw.shape, lambda i: (0, 0)),
                  pl.BlockSpec((1, LANES), lambda i: (0, 0))],
        out_specs=[pl.BlockSpec((tm, d), lambda i: (i, 0)),
                   pl.BlockSpec((tm, d // 2), lambda i: (i, 0)),
                   pl.BlockSpec((tm, LANES), lambda i: (i, 0)),
                   pl.BlockSpec((2 * TOP_K, tm), lambda i: (0, i)),
                   pl.BlockSpec((None, 8, LANES), lambda i: (i, 0, 0))],
        out_shape=[jax.ShapeDtypeStruct((t, d), F32),
                   jax.ShapeDtypeStruct((t, d // 2), jnp.uint32),
                   jax.ShapeDtypeStruct((t, LANES), F32),
                   jax.ShapeDtypeStruct((2 * TOP_K, t), F32),
                   jax.ShapeDtypeStruct((t // tm, 8, LANES), F32)],
        scratch_shapes=[pltpu.VMEM((1, LANES), F32)],
        compiler_params=_params(),
        name="out_proj",
    )(x2, out_a, out_b, w, g, rw, rb)


DISPATCH_ROWS = 1024


def _dispatch_kernel(pad_ref, end_ref, nb_ref, dest_hbm, h_ref, xs_hbm, dest_s0, dest_s1, zeros,
                     sem_i, sem_z, sem_o, *, bm):
    i = pl.program_id(0)
    steps = pl.num_programs(0)
    tm = h_ref.shape[0]
    n_blocks = xs_hbm.shape[0] // bm
    dest_s = (dest_s0, dest_s1)

    def idx_copy(step, slot):
        return pltpu.make_async_copy(dest_hbm.at[step], dest_s[slot], sem_i.at[slot])

    @pl.when(i == 0)
    def _():
        idx_copy(0, 0).start()
        zeros[...] = jnp.zeros_like(zeros)

        def zero_block(start):
            return pltpu.make_async_copy(zeros, xs_hbm.at[pl.ds(pl.multiple_of(start, bm), bm), :], sem_z)

        for e in range(N_EXPERTS):
            @pl.when(pad_ref[e] > 0)
            def _():
                zero_block(end_ref[e] - bm).start()

        def start_unused(b, carry):
            zero_block(b * bm).start()
            return carry

        def wait_unused(b, carry):
            zero_block(b * bm).wait()
            return carry

        lax.fori_loop(nb_ref[0], n_blocks, start_unused, 0)
        for e in range(N_EXPERTS):
            @pl.when(pad_ref[e] > 0)
            def _():
                zero_block(end_ref[e] - bm).wait()
        lax.fori_loop(nb_ref[0], n_blocks, wait_unused, 0)

    for slot in range(2):
        @pl.when(i % 2 == slot)
        def _(slot=slot):
            @pl.when(i + 1 < steps)
            def _():
                idx_copy(i + 1, 1 - slot).start()

            idx_copy(i, slot).wait()

            def body(t, carry):
                for k in range(TOP_K):
                    pltpu.make_async_copy(h_ref.at[pl.ds(t, 1), :],
                                          xs_hbm.at[pl.ds(dest_s[slot][k * tm + t], 1), :],
                                          sem_o).start(priority=k % 2)
                return carry

            lax.fori_loop(0, tm, body, 0, unroll=8)
    for _ in range(TOP_K):
        pltpu.make_async_copy(h_ref, xs_hbm.at[pl.ds(0, tm), :], sem_o).wait()


def _dispatch(padded, pad_end, n_used, dest, hp, n_slots, tm, bm):
    t, half = hp.shape
    grid_spec = pltpu.PrefetchScalarGridSpec(
        num_scalar_prefetch=3,
        grid=(t // tm,),
        in_specs=[pl.BlockSpec(memory_space=pl.ANY),
                  pl.BlockSpec((tm, half), lambda i, p, e, n: (i, 0))],
        out_specs=pl.BlockSpec(memory_space=pl.ANY),
        scratch_shapes=[pltpu.SMEM((tm * TOP_K,), jnp.int32),
                        pltpu.SMEM((tm * TOP_K,), jnp.int32),
                        pltpu.VMEM((bm, half), jnp.uint32),
                        pltpu.SemaphoreType.DMA((2,)),
                        pltpu.SemaphoreType.DMA(()),
                        pltpu.SemaphoreType.DMA(())],
    )
    return pl.pallas_call(
        functools.partial(_dispatch_kernel, bm=bm),
        grid_spec=grid_spec,
        out_shape=jax.ShapeDtypeStruct((n_slots, half), jnp.uint32),
        compiler_params=_params(),
        name="dispatch",
    )(padded, pad_end, n_used, dest, hp)


EXPERT_SUB_ROWS = 256


def _expert_ffn_kernel(be_ref, nb_ref, next_ref, x_ref, wgu_hbm, bgu_ref, wd_hbm, bd_ref, y_ref,
                       wgu_f, wd_f, wgu_s, wd_s, sem_w):
    i = pl.program_id(0)
    d_ff = wd_f.shape[0]
    half = x_ref.shape[1]

    def weight_copies(e):
        return (pltpu.make_async_copy(wgu_hbm.at[e], wgu_f, sem_w.at[0]),
                pltpu.make_async_copy(wd_hbm.at[e], wd_f, sem_w.at[1]))

    @pl.when(jnp.logical_and(i == 0, nb_ref[0] > 0))
    def _():
        for copy in weight_copies(be_ref[0]):
            copy.start()

    @pl.when(i >= nb_ref[0])
    def _():
        y_ref[...] = jnp.zeros_like(y_ref)

    @pl.when(i < nb_ref[0])
    def _():
        changed = jnp.logical_or(i == 0, be_ref[i] != be_ref[jnp.maximum(i - 1, 0)])

        @pl.when(changed)
        def _():
            for copy in weight_copies(be_ref[i]):
                copy.wait()
            wgu_s[...] = wgu_f[...].astype(BF16)
            wd_s[...] = wd_f[...].astype(BF16)

            @pl.when(next_ref[i] >= 0)
            def _():
                for copy in weight_copies(next_ref[i]):
                    copy.start()

        for r0 in range(0, x_ref.shape[0], EXPERT_SUB_ROWS):
            rows = slice(r0, r0 + EXPERT_SUB_ROWS)
            x_lo, x_hi = _unpack_pairs(x_ref[rows, :])
            gu = (_dot(x_lo.astype(BF16), wgu_s[:half, :]) + _dot(x_hi.astype(BF16), wgu_s[half:, :])
                  + bgu_ref[...])
            gate = jnp.minimum(gu[:, :d_ff], SWIGLU_LIMIT)
            up = jnp.clip(gu[:, d_ff:], -SWIGLU_LIMIT, SWIGLU_LIMIT)
            glu = gate * jax.nn.sigmoid(SWIGLU_ALPHA * gate)
            act = ((up + 1.0) * glu).astype(BF16)
            y = _dot(act, wd_s[...]) + bd_ref[...]
            y_ref[rows, :] = _pack_pairs(y.astype(BF16).astype(F32))


def _expert_ffn(block_expert, n_used, next_expert, xs, w_gate_up, b_gate_up, w_down, b_down, bm):
    n_slots, half = xs.shape
    n_e, d, two_ff = w_gate_up.shape
    d_ff = two_ff // 2
    n_blocks = n_slots // bm

    def used(i, nb):
        return jnp.minimum(i, jnp.maximum(nb[0] - 1, 0))

    grid_spec = pltpu.PrefetchScalarGridSpec(
        num_scalar_prefetch=3,
        grid=(n_blocks,),
        in_specs=[pl.BlockSpec((bm, half), lambda i, be, nb, nx: (used(i, nb), 0)),
                  pl.BlockSpec(memory_space=pl.ANY),
                  pl.BlockSpec((None, 1, two_ff), lambda i, be, nb, nx: (be[i], 0, 0)),
                  pl.BlockSpec(memory_space=pl.ANY),
                  pl.BlockSpec((None, 1, d), lambda i, be, nb, nx: (be[i], 0, 0))],
        out_specs=pl.BlockSpec((bm, half), lambda i, be, nb, nx: (i, 0)),
        scratch_shapes=[pltpu.VMEM((d, two_ff), F32), pltpu.VMEM((d_ff, d), F32),
                        pltpu.VMEM((d, two_ff), BF16), pltpu.VMEM((d_ff, d), BF16),
                        pltpu.SemaphoreType.DMA((2,))],
    )
    return pl.pallas_call(
        _expert_ffn_kernel,
        grid_spec=grid_spec,
        out_shape=jax.ShapeDtypeStruct((n_slots, half), jnp.uint32),
        compiler_params=_params(),
        name="expert_ffn",
    )(block_expert, n_used, next_expert, xs, w_gate_up, b_gate_up.reshape(n_e, 1, two_ff),
      w_down, b_down.reshape(n_e, 1, d))


COMBINE_LOCAL_ROWS = 2560
SEG_ALIGN = 8


def _pieces(limit):
    sizes = []
    size = 1 << (limit.bit_length() - 1)
    while size >= SEG_ALIGN:
        sizes.append(size)
        size //= 2
    return sizes


def _combine_kernel(src_ref, len_ref, dst_ref, tot_ref, route_ref, local_ref, x1_ref, ys_hbm, o_ref, ybuf,
                    sem_g):
    j = pl.program_id(0)
    blocks = pl.num_programs(0) - 1
    tm = x1_ref.shape[0]
    half = x1_ref.shape[1] // 2
    local_rows = ybuf.shape[1]

    @pl.when(j == 0)
    def _():
        ybuf[...] = jnp.zeros_like(ybuf)

    @pl.when(j < blocks)
    def _():
        for s in range(2):
            @pl.when(j % 2 == s)
            def _(s=s):
                for e in range(N_EXPERTS):
                    seg = j * N_EXPERTS + e
                    n = len_ref[seg]
                    src = src_ref[seg]
                    dst = dst_ref[seg]
                    for size in _pieces(tm + 2 * SEG_ALIGN):
                        @pl.when((n & size) != 0)
                        def _(size=size, src=src, dst=dst):
                            pltpu.make_async_copy(
                                ys_hbm.at[pl.ds(pl.multiple_of(src, SEG_ALIGN), size), :],
                                ybuf.at[s, pl.ds(pl.multiple_of(dst, SEG_ALIGN), size), :],
                                sem_g.at[s]).start()
                        step = jnp.where((n & size) != 0, size, 0)
                        src = src + step
                        dst = dst + step

    @pl.when(j >= 1)
    def _():
        slot = (j - 1) % 2
        total = tot_ref[j - 1]
        for size in _pieces(local_rows):
            @pl.when((total & size) != 0)
            def _(size=size):
                pltpu.make_async_copy(ys_hbm.at[pl.ds(0, size), :], ybuf.at[slot, pl.ds(0, size), :],
                                      sem_g.at[slot]).wait()

        route = route_ref[...]
        local_t = local_ref[...]
        local = jnp.concatenate(
            [local_t, jnp.zeros((LANES - local_t.shape[0], tm), F32)], axis=0).T
        col = lax.broadcasted_iota(jnp.int32, (tm, local_rows), 1).astype(F32)
        g = jnp.zeros((tm, local_rows), F32)
        for k in range(TOP_K):
            pos = local[:, k: k + 1]
            gate = route[:, 2 * TOP_K + k: 2 * TOP_K + k + 1]
            g = jnp.where(col == pos, gate, g)
        g = g.astype(BF16)
        lo, hi = _unpack_pairs(ybuf[slot])
        o_ref[:, :half] = x1_ref[:, :half] + _dot(g, lo.astype(BF16))
        o_ref[:, half:] = x1_ref[:, half:] + _dot(g, hi.astype(BF16))


def _combine(tables, route, local_t, x1, ys, tm, local_rows):
    t, d = x1.shape
    half = d // 2

    def summed(j, *_):
        return (jnp.maximum(j - 1, 0), 0)

    grid_spec = pltpu.PrefetchScalarGridSpec(
        num_scalar_prefetch=4,
        grid=(t // tm + 1,),
        in_specs=[pl.BlockSpec((tm, LANES), summed),
                  pl.BlockSpec((local_t.shape[0], tm), lambda j, *_: (0, jnp.maximum(j - 1, 0))),
                  pl.BlockSpec((tm, d), summed),
                  pl.BlockSpec(memory_space=pl.ANY)],
        out_specs=pl.BlockSpec((tm, d), summed),
        scratch_shapes=[pltpu.VMEM((2, local_rows, half), jnp.uint32),
                        pltpu.SemaphoreType.DMA((2,))],
    )
    return pl.pallas_call(
        _combine_kernel,
        grid_spec=grid_spec,
        out_shape=jax.ShapeDtypeStruct((t, d), F32),
        compiler_params=_params(),
        name="combine",
    )(*tables, route, local_t, x1, ys)


def _per_expert(table, idx):
    out = jnp.zeros(idx.shape, table.dtype)
    for e in range(N_EXPERTS):
        out = jnp.where(idx == e, table[:, e][None, :, None], out)
    return out


def _combine_tables(idx, rank, n, pad_start, tm, local_rows):
    before = jnp.cumsum(n, axis=0) - n
    first = pad_start[None, :] + before
    src = first // SEG_ALIGN * SEG_ALIGN
    length = jnp.where(n > 0, (first + n + SEG_ALIGN - 1) // SEG_ALIGN * SEG_ALIGN - src, 0)
    dst = jnp.cumsum(length, axis=1) - length
    total = jnp.sum(length, axis=1)
    assert local_rows >= tm * TOP_K + N_EXPERTS * 2 * (SEG_ALIGN - 1)
    local = (_per_expert(dst + first - src - before, idx) + rank).reshape(TOP_K, -1).astype(F32)
    tables = (src.reshape(-1).astype(jnp.int32), length.reshape(-1).astype(jnp.int32),
              dst.reshape(-1).astype(jnp.int32), total.astype(jnp.int32))
    return tables, local


def _slot_layout(route_t, n, tm, bm):
    t = route_t.shape[1]
    blocks = t // tm
    idx = route_t[:TOP_K].astype(jnp.int32).reshape(TOP_K, blocks, tm)
    rank = route_t[TOP_K:].astype(jnp.int32).reshape(TOP_K, blocks, tm)
    counts = jnp.sum(n, axis=0)
    padded = (counts + bm - 1) // bm * bm
    pad_end = jnp.cumsum(padded).astype(jnp.int32)
    pad_start = pad_end - padded
    dest = _per_expert(jnp.broadcast_to(pad_start[None, :], n.shape), idx) + rank
    dest = dest.reshape(TOP_K, t // DISPATCH_ROWS, DISPATCH_ROWS).transpose(1, 0, 2)
    dest = dest.reshape(t // DISPATCH_ROWS, TOP_K * DISPATCH_ROWS)
    n_blocks = -(-t * TOP_K // bm) + N_EXPERTS
    block_start = jnp.arange(n_blocks, dtype=jnp.int32) * bm
    block_expert = jnp.minimum(jnp.sum(pad_end[None, :] <= block_start[:, None], axis=1),
                               N_EXPERTS - 1).astype(jnp.int32)
    n_used = (pad_end[-1:] // bm).astype(jnp.int32)
    following = jnp.take(pad_end, block_expert) // bm
    next_expert = jnp.where(following < n_used[0],
                            jnp.take(block_expert, jnp.minimum(following, n_blocks - 1)), -1).astype(jnp.int32)
    return idx, rank, padded, pad_end, dest, block_expert, next_expert, n_used, n_blocks * bm


def _rot_cols(w):
    half = w.shape[-1] // 2
    return jnp.concatenate([-w[..., half:], w[..., :half]], axis=-1)


def _swap_halves(g):
    half = g.shape[-1] // 2
    return jnp.concatenate([g[..., half:], g[..., :half]], axis=-1)


def kernel(x, attn_norm_g, w_in, a_q_norm_g, a_k_norm_g, rel_bias, q_a_norm_g, w_q_b, kv_a_norm_g,
           w_kv_b, b_q_norm_g, b_k_norm_g, w_out, ffn_norm_g, router_w, router_b, w_gate_up,
           b_gate_up, w_down, b_down):
    batch, seq, d = x.shape
    t = batch * seq
    depth = w_in.shape[0]
    tm = 512
    expert_bm = 512

    pos = jnp.arange(seq, dtype=F32)
    inv_freq = ROPE_THETA ** (-jnp.arange(0, QK_ROPE_DIM, 2, dtype=F32) / QK_ROPE_DIM)
    ang = pos[:, None] * inv_freq[None, :]
    cos, sin = jnp.cos(ang), jnp.sin(ang)
    cs = jnp.concatenate([cos, cos, sin, sin], axis=-1)

    row = jnp.arange(LANES)[:, None] // A_HEAD_DIM
    col = jnp.arange(LANES)[None, :] // A_HEAD_DIM
    head_block_ones = (row == col).astype(BF16)
    band_bias = _band_bias(rel_bias)

    x2 = x.reshape(t, d)
    for layer in range(depth):
        kpe_off = 3 * A_WIDTH + Q_LORA_RANK + KV_LORA_RANK
        w_kpe = w_in[layer][:, kpe_off:]
        w_in_r = jnp.concatenate([w_in[layer][:, :kpe_off], w_kpe, _rot_cols(w_kpe)], axis=1).astype(BF16)

        wq = w_q_b[layer].reshape(Q_LORA_RANK, B_HEADS, QK_HEAD_DIM)
        wq_rope = wq[..., QK_NOPE_DIM:]
        wq_r = jnp.concatenate([wq, _rot_cols(wq_rope)], axis=-1).reshape(Q_LORA_RANK, -1).astype(BF16)
        wkv = w_kv_b[layer].reshape(KV_LORA_RANK, B_HEADS, QK_NOPE_DIM + V_HEAD_DIM)
        wkv_r = jnp.concatenate([wkv[..., :QK_NOPE_DIM].reshape(KV_LORA_RANK, -1),
                                 wkv[..., QK_NOPE_DIM:].reshape(KV_LORA_RANK, -1)], axis=1).astype(BF16)
        gc = jnp.concatenate([q_a_norm_g[layer], kv_a_norm_g[layer]])[None, :]
        gq, gk = b_q_norm_g[layer], b_k_norm_g[layer]

        def rope_gain(gr):
            return jnp.concatenate([gr, _swap_halves(gr)])

        gqk = jnp.stack([gq[:QK_NOPE_DIM], rope_gain(gq[QK_NOPE_DIM:]),
                         gk[:QK_NOPE_DIM], rope_gain(gk[QK_NOPE_DIM:])])
        g_a = jnp.stack([jnp.tile(a_q_norm_g[layer], 2), jnp.tile(a_k_norm_g[layer], 2)])
        g_a = jnp.broadcast_to(g_a[None], (A_WIDTH // LANES, 2, LANES))

        proj, qb, kb, vb = _in_proj(x2, attn_norm_g[layer][None, :], w_in_r, cs, wq_r, wkv_r, gc, gqk, seq, tm)
        out_a, out_b = _attention(proj, head_block_ones, g_a, band_bias, qb, kb, vb, batch, seq, 512)

        rw = jnp.pad(router_w[layer], ((0, 0), (0, LANES - N_EXPERTS)))
        rw_hi = rw.astype(BF16)
        rw_lo = (rw - rw_hi.astype(F32)).astype(BF16)
        rb = jnp.pad(router_b[layer], (0, LANES - N_EXPERTS), constant_values=-jnp.inf)[None, :]
        x1, hp, route, route_t, in_block = _out_proj(
            x2, out_a, out_b, w_out[layer].astype(BF16), ffn_norm_g[layer][None, :],
            jnp.concatenate([rw_hi, rw_lo], axis=1), rb, tm)

        n = in_block[:, 0, :N_EXPERTS].astype(jnp.int32)
        idx, rank, padded, pad_end, dest, block_expert, next_expert, n_used, n_slots = _slot_layout(
            route_t, n, tm, expert_bm)
        xs = _dispatch(padded, pad_end, n_used, dest, hp, n_slots, DISPATCH_ROWS, expert_bm)
        ys = _expert_ffn(block_expert, n_used, next_expert, xs, w_gate_up[layer], b_gate_up[layer],
                         w_down[layer], b_down[layer], expert_bm)
        tables, local_t = _combine_tables(idx, rank, n, pad_end - padded, tm, COMBINE_LOCAL_ROWS)
        x2 = _combine(tables, route, local_t, x1, ys, tm, COMBINE_LOCAL_ROWS)
    return x2.reshape(batch, seq, d)
```

```python
import functools
import math

import jax
import jax.numpy as jnp
from jax import lax
from jax.experimental import pallas as pl
from jax.experimental.pallas import tpu as pltpu

A_HEADS = 8
A_HEAD_DIM = 64
A_WIDTH = A_HEADS * A_HEAD_DIM
DILATED_PATTERNS = ((128, 1), (512, 4), (2048, 16))
BAND_BLOCK = 128

B_HEADS = 4
QK_NOPE_DIM = 128
QK_ROPE_DIM = 64
QK_HEAD_DIM = QK_NOPE_DIM + QK_ROPE_DIM
V_HEAD_DIM = 128
Q_LORA_RANK = 256
KV_LORA_RANK = 256
B_WIDTH = B_HEADS * V_HEAD_DIM
ROPE_THETA = 10000.0

NUM_BUCKETS = 32
MAX_DISTANCE = 2048

N_EXPERTS = 32
TOP_K = 4
SWIGLU_LIMIT = 7.0
SWIGLU_ALPHA = 1.702
RMS_EPS = 1e-6

LANES = 128
MASK_VALUE = -1e30
LOG2E = math.log2(math.e)
PROJ_WIDTH = 3 * A_WIDTH + Q_LORA_RANK + KV_LORA_RANK + 2 * QK_ROPE_DIM
VMEM_LIMIT = 56 * 1024 * 1024

F32 = jnp.float32
BF16 = jnp.bfloat16


def _dot(a, b):
    return jnp.dot(a, b, preferred_element_type=F32)


def _dot_nt(a, b):
    return lax.dot_general(a, b, (((1,), (1,)), ((), ())), preferred_element_type=F32)


def _group_sum(x, m):
    return _dot(x.astype(BF16), m)


def _params(n_parallel=1):
    return pltpu.CompilerParams(
        dimension_semantics=("arbitrary",) * n_parallel, vmem_limit_bytes=VMEM_LIMIT)


IN_PROJ_SUB_ROWS = 256


def _mla_qkv(c, kper, cs, wq_ref, wkv_ref, gc, gqk, q_ref, k_ref, v_ref, rows, q_scale):
    def lora_norm(z, g):
        ms = jnp.mean(z * z, axis=-1, keepdims=True)
        return (z * lax.rsqrt(ms + RMS_EPS) * g).astype(BF16)

    cq = lora_norm(c[:, :Q_LORA_RANK], gc[:, :Q_LORA_RANK])
    ckv = lora_norm(c[:, Q_LORA_RANK:], gc[:, Q_LORA_RANK:])
    qb = _dot(cq, wq_ref[...])
    kvb = _dot(ckv, wkv_ref[...])
    n = c.shape[0]

    row = lax.broadcasted_iota(jnp.int32, (LANES, LANES), 0)
    ones_all = jnp.ones((LANES, LANES), BF16)
    ones_lo = jnp.where(row < QK_ROPE_DIM, 1.0, 0.0).astype(BF16)
    lane = lax.broadcasted_iota(jnp.int32, (n, LANES), 1)

    def rope(z, g_row):
        t = z * (g_row * cs)
        return jnp.where(lane < QK_ROPE_DIM, t + pltpu.roll(t, QK_ROPE_DIM, 1), 0.0)

    k_rope = rope(kper, gqk[3:4])
    kpe_ss = _group_sum(kper * kper, ones_lo)
    for h in range(B_HEADS):
        qn = qb[:, 2 * LANES * h: 2 * LANES * h + LANES]
        qr = qb[:, 2 * LANES * h + LANES: 2 * LANES * (h + 1)]
        ss = _group_sum(qn * qn, ones_all) + _group_sum(qr * qr, ones_lo)
        rs = lax.rsqrt(ss * (1.0 / QK_HEAD_DIM) + RMS_EPS) * q_scale
        q_ref[h, rows, :LANES] = (qn * gqk[0:1] * rs).astype(BF16)
        q_ref[h, rows, LANES:] = (rope(qr, gqk[1:2]) * rs).astype(BF16)
        kn = kvb[:, LANES * h: LANES * (h + 1)]
        ssk = _group_sum(kn * kn, ones_all) + kpe_ss
        rsk = lax.rsqrt(ssk * (1.0 / QK_HEAD_DIM) + RMS_EPS)
        k_ref[h, rows, :LANES] = (kn * gqk[2:3] * rsk).astype(BF16)
        k_ref[h, rows, LANES:] = (k_rope * rsk).astype(BF16)
    v_ref[rows, :] = kvb[:, B_HEADS * QK_NOPE_DIM:].astype(BF16)


def _in_proj_kernel(x_ref, g_ref, w_ref, cs_ref, wq_ref, wkv_ref, gc_ref, gqk_ref,
                    a_ref, q_ref, k_ref, v_ref, *, q_scale):
    a_width = a_ref.shape[1]
    c_width = Q_LORA_RANK + KV_LORA_RANK
    for r0 in range(0, x_ref.shape[0], IN_PROJ_SUB_ROWS):
        rows = slice(r0, r0 + IN_PROJ_SUB_ROWS)
        x = x_ref[rows, :]
        ms = jnp.mean(x * x, axis=-1, keepdims=True)
        h = (x * lax.rsqrt(ms + RMS_EPS) * g_ref[...]).astype(BF16)
        proj = _dot(h, w_ref[...])
        a_ref[rows, :] = proj[:, :a_width]
        _mla_qkv(proj[:, a_width:a_width + c_width], proj[:, a_width + c_width:], cs_ref[rows, :],
                 wq_ref, wkv_ref, gc_ref[...], gqk_ref[...], q_ref, k_ref, v_ref, rows, q_scale)


def _in_proj(x2, g, w, cs, wq, wkv, gc, gqk, seq, tm):
    t, d = x2.shape
    a_width = 3 * A_WIDTH
    pos_blocks = seq // tm
    kernel = functools.partial(_in_proj_kernel, q_scale=QK_HEAD_DIM ** -0.5 * LOG2E)
    return pl.pallas_call(
        kernel,
        grid=(t // tm,),
        in_specs=[pl.BlockSpec((tm, d), lambda i: (i, 0)),
                  pl.BlockSpec((1, d), lambda i: (0, 0)),
                  pl.BlockSpec(w.shape, lambda i: (0, 0)),
                  pl.BlockSpec((tm, LANES), lambda i: (i % pos_blocks, 0)),
                  pl.BlockSpec(wq.shape, lambda i: (0, 0)),
                  pl.BlockSpec(wkv.shape, lambda i: (0, 0)),
                  pl.BlockSpec(gc.shape, lambda i: (0, 0)),
                  pl.BlockSpec(gqk.shape, lambda i: (0, 0))],
        out_specs=[pl.BlockSpec((tm, a_width), lambda i: (i, 0)),
                   pl.BlockSpec((B_HEADS, tm, 2 * LANES), lambda i: (0, i, 0)),
                   pl.BlockSpec((B_HEADS, tm, 2 * LANES), lambda i: (0, i, 0)),
                   pl.BlockSpec((tm, B_WIDTH), lambda i: (i, 0))],
        out_shape=[jax.ShapeDtypeStruct((t, a_width), F32),
                   jax.ShapeDtypeStruct((B_HEADS, t, 2 * LANES), BF16),
                   jax.ShapeDtypeStruct((B_HEADS, t, 2 * LANES), BF16),
                   jax.ShapeDtypeStruct((t, B_WIDTH), BF16)],
        compiler_params=_params(),
        name="in_proj",
    )(x2, g, w, cs, wq, wkv, gc, gqk)


def _mla_attention_kernel(q_ref, k_ref, v_ref, o_ref, *, tq):
    seq = q_ref.shape[0]
    row = lax.broadcasted_iota(jnp.int32, (tq, tq), 0)
    col = lax.broadcasted_iota(jnp.int32, (tq, tq), 1)
    diag_mask = jnp.where(col <= row, 0.0, MASK_VALUE)
    v_ext = jnp.concatenate([v_ref[...], jnp.ones(v_ref.shape, v_ref.dtype)], axis=1)
    dv = v_ref.shape[1]
    for i in range(seq // tq):
        q = q_ref[i * tq:(i + 1) * tq, :]
        s_diag = _dot_nt(q, k_ref[i * tq:(i + 1) * tq, :]) + diag_mask
        m = jnp.max(s_diag, axis=-1, keepdims=True)
        if i > 0:
            s_past = _dot_nt(q, k_ref[:i * tq, :])
            m = jnp.maximum(m, jnp.max(s_past, axis=-1, keepdims=True))
        o = _dot(jnp.exp2(s_diag - m).astype(BF16), v_ext[i * tq:(i + 1) * tq, :])
        if i > 0:
            o = o + _dot(jnp.exp2(s_past - m).astype(BF16), v_ext[:i * tq, :])
        o_ref[i * tq:(i + 1) * tq, :] = (o[:, :dv] / o[:, dv:]).astype(o_ref.dtype)


def _mla_attention(qb, kb, vb, batch, seq, tq):
    t = vb.shape[0]
    kernel = functools.partial(_mla_attention_kernel, tq=tq)
    return pl.pallas_call(
        kernel,
        grid=(batch, B_HEADS),
        in_specs=[pl.BlockSpec((None, seq, 2 * LANES), lambda b, h: (h, b, 0)),
                  pl.BlockSpec((None, seq, 2 * LANES), lambda b, h: (h, b, 0)),
                  pl.BlockSpec((seq, V_HEAD_DIM), lambda b, h: (b, h))],
        out_specs=pl.BlockSpec((seq, V_HEAD_DIM), lambda b, h: (b, h)),
        out_shape=jax.ShapeDtypeStruct((t, B_WIDTH), BF16),
        compiler_params=_params(2),
        name="mla_attention",
    )(qb, kb, vb)


def _dilated_kernel(q_ref, k_ref, v_ref, bd_ref, g_ref, bias_ref, o_ref, qs, ks, acc_o, acc_l, acc_m,
                    *, unroll):
    seq = q_ref.shape[0]
    n_tiles = seq // BAND_BLOCK
    low = lax.broadcasted_iota(jnp.int32, (BAND_BLOCK, LANES), 1) < A_HEAD_DIM
    bd = bd_ref[...]
    g = g_ref[...]

    def head_norm(z, g_row):
        ss = _group_sum(z * z, bd)
        return z * lax.rsqrt(ss * (1.0 / A_HEAD_DIM) + RMS_EPS) * g_row

    qs[...] = head_norm(q_ref[...], g[0:1]) * (A_HEAD_DIM ** -0.5 * LOG2E)
    ks[...] = head_norm(k_ref[...], g[1:2])
    ones = jnp.ones((2 * BAND_BLOCK, LANES), BF16)

    for p, (window, dil) in enumerate(DILATED_PATTERNS):
        blocks_per_class = n_tiles // dil
        span = BAND_BLOCK * dil

        def rows_at(start, dil=dil):
            if dil == 1:
                return pl.ds(start, BAND_BLOCK)
            return pl.ds(start, BAND_BLOCK, stride=dil)

        def tile(i, carry, p=p, blocks_per_class=blocks_per_class, span=span, rows_at=rows_at):
            r = i // blocks_per_class
            j = i % blocks_per_class
            cur = rows_at(r + j * span)
            prev = rows_at(r + jnp.maximum(j - 1, 0) * span)
            first = jnp.where(j == 0, 1, 0)
            q = qs[cur, :]
            q2 = jnp.concatenate([jnp.where(low, q, 0.0), jnp.where(low, 0.0, q)], axis=0).astype(BF16)
            k_band = jnp.concatenate([ks[prev, :], ks[cur, :]], axis=0).astype(BF16)
            s = _dot_nt(q2, k_band) + bias_ref[p, first]
            m = jnp.max(s, axis=-1, keepdims=True)
            pr = jnp.exp2(s - m).astype(BF16)
            v_band = jnp.concatenate([v_ref[prev, :], v_ref[cur, :]], axis=0).astype(BF16)
            o = _dot(pr, jnp.concatenate([v_band, ones], axis=1))
            top, bot = o[:BAND_BLOCK], o[BAND_BLOCK:]
            acc_o[p, cur, :] = jnp.where(low, top[:, :LANES], bot[:, :LANES])
            acc_l[p, cur, :] = jnp.where(low, top[:, LANES:], bot[:, LANES:])
            acc_m[p, cur, :] = jnp.where(low, m[:BAND_BLOCK], m[BAND_BLOCK:])
            return carry

        lax.fori_loop(0, n_tiles, tile, 0, unroll=unroll)

    m_all = jnp.maximum(jnp.maximum(acc_m[0], acc_m[1]), acc_m[2])
    num = jnp.zeros((seq, LANES), F32)
    den = jnp.zeros((seq, LANES), F32)
    for p in range(len(DILATED_PATTERNS)):
        w = jnp.exp2(acc_m[p] - m_all)
        num = num + w * acc_o[p]
        den = den + w * acc_l[p]
    o_ref[...] = (num / den).astype(o_ref.dtype)


def _dilated_attention(proj, bd, g, bias, batch, seq):
    t = proj.shape[0]
    pairs = A_WIDTH // LANES
    n_pat = len(DILATED_PATTERNS)
    return pl.pallas_call(
        functools.partial(_dilated_kernel, unroll=16),
        grid=(batch, pairs),
        in_specs=[pl.BlockSpec((seq, LANES), lambda b, c: (b, c)),
                  pl.BlockSpec((seq, LANES), lambda b, c: (b, pairs + c)),
                  pl.BlockSpec((seq, LANES), lambda b, c: (b, 2 * pairs + c)),
                  pl.BlockSpec((LANES, LANES), lambda b, c: (0, 0)),
                  pl.BlockSpec((None, 2, LANES), lambda b, c: (c, 0, 0)),
                  pl.BlockSpec((None, n_pat, 2, 2 * BAND_BLOCK, 2 * BAND_BLOCK),
                               lambda b, c: (c, 0, 0, 0, 0))],
        out_specs=pl.BlockSpec((seq, LANES), lambda b, c: (b, c)),
        out_shape=jax.ShapeDtypeStruct((t, A_WIDTH), BF16),
        scratch_shapes=[pltpu.VMEM((seq, LANES), F32)] * 2
                       + [pltpu.VMEM((n_pat, seq, LANES), F32)] * 3,
        compiler_params=_params(2),
        name="dilated_attention",
    )(proj, proj, proj, bd, g, bias)


def _attention_kernel(qa_ref, ka_ref, va_ref, bd_ref, g_ref, bias_ref, qb_ref, kb_ref, vb_ref,
                      oa_ref, ob_ref, qs, ks, acc_o, acc_l, acc_m, *, unroll, tq):
    _mla_attention_kernel(qb_ref, kb_ref, vb_ref, ob_ref, tq=tq)
    _dilated_kernel(qa_ref, ka_ref, va_ref, bd_ref, g_ref, bias_ref, oa_ref, qs, ks, acc_o, acc_l, acc_m,
                    unroll=unroll)


def _attention(proj, bd, g, bias, qb, kb, vb, batch, seq, tq):
    t = proj.shape[0]
    pairs = A_WIDTH // LANES
    assert pairs == B_HEADS
    n_pat = len(DILATED_PATTERNS)
    return pl.pallas_call(
        functools.partial(_attention_kernel, unroll=16, tq=tq),
        grid=(batch, pairs),
        in_specs=[pl.BlockSpec((seq, LANES), lambda b, c: (b, c)),
                  pl.BlockSpec((seq, LANES), lambda b, c: (b, pairs + c)),
                  pl.BlockSpec((seq, LANES), lambda b, c: (b, 2 * pairs + c)),
                  pl.BlockSpec((LANES, LANES), lambda b, c: (0, 0)),
                  pl.BlockSpec((None, 2, LANES), lambda b, c: (c, 0, 0)),
                  pl.BlockSpec((None, n_pat, 2, 2 * BAND_BLOCK, 2 * BAND_BLOCK),
                               lambda b, c: (c, 0, 0, 0, 0)),
                  pl.BlockSpec((None, seq, 2 * LANES), lambda b, h: (h, b, 0)),
                  pl.BlockSpec((None, seq, 2 * LANES), lambda b, h: (h, b, 0)),
                  pl.BlockSpec((seq, V_HEAD_DIM), lambda b, h: (b, h))],
        out_specs=[pl.BlockSpec((seq, LANES), lambda b, c: (b, c)),
                   pl.BlockSpec((seq, V_HEAD_DIM), lambda b, h: (b, h))],
        out_shape=[jax.ShapeDtypeStruct((t, A_WIDTH), BF16),
                   jax.ShapeDtypeStruct((t, B_WIDTH), BF16)],
        scratch_shapes=[pltpu.VMEM((seq, LANES), F32)] * 2
                       + [pltpu.VMEM((n_pat, seq, LANES), F32)] * 3,
        compiler_params=_params(2),
        name="attention",
    )(proj, proj, proj, bd, g, bias, qb, kb, vb)


def _t5_bucket(dist):
    max_exact = NUM_BUCKETS // 2
    df = jnp.maximum(dist, 1).astype(F32)
    log_bucket = max_exact + (jnp.log(df / max_exact) / math.log(MAX_DISTANCE / max_exact)
                              * (NUM_BUCKETS - max_exact)).astype(jnp.int32)
    log_bucket = jnp.minimum(log_bucket, NUM_BUCKETS - 1)
    return jnp.where(dist < max_exact, dist, log_bucket)


def _band_bias(rel_bias):
    n = BAND_BLOCK
    qi = jnp.arange(n)[:, None]
    kj = jnp.arange(n)[None, :]
    buckets = jnp.arange(NUM_BUCKETS)
    tables = []
    for window, dil in DILATED_PATTERNS:
        steps = window // dil
        halves = []
        for back in (qi - kj + n, qi - kj):
            onehot = (_t5_bucket(jnp.maximum(back, 0) * dil)[:, :, None] == buckets).astype(F32)
            vals = jnp.einsum('qkb,bh->hqk', onehot, rel_bias.astype(F32),
                              precision=lax.Precision.HIGHEST) * LOG2E
            halves.append(jnp.where(((back >= 0) & (back <= steps))[None], vals, MASK_VALUE))
        prev, cur = halves
        normal = jnp.concatenate([prev, cur], axis=-1)
        first = jnp.concatenate([jnp.full_like(prev, MASK_VALUE), cur], axis=-1)
        tables.append(jnp.stack([normal, first], axis=1))
    tab = jnp.stack(tables, axis=1)
    tab = tab.reshape(A_HEADS // 2, 2, len(DILATED_PATTERNS), 2, n, 2 * n)
    return tab.transpose(0, 2, 3, 1, 4, 5).reshape(A_HEADS // 2, len(DILATED_PATTERNS), 2, 2 * n, 2 * n)


def _pack_pairs(v):
    half = v.shape[1] // 2
    bits = pltpu.bitcast(v, jnp.uint32)
    return (bits[:, :half] >> 16) | (bits[:, half:] & jnp.uint32(0xFFFF0000))


def _unpack_pairs(p):
    return (pltpu.bitcast(p << 16, F32), pltpu.bitcast(p & jnp.uint32(0xFFFF0000), F32))


def _out_proj_kernel(x_ref, a_ref, b_ref, w_ref, g_ref, rw_ref, rb_ref,
                     x1_ref, hp_ref, route_ref, route_t_ref, blk_ref, carry):
    i = pl.program_id(0)
    tm = x_ref.shape[0]

    @pl.when(i == 0)
    def _():
        carry[...] = jnp.zeros_like(carry)

    w = w_ref[...]
    x1 = x_ref[...] + _dot(a_ref[...], w[:A_WIDTH]) + _dot(b_ref[...], w[A_WIDTH:])
    x1_ref[...] = x1
    ms = jnp.mean(x1 * x1, axis=-1, keepdims=True)
    h = x1 * lax.rsqrt(ms + RMS_EPS) * g_ref[...]
    hi = h.astype(BF16)
    hi_f = hi.astype(F32)
    hp_ref[...] = _pack_pairs(hi_f)
    lo = (h - hi_f).astype(BF16)
    rw = rw_ref[...]
    hw = _dot(hi, rw)
    logits = hw[:, :LANES] + hw[:, LANES:] + _dot(lo, rw[:, :LANES]) + rb_ref[...]

    lane = lax.broadcasted_iota(jnp.int32, (tm, LANES), 1).astype(F32)
    remaining = logits
    vals, hots = [], []
    for _ in range(TOP_K):
        m = jnp.max(remaining, axis=-1, keepdims=True)
        first = jnp.min(jnp.where(remaining == m, lane, float(LANES)), axis=-1, keepdims=True)
        hot = lane == first
        remaining = jnp.where(hot, -jnp.inf, remaining)
        vals.append(m)
        hots.append(hot)
    exps = [jnp.exp(v - vals[0]) for v in vals]
    den = exps[0] + exps[1] + exps[2] + exps[3]

    chosen = jnp.zeros((tm, LANES), F32)
    for hot in hots:
        chosen = chosen + jnp.where(hot, 1.0, 0.0)
    r = lax.broadcasted_iota(jnp.int32, (tm, tm), 0)
    c = lax.broadcasted_iota(jnp.int32, (tm, tm), 1)
    earlier = jnp.where(r > c, 1.0, 0.0).astype(BF16)
    before = carry[...] + _dot(earlier, chosen.astype(BF16))
    in_block = jnp.sum(chosen, axis=0, keepdims=True)
    carry[...] = carry[...] + in_block
    blk_ref[...] = jnp.broadcast_to(in_block, blk_ref.shape)

    route = jnp.zeros((tm, LANES), F32)
    for k in range(TOP_K):
        first = jnp.sum(jnp.where(hots[k], lane, 0.0), axis=-1, keepdims=True)
        rank = jnp.sum(jnp.where(hots[k], before, 0.0), axis=-1, keepdims=True)
        route = route + jnp.where(lane == float(k), first, 0.0)
        route = route + jnp.where(lane == float(TOP_K + k), rank, 0.0)
        route = route + jnp.where(lane == float(2 * TOP_K + k), exps[k] / den, 0.0)
    route_ref[...] = route
    route_t_ref[...] = route.T[:route_t_ref.shape[0], :]


def _out_proj(x2, out_a, out_b, w, g, rw, rb, tm):
    t, d = x2.shape
    return pl.pallas_call(
        _out_proj_kernel,
        grid=(t // tm,),
        in_specs=[pl.BlockSpec((tm, d), lambda i: (i, 0)),
                  pl.BlockSpec((tm, A_WIDTH), lambda i: (i, 0)),
                  pl.BlockSpec((tm, B_WIDTH), lambda i: (i, 0)),
                  pl.BlockSpec(w.shape, lambda i: (0, 0)),
                  pl.BlockSpec((1, d), lambda i: (0, 0)),
                  pl.BlockSpec(rw.shape, lambda i: (0, 0)),
                  pl.BlockSpec((1, LANES), lambda i: (0, 0))],
        out_specs=[pl.BlockSpec((tm, d), lambda i: (i, 0)),
                   pl.BlockSpec((tm, d // 2), lambda i: (i, 0)),
                   pl.BlockSpec((tm, LANES), lambda i: (i, 0)),
                   pl.BlockSpec((2 * TOP_K, tm), lambda i: (0, i)),
                   pl.BlockSpec((None, 8, LANES), lambda i: (i, 0, 0))],
        out_shape=[jax.ShapeDtypeStruct((t, d), F32),
                   jax.ShapeDtypeStruct((t, d // 2), jnp.uint32),
                   jax.ShapeDtypeStruct((t, LANES), F32),
                   jax.ShapeDtypeStruct((2 * TOP_K, t), F32),
                   jax.ShapeDtypeStruct((t // tm, 8, LANES), F32)],
        scratch_shapes=[pltpu.VMEM((1, LANES), F32)],
        compiler_params=_params(),
        name="out_proj",
    )(x2, out_a, out_b, w, g, rw, rb)


DISPATCH_ROWS = 2048


def _dispatch_kernel(pad_ref, end_ref, nb_ref, dest_hbm, h_ref, xs_hbm, dest_s0, dest_s1, zeros,
                     sem_i, sem_z, sem_o, *, bm):
    i = pl.program_id(0)
    steps = pl.num_programs(0)
    tm = h_ref.shape[0]
    n_blocks = xs_hbm.shape[0] // bm
    dest_s = (dest_s0, dest_s1)

    def idx_copy(step, slot):
        return pltpu.make_async_copy(dest_hbm.at[step], dest_s[slot], sem_i.at[slot])

    @pl.when(i == 0)
    def _():
        idx_copy(0, 0).start()
        zeros[...] = jnp.zeros_like(zeros)

        def zero_block(start):
            return pltpu.make_async_copy(zeros, xs_hbm.at[pl.ds(pl.multiple_of(start, bm), bm), :], sem_z)

        for e in range(N_EXPERTS):
            @pl.when(pad_ref[e] > 0)
            def _():
                zero_block(end_ref[e] - bm).start()

        def start_unused(b, carry):
            zero_block(b * bm).start()
            return carry

        def wait_unused(b, carry):
            zero_block(b * bm).wait()
            return carry

        lax.fori_loop(nb_ref[0], n_blocks, start_unused, 0)
        for e in range(N_EXPERTS):
            @pl.when(pad_ref[e] > 0)
            def _():
                zero_block(end_ref[e] - bm).wait()
        lax.fori_loop(nb_ref[0], n_blocks, wait_unused, 0)

    for slot in range(2):
        @pl.when(i % 2 == slot)
        def _(slot=slot):
            @pl.when(i + 1 < steps)
            def _():
                idx_copy(i + 1, 1 - slot).start()

            idx_copy(i, slot).wait()

            def body(t, carry):
                for k in range(TOP_K):
                    pltpu.make_async_copy(h_ref.at[pl.ds(t, 1), :],
                                          xs_hbm.at[pl.ds(dest_s[slot][k * tm + t], 1), :],
                                          sem_o).start(priority=k % 2)
                return carry

            lax.fori_loop(0, tm, body, 0, unroll=8)
    for _ in range(TOP_K):
        pltpu.make_async_copy(h_ref, xs_hbm.at[pl.ds(0, tm), :], sem_o).wait()


def _dispatch(padded, pad_end, n_used, dest, hp, n_slots, tm, bm):
    t, half = hp.shape
    grid_spec = pltpu.PrefetchScalarGridSpec(
        num_scalar_prefetch=3,
        grid=(t // tm,),
        in_specs=[pl.BlockSpec(memory_space=pl.ANY),
                  pl.BlockSpec((tm, half), lambda i, p, e, n: (i, 0))],
        out_specs=pl.BlockSpec(memory_space=pl.ANY),
        scratch_shapes=[pltpu.SMEM((tm * TOP_K,), jnp.int32),
                        pltpu.SMEM((tm * TOP_K,), jnp.int32),
                        pltpu.VMEM((bm, half), jnp.uint32),
                        pltpu.SemaphoreType.DMA((2,)),
                        pltpu.SemaphoreType.DMA(()),
                        pltpu.SemaphoreType.DMA(())],
    )
    return pl.pallas_call(
        functools.partial(_dispatch_kernel, bm=bm),
        grid_spec=grid_spec,
        out_shape=jax.ShapeDtypeStruct((n_slots, half), jnp.uint32),
        compiler_params=_params(),
        name="dispatch",
    )(padded, pad_end, n_used, dest, hp)


EXPERT_SUB_ROWS = 256


def _expert_ffn_kernel(be_ref, nb_ref, next_ref, x_ref, wgu_hbm, bgu_ref, wd_hbm, bd_ref, y_ref,
                       wgu_f, wd_f, wgu_s, wd_s, sem_w):
    i = pl.program_id(0)
    d_ff = wd_f.shape[0]
    half = x_ref.shape[1]

    def weight_copies(e):
        return (pltpu.make_async_copy(wgu_hbm.at[e], wgu_f, sem_w.at[0]),
                pltpu.make_async_copy(wd_hbm.at[e], wd_f, sem_w.at[1]))

    @pl.when(jnp.logical_and(i == 0, nb_ref[0] > 0))
    def _():
        for copy in weight_copies(be_ref[0]):
            copy.start()

    @pl.when(i >= nb_ref[0])
    def _():
        y_ref[...] = jnp.zeros_like(y_ref)

    @pl.when(i < nb_ref[0])
    def _():
        changed = jnp.logical_or(i == 0, be_ref[i] != be_ref[jnp.maximum(i - 1, 0)])

        @pl.when(changed)
        def _():
            for copy in weight_copies(be_ref[i]):
                copy.wait()
            wgu_s[...] = wgu_f[...].astype(BF16)
            wd_s[...] = wd_f[...].astype(BF16)

            @pl.when(next_ref[i] >= 0)
            def _():
                for copy in weight_copies(next_ref[i]):
                    copy.start()

        for r0 in range(0, x_ref.shape[0], EXPERT_SUB_ROWS):
            rows = slice(r0, r0 + EXPERT_SUB_ROWS)
            x_lo, x_hi = _unpack_pairs(x_ref[rows, :])
            gu = (_dot(x_lo.astype(BF16), wgu_s[:half, :]) + _dot(x_hi.astype(BF16), wgu_s[half:, :])
                  + bgu_ref[...])
            gate = jnp.minimum(gu[:, :d_ff], SWIGLU_LIMIT)
            up = jnp.clip(gu[:, d_ff:], -SWIGLU_LIMIT, SWIGLU_LIMIT)
            glu = gate * jax.nn.sigmoid(SWIGLU_ALPHA * gate)
            act = ((up + 1.0) * glu).astype(BF16)
            y = _dot(act, wd_s[...]) + bd_ref[...]
            y_ref[rows, :] = _pack_pairs(y.astype(BF16).astype(F32))


def _expert_ffn(block_expert, n_used, next_expert, xs, w_gate_up, b_gate_up, w_down, b_down, bm):
    n_slots, half = xs.shape
    n_e, d, two_ff = w_gate_up.shape
    d_ff = two_ff // 2
    n_blocks = n_slots // bm

    def used(i, nb):
        return jnp.minimum(i, jnp.maximum(nb[0] - 1, 0))

    grid_spec = pltpu.PrefetchScalarGridSpec(
        num_scalar_prefetch=3,
        grid=(n_blocks,),
        in_specs=[pl.BlockSpec((bm, half), lambda i, be, nb, nx: (used(i, nb), 0)),
                  pl.BlockSpec(memory_space=pl.ANY),
                  pl.BlockSpec((None, 1, two_ff), lambda i, be, nb, nx: (be[i], 0, 0)),
                  pl.BlockSpec(memory_space=pl.ANY),
                  pl.BlockSpec((None, 1, d), lambda i, be, nb, nx: (be[i], 0, 0))],
        out_specs=pl.BlockSpec((bm, half), lambda i, be, nb, nx: (i, 0)),
        scratch_shapes=[pltpu.VMEM((d, two_ff), F32), pltpu.VMEM((d_ff, d), F32),
                        pltpu.VMEM((d, two_ff), BF16), pltpu.VMEM((d_ff, d), BF16),
                        pltpu.SemaphoreType.DMA((2,))],
    )
    return pl.pallas_call(
        _expert_ffn_kernel,
        grid_spec=grid_spec,
        out_shape=jax.ShapeDtypeStruct((n_slots, half), jnp.uint32),
        compiler_params=_params(),
        name="expert_ffn",
    )(block_expert, n_used, next_expert, xs, w_gate_up, b_gate_up.reshape(n_e, 1, two_ff),
      w_down, b_down.reshape(n_e, 1, d))


COMBINE_LOCAL_ROWS = 2560
SEG_ALIGN = 8


def _pieces(limit):
    sizes = []
    size = 1 << (limit.bit_length() - 1)
    while size >= SEG_ALIGN:
        sizes.append(size)
        size //= 2
    return sizes


def _combine_kernel(src_ref, len_ref, dst_ref, tot_ref, route_ref, local_ref, x1_ref, ys_hbm, o_ref, ybuf,
                    sem_g):
    j = pl.program_id(0)
    blocks = pl.num_programs(0) - 1
    tm = x1_ref.shape[0]
    half = x1_ref.shape[1] // 2
    local_rows = ybuf.shape[1]

    @pl.when(j == 0)
    def _():
        ybuf[...] = jnp.zeros_like(ybuf)

    @pl.when(j < blocks)
    def _():
        for s in range(2):
            @pl.when(j % 2 == s)
            def _(s=s):
                for e in range(N_EXPERTS):
                    seg = j * N_EXPERTS + e
                    n = len_ref[seg]
                    src = src_ref[seg]
                    dst = dst_ref[seg]
                    for size in _pieces(tm + 2 * SEG_ALIGN):
                        @pl.when((n & size) != 0)
                        def _(size=size, src=src, dst=dst):
                            pltpu.make_async_copy(
                                ys_hbm.at[pl.ds(pl.multiple_of(src, SEG_ALIGN), size), :],
                                ybuf.at[s, pl.ds(pl.multiple_of(dst, SEG_ALIGN), size), :],
                                sem_g.at[s]).start()
                        step = jnp.where((n & size) != 0, size, 0)
                        src = src + step
                        dst = dst + step

    @pl.when(j >= 1)
    def _():
        slot = (j - 1) % 2
        total = tot_ref[j - 1]
        for size in _pieces(local_rows):
            @pl.when((total & size) != 0)
            def _(size=size):
                pltpu.make_async_copy(ys_hbm.at[pl.ds(0, size), :], ybuf.at[slot, pl.ds(0, size), :],
                                      sem_g.at[slot]).wait()

        route = route_ref[...]
        local_t = local_ref[...]
        local = jnp.concatenate(
            [local_t, jnp.zeros((LANES - local_t.shape[0], tm), F32)], axis=0).T
        col = lax.broadcasted_iota(jnp.int32, (tm, local_rows), 1).astype(F32)
        g = jnp.zeros((tm, local_rows), F32)
        for k in range(TOP_K):
            pos = local[:, k: k + 1]
            gate = route[:, 2 * TOP_K + k: 2 * TOP_K + k + 1]
            g = jnp.where(col == pos, gate, g)
        g = g.astype(BF16)
        lo, hi = _unpack_pairs(ybuf[slot])
        o_ref[:, :half] = x1_ref[:, :half] + _dot(g, lo.astype(BF16))
        o_ref[:, half:] = x1_ref[:, half:] + _dot(g, hi.astype(BF16))


def _combine(tables, route, local_t, x1, ys, tm, local_rows):
    t, d = x1.shape
    half = d // 2

    def summed(j, *_):
        return (jnp.maximum(j - 1, 0), 0)

    grid_spec = pltpu.PrefetchScalarGridSpec(
        num_scalar_prefetch=4,
        grid=(t // tm + 1,),
        in_specs=[pl.BlockSpec((tm, LANES), summed),
                  pl.BlockSpec((local_t.shape[0], tm), lambda j, *_: (0, jnp.maximum(j - 1, 0))),
                  pl.BlockSpec((tm, d), summed),
                  pl.BlockSpec(memory_space=pl.ANY)],
        out_specs=pl.BlockSpec((tm, d), summed),
        scratch_shapes=[pltpu.VMEM((2, local_rows, half), jnp.uint32),
                        pltpu.SemaphoreType.DMA((2,))],
    )
    return pl.pallas_call(
        _combine_kernel,
        grid_spec=grid_spec,
        out_shape=jax.ShapeDtypeStruct((t, d), F32),
        compiler_params=_params(),
        name="combine",
    )(*tables, route, local_t, x1, ys)


def _per_expert(table, idx):
    out = jnp.zeros(idx.shape, table.dtype)
    for e in range(N_EXPERTS):
        out = jnp.where(idx == e, table[:, e][None, :, None], out)
    return out


def _combine_tables(idx, rank, n, pad_start, tm, local_rows):
    before = jnp.cumsum(n, axis=0) - n
    first = pad_start[None, :] + before
    src = first // SEG_ALIGN * SEG_ALIGN
    length = jnp.where(n > 0, (first + n + SEG_ALIGN - 1) // SEG_ALIGN * SEG_ALIGN - src, 0)
    dst = jnp.cumsum(length, axis=1) - length
    total = jnp.sum(length, axis=1)
    assert local_rows >= tm * TOP_K + N_EXPERTS * 2 * (SEG_ALIGN - 1)
    local = (_per_expert(dst + first - src - before, idx) + rank).reshape(TOP_K, -1).astype(F32)
    tables = (src.reshape(-1).astype(jnp.int32), length.reshape(-1).astype(jnp.int32),
              dst.reshape(-1).astype(jnp.int32), total.astype(jnp.int32))
    return tables, local


def _slot_layout(route_t, n, tm, bm):
    t = route_t.shape[1]
    blocks = t // tm
    idx = route_t[:TOP_K].astype(jnp.int32).reshape(TOP_K, blocks, tm)
    rank = route_t[TOP_K:].astype(jnp.int32).reshape(TOP_K, blocks, tm)
    counts = jnp.sum(n, axis=0)
    padded = (counts + bm - 1) // bm * bm
    pad_end = jnp.cumsum(padded).astype(jnp.int32)
    pad_start = pad_end - padded
    dest = _per_expert(jnp.broadcast_to(pad_start[None, :], n.shape), idx) + rank
    dest = dest.reshape(TOP_K, t // DISPATCH_ROWS, DISPATCH_ROWS).transpose(1, 0, 2)
    dest = dest.reshape(t // DISPATCH_ROWS, TOP_K * DISPATCH_ROWS)
    n_blocks = -(-t * TOP_K // bm) + N_EXPERTS
    block_start = jnp.arange(n_blocks, dtype=jnp.int32) * bm
    block_expert = jnp.minimum(jnp.sum(pad_end[None, :] <= block_start[:, None], axis=1),
                               N_EXPERTS - 1).astype(jnp.int32)
    n_used = (pad_end[-1:] // bm).astype(jnp.int32)
    following = jnp.take(pad_end, block_expert) // bm
    next_expert = jnp.where(following < n_used[0],
                            jnp.take(block_expert, jnp.minimum(following, n_blocks - 1)), -1).astype(jnp.int32)
    return idx, rank, padded, pad_end, dest, block_expert, next_expert, n_used, n_blocks * bm


def _rot_cols(w):
    half = w.shape[-1] // 2
    return jnp.concatenate([-w[..., half:], w[..., :half]], axis=-1)


def _swap_halves(g):
    half = g.shape[-1] // 2
    return jnp.concatenate([g[..., half:], g[..., :half]], axis=-1)


def kernel(x, attn_norm_g, w_in, a_q_norm_g, a_k_norm_g, rel_bias, q_a_norm_g, w_q_b, kv_a_norm_g,
           w_kv_b, b_q_norm_g, b_k_norm_g, w_out, ffn_norm_g, router_w, router_b, w_gate_up,
           b_gate_up, w_down, b_down):
    batch, seq, d = x.shape
    t = batch * seq
    depth = w_in.shape[0]
    tm = 512
    expert_bm = 512

    pos = jnp.arange(seq, dtype=F32)
    inv_freq = ROPE_THETA ** (-jnp.arange(0, QK_ROPE_DIM, 2, dtype=F32) / QK_ROPE_DIM)
    ang = pos[:, None] * inv_freq[None, :]
    cos, sin = jnp.cos(ang), jnp.sin(ang)
    cs = jnp.concatenate([cos, cos, sin, sin], axis=-1)

    row = jnp.arange(LANES)[:, None] // A_HEAD_DIM
    col = jnp.arange(LANES)[None, :] // A_HEAD_DIM
    head_block_ones = (row == col).astype(BF16)
    band_bias = _band_bias(rel_bias)

    x2 = x.reshape(t, d)
    for layer in range(depth):
        kpe_off = 3 * A_WIDTH + Q_LORA_RANK + KV_LORA_RANK
        w_kpe = w_in[layer][:, kpe_off:]
        w_in_r = jnp.concatenate([w_in[layer][:, :kpe_off], w_kpe, _rot_cols(w_kpe)], axis=1).astype(BF16)

        wq = w_q_b[layer].reshape(Q_LORA_RANK, B_HEADS, QK_HEAD_DIM)
        wq_rope = wq[..., QK_NOPE_DIM:]
        wq_r = jnp.concatenate([wq, _rot_cols(wq_rope)], axis=-1).reshape(Q_LORA_RANK, -1).astype(BF16)
        wkv = w_kv_b[layer].reshape(KV_LORA_RANK, B_HEADS, QK_NOPE_DIM + V_HEAD_DIM)
        wkv_r = jnp.concatenate([wkv[..., :QK_NOPE_DIM].reshape(KV_LORA_RANK, -1),
                                 wkv[..., QK_NOPE_DIM:].reshape(KV_LORA_RANK, -1)], axis=1).astype(BF16)
        gc = jnp.concatenate([q_a_norm_g[layer], kv_a_norm_g[layer]])[None, :]
        gq, gk = b_q_norm_g[layer], b_k_norm_g[layer]

        def rope_gain(gr):
            return jnp.concatenate([gr, _swap_halves(gr)])

        gqk = jnp.stack([gq[:QK_NOPE_DIM], rope_gain(gq[QK_NOPE_DIM:]),
                         gk[:QK_NOPE_DIM], rope_gain(gk[QK_NOPE_DIM:])])
        g_a = jnp.stack([jnp.tile(a_q_norm_g[layer], 2), jnp.tile(a_k_norm_g[layer], 2)])
        g_a = jnp.broadcast_to(g_a[None], (A_WIDTH // LANES, 2, LANES))

        proj, qb, kb, vb = _in_proj(x2, attn_norm_g[layer][None, :], w_in_r, cs, wq_r, wkv_r, gc, gqk, seq, tm)
        out_a, out_b = _attention(proj, head_block_ones, g_a, band_bias, qb, kb, vb, batch, seq, 512)

        rw = jnp.pad(router_w[layer], ((0, 0), (0, LANES - N_EXPERTS)))
        rw_hi = rw.astype(BF16)
        rw_lo = (rw - rw_hi.astype(F32)).astype(BF16)
        rb = jnp.pad(router_b[layer], (0, LANES - N_EXPERTS), constant_values=-jnp.inf)[None, :]
        x1, hp, route, route_t, in_block = _out_proj(
            x2, out_a, out_b, w_out[layer].astype(BF16), ffn_norm_g[layer][None, :],
            jnp.concatenate([rw_hi, rw_lo], axis=1), rb, tm)

        n = in_block[:, 0, :N_EXPERTS].astype(jnp.int32)
        idx, rank, padded, pad_end, dest, block_expert, next_expert, n_used, n_slots = _slot_layout(
            route_t, n, tm, expert_bm)
        xs = _dispatch(padded, pad_end, n_used, dest, hp, n_slots, DISPATCH_ROWS, expert_bm)
        ys = _expert_ffn(block_expert, n_used, next_expert, xs, w_gate_up[layer], b_gate_up[layer],
                         w_down[layer], b_down[layer], expert_bm)
        tables, local_t = _combine_tables(idx, rank, n, pad_end - padded, tm, COMBINE_LOCAL_ROWS)
        x2 = _combine(tables, route, local_t, x1, ys, tm, COMBINE_LOCAL_ROWS)
    return x2.reshape(batch, seq, d)
```

```python
import functools
import math

import jax
import jax.numpy as jnp
from jax import lax
from jax.experimental import pallas as pl
from jax.experimental.pallas import tpu as pltpu

A_HEADS = 8
A_HEAD_DIM = 64
A_WIDTH = A_HEADS * A_HEAD_DIM
DILATED_PATTERNS = ((128, 1), (512, 4), (2048, 16))
BAND_BLOCK = 128

B_HEADS = 4
QK_NOPE_DIM = 128
QK_ROPE_DIM = 64
QK_HEAD_DIM = QK_NOPE_DIM + QK_ROPE_DIM
V_HEAD_DIM = 128
Q_LORA_RANK = 256
KV_LORA_RANK = 256
B_WIDTH = B_HEADS * V_HEAD_DIM
ROPE_THETA = 10000.0

NUM_BUCKETS = 32
MAX_DISTANCE = 2048

N_EXPERTS = 32
TOP_K = 4
SWIGLU_LIMIT = 7.0
SWIGLU_ALPHA = 1.702
RMS_EPS = 1e-6

LANES = 128
MASK_VALUE = -1e30
LOG2E = math.log2(math.e)
PROJ_WIDTH = 3 * A_WIDTH + Q_LORA_RANK + KV_LORA_RANK + 2 * QK_ROPE_DIM
VMEM_LIMIT = 56 * 1024 * 1024

F32 = jnp.float32
BF16 = jnp.bfloat16


def _dot(a, b):
    return jnp.dot(a, b, preferred_element_type=F32)


def _dot_nt(a, b):
    return lax.dot_general(a, b, (((1,), (1,)), ((), ())), preferred_element_type=F32)


def _group_sum(x, m):
    return _dot(x.astype(BF16), m)


def _params(n_parallel=1):
    return pltpu.CompilerParams(
        dimension_semantics=("arbitrary",) * n_parallel, vmem_limit_bytes=VMEM_LIMIT)


IN_PROJ_SUB_ROWS = 256


def _mla_qkv(c, kper, cs, wq_ref, wkv_ref, gc, gqk, q_ref, k_ref, v_ref, rows, q_scale):
    def lora_norm(z, g):
        ms = jnp.mean(z * z, axis=-1, keepdims=True)
        return (z * lax.rsqrt(ms + RMS_EPS) * g).astype(BF16)

    cq = lora_norm(c[:, :Q_LORA_RANK], gc[:, :Q_LORA_RANK])
    ckv = lora_norm(c[:, Q_LORA_RANK:], gc[:, Q_LORA_RANK:])
    qb = _dot(cq, wq_ref[...])
    kvb = _dot(ckv, wkv_ref[...])
    n = c.shape[0]

    row = lax.broadcasted_iota(jnp.int32, (LANES, LANES), 0)
    ones_all = jnp.ones((LANES, LANES), BF16)
    ones_lo = jnp.where(row < QK_ROPE_DIM, 1.0, 0.0).astype(BF16)
    lane = lax.broadcasted_iota(jnp.int32, (n, LANES), 1)

    def rope(z, g_row):
        t = z * (g_row * cs)
        return jnp.where(lane < QK_ROPE_DIM, t + pltpu.roll(t, QK_ROPE_DIM, 1), 0.0)

    k_rope = rope(kper, gqk[3:4])
    kpe_ss = _group_sum(kper * kper, ones_lo)
    for h in range(B_HEADS):
        qn = qb[:, 2 * LANES * h: 2 * LANES * h + LANES]
        qr = qb[:, 2 * LANES * h + LANES: 2 * LANES * (h + 1)]
        ss = _group_sum(qn * qn, ones_all) + _group_sum(qr * qr, ones_lo)
        rs = lax.rsqrt(ss * (1.0 / QK_HEAD_DIM) + RMS_EPS) * q_scale
        q_ref[h, rows, :LANES] = (qn * gqk[0:1] * rs).astype(BF16)
        q_ref[h, rows, LANES:] = (rope(qr, gqk[1:2]) * rs).astype(BF16)
        kn = kvb[:, LANES * h: LANES * (h + 1)]
        ssk = _group_sum(kn * kn, ones_all) + kpe_ss
        rsk = lax.rsqrt(ssk * (1.0 / QK_HEAD_DIM) + RMS_EPS)
        k_ref[h, rows, :LANES] = (kn * gqk[2:3] * rsk).astype(BF16)
        k_ref[h, rows, LANES:] = (k_rope * rsk).astype(BF16)
    v_ref[rows, :] = kvb[:, B_HEADS * QK_NOPE_DIM:].astype(BF16)


def _in_proj_kernel(x_ref, g_ref, w_ref, cs_ref, wq_ref, wkv_ref, gc_ref, gqk_ref,
                    a_ref, q_ref, k_ref, v_ref, *, q_scale):
    a_width = a_ref.shape[1]
    c_width = Q_LORA_RANK + KV_LORA_RANK
    for r0 in range(0, x_ref.shape[0], IN_PROJ_SUB_ROWS):
        rows = slice(r0, r0 + IN_PROJ_SUB_ROWS)
        x = x_ref[rows, :]
        ms = jnp.mean(x * x, axis=-1, keepdims=True)
        h = (x * lax.rsqrt(ms + RMS_EPS) * g_ref[...]).astype(BF16)
        proj = _dot(h, w_ref[...])
        a_ref[rows, :] = proj[:, :a_width]
        _mla_qkv(proj[:, a_width:a_width + c_width], proj[:, a_width + c_width:], cs_ref[rows, :],
                 wq_ref, wkv_ref, gc_ref[...], gqk_ref[...], q_ref, k_ref, v_ref, rows, q_scale)


def _in_proj(x2, g, w, cs, wq, wkv, gc, gqk, seq, tm):
    t, d = x2.shape
    a_width = 3 * A_WIDTH
    pos_blocks = seq // tm
    kernel = functools.partial(_in_proj_kernel, q_scale=QK_HEAD_DIM ** -0.5 * LOG2E)
    return pl.pallas_call(
        kernel,
        grid=(t // tm,),
        in_specs=[pl.BlockSpec((tm, d), lambda i: (i, 0)),
                  pl.BlockSpec((1, d), lambda i: (0, 0)),
                  pl.BlockSpec(w.shape, lambda i: (0, 0)),
                  pl.BlockSpec((tm, LANES), lambda i: (i % pos_blocks, 0)),
                  pl.BlockSpec(wq.shape, lambda i: (0, 0)),
                  pl.BlockSpec(wkv.shape, lambda i: (0, 0)),
                  pl.BlockSpec(gc.shape, lambda i: (0, 0)),
                  pl.BlockSpec(gqk.shape, lambda i: (0, 0))],
        out_specs=[pl.BlockSpec((tm, a_width), lambda i: (i, 0)),
                   pl.BlockSpec((B_HEADS, tm, 2 * LANES), lambda i: (0, i, 0)),
                   pl.BlockSpec((B_HEADS, tm, 2 * LANES), lambda i: (0, i, 0)),
                   pl.BlockSpec((tm, B_WIDTH), lambda i: (i, 0))],
        out_shape=[jax.ShapeDtypeStruct((t, a_width), F32),
                   jax.ShapeDtypeStruct((B_HEADS, t, 2 * LANES), BF16),
                   jax.ShapeDtypeStruct((B_HEADS, t, 2 * LANES), BF16),
                   jax.ShapeDtypeStruct((t, B_WIDTH), BF16)],
        compiler_params=_params(),
        name="in_proj",
    )(x2, g, w, cs, wq, wkv, gc, gqk)


def _mla_attention(q_ref, k_ref, v_ref, o_ref, *, tq):
    seq = q_ref.shape[0]
    row = lax.broadcasted_iota(jnp.int32, (tq, tq), 0)
    col = lax.broadcasted_iota(jnp.int32, (tq, tq), 1)
    diag_mask = jnp.where(col <= row, 0.0, MASK_VALUE)
    v_ext = jnp.concatenate([v_ref[...], jnp.ones(v_ref.shape, v_ref.dtype)], axis=1)
    dv = v_ref.shape[1]
    for i in range(seq // tq):
        q = q_ref[i * tq:(i + 1) * tq, :]
        s_diag = _dot_nt(q, k_ref[i * tq:(i + 1) * tq, :]) + diag_mask
        m = jnp.max(s_diag, axis=-1, keepdims=True)
        if i > 0:
            s_past = _dot_nt(q, k_ref[:i * tq, :])
            m = jnp.maximum(m, jnp.max(s_past, axis=-1, keepdims=True))
        o = _dot(jnp.exp2(s_diag - m).astype(BF16), v_ext[i * tq:(i + 1) * tq, :])
        if i > 0:
            o = o + _dot(jnp.exp2(s_past - m).astype(BF16), v_ext[:i * tq, :])
        o_ref[i * tq:(i + 1) * tq, :] = (o[:, :dv] / o[:, dv:]).astype(o_ref.dtype)


def _dilated_attention(q_ref, k_ref, v_ref, bd_ref, g_ref, bias_ref, o_ref, qs, ks, acc_o, acc_l, acc_m,
                       *, unroll):
    seq = q_ref.shape[0]
    n_tiles = seq // BAND_BLOCK
    low = lax.broadcasted_iota(jnp.int32, (BAND_BLOCK, LANES), 1) < A_HEAD_DIM
    bd = bd_ref[...]
    g = g_ref[...]

    def head_norm(z, g_row):
        ss = _group_sum(z * z, bd)
        return z * lax.rsqrt(ss * (1.0 / A_HEAD_DIM) + RMS_EPS) * g_row

    qs[...] = head_norm(q_ref[...], g[0:1]) * (A_HEAD_DIM ** -0.5 * LOG2E)
    ks[...] = head_norm(k_ref[...], g[1:2])
    ones = jnp.ones((2 * BAND_BLOCK, LANES), BF16)

    for p, (window, dil) in enumerate(DILATED_PATTERNS):
        blocks_per_class = n_tiles // dil
        span = BAND_BLOCK * dil

        def rows_at(start, dil=dil):
            if dil == 1:
                return pl.ds(start, BAND_BLOCK)
            return pl.ds(start, BAND_BLOCK, stride=dil)

        def tile(i, carry, p=p, blocks_per_class=blocks_per_class, span=span, rows_at=rows_at):
            r = i // blocks_per_class
            j = i % blocks_per_class
            cur = rows_at(r + j * span)
            prev = rows_at(r + jnp.maximum(j - 1, 0) * span)
            first = jnp.where(j == 0, 1, 0)
            q = qs[cur, :]
            q2 = jnp.concatenate([jnp.where(low, q, 0.0), jnp.where(low, 0.0, q)], axis=0).astype(BF16)
            k_band = jnp.concatenate([ks[prev, :], ks[cur, :]], axis=0).astype(BF16)
            s = _dot_nt(q2, k_band) + bias_ref[p, first]
            m = jnp.max(s, axis=-1, keepdims=True)
            pr = jnp.exp2(s - m).astype(BF16)
            v_band = jnp.concatenate([v_ref[prev, :], v_ref[cur, :]], axis=0).astype(BF16)
            o = _dot(pr, jnp.concatenate([v_band, ones], axis=1))
            top, bot = o[:BAND_BLOCK], o[BAND_BLOCK:]
            acc_o[p, cur, :] = jnp.where(low, top[:, :LANES], bot[:, :LANES])
            acc_l[p, cur, :] = jnp.where(low, top[:, LANES:], bot[:, LANES:])
            acc_m[p, cur, :] = jnp.where(low, m[:BAND_BLOCK], m[BAND_BLOCK:])
            return carry

        lax.fori_loop(0, n_tiles, tile, 0, unroll=unroll)

    m_all = jnp.maximum(jnp.maximum(acc_m[0], acc_m[1]), acc_m[2])
    num = jnp.zeros((seq, LANES), F32)
    den = jnp.zeros((seq, LANES), F32)
    for p in range(len(DILATED_PATTERNS)):
        w = jnp.exp2(acc_m[p] - m_all)
        num = num + w * acc_o[p]
        den = den + w * acc_l[p]
    o_ref[...] = (num / den).astype(o_ref.dtype)


def _attention_kernel(qa_ref, ka_ref, va_ref, bd_ref, g_ref, bias_ref, qb_ref, kb_ref, vb_ref,
                      oa_ref, ob_ref, qs, ks, acc_o, acc_l, acc_m, *, unroll, tq):
    _mla_attention(qb_ref, kb_ref, vb_ref, ob_ref, tq=tq)
    _dilated_attention(qa_ref, ka_ref, va_ref, bd_ref, g_ref, bias_ref, oa_ref, qs, ks, acc_o, acc_l, acc_m,
                       unroll=unroll)


def _attention(proj, bd, g, bias, qb, kb, vb, batch, seq, tq):
    t = proj.shape[0]
    pairs = A_WIDTH // LANES
    assert pairs == B_HEADS
    n_pat = len(DILATED_PATTERNS)
    return pl.pallas_call(
        functools.partial(_attention_kernel, unroll=16, tq=tq),
        grid=(batch, pairs),
        in_specs=[pl.BlockSpec((seq, LANES), lambda b, c: (b, c)),
                  pl.BlockSpec((seq, LANES), lambda b, c: (b, pairs + c)),
                  pl.BlockSpec((seq, LANES), lambda b, c: (b, 2 * pairs + c)),
                  pl.BlockSpec((LANES, LANES), lambda b, c: (0, 0)),
                  pl.BlockSpec((None, 2, LANES), lambda b, c: (c, 0, 0)),
                  pl.BlockSpec((None, n_pat, 2, 2 * BAND_BLOCK, 2 * BAND_BLOCK),
                               lambda b, c: (c, 0, 0, 0, 0)),
                  pl.BlockSpec((None, seq, 2 * LANES), lambda b, h: (h, b, 0)),
                  pl.BlockSpec((None, seq, 2 * LANES), lambda b, h: (h, b, 0)),
                  pl.BlockSpec((seq, V_HEAD_DIM), lambda b, h: (b, h))],
        out_specs=[pl.BlockSpec((seq, LANES), lambda b, c: (b, c)),
                   pl.BlockSpec((seq, V_HEAD_DIM), lambda b, h: (b, h))],
        out_shape=[jax.ShapeDtypeStruct((t, A_WIDTH), BF16),
                   jax.ShapeDtypeStruct((t, B_WIDTH), BF16)],
        scratch_shapes=[pltpu.VMEM((seq, LANES), F32)] * 2
                       + [pltpu.VMEM((n_pat, seq, LANES), F32)] * 3,
        compiler_params=_params(2),
        name="attention",
    )(proj, proj, proj, bd, g, bias, qb, kb, vb)


def _t5_bucket(dist):
    max_exact = NUM_BUCKETS // 2
    df = jnp.maximum(dist, 1).astype(F32)
    log_bucket = max_exact + (jnp.log(df / max_exact) / math.log(MAX_DISTANCE / max_exact)
                              * (NUM_BUCKETS - max_exact)).astype(jnp.int32)
    log_bucket = jnp.minimum(log_bucket, NUM_BUCKETS - 1)
    return jnp.where(dist < max_exact, dist, log_bucket)


def _band_bias(rel_bias):
    n = BAND_BLOCK
    qi = jnp.arange(n)[:, None]
    kj = jnp.arange(n)[None, :]
    buckets = jnp.arange(NUM_BUCKETS)
    tables = []
    for window, dil in DILATED_PATTERNS:
        steps = window // dil
        halves = []
        for back in (qi - kj + n, qi - kj):
            onehot = (_t5_bucket(jnp.maximum(back, 0) * dil)[:, :, None] == buckets).astype(F32)
            vals = jnp.einsum('qkb,bh->hqk', onehot, rel_bias.astype(F32),
                              precision=lax.Precision.HIGHEST) * LOG2E
            halves.append(jnp.where(((back >= 0) & (back <= steps))[None], vals, MASK_VALUE))
        prev, cur = halves
        normal = jnp.concatenate([prev, cur], axis=-1)
        first = jnp.concatenate([jnp.full_like(prev, MASK_VALUE), cur], axis=-1)
        tables.append(jnp.stack([normal, first], axis=1))
    tab = jnp.stack(tables, axis=1)
    tab = tab.reshape(A_HEADS // 2, 2, len(DILATED_PATTERNS), 2, n, 2 * n)
    return tab.transpose(0, 2, 3, 1, 4, 5).reshape(A_HEADS // 2, len(DILATED_PATTERNS), 2, 2 * n, 2 * n)


def _pack_pairs(v):
    half = v.shape[1] // 2
    bits = pltpu.bitcast(v, jnp.uint32)
    return (bits[:, :half] >> 16) | (bits[:, half:] & jnp.uint32(0xFFFF0000))


def _unpack_pairs(p):
    return (pltpu.bitcast(p << 16, F32), pltpu.bitcast(p & jnp.uint32(0xFFFF0000), F32))


def _out_proj_kernel(x_ref, a_ref, b_ref, w_ref, g_ref, rw_ref, rb_ref,
                     x1_ref, hp_ref, route_ref, route_t_ref, blk_ref, carry):
    i = pl.program_id(0)
    tm = x_ref.shape[0]

    @pl.when(i == 0)
    def _():
        carry[...] = jnp.zeros_like(carry)

    w = w_ref[...]
    x1 = x_ref[...] + _dot(a_ref[...], w[:A_WIDTH]) + _dot(b_ref[...], w[A_WIDTH:])
    x1_ref[...] = x1
    ms = jnp.mean(x1 * x1, axis=-1, keepdims=True)
    h = x1 * lax.rsqrt(ms + RMS_EPS) * g_ref[...]
    hi = h.astype(BF16)
    hi_f = hi.astype(F32)
    hp_ref[...] = _pack_pairs(hi_f)
    lo = (h - hi_f).astype(BF16)
    rw = rw_ref[...]
    hw = _dot(hi, rw)
    logits = hw[:, :LANES] + hw[:, LANES:] + _dot(lo, rw[:, :LANES]) + rb_ref[...]

    lane = lax.broadcasted_iota(jnp.int32, (tm, LANES), 1).astype(F32)
    remaining = logits
    vals, hots = [], []
    for _ in range(TOP_K):
        m = jnp.max(remaining, axis=-1, keepdims=True)
        first = jnp.min(jnp.where(remaining == m, lane, float(LANES)), axis=-1, keepdims=True)
        hot = lane == first
        remaining = jnp.where(hot, -jnp.inf, remaining)
        vals.append(m)
        hots.append(hot)
    exps = [jnp.exp(v - vals[0]) for v in vals]
    den = exps[0] + exps[1] + exps[2] + exps[3]

    chosen = jnp.zeros((tm, LANES), F32)
    for hot in hots:
        chosen = chosen + jnp.where(hot, 1.0, 0.0)
    r = lax.broadcasted_iota(jnp.int32, (tm, tm), 0)
    c = lax.broadcasted_iota(jnp.int32, (tm, tm), 1)
    earlier = jnp.where(r > c, 1.0, 0.0).astype(BF16)
    before = carry[...] + _dot(earlier, chosen.astype(BF16))
    in_block = jnp.sum(chosen, axis=0, keepdims=True)
    carry[...] = carry[...] + in_block
    blk_ref[...] = jnp.broadcast_to(in_block, blk_ref.shape)

    route = jnp.zeros((tm, LANES), F32)
    for k in range(TOP_K):
        first = jnp.sum(jnp.where(hots[k], lane, 0.0), axis=-1, keepdims=True)
        rank = jnp.sum(jnp.where(hots[k], before, 0.0), axis=-1, keepdims=True)
        route = route + jnp.where(lane == float(k), first, 0.0)
        route = route + jnp.where(lane == float(TOP_K + k), rank, 0.0)
        route = route + jnp.where(lane == float(2 * TOP_K + k), exps[k] / den, 0.0)
    route_ref[...] = route
    route_t_ref[...] = route.T[:route_t_ref.shape[0], :]


def _out_proj(x2, out_a, out_b, w, g, rw, rb, tm):
    t, d = x2.shape
    return pl.pallas_call(
        _out_proj_kernel,
        grid=(t // tm,),
        in_specs=[pl.BlockSpec((tm, d), lambda i: (i, 0)),
                  pl.BlockSpec((tm, A_WIDTH), lambda i: (i, 0)),
                  pl.BlockSpec((tm, B_WIDTH), lambda i: (i, 0)),
                  pl.BlockSpec(w.shape, lambda i: (0, 0)),
                  pl.BlockSpec((1, d), lambda i: (0, 0)),
                  pl.BlockSpec(rw.shape, lambda i: (0, 0)),
                  pl.BlockSpec((1, LANES), lambda i: (0, 0))],
        out_specs=[pl.BlockSpec((tm, d), lambda i: (i, 0)),
                   pl.BlockSpec((tm, d // 2), lambda i: (i, 0)),
                   pl.BlockSpec((tm, LANES), lambda i: (i, 0)),
                   pl.BlockSpec((2 * TOP_K, tm), lambda i: (0, i)),
                   pl.BlockSpec((None, 8, LANES), lambda i: (i, 0, 0))],
        out_shape=[jax.ShapeDtypeStruct((t, d), F32),
                   jax.ShapeDtypeStruct((t, d // 2), jnp.uint32),
                   jax.ShapeDtypeStruct((t, LANES), F32),
                   jax.ShapeDtypeStruct((2 * TOP_K, t), F32),
                   jax.ShapeDtypeStruct((t // tm, 8, LANES), F32)],
        scratch_shapes=[pltpu.VMEM((1, LANES), F32)],
        compiler_params=_params(),
        name="out_proj",
    )(x2, out_a, out_b, w, g, rw, rb)


DISPATCH_ROWS = 1024


def _dispatch_kernel(pad_ref, end_ref, nb_ref, dest_hbm, h_ref, xs_hbm, dest_s0, dest_s1, zeros,
                     sem_i, sem_z, sem_o, *, bm):
    i = pl.program_id(0)
    steps = pl.num_programs(0)
    tm = h_ref.shape[0]
    n_blocks = xs_hbm.shape[0] // bm
    dest_s = (dest_s0, dest_s1)

    def idx_copy(step, slot):
        return pltpu.make_async_copy(dest_hbm.at[step], dest_s[slot], sem_i.at[slot])

    @pl.when(i == 0)
    def _():
        idx_copy(0, 0).start()
        zeros[...] = jnp.zeros_like(zeros)

        def zero_block(start):
            return pltpu.make_async_copy(zeros, xs_hbm.at[pl.ds(pl.multiple_of(start, bm), bm), :], sem_z)

        for e in range(N_EXPERTS):
            @pl.when(pad_ref[e] > 0)
            def _():
                zero_block(end_ref[e] - bm).start()

        def start_unused(b, carry):
            zero_block(b * bm).start()
            return carry

        def wait_unused(b, carry):
            zero_block(b * bm).wait()
            return carry

        lax.fori_loop(nb_ref[0], n_blocks, start_unused, 0)
        for e in range(N_EXPERTS):
            @pl.when(pad_ref[e] > 0)
            def _():
                zero_block(end_ref[e] - bm).wait()
        lax.fori_loop(nb_ref[0], n_blocks, wait_unused, 0)

    for slot in range(2):
        @pl.when(i % 2 == slot)
        def _(slot=slot):
            @pl.when(i + 1 < steps)
            def _():
                idx_copy(i + 1, 1 - slot).start()

            idx_copy(i, slot).wait()

            def body(t, carry):
                for k in range(TOP_K):
                    pltpu.make_async_copy(h_ref.at[pl.ds(t, 1), :],
                                          xs_hbm.at[pl.ds(dest_s[slot][k * tm + t], 1), :],
                                          sem_o).start(priority=k % 2)
                return carry

            lax.fori_loop(0, tm, body, 0, unroll=8)
    for _ in range(TOP_K):
        pltpu.make_async_copy(h_ref, xs_hbm.at[pl.ds(0, tm), :], sem_o).wait()


def _dispatch(padded, pad_end, n_used, dest, hp, n_slots, tm, bm):
    t, half = hp.shape
    grid_spec = pltpu.PrefetchScalarGridSpec(
        num_scalar_prefetch=3,
        grid=(t // tm,),
        in_specs=[pl.BlockSpec(memory_space=pl.ANY),
                  pl.BlockSpec((tm, half), lambda i, p, e, n: (i, 0))],
        out_specs=pl.BlockSpec(memory_space=pl.ANY),
        scratch_shapes=[pltpu.SMEM((tm * TOP_K,), jnp.int32),
                        pltpu.SMEM((tm * TOP_K,), jnp.int32),
                        pltpu.VMEM((bm, half), jnp.uint32),
                        pltpu.SemaphoreType.DMA((2,)),
                        pltpu.SemaphoreType.DMA(()),
                        pltpu.SemaphoreType.DMA(())],
    )
    return pl.pallas_call(
        functools.partial(_dispatch_kernel, bm=bm),
        grid_spec=grid_spec,
        out_shape=jax.ShapeDtypeStruct((n_slots, half), jnp.uint32),
        compiler_params=_params(),
        name="dispatch",
    )(padded, pad_end, n_used, dest, hp)


EXPERT_SUB_ROWS = 256


def _expert_ffn_kernel(be_ref, nb_ref, next_ref, x_ref, wgu_hbm, bgu_ref, wd_hbm, bd_ref, y_ref,
                       wgu_f, wd_f, wgu_s, wd_s, sem_w):
    i = pl.program_id(0)
    d_ff = wd_f.shape[0]
    half = x_ref.shape[1]

    def weight_copies(e):
        return (pltpu.make_async_copy(wgu_hbm.at[e], wgu_f, sem_w.at[0]),
                pltpu.make_async_copy(wd_hbm.at[e], wd_f, sem_w.at[1]))

    @pl.when(jnp.logical_and(i == 0, nb_ref[0] > 0))
    def _():
        for copy in weight_copies(be_ref[0]):
            copy.start()

    @pl.when(i >= nb_ref[0])
    def _():
        y_ref[...] = jnp.zeros_like(y_ref)

    @pl.when(i < nb_ref[0])
    def _():
        changed = jnp.logical_or(i == 0, be_ref[i] != be_ref[jnp.maximum(i - 1, 0)])

        @pl.when(changed)
        def _():
            for copy in weight_copies(be_ref[i]):
                copy.wait()
            wgu_s[...] = wgu_f[...].astype(BF16)
            wd_s[...] = wd_f[...].astype(BF16)

            @pl.when(next_ref[i] >= 0)
            def _():
                for copy in weight_copies(next_ref[i]):
                    copy.start()

        for r0 in range(0, x_ref.shape[0], EXPERT_SUB_ROWS):
            rows = slice(r0, r0 + EXPERT_SUB_ROWS)
            x_lo, x_hi = _unpack_pairs(x_ref[rows, :])
            gu = (_dot(x_lo.astype(BF16), wgu_s[:half, :]) + _dot(x_hi.astype(BF16), wgu_s[half:, :])
                  + bgu_ref[...])
            gate = jnp.minimum(gu[:, :d_ff], SWIGLU_LIMIT)
            up = jnp.clip(gu[:, d_ff:], -SWIGLU_LIMIT, SWIGLU_LIMIT)
            glu = gate * jax.nn.sigmoid(SWIGLU_ALPHA * gate)
            act = ((up + 1.0) * glu).astype(BF16)
            y = _dot(act, wd_s[...]) + bd_ref[...]
            y_ref[rows, :] = _pack_pairs(y.astype(BF16).astype(F32))


def _expert_ffn(block_expert, n_used, next_expert, xs, w_gate_up, b_gate_up, w_down, b_down, bm):
    n_slots, half = xs.shape
    n_e, d, two_ff = w_gate_up.shape
    d_ff = two_ff // 2
    n_blocks = n_slots // bm

    def used(i, nb):
        return jnp.minimum(i, jnp.maximum(nb[0] - 1, 0))

    grid_spec = pltpu.PrefetchScalarGridSpec(
        num_scalar_prefetch=3,
        grid=(n_blocks,),
        in_specs=[pl.BlockSpec((bm, half), lambda i, be, nb, nx: (used(i, nb), 0)),
                  pl.BlockSpec(memory_space=pl.ANY),
                  pl.BlockSpec((None, 1, two_ff), lambda i, be, nb, nx: (be[i], 0, 0)),
                  pl.BlockSpec(memory_space=pl.ANY),
                  pl.BlockSpec((None, 1, d), lambda i, be, nb, nx: (be[i], 0, 0))],
        out_specs=pl.BlockSpec((bm, half), lambda i, be, nb, nx: (i, 0)),
        scratch_shapes=[pltpu.VMEM((d, two_ff), F32), pltpu.VMEM((d_ff, d), F32),
                        pltpu.VMEM((d, two_ff), BF16), pltpu.VMEM((d_ff, d), BF16),
                        pltpu.SemaphoreType.DMA((2,))],
    )
    return pl.pallas_call(
        _expert_ffn_kernel,
        grid_spec=grid_spec,
        out_shape=jax.ShapeDtypeStruct((n_slots, half), jnp.uint32),
        compiler_params=_params(),
        name="expert_ffn",
    )(block_expert, n_used, next_expert, xs, w_gate_up, b_gate_up.reshape(n_e, 1, two_ff),
      w_down, b_down.reshape(n_e, 1, d))


COMBINE_LOCAL_ROWS = 2560
SEG_ALIGN = 8


def _pieces(limit):
    sizes = []
    size = 1 << (limit.bit_length() - 1)
    while size >= SEG_ALIGN:
        sizes.append(size)
        size //= 2
    return sizes


def _combine_kernel(src_ref, len_ref, dst_ref, tot_ref, route_ref, local_ref, x1_ref, ys_hbm, o_ref, ybuf,
                    sem_g):
    j = pl.program_id(0)
    blocks = pl.num_programs(0) - 1
    tm = x1_ref.shape[0]
    half = x1_ref.shape[1] // 2
    local_rows = ybuf.shape[1]

    @pl.when(j == 0)
    def _():
        ybuf[...] = jnp.zeros_like(ybuf)

    @pl.when(j < blocks)
    def _():
        for s in range(2):
            @pl.when(j % 2 == s)
            def _(s=s):
                for e in range(N_EXPERTS):
                    seg = j * N_EXPERTS + e
                    n = len_ref[seg]
                    src = src_ref[seg]
                    dst = dst_ref[seg]
                    for size in _pieces(tm + 2 * SEG_ALIGN):
                        @pl.when((n & size) != 0)
                        def _(size=size, src=src, dst=dst):
                            pltpu.make_async_copy(
                                ys_hbm.at[pl.ds(pl.multiple_of(src, SEG_ALIGN), size), :],
                                ybuf.at[s, pl.ds(pl.multiple_of(dst, SEG_ALIGN), size), :],
                                sem_g.at[s]).start()
                        step = jnp.where((n & size) != 0, size, 0)
                        src = src + step
                        dst = dst + step

    @pl.when(j >= 1)
    def _():
        slot = (j - 1) % 2
        total = tot_ref[j - 1]
        for size in _pieces(local_rows):
            @pl.when((total & size) != 0)
            def _(size=size):
                pltpu.make_async_copy(ys_hbm.at[pl.ds(0, size), :], ybuf.at[slot, pl.ds(0, size), :],
                                      sem_g.at[slot]).wait()

        route = route_ref[...]
        local_t = local_ref[...]
        local = jnp.concatenate(
            [local_t, jnp.zeros((LANES - local_t.shape[0], tm), F32)], axis=0).T
        col = lax.broadcasted_iota(jnp.int32, (tm, local_rows), 1).astype(F32)
        g = jnp.zeros((tm, local_rows), F32)
        for k in range(TOP_K):
            pos = local[:, k: k + 1]
            gate = route[:, 2 * TOP_K + k: 2 * TOP_K + k + 1]
            g = jnp.where(col == pos, gate, g)
        g = g.astype(BF16)
        lo, hi = _unpack_pairs(ybuf[slot])
        o_ref[:, :half] = x1_ref[:, :half] + _dot(g, lo.astype(BF16))
        o_ref[:, half:] = x1_ref[:, half:] + _dot(g, hi.astype(BF16))


def _combine(tables, route, local_t, x1, ys, tm, local_rows):
    t, d = x1.shape
    half = d // 2

    def summed(j, *_):
        return (jnp.maximum(j - 1, 0), 0)

    grid_spec = pltpu.PrefetchScalarGridSpec(
        num_scalar_prefetch=4,
        grid=(t // tm + 1,),
        in_specs=[pl.BlockSpec((tm, LANES), summed),
                  pl.BlockSpec((local_t.shape[0], tm), lambda j, *_: (0, jnp.maximum(j - 1, 0))),
                  pl.BlockSpec((tm, d), summed),
                  pl.BlockSpec(memory_space=pl.ANY)],
        out_specs=pl.BlockSpec((tm, d), summed),
        scratch_shapes=[pltpu.VMEM((2, local_rows, half), jnp.uint32),
                        pltpu.SemaphoreType.DMA((2,))],
    )
    return pl.pallas_call(
        _combine_kernel,
        grid_spec=grid_spec,
        out_shape=jax.ShapeDtypeStruct((t, d), F32),
        compiler_params=_params(),
        name="combine",
    )(*tables, route, local_t, x1, ys)


def _per_expert(table, idx):
    out = jnp.zeros(idx.shape, table.dtype)
    for e in range(N_EXPERTS):
        out = jnp.where(idx == e, table[:, e][None, :, None], out)
    return out


def _combine_tables(idx, rank, n, pad_start, tm, local_rows):
    before = jnp.cumsum(n, axis=0) - n
    first = pad_start[None, :] + before
    src = first // SEG_ALIGN * SEG_ALIGN
    length = jnp.where(n > 0, (first + n + SEG_ALIGN - 1) // SEG_ALIGN * SEG_ALIGN - src, 0)
    dst = jnp.cumsum(length, axis=1) - length
    total = jnp.sum(length, axis=1)
    assert local_rows >= tm * TOP_K + N_EXPERTS * 2 * (SEG_ALIGN - 1)
    local = (_per_expert(dst + first - src - before, idx) + rank).reshape(TOP_K, -1).astype(F32)
    tables = (src.reshape(-1).astype(jnp.int32), length.reshape(-1).astype(jnp.int32),
              dst.reshape(-1).astype(jnp.int32), total.astype(jnp.int32))
    return tables, local


def _slot_layout(route_t, n, tm, bm):
    t = route_t.shape[1]
    blocks = t // tm
    idx = route_t[:TOP_K].astype(jnp.int32).reshape(TOP_K, blocks, tm)
    rank = route_t[TOP_K:].astype(jnp.int32).reshape(TOP_K, blocks, tm)
    counts = jnp.sum(n, axis=0)
    padded = (counts + bm - 1) // bm * bm
    pad_end = jnp.cumsum(padded).astype(jnp.int32)
    pad_start = pad_end - padded
    dest = _per_expert(jnp.broadcast_to(pad_start[None, :], n.shape), idx) + rank
    dest = dest.reshape(TOP_K, t // DISPATCH_ROWS, DISPATCH_ROWS).transpose(1, 0, 2)
    dest = dest.reshape(t // DISPATCH_ROWS, TOP_K * DISPATCH_ROWS)
    n_blocks = -(-t * TOP_K // bm) + N_EXPERTS
    block_start = jnp.arange(n_blocks, dtype=jnp.int32) * bm
    block_expert = jnp.minimum(jnp.sum(pad_end[None, :] <= block_start[:, None], axis=1),
                               N_EXPERTS - 1).astype(jnp.int32)
    n_used = (pad_end[-1:] // bm).astype(jnp.int32)
    following = jnp.take(pad_end, block_expert) // bm
    next_expert = jnp.where(following < n_used[0],
                            jnp.take(block_expert, jnp.minimum(following, n_blocks - 1)), -1).astype(jnp.int32)
    return idx, rank, padded, pad_end, dest, block_expert, next_expert, n_used, n_blocks * bm


def _rot_cols(w):
    half = w.shape[-1] // 2
    return jnp.concatenate([-w[..., half:], w[..., :half]], axis=-1)


def _swap_halves(g):
    half = g.shape[-1] // 2
    return jnp.concatenate([g[..., half:], g[..., :half]], axis=-1)


def kernel(x, attn_norm_g, w_in, a_q_norm_g, a_k_norm_g, rel_bias, q_a_norm_g, w_q_b, kv_a_norm_g,
           w_kv_b, b_q_norm_g, b_k_norm_g, w_out, ffn_norm_g, router_w, router_b, w_gate_up,
           b_gate_up, w_down, b_down):
    batch, seq, d = x.shape
    t = batch * seq
    depth = w_in.shape[0]
    tm = 512
    expert_bm = 512

    pos = jnp.arange(seq, dtype=F32)
    inv_freq = ROPE_THETA ** (-jnp.arange(0, QK_ROPE_DIM, 2, dtype=F32) / QK_ROPE_DIM)
    ang = pos[:, None] * inv_freq[None, :]
    cos, sin = jnp.cos(ang), jnp.sin(ang)
    cs = jnp.concatenate([cos, cos, sin, sin], axis=-1)

    row = jnp.arange(LANES)[:, None] // A_HEAD_DIM
    col = jnp.arange(LANES)[None, :] // A_HEAD_DIM
    head_block_ones = (row == col).astype(BF16)
    band_bias = _band_bias(rel_bias)

    x2 = x.reshape(t, d)
    for layer in range(depth):
        kpe_off = 3 * A_WIDTH + Q_LORA_RANK + KV_LORA_RANK
        w_kpe = w_in[layer][:, kpe_off:]
        w_in_r = jnp.concatenate([w_in[layer][:, :kpe_off], w_kpe, _rot_cols(w_kpe)], axis=1).astype(BF16)

        wq = w_q_b[layer].reshape(Q_LORA_RANK, B_HEADS, QK_HEAD_DIM)
        wq_rope = wq[..., QK_NOPE_DIM:]
        wq_r = jnp.concatenate([wq, _rot_cols(wq_rope)], axis=-1).reshape(Q_LORA_RANK, -1).astype(BF16)
        wkv = w_kv_b[layer].reshape(KV_LORA_RANK, B_HEADS, QK_NOPE_DIM + V_HEAD_DIM)
        wkv_r = jnp.concatenate([wkv[..., :QK_NOPE_DIM].reshape(KV_LORA_RANK, -1),
                                 wkv[..., QK_NOPE_DIM:].reshape(KV_LORA_RANK, -1)], axis=1).astype(BF16)
        gc = jnp.concatenate([q_a_norm_g[layer], kv_a_norm_g[layer]])[None, :]
        gq, gk = b_q_norm_g[layer], b_k_norm_g[layer]

        def rope_gain(gr):
            return jnp.concatenate([gr, _swap_halves(gr)])

        gqk = jnp.stack([gq[:QK_NOPE_DIM], rope_gain(gq[QK_NOPE_DIM:]),
                         gk[:QK_NOPE_DIM], rope_gain(gk[QK_NOPE_DIM:])])
        g_a = jnp.stack([jnp.tile(a_q_norm_g[layer], 2), jnp.tile(a_k_norm_g[layer], 2)])
        g_a = jnp.broadcast_to(g_a[None], (A_WIDTH // LANES, 2, LANES))

        proj, qb, kb, vb = _in_proj(x2, attn_norm_g[layer][None, :], w_in_r, cs, wq_r, wkv_r, gc, gqk, seq, tm)
        out_a, out_b = _attention(proj, head_block_ones, g_a, band_bias, qb, kb, vb, batch, seq, 512)

        rw = jnp.pad(router_w[layer], ((0, 0), (0, LANES - N_EXPERTS)))
        rw_hi = rw.astype(BF16)
        rw_lo = (rw - rw_hi.astype(F32)).astype(BF16)
        rb = jnp.pad(router_b[layer], (0, LANES - N_EXPERTS), constant_values=-jnp.inf)[None, :]
        x1, hp, route, route_t, in_block = _out_proj(
            x2, out_a, out_b, w_out[layer].astype(BF16), ffn_norm_g[layer][None, :],
            jnp.concatenate([rw_hi, rw_lo], axis=1), rb, tm)

        n = in_block[:, 0, :N_EXPERTS].astype(jnp.int32)
        idx, rank, padded, pad_end, dest, block_expert, next_expert, n_used, n_slots = _slot_layout(
            route_t, n, tm, expert_bm)
        xs = _dispatch(padded, pad_end, n_used, dest, hp, n_slots, DISPATCH_ROWS, expert_bm)
        ys = _expert_ffn(block_expert, n_used, next_expert, xs, w_gate_up[layer], b_gate_up[layer],
                         w_down[layer], b_down[layer], expert_bm)
        tables, local_t = _combine_tables(idx, rank, n, pad_end - padded, tm, COMBINE_LOCAL_ROWS)
        x2 = _combine(tables, route, local_t, x1, ys, tm, COMBINE_LOCAL_ROWS)
    return x2.reshape(batch, seq, d)
```

```python
import functools
import math

import jax
import jax.numpy as jnp
from jax import lax
from jax.experimental import pallas as pl
from jax.experimental.pallas import tpu as pltpu

A_HEADS = 8
A_HEAD_DIM = 64
A_WIDTH = A_HEADS * A_HEAD_DIM
DILATED_PATTERNS = ((128, 1), (512, 4), (2048, 16))
BAND_BLOCK = 128

B_HEADS = 4
QK_NOPE_DIM = 128
QK_ROPE_DIM = 64
QK_HEAD_DIM = QK_NOPE_DIM + QK_ROPE_DIM
V_HEAD_DIM = 128
Q_LORA_RANK = 256
KV_LORA_RANK = 256
B_WIDTH = B_HEADS * V_HEAD_DIM
ROPE_THETA = 10000.0

NUM_BUCKETS = 32
MAX_DISTANCE = 2048

N_EXPERTS = 32
TOP_K = 4
SWIGLU_LIMIT = 7.0
SWIGLU_ALPHA = 1.702
RMS_EPS = 1e-6

LANES = 128
MASK_VALUE = -1e30
LOG2E = math.log2(math.e)
PROJ_WIDTH = 3 * A_WIDTH + Q_LORA_RANK + KV_LORA_RANK + 2 * QK_ROPE_DIM
VMEM_LIMIT = 56 * 1024 * 1024

F32 = jnp.float32
BF16 = jnp.bfloat16


def _dot(a, b):
    return jnp.dot(a, b, preferred_element_type=F32)


def _dot_nt(a, b):
    return lax.dot_general(a, b, (((1,), (1,)), ((), ())), preferred_element_type=F32)


def _group_sum(x, m):
    return _dot(x.astype(BF16), m)


def _params(n_parallel=1):
    return pltpu.CompilerParams(
        dimension_semantics=("arbitrary",) * n_parallel, vmem_limit_bytes=VMEM_LIMIT)


IN_PROJ_SUB_ROWS = 256


def _mla_qkv(c, kper, cs, wq_ref, wkv_ref, gc, gqk, q_ref, k_ref, v_ref, rows, q_scale):
    def lora_norm(z, g):
        ms = jnp.mean(z * z, axis=-1, keepdims=True)
        return (z * lax.rsqrt(ms + RMS_EPS) * g).astype(BF16)

    cq = lora_norm(c[:, :Q_LORA_RANK], gc[:, :Q_LORA_RANK])
    ckv = lora_norm(c[:, Q_LORA_RANK:], gc[:, Q_LORA_RANK:])
    qb = _dot(cq, wq_ref[...])
    kvb = _dot(ckv, wkv_ref[...])
    n = c.shape[0]

    row = lax.broadcasted_iota(jnp.int32, (LANES, LANES), 0)
    ones_all = jnp.ones((LANES, LANES), BF16)
    ones_lo = jnp.where(row < QK_ROPE_DIM, 1.0, 0.0).astype(BF16)
    lane = lax.broadcasted_iota(jnp.int32, (n, LANES), 1)

    def rope(z, g_row):
        t = z * (g_row * cs)
        return jnp.where(lane < QK_ROPE_DIM, t + pltpu.roll(t, QK_ROPE_DIM, 1), 0.0)

    k_rope = rope(kper, gqk[3:4])
    kpe_ss = _group_sum(kper * kper, ones_lo)
    for h in range(B_HEADS):
        qn = qb[:, 2 * LANES * h: 2 * LANES * h + LANES]
        qr = qb[:, 2 * LANES * h + LANES: 2 * LANES * (h + 1)]
        ss = _group_sum(qn * qn, ones_all) + _group_sum(qr * qr, ones_lo)
        rs = lax.rsqrt(ss * (1.0 / QK_HEAD_DIM) + RMS_EPS) * q_scale
        q_ref[h, rows, :LANES] = (qn * gqk[0:1] * rs).astype(BF16)
        q_ref[h, rows, LANES:] = (rope(qr, gqk[1:2]) * rs).astype(BF16)
        kn = kvb[:, LANES * h: LANES * (h + 1)]
        ssk = _group_sum(kn * kn, ones_all) + kpe_ss
        rsk = lax.rsqrt(ssk * (1.0 / QK_HEAD_DIM) + RMS_EPS)
        k_ref[h, rows, :LANES] = (kn * gqk[2:3] * rsk).astype(BF16)
        k_ref[h, rows, LANES:] = (k_rope * rsk).astype(BF16)
    v_ref[rows, :] = kvb[:, B_HEADS * QK_NOPE_DIM:].astype(BF16)


def _in_proj_kernel(x_ref, g_ref, w_ref, cs_ref, wq_ref, wkv_ref, gc_ref, gqk_ref,
                    a_ref, q_ref, k_ref, v_ref, *, q_scale):
    a_width = a_ref.shape[1]
    c_width = Q_LORA_RANK + KV_LORA_RANK
    for r0 in range(0, x_ref.shape[0], IN_PROJ_SUB_ROWS):
        rows = slice(r0, r0 + IN_PROJ_SUB_ROWS)
        x = x_ref[rows, :]
        ms = jnp.mean(x * x, axis=-1, keepdims=True)
        h = (x * lax.rsqrt(ms + RMS_EPS) * g_ref[...]).astype(BF16)
        proj = _dot(h, w_ref[...])
        a_ref[rows, :] = proj[:, :a_width]
        _mla_qkv(proj[:, a_width:a_width + c_width], proj[:, a_width + c_width:], cs_ref[rows, :],
                 wq_ref, wkv_ref, gc_ref[...], gqk_ref[...], q_ref, k_ref, v_ref, rows, q_scale)


def _in_proj(x2, g, w, cs, wq, wkv, gc, gqk, seq, tm):
    t, d = x2.shape
    a_width = 3 * A_WIDTH
    pos_blocks = seq // tm
    kernel = functools.partial(_in_proj_kernel, q_scale=QK_HEAD_DIM ** -0.5 * LOG2E)
    return pl.pallas_call(
        kernel,
        grid=(t // tm,),
        in_specs=[pl.BlockSpec((tm, d), lambda i: (i, 0)),
                  pl.BlockSpec((1, d), lambda i: (0, 0)),
                  pl.BlockSpec(w.shape, lambda i: (0, 0)),
                  pl.BlockSpec((tm, LANES), lambda i: (i % pos_blocks, 0)),
                  pl.BlockSpec(wq.shape, lambda i: (0, 0)),
                  pl.BlockSpec(wkv.shape, lambda i: (0, 0)),
                  pl.BlockSpec(gc.shape, lambda i: (0, 0)),
                  pl.BlockSpec(gqk.shape, lambda i: (0, 0))],
        out_specs=[pl.BlockSpec((tm, a_width), lambda i: (i, 0)),
                   pl.BlockSpec((B_HEADS, tm, 2 * LANES), lambda i: (0, i, 0)),
                   pl.BlockSpec((B_HEADS, tm, 2 * LANES), lambda i: (0, i, 0)),
                   pl.BlockSpec((tm, B_WIDTH), lambda i: (i, 0))],
        out_shape=[jax.ShapeDtypeStruct((t, a_width), F32),
                   jax.ShapeDtypeStruct((B_HEADS, t, 2 * LANES), BF16),
                   jax.ShapeDtypeStruct((B_HEADS, t, 2 * LANES), BF16),
                   jax.ShapeDtypeStruct((t, B_WIDTH), BF16)],
        compiler_params=_params(),
        name="in_proj",
    )(x2, g, w, cs, wq, wkv, gc, gqk)


def _mla_attention(q_ref, k_ref, v_ref, o_ref, *, tq):
    seq = q_ref.shape[0]
    row = lax.broadcasted_iota(jnp.int32, (tq, tq), 0)
    col = lax.broadcasted_iota(jnp.int32, (tq, tq), 1)
    diag_mask = jnp.where(col <= row, 0.0, MASK_VALUE)
    v_ext = jnp.concatenate([v_ref[...], jnp.ones(v_ref.shape, v_ref.dtype)], axis=1)
    dv = v_ref.shape[1]
    for i in reversed(range(seq // tq)):
        q = q_ref[i * tq:(i + 1) * tq, :]
        s_diag = _dot_nt(q, k_ref[i * tq:(i + 1) * tq, :]) + diag_mask
        m = jnp.max(s_diag, axis=-1, keepdims=True)
        if i > 0:
            s_past = _dot_nt(q, k_ref[:i * tq, :])
            m = jnp.maximum(m, jnp.max(s_past, axis=-1, keepdims=True))
        o = _dot(jnp.exp2(s_diag - m).astype(BF16), v_ext[i * tq:(i + 1) * tq, :])
        if i > 0:
            o = o + _dot(jnp.exp2(s_past - m).astype(BF16), v_ext[:i * tq, :])
        o_ref[i * tq:(i + 1) * tq, :] = (o[:, :dv] / o[:, dv:]).astype(o_ref.dtype)


def _dilated_attention(q_ref, k_ref, v_ref, bd_ref, g_ref, bias_ref, o_ref, qs, ks, acc_o, acc_l, acc_m,
                       *, unroll):
    seq = q_ref.shape[0]
    n_tiles = seq // BAND_BLOCK
    low = lax.broadcasted_iota(jnp.int32, (BAND_BLOCK, LANES), 1) < A_HEAD_DIM
    bd = bd_ref[...]
    g = g_ref[...]

    def head_norm(z, g_row):
        ss = _group_sum(z * z, bd)
        return z * lax.rsqrt(ss * (1.0 / A_HEAD_DIM) + RMS_EPS) * g_row

    qs[...] = head_norm(q_ref[...], g[0:1]) * (A_HEAD_DIM ** -0.5 * LOG2E)
    ks[...] = head_norm(k_ref[...], g[1:2])
    ones = jnp.ones((2 * BAND_BLOCK, LANES), BF16)

    for p, (window, dil) in reversed(list(enumerate(DILATED_PATTERNS))):
        blocks_per_class = n_tiles // dil
        span = BAND_BLOCK * dil

        def rows_at(start, dil=dil):
            if dil == 1:
                return pl.ds(start, BAND_BLOCK)
            return pl.ds(start, BAND_BLOCK, stride=dil)

        def tile(i, carry, p=p, blocks_per_class=blocks_per_class, span=span, rows_at=rows_at):
            r = i // blocks_per_class
            j = i % blocks_per_class
            cur = rows_at(r + j * span)
            prev = rows_at(r + jnp.maximum(j - 1, 0) * span)
            first = jnp.where(j == 0, 1, 0)
            q = qs[cur, :]
            q2 = jnp.concatenate([jnp.where(low, q, 0.0), jnp.where(low, 0.0, q)], axis=0).astype(BF16)
            k_band = jnp.concatenate([ks[prev, :], ks[cur, :]], axis=0).astype(BF16)
            s = _dot_nt(q2, k_band) + bias_ref[p, first]
            m = jnp.max(s, axis=-1, keepdims=True)
            pr = jnp.exp2(s - m).astype(BF16)
            v_band = jnp.concatenate([v_ref[prev, :], v_ref[cur, :]], axis=0).astype(BF16)
            o = _dot(pr, jnp.concatenate([v_band, ones], axis=1))
            top, bot = o[:BAND_BLOCK], o[BAND_BLOCK:]
            acc_o[p, cur, :] = jnp.where(low, top[:, :LANES], bot[:, :LANES])
            acc_l[p, cur, :] = jnp.where(low, top[:, LANES:], bot[:, LANES:])
            acc_m[p, cur, :] = jnp.where(low, m[:BAND_BLOCK], m[BAND_BLOCK:])
            return carry

        lax.fori_loop(0, n_tiles, tile, 0, unroll=unroll)

    m_all = jnp.maximum(jnp.maximum(acc_m[0], acc_m[1]), acc_m[2])
    num = jnp.zeros((seq, LANES), F32)
    den = jnp.zeros((seq, LANES), F32)
    for p in range(len(DILATED_PATTERNS)):
        w = jnp.exp2(acc_m[p] - m_all)
        num = num + w * acc_o[p]
        den = den + w * acc_l[p]
    o_ref[...] = (num / den).astype(o_ref.dtype)


def _attention_kernel(qa_ref, ka_ref, va_ref, bd_ref, g_ref, bias_ref, qb_ref, kb_ref, vb_ref,
                      oa_ref, ob_ref, qs, ks, acc_o, acc_l, acc_m, *, unroll, tq):
    _mla_attention(qb_ref, kb_ref, vb_ref, ob_ref, tq=tq)
    _dilated_attention(qa_ref, ka_ref, va_ref, bd_ref, g_ref, bias_ref, oa_ref, qs, ks, acc_o, acc_l, acc_m,
                       unroll=unroll)


def _attention(proj, bd, g, bias, qb, kb, vb, batch, seq, tq):
    t = proj.shape[0]
    pairs = A_WIDTH // LANES
    assert pairs == B_HEADS
    n_pat = len(DILATED_PATTERNS)
    return pl.pallas_call(
        functools.partial(_attention_kernel, unroll=16, tq=tq),
        grid=(batch, pairs),
        in_specs=[pl.BlockSpec((seq, LANES), lambda b, c: (b, c)),
                  pl.BlockSpec((seq, LANES), lambda b, c: (b, pairs + c)),
                  pl.BlockSpec((seq, LANES), lambda b, c: (b, 2 * pairs + c)),
                  pl.BlockSpec((LANES, LANES), lambda b, c: (0, 0)),
                  pl.BlockSpec((None, 2, LANES), lambda b, c: (c, 0, 0)),
                  pl.BlockSpec((None, n_pat, 2, 2 * BAND_BLOCK, 2 * BAND_BLOCK),
                               lambda b, c: (c, 0, 0, 0, 0)),
                  pl.BlockSpec((None, seq, 2 * LANES), lambda b, h: (h, b, 0)),
                  pl.BlockSpec((None, seq, 2 * LANES), lambda b, h: (h, b, 0)),
                  pl.BlockSpec((seq, V_HEAD_DIM), lambda b, h: (b, h))],
        out_specs=[pl.BlockSpec((seq, LANES), lambda b, c: (b, c)),
                   pl.BlockSpec((seq, V_HEAD_DIM), lambda b, h: (b, h))],
        out_shape=[jax.ShapeDtypeStruct((t, A_WIDTH), BF16),
                   jax.ShapeDtypeStruct((t, B_WIDTH), BF16)],
        scratch_shapes=[pltpu.VMEM((seq, LANES), F32)] * 2
                       + [pltpu.VMEM((n_pat, seq, LANES), F32)] * 3,
        compiler_params=_params(2),
        name="attention",
    )(proj, proj, proj, bd, g, bias, qb, kb, vb)


def _t5_bucket(dist):
    max_exact = NUM_BUCKETS // 2
    df = jnp.maximum(dist, 1).astype(F32)
    log_bucket = max_exact + (jnp.log(df / max_exact) / math.log(MAX_DISTANCE / max_exact)
                              * (NUM_BUCKETS - max_exact)).astype(jnp.int32)
    log_bucket = jnp.minimum(log_bucket, NUM_BUCKETS - 1)
    return jnp.where(dist < max_exact, dist, log_bucket)


def _band_bias(rel_bias):
    n = BAND_BLOCK
    qi = jnp.arange(n)[:, None]
    kj = jnp.arange(n)[None, :]
    buckets = jnp.arange(NUM_BUCKETS)
    tables = []
    for window, dil in DILATED_PATTERNS:
        steps = window // dil
        halves = []
        for back in (qi - kj + n, qi - kj):
            onehot = (_t5_bucket(jnp.maximum(back, 0) * dil)[:, :, None] == buckets).astype(F32)
            vals = jnp.einsum('qkb,bh->hqk', onehot, rel_bias.astype(F32),
                              precision=lax.Precision.HIGHEST) * LOG2E
            halves.append(jnp.where(((back >= 0) & (back <= steps))[None], vals, MASK_VALUE))
        prev, cur = halves
        normal = jnp.concatenate([prev, cur], axis=-1)
        first = jnp.concatenate([jnp.full_like(prev, MASK_VALUE), cur], axis=-1)
        tables.append(jnp.stack([normal, first], axis=1))
    tab = jnp.stack(tables, axis=1)
    tab = tab.reshape(A_HEADS // 2, 2, len(DILATED_PATTERNS), 2, n, 2 * n)
    return tab.transpose(0, 2, 3, 1, 4, 5).reshape(A_HEADS // 2, len(DILATED_PATTERNS), 2, 2 * n, 2 * n)


def _pack_pairs(v):
    half = v.shape[1] // 2
    bits = pltpu.bitcast(v, jnp.uint32)
    return (bits[:, :half] >> 16) | (bits[:, half:] & jnp.uint32(0xFFFF0000))


def _unpack_pairs(p):
    return (pltpu.bitcast(p << 16, F32), pltpu.bitcast(p & jnp.uint32(0xFFFF0000), F32))


def _out_proj_kernel(x_ref, a_ref, b_ref, w_ref, g_ref, rw_ref, rb_ref,
                     x1_ref, hp_ref, route_ref, route_t_ref, blk_ref, carry):
    i = pl.program_id(0)
    tm = x_ref.shape[0]

    @pl.when(i == 0)
    def _():
        carry[...] = jnp.zeros_like(carry)

    w = w_ref[...]
    x1 = x_ref[...] + _dot(a_ref[...], w[:A_WIDTH]) + _dot(b_ref[...], w[A_WIDTH:])
    x1_ref[...] = x1
    ms = jnp.mean(x1 * x1, axis=-1, keepdims=True)
    h = x1 * lax.rsqrt(ms + RMS_EPS) * g_ref[...]
    hi = h.astype(BF16)
    hi_f = hi.astype(F32)
    hp_ref[...] = _pack_pairs(hi_f)
    lo = (h - hi_f).astype(BF16)
    rw = rw_ref[...]
    hw = _dot(hi, rw)
    logits = hw[:, :LANES] + hw[:, LANES:] + _dot(lo, rw[:, :LANES]) + rb_ref[...]

    lane = lax.broadcasted_iota(jnp.int32, (tm, LANES), 1).astype(F32)
    remaining = logits
    vals, hots = [], []
    for _ in range(TOP_K):
        m = jnp.max(remaining, axis=-1, keepdims=True)
        first = jnp.min(jnp.where(remaining == m, lane, float(LANES)), axis=-1, keepdims=True)
        hot = lane == first
        remaining = jnp.where(hot, -jnp.inf, remaining)
        vals.append(m)
        hots.append(hot)
    exps = [jnp.exp(v - vals[0]) for v in vals]
    den = exps[0] + exps[1] + exps[2] + exps[3]

    chosen = jnp.zeros((tm, LANES), F32)
    for hot in hots:
        chosen = chosen + jnp.where(hot, 1.0, 0.0)
    r = lax.broadcasted_iota(jnp.int32, (tm, tm), 0)
    c = lax.broadcasted_iota(jnp.int32, (tm, tm), 1)
    earlier = jnp.where(r > c, 1.0, 0.0).astype(BF16)
    before = carry[...] + _dot(earlier, chosen.astype(BF16))
    in_block = jnp.sum(chosen, axis=0, keepdims=True)
    carry[...] = carry[...] + in_block
    blk_ref[...] = jnp.broadcast_to(in_block, blk_ref.shape)

    route = jnp.zeros((tm, LANES), F32)
    for k in range(TOP_K):
        first = jnp.sum(jnp.where(hots[k], lane, 0.0), axis=-1, keepdims=True)
        rank = jnp.sum(jnp.where(hots[k], before, 0.0), axis=-1, keepdims=True)
        route = route + jnp.where(lane == float(k), first, 0.0)
        route = route + jnp.where(lane == float(TOP_K + k), rank, 0.0)
        route = route + jnp.where(lane == float(2 * TOP_K + k), exps[k] / den, 0.0)
    route_ref[...] = route
    route_t_ref[...] = route.T[:route_t_ref.shape[0], :]


def _out_proj(x2, out_a, out_b, w, g, rw, rb, tm):
    t, d = x2.shape
    return pl.pallas_call(
        _out_proj_kernel,
        grid=(t // tm,),
        in_specs=[pl.BlockSpec((tm, d), lambda i: (i, 0)),
                  pl.BlockSpec((tm, A_WIDTH), lambda i: (i, 0)),
                  pl.BlockSpec((tm, B_WIDTH), lambda i: (i, 0)),
                  pl.BlockSpec(w.shape, lambda i: (0, 0)),
                  pl.BlockSpec((1, d), lambda i: (0, 0)),
                  pl.BlockSpec(rw.shape, lambda i: (0, 0)),
                  pl.BlockSpec((1, LANES), lambda i: (0, 0))],
        out_specs=[pl.BlockSpec((tm, d), lambda i: (i, 0)),
                   pl.BlockSpec((tm, d // 2), lambda i: (i, 0)),
                   pl.BlockSpec((tm, LANES), lambda i: (i, 0)),
                   pl.BlockSpec((2 * TOP_K, tm), lambda i: (0, i)),
                   pl.BlockSpec((None, 8, LANES), lambda i: (i, 0, 0))],
        out_shape=[jax.ShapeDtypeStruct((t, d), F32),
                   jax.ShapeDtypeStruct((t, d // 2), jnp.uint32),
                   jax.ShapeDtypeStruct((t, LANES), F32),
                   jax.ShapeDtypeStruct((2 * TOP_K, t), F32),
                   jax.ShapeDtypeStruct((t // tm, 8, LANES), F32)],
        scratch_shapes=[pltpu.VMEM((1, LANES), F32)],
        compiler_params=_params(),
        name="out_proj",
    )(x2, out_a, out_b, w, g, rw, rb)


DISPATCH_ROWS = 1024


def _dispatch_kernel(pad_ref, end_ref, nb_ref, dest_hbm, h_ref, xs_hbm, dest_s0, dest_s1, zeros,
                     sem_i, sem_z, sem_o, *, bm):
    i = pl.program_id(0)
    steps = pl.num_programs(0)
    tm = h_ref.shape[0]
    n_blocks = xs_hbm.shape[0] // bm
    dest_s = (dest_s0, dest_s1)

    def idx_copy(step, slot):
        return pltpu.make_async_copy(dest_hbm.at[step], dest_s[slot], sem_i.at[slot])

    @pl.when(i == 0)
    def _():
        idx_copy(0, 0).start()
        zeros[...] = jnp.zeros_like(zeros)

        def zero_block(start):
            return pltpu.make_async_copy(zeros, xs_hbm.at[pl.ds(pl.multiple_of(start, bm), bm), :], sem_z)

        for e in range(N_EXPERTS):
            @pl.when(pad_ref[e] > 0)
            def _():
                zero_block(end_ref[e] - bm).start()

        def start_unused(b, carry):
            zero_block(b * bm).start()
            return carry

        def wait_unused(b, carry):
            zero_block(b * bm).wait()
            return carry

        lax.fori_loop(nb_ref[0], n_blocks, start_unused, 0)
        for e in range(N_EXPERTS):
            @pl.when(pad_ref[e] > 0)
            def _():
                zero_block(end_ref[e] - bm).wait()
        lax.fori_loop(nb_ref[0], n_blocks, wait_unused, 0)

    for slot in range(2):
        @pl.when(i % 2 == slot)
        def _(slot=slot):
            @pl.when(i + 1 < steps)
            def _():
                idx_copy(i + 1, 1 - slot).start()

            idx_copy(i, slot).wait()

            def body(t, carry):
                for k in range(TOP_K):
                    pltpu.make_async_copy(h_ref.at[pl.ds(t, 1), :],
                                          xs_hbm.at[pl.ds(dest_s[slot][k * tm + t], 1), :],
                                          sem_o).start(priority=k % 2)
                return carry

            lax.fori_loop(0, tm, body, 0, unroll=8)
    for _ in range(TOP_K):
        pltpu.make_async_copy(h_ref, xs_hbm.at[pl.ds(0, tm), :], sem_o).wait()


def _dispatch(padded, pad_end, n_used, dest, hp, n_slots, tm, bm):
    t, half = hp.shape
    grid_spec = pltpu.PrefetchScalarGridSpec(
        num_scalar_prefetch=3,
        grid=(t // tm,),
        in_specs=[pl.BlockSpec(memory_space=pl.ANY),
                  pl.BlockSpec((tm, half), lambda i, p, e, n: (i, 0))],
        out_specs=pl.BlockSpec(memory_space=pl.ANY),
        scratch_shapes=[pltpu.SMEM((tm * TOP_K,), jnp.int32),
                        pltpu.SMEM((tm * TOP_K,), jnp.int32),
                        pltpu.VMEM((bm, half), jnp.uint32),
                        pltpu.SemaphoreType.DMA((2,)),
                        pltpu.SemaphoreType.DMA(()),
                        pltpu.SemaphoreType.DMA(())],
    )
    return pl.pallas_call(
        functools.partial(_dispatch_kernel, bm=bm),
        grid_spec=grid_spec,
        out_shape=jax.ShapeDtypeStruct((n_slots, half), jnp.uint32),
        compiler_params=_params(),
        name="dispatch",
    )(padded, pad_end, n_used, dest, hp)


EXPERT_SUB_ROWS = 256


def _expert_ffn_kernel(be_ref, nb_ref, next_ref, x_ref, wgu_hbm, bgu_ref, wd_hbm, bd_ref, y_ref,
                       wgu_f, wd_f, wgu_s, wd_s, sem_w):
    i = pl.program_id(0)
    d_ff = wd_f.shape[0]
    half = x_ref.shape[1]

    def weight_copies(e):
        return (pltpu.make_async_copy(wgu_hbm.at[e], wgu_f, sem_w.at[0]),
                pltpu.make_async_copy(wd_hbm.at[e], wd_f, sem_w.at[1]))

    @pl.when(jnp.logical_and(i == 0, nb_ref[0] > 0))
    def _():
        for copy in weight_copies(be_ref[0]):
            copy.start()

    @pl.when(i >= nb_ref[0])
    def _():
        y_ref[...] = jnp.zeros_like(y_ref)

    @pl.when(i < nb_ref[0])
    def _():
        changed = jnp.logical_or(i == 0, be_ref[i] != be_ref[jnp.maximum(i - 1, 0)])

        @pl.when(changed)
        def _():
            for copy in weight_copies(be_ref[i]):
                copy.wait()
            wgu_s[...] = wgu_f[...].astype(BF16)
            wd_s[...] = wd_f[...].astype(BF16)

            @pl.when(next_ref[i] >= 0)
            def _():
                for copy in weight_copies(next_ref[i]):
                    copy.start()

        for r0 in range(0, x_ref.shape[0], EXPERT_SUB_ROWS):
            rows = slice(r0, r0 + EXPERT_SUB_ROWS)
            x_lo, x_hi = _unpack_pairs(x_ref[rows, :])
            gu = (_dot(x_lo.astype(BF16), wgu_s[:half, :]) + _dot(x_hi.astype(BF16), wgu_s[half:, :])
                  + bgu_ref[...])
            gate = jnp.minimum(gu[:, :d_ff], SWIGLU_LIMIT)
            up = jnp.clip(gu[:, d_ff:], -SWIGLU_LIMIT, SWIGLU_LIMIT)
            glu = gate * jax.nn.sigmoid(SWIGLU_ALPHA * gate)
            act = ((up + 1.0) * glu).astype(BF16)
            y = _dot(act, wd_s[...]) + bd_ref[...]
            y_ref[rows, :] = _pack_pairs(y.astype(BF16).astype(F32))


def _expert_ffn(block_expert, n_used, next_expert, xs, w_gate_up, b_gate_up, w_down, b_down, bm):
    n_slots, half = xs.shape
    n_e, d, two_ff = w_gate_up.shape
    d_ff = two_ff // 2
    n_blocks = n_slots // bm

    def used(i, nb):
        return jnp.minimum(i, jnp.maximum(nb[0] - 1, 0))

    grid_spec = pltpu.PrefetchScalarGridSpec(
        num_scalar_prefetch=3,
        grid=(n_blocks,),
        in_specs=[pl.BlockSpec((bm, half), lambda i, be, nb, nx: (used(i, nb), 0)),
                  pl.BlockSpec(memory_space=pl.ANY),
                  pl.BlockSpec((None, 1, two_ff), lambda i, be, nb, nx: (be[i], 0, 0)),
                  pl.BlockSpec(memory_space=pl.ANY),
                  pl.BlockSpec((None, 1, d), lambda i, be, nb, nx: (be[i], 0, 0))],
        out_specs=pl.BlockSpec((bm, half), lambda i, be, nb, nx: (i, 0)),
        scratch_shapes=[pltpu.VMEM((d, two_ff), F32), pltpu.VMEM((d_ff, d), F32),
                        pltpu.VMEM((d, two_ff), BF16), pltpu.VMEM((d_ff, d), BF16),
                        pltpu.SemaphoreType.DMA((2,))],
    )
    return pl.pallas_call(
        _expert_ffn_kernel,
        grid_spec=grid_spec,
        out_shape=jax.ShapeDtypeStruct((n_slots, half), jnp.uint32),
        compiler_params=_params(),
        name="expert_ffn",
    )(block_expert, n_used, next_expert, xs, w_gate_up, b_gate_up.reshape(n_e, 1, two_ff),
      w_down, b_down.reshape(n_e, 1, d))


COMBINE_LOCAL_ROWS = 2560
SEG_ALIGN = 8


def _pieces(limit):
    sizes = []
    size = 1 << (limit.bit_length() - 1)
    while size >= SEG_ALIGN:
        sizes.append(size)
        size //= 2
    return sizes


def _combine_kernel(src_ref, len_ref, dst_ref, tot_ref, route_ref, local_ref, x1_ref, ys_hbm, o_ref, ybuf,
                    sem_g):
    j = pl.program_id(0)
    blocks = pl.num_programs(0) - 1
    tm = x1_ref.shape[0]
    half = x1_ref.shape[1] // 2
    local_rows = ybuf.shape[1]

    @pl.when(j == 0)
    def _():
        ybuf[...] = jnp.zeros_like(ybuf)

    @pl.when(j < blocks)
    def _():
        for s in range(2):
            @pl.when(j % 2 == s)
            def _(s=s):
                for e in range(N_EXPERTS):
                    seg = j * N_EXPERTS + e
                    n = len_ref[seg]
                    src = src_ref[seg]
                    dst = dst_ref[seg]
                    for size in _pieces(tm + 2 * SEG_ALIGN):
                        @pl.when((n & size) != 0)
                        def _(size=size, src=src, dst=dst):
                            pltpu.make_async_copy(
                                ys_hbm.at[pl.ds(pl.multiple_of(src, SEG_ALIGN), size), :],
                                ybuf.at[s, pl.ds(pl.multiple_of(dst, SEG_ALIGN), size), :],
                                sem_g.at[s]).start()
                        step = jnp.where((n & size) != 0, size, 0)
                        src = src + step
                        dst = dst + step

    @pl.when(j >= 1)
    def _():
        slot = (j - 1) % 2
        total = tot_ref[j - 1]
        for size in _pieces(local_rows):
            @pl.when((total & size) != 0)
            def _(size=size):
                pltpu.make_async_copy(ys_hbm.at[pl.ds(0, size), :], ybuf.at[slot, pl.ds(0, size), :],
                                      sem_g.at[slot]).wait()

        route = route_ref[...]
        local_t = local_ref[...]
        local = jnp.concatenate(
            [local_t, jnp.zeros((LANES - local_t.shape[0], tm), F32)], axis=0).T
        col = lax.broadcasted_iota(jnp.int32, (tm, local_rows), 1).astype(F32)
        g = jnp.zeros((tm, local_rows), F32)
        for k in range(TOP_K):
            pos = local[:, k: k + 1]
            gate = route[:, 2 * TOP_K + k: 2 * TOP_K + k + 1]
            g = jnp.where(col == pos, gate, g)
        g = g.astype(BF16)
        lo, hi = _unpack_pairs(ybuf[slot])
        o_ref[:, :half] = x1_ref[:, :half] + _dot(g, lo.astype(BF16))
        o_ref[:, half:] = x1_ref[:, half:] + _dot(g, hi.astype(BF16))


def _combine(tables, route, local_t, x1, ys, tm, local_rows):
    t, d = x1.shape
    half = d // 2

    def summed(j, *_):
        return (jnp.maximum(j - 1, 0), 0)

    grid_spec = pltpu.PrefetchScalarGridSpec(
        num_scalar_prefetch=4,
        grid=(t // tm + 1,),
        in_specs=[pl.BlockSpec((tm, LANES), summed),
                  pl.BlockSpec((local_t.shape[0], tm), lambda j, *_: (0, jnp.maximum(j - 1, 0))),
                  pl.BlockSpec((tm, d), summed),
                  pl.BlockSpec(memory_space=pl.ANY)],
        out_specs=pl.BlockSpec((tm, d), summed),
        scratch_shapes=[pltpu.VMEM((2, local_rows, half), jnp.uint32),
                        pltpu.SemaphoreType.DMA((2,))],
    )
    return pl.pallas_call(
        _combine_kernel,
        grid_spec=grid_spec,
        out_shape=jax.ShapeDtypeStruct((t, d), F32),
        compiler_params=_params(),
        name="combine",
    )(*tables, route, local_t, x1, ys)


def _per_expert(table, idx):
    out = jnp.zeros(idx.shape, table.dtype)
    for e in range(N_EXPERTS):
        out = jnp.where(idx == e, table[:, e][None, :, None], out)
    return out


def _combine_tables(idx, rank, n, pad_start, tm, local_rows):
    before = jnp.cumsum(n, axis=0) - n
    first = pad_start[None, :] + before
    src = first // SEG_ALIGN * SEG_ALIGN
    length = jnp.where(n > 0, (first + n + SEG_ALIGN - 1) // SEG_ALIGN * SEG_ALIGN - src, 0)
    dst = jnp.cumsum(length, axis=1) - length
    total = jnp.sum(length, axis=1)
    assert local_rows >= tm * TOP_K + N_EXPERTS * 2 * (SEG_ALIGN - 1)
    local = (_per_expert(dst + first - src - before, idx) + rank).reshape(TOP_K, -1).astype(F32)
    tables = (src.reshape(-1).astype(jnp.int32), length.reshape(-1).astype(jnp.int32),
              dst.reshape(-1).astype(jnp.int32), total.astype(jnp.int32))
    return tables, local


def _slot_layout(route_t, n, tm, bm):
    t = route_t.shape[1]
    blocks = t // tm
    idx = route_t[:TOP_K].astype(jnp.int32).reshape(TOP_K, blocks, tm)
    rank = route_t[TOP_K:].astype(jnp.int32).reshape(TOP_K, blocks, tm)
    counts = jnp.sum(n, axis=0)
    padded = (counts + bm - 1) // bm * bm
    pad_end = jnp.cumsum(padded).astype(jnp.int32)
    pad_start = pad_end - padded
    dest = _per_expert(jnp.broadcast_to(pad_start[None, :], n.shape), idx) + rank
    dest = dest.reshape(TOP_K, t // DISPATCH_ROWS, DISPATCH_ROWS).transpose(1, 0, 2)
    dest = dest.reshape(t // DISPATCH_ROWS, TOP_K * DISPATCH_ROWS)
    n_blocks = -(-t * TOP_K // bm) + N_EXPERTS
    block_start = jnp.arange(n_blocks, dtype=jnp.int32) * bm
    block_expert = jnp.minimum(jnp.sum(pad_end[None, :] <= block_start[:, None], axis=1),
                               N_EXPERTS - 1).astype(jnp.int32)
    n_used = (pad_end[-1:] // bm).astype(jnp.int32)
    following = jnp.take(pad_end, block_expert) // bm
    next_expert = jnp.where(following < n_used[0],
                            jnp.take(block_expert, jnp.minimum(following, n_blocks - 1)), -1).astype(jnp.int32)
    return idx, rank, padded, pad_end, dest, block_expert, next_expert, n_used, n_blocks * bm


def _rot_cols(w):
    half = w.shape[-1] // 2
    return jnp.concatenate([-w[..., half:], w[..., :half]], axis=-1)


def _swap_halves(g):
    half = g.shape[-1] // 2
    return jnp.concatenate([g[..., half:], g[..., :half]], axis=-1)


def kernel(x, attn_norm_g, w_in, a_q_norm_g, a_k_norm_g, rel_bias, q_a_norm_g, w_q_b, kv_a_norm_g,
           w_kv_b, b_q_norm_g, b_k_norm_g, w_out, ffn_norm_g, router_w, router_b, w_gate_up,
           b_gate_up, w_down, b_down):
    batch, seq, d = x.shape
    t = batch * seq
    depth = w_in.shape[0]
    tm = 512
    expert_bm = 512

    pos = jnp.arange(seq, dtype=F32)
    inv_freq = ROPE_THETA ** (-jnp.arange(0, QK_ROPE_DIM, 2, dtype=F32) / QK_ROPE_DIM)
    ang = pos[:, None] * inv_freq[None, :]
    cos, sin = jnp.cos(ang), jnp.sin(ang)
    cs = jnp.concatenate([cos, cos, sin, sin], axis=-1)

    row = jnp.arange(LANES)[:, None] // A_HEAD_DIM
    col = jnp.arange(LANES)[None, :] // A_HEAD_DIM
    head_block_ones = (row == col).astype(BF16)
    band_bias = _band_bias(rel_bias)

    x2 = x.reshape(t, d)
    for layer in range(depth):
        kpe_off = 3 * A_WIDTH + Q_LORA_RANK + KV_LORA_RANK
        w_kpe = w_in[layer][:, kpe_off:]
        w_in_r = jnp.concatenate([w_in[layer][:, :kpe_off], w_kpe, _rot_cols(w_kpe)], axis=1).astype(BF16)

        wq = w_q_b[layer].reshape(Q_LORA_RANK, B_HEADS, QK_HEAD_DIM)
        wq_rope = wq[..., QK_NOPE_DIM:]
        wq_r = jnp.concatenate([wq, _rot_cols(wq_rope)], axis=-1).reshape(Q_LORA_RANK, -1).astype(BF16)
        wkv = w_kv_b[layer].reshape(KV_LORA_RANK, B_HEADS, QK_NOPE_DIM + V_HEAD_DIM)
        wkv_r = jnp.concatenate([wkv[..., :QK_NOPE_DIM].reshape(KV_LORA_RANK, -1),
                                 wkv[..., QK_NOPE_DIM:].reshape(KV_LORA_RANK, -1)], axis=1).astype(BF16)
        gc = jnp.concatenate([q_a_norm_g[layer], kv_a_norm_g[layer]])[None, :]
        gq, gk = b_q_norm_g[layer], b_k_norm_g[layer]

        def rope_gain(gr):
            return jnp.concatenate([gr, _swap_halves(gr)])

        gqk = jnp.stack([gq[:QK_NOPE_DIM], rope_gain(gq[QK_NOPE_DIM:]),
                         gk[:QK_NOPE_DIM], rope_gain(gk[QK_NOPE_DIM:])])
        g_a = jnp.stack([jnp.tile(a_q_norm_g[layer], 2), jnp.tile(a_k_norm_g[layer], 2)])
        g_a = jnp.broadcast_to(g_a[None], (A_WIDTH // LANES, 2, LANES))

        proj, qb, kb, vb = _in_proj(x2, attn_norm_g[layer][None, :], w_in_r, cs, wq_r, wkv_r, gc, gqk, seq, tm)
        out_a, out_b = _attention(proj, head_block_ones, g_a, band_bias, qb, kb, vb, batch, seq, 512)

        rw = jnp.pad(router_w[layer], ((0, 0), (0, LANES - N_EXPERTS)))
        rw_hi = rw.astype(BF16)
        rw_lo = (rw - rw_hi.astype(F32)).astype(BF16)
        rb = jnp.pad(router_b[layer], (0, LANES - N_EXPERTS), constant_values=-jnp.inf)[None, :]
        x1, hp, route, route_t, in_block = _out_proj(
            x2, out_a, out_b, w_out[layer].astype(BF16), ffn_norm_g[layer][None, :],
            jnp.concatenate([rw_hi, rw_lo], axis=1), rb, tm)

        n = in_block[:, 0, :N_EXPERTS].astype(jnp.int32)
        idx, rank, padded, pad_end, dest, block_expert, next_expert, n_used, n_slots = _slot_layout(
            route_t, n, tm, expert_bm)
        xs = _dispatch(padded, pad_end, n_used, dest, hp, n_slots, DISPATCH_ROWS, expert_bm)
        ys = _expert_ffn(block_expert, n_used, next_expert, xs, w_gate_up[layer], b_gate_up[layer],
                         w_down[layer], b_down[layer], expert_bm)
        tables, local_t = _combine_tables(idx, rank, n, pad_end - padded, tm, COMBINE_LOCAL_ROWS)
        x2 = _combine(tables, route, local_t, x1, ys, tm, COMBINE_LOCAL_ROWS)
    return x2.reshape(batch, seq, d)
```

```python
import functools
import math

import jax
import jax.numpy as jnp
from jax import lax
from jax.experimental import pallas as pl
from jax.experimental.pallas import tpu as pltpu

A_HEADS = 8
A_HEAD_DIM = 64
A_WIDTH = A_HEADS * A_HEAD_DIM
DILATED_PATTERNS = ((128, 1), (512, 4), (2048, 16))
BAND_BLOCK = 128

B_HEADS = 4
QK_NOPE_DIM = 128
QK_ROPE_DIM = 64
QK_HEAD_DIM = QK_NOPE_DIM + QK_ROPE_DIM
V_HEAD_DIM = 128
Q_LORA_RANK = 256
KV_LORA_RANK = 256
B_WIDTH = B_HEADS * V_HEAD_DIM
ROPE_THETA = 10000.0

NUM_BUCKETS = 32
MAX_DISTANCE = 2048

N_EXPERTS = 32
TOP_K = 4
SWIGLU_LIMIT = 7.0
SWIGLU_ALPHA = 1.702
RMS_EPS = 1e-6

LANES = 128
MASK_VALUE = -1e30
LOG2E = math.log2(math.e)
PROJ_WIDTH = 3 * A_WIDTH + Q_LORA_RANK + KV_LORA_RANK + 2 * QK_ROPE_DIM
VMEM_LIMIT = 56 * 1024 * 1024

F32 = jnp.float32
BF16 = jnp.bfloat16


def _dot(a, b):
    return jnp.dot(a, b, preferred_element_type=F32)


def _dot_nt(a, b):
    return lax.dot_general(a, b, (((1,), (1,)), ((), ())), preferred_element_type=F32)


def _group_sum(x, m):
    return _dot(x.astype(BF16), m)


def _params(n_parallel=1):
    return pltpu.CompilerParams(
        dimension_semantics=("arbitrary",) * n_parallel, vmem_limit_bytes=VMEM_LIMIT)


IN_PROJ_SUB_ROWS = 256


def _mla_qkv(c, kper, cs, wq_ref, wkv_ref, gc, gqk, q_ref, k_ref, v_ref, rows, q_scale):
    def lora_norm(z, g):
        ms = jnp.mean(z * z, axis=-1, keepdims=True)
        return (z * lax.rsqrt(ms + RMS_EPS) * g).astype(BF16)

    cq = lora_norm(c[:, :Q_LORA_RANK], gc[:, :Q_LORA_RANK])
    ckv = lora_norm(c[:, Q_LORA_RANK:], gc[:, Q_LORA_RANK:])
    qb = _dot(cq, wq_ref[...])
    kvb = _dot(ckv, wkv_ref[...])
    n = c.shape[0]

    row = lax.broadcasted_iota(jnp.int32, (LANES, LANES), 0)
    ones_all = jnp.ones((LANES, LANES), BF16)
    ones_lo = jnp.where(row < QK_ROPE_DIM, 1.0, 0.0).astype(BF16)
    lane = lax.broadcasted_iota(jnp.int32, (n, LANES), 1)

    def rope(z, g_row):
        t = z * (g_row * cs)
        return jnp.where(lane < QK_ROPE_DIM, t + pltpu.roll(t, QK_ROPE_DIM, 1), 0.0)

    k_rope = rope(kper, gqk[3:4])
    kpe_ss = _group_sum(kper * kper, ones_lo)
    for h in range(B_HEADS):
        qn = qb[:, 2 * LANES * h: 2 * LANES * h + LANES]
        qr = qb[:, 2 * LANES * h + LANES: 2 * LANES * (h + 1)]
        ss = _group_sum(qn * qn, ones_all) + _group_sum(qr * qr, ones_lo)
        rs = lax.rsqrt(ss * (1.0 / QK_HEAD_DIM) + RMS_EPS) * q_scale
        q_ref[h, rows, :LANES] = (qn * gqk[0:1] * rs).astype(BF16)
        q_ref[h, rows, LANES:] = (rope(qr, gqk[1:2]) * rs).astype(BF16)
        kn = kvb[:, LANES * h: LANES * (h + 1)]
        ssk = _group_sum(kn * kn, ones_all) + kpe_ss
        rsk = lax.rsqrt(ssk * (1.0 / QK_HEAD_DIM) + RMS_EPS)
        k_ref[h, rows, :LANES] = (kn * gqk[2:3] * rsk).astype(BF16)
        k_ref[h, rows, LANES:] = (k_rope * rsk).astype(BF16)
    v_ref[rows, :] = kvb[:, B_HEADS * QK_NOPE_DIM:].astype(BF16)


def _in_proj_kernel(x_ref, g_ref, w_ref, cs_ref, wq_ref, wkv_ref, gc_ref, gqk_ref,
                    a_ref, q_ref, k_ref, v_ref, *, q_scale):
    a_width = a_ref.shape[1]
    c_width = Q_LORA_RANK + KV_LORA_RANK
    for r0 in range(0, x_ref.shape[0], IN_PROJ_SUB_ROWS):
        rows = slice(r0, r0 + IN_PROJ_SUB_ROWS)
        x = x_ref[rows, :]
        ms = jnp.mean(x * x, axis=-1, keepdims=True)
        h = (x * lax.rsqrt(ms + RMS_EPS) * g_ref[...]).astype(BF16)
        proj = _dot(h, w_ref[...])
        a_ref[rows, :] = proj[:, :a_width]
        _mla_qkv(proj[:, a_width:a_width + c_width], proj[:, a_width + c_width:], cs_ref[rows, :],
                 wq_ref, wkv_ref, gc_ref[...], gqk_ref[...], q_ref, k_ref, v_ref, rows, q_scale)


def _in_proj(x2, g, w, cs, wq, wkv, gc, gqk, seq, tm):
    t, d = x2.shape
    a_width = 3 * A_WIDTH
    pos_blocks = seq // tm
    kernel = functools.partial(_in_proj_kernel, q_scale=QK_HEAD_DIM ** -0.5 * LOG2E)
    return pl.pallas_call(
        kernel,
        grid=(t // tm,),
        in_specs=[pl.BlockSpec((tm, d), lambda i: (i, 0)),
                  pl.BlockSpec((1, d), lambda i: (0, 0)),
                  pl.BlockSpec(w.shape, lambda i: (0, 0)),
                  pl.BlockSpec((tm, LANES), lambda i: (i % pos_blocks, 0)),
                  pl.BlockSpec(wq.shape, lambda i: (0, 0)),
                  pl.BlockSpec(wkv.shape, lambda i: (0, 0)),
                  pl.BlockSpec(gc.shape, lambda i: (0, 0)),
                  pl.BlockSpec(gqk.shape, lambda i: (0, 0))],
        out_specs=[pl.BlockSpec((tm, a_width), lambda i: (i, 0)),
                   pl.BlockSpec((B_HEADS, tm, 2 * LANES), lambda i: (0, i, 0)),
                   pl.BlockSpec((B_HEADS, tm, 2 * LANES), lambda i: (0, i, 0)),
                   pl.BlockSpec((tm, B_WIDTH), lambda i: (i, 0))],
        out_shape=[jax.ShapeDtypeStruct((t, a_width), F32),
                   jax.ShapeDtypeStruct((B_HEADS, t, 2 * LANES), BF16),
                   jax.ShapeDtypeStruct((B_HEADS, t, 2 * LANES), BF16),
                   jax.ShapeDtypeStruct((t, B_WIDTH), BF16)],
        compiler_params=_params(),
        name="in_proj",
    )(x2, g, w, cs, wq, wkv, gc, gqk)


def _mla_attention(q_ref, k_ref, v_ref, o_ref, *, tq):
    seq = q_ref.shape[0]
    row = lax.broadcasted_iota(jnp.int32, (tq, tq), 0)
    col = lax.broadcasted_iota(jnp.int32, (tq, tq), 1)
    diag_mask = jnp.where(col <= row, 0.0, MASK_VALUE)
    v_ext = jnp.concatenate([v_ref[...], jnp.ones(v_ref.shape, v_ref.dtype)], axis=1)
    dv = v_ref.shape[1]
    for i in reversed(range(seq // tq)):
        q = q_ref[i * tq:(i + 1) * tq, :]
        s_diag = _dot_nt(q, k_ref[i * tq:(i + 1) * tq, :]) + diag_mask
        m = jnp.max(s_diag, axis=-1, keepdims=True)
        if i > 0:
            s_past = _dot_nt(q, k_ref[:i * tq, :])
            m = jnp.maximum(m, jnp.max(s_past, axis=-1, keepdims=True))
        o = _dot(jnp.exp2(s_diag - m).astype(BF16), v_ext[i * tq:(i + 1) * tq, :])
        if i > 0:
            o = o + _dot(jnp.exp2(s_past - m).astype(BF16), v_ext[:i * tq, :])
        o_ref[i * tq:(i + 1) * tq, :] = (o[:, :dv] / o[:, dv:]).astype(o_ref.dtype)


def _dilated_attention(q_ref, k_ref, v_ref, bd_ref, g_ref, bias_ref, o_ref, qs, ks, acc_o, acc_l, acc_m,
                       *, unroll):
    seq = q_ref.shape[0]
    n_tiles = seq // BAND_BLOCK
    low = lax.broadcasted_iota(jnp.int32, (BAND_BLOCK, LANES), 1) < A_HEAD_DIM
    bd = bd_ref[...]
    g = g_ref[...]

    def head_norm(z, g_row):
        ss = _group_sum(z * z, bd)
        return z * lax.rsqrt(ss * (1.0 / A_HEAD_DIM) + RMS_EPS) * g_row

    qs[...] = head_norm(q_ref[...], g[0:1]) * (A_HEAD_DIM ** -0.5 * LOG2E)
    ks[...] = head_norm(k_ref[...], g[1:2])
    ones = jnp.ones((2 * BAND_BLOCK, LANES), BF16)

    for p, (window, dil) in reversed(list(enumerate(DILATED_PATTERNS))):
        blocks_per_class = n_tiles // dil
        span = BAND_BLOCK * dil

        def rows_at(start, dil=dil):
            if dil == 1:
                return pl.ds(start, BAND_BLOCK)
            return pl.ds(start, BAND_BLOCK, stride=dil)

        def tile(i, carry, p=p, blocks_per_class=blocks_per_class, span=span, rows_at=rows_at):
            r = i // blocks_per_class
            j = i % blocks_per_class
            cur = rows_at(r + j * span)
            prev = rows_at(r + jnp.maximum(j - 1, 0) * span)
            first = jnp.where(j == 0, 1, 0)
            q = qs[cur, :]
            q2 = jnp.concatenate([jnp.where(low, q, 0.0), jnp.where(low, 0.0, q)], axis=0).astype(BF16)
            k_band = jnp.concatenate([ks[prev, :], ks[cur, :]], axis=0).astype(BF16)
            s = _dot_nt(q2, k_band) + bias_ref[p, first]
            m = jnp.max(s, axis=-1, keepdims=True)
            pr = jnp.exp2(s - m).astype(BF16)
            v_band = jnp.concatenate([v_ref[prev, :], v_ref[cur, :]], axis=0).astype(BF16)
            o = _dot(pr, jnp.concatenate([v_band, ones], axis=1))
            top, bot = o[:BAND_BLOCK], o[BAND_BLOCK:]
            acc_o[p, cur, :] = jnp.where(low, top[:, :LANES], bot[:, :LANES])
            acc_l[p, cur, :] = jnp.where(low, top[:, LANES:], bot[:, LANES:])
            acc_m[p, cur, :] = jnp.where(low, m[:BAND_BLOCK], m[BAND_BLOCK:])
            return carry

        lax.fori_loop(0, n_tiles, tile, 0, unroll=unroll)

    m_all = jnp.maximum(jnp.maximum(acc_m[0], acc_m[1]), acc_m[2])
    num = jnp.zeros((seq, LANES), F32)
    den = jnp.zeros((seq, LANES), F32)
    for p in range(len(DILATED_PATTERNS)):
        w = jnp.exp2(acc_m[p] - m_all)
        num = num + w * acc_o[p]
        den = den + w * acc_l[p]
    o_ref[...] = (num / den).astype(o_ref.dtype)


def _attention_kernel(qa_ref, ka_ref, va_ref, bd_ref, g_ref, bias_ref, qb_ref, kb_ref, vb_ref,
                      oa_ref, ob_ref, qs, ks, acc_o, acc_l, acc_m, *, unroll, tq):
    _mla_attention(qb_ref, kb_ref, vb_ref, ob_ref, tq=tq)
    _dilated_attention(qa_ref, ka_ref, va_ref, bd_ref, g_ref, bias_ref, oa_ref, qs, ks, acc_o, acc_l, acc_m,
                       unroll=unroll)


def _attention(proj, bd, g, bias, qb, kb, vb, batch, seq, tq):
    t = proj.shape[0]
    pairs = A_WIDTH // LANES
    assert pairs == B_HEADS
    n_pat = len(DILATED_PATTERNS)
    return pl.pallas_call(
        functools.partial(_attention_kernel, unroll=16, tq=tq),
        grid=(batch, pairs),
        in_specs=[pl.BlockSpec((seq, LANES), lambda b, c: (b, c)),
                  pl.BlockSpec((seq, LANES), lambda b, c: (b, pairs + c)),
                  pl.BlockSpec((seq, LANES), lambda b, c: (b, 2 * pairs + c)),
                  pl.BlockSpec((LANES, LANES), lambda b, c: (0, 0)),
                  pl.BlockSpec((None, 2, LANES), lambda b, c: (c, 0, 0)),
                  pl.BlockSpec((None, n_pat, 2, 2 * BAND_BLOCK, 2 * BAND_BLOCK),
                               lambda b, c: (c, 0, 0, 0, 0)),
                  pl.BlockSpec((None, seq, 2 * LANES), lambda b, h: (h, b, 0)),
                  pl.BlockSpec((None, seq, 2 * LANES), lambda b, h: (h, b, 0)),
                  pl.BlockSpec((seq, V_HEAD_DIM), lambda b, h: (b, h))],
        out_specs=[pl.BlockSpec((seq, LANES), lambda b, c: (b, c)),
                   pl.BlockSpec((seq, V_HEAD_DIM), lambda b, h: (b, h))],
        out_shape=[jax.ShapeDtypeStruct((t, A_WIDTH), BF16),
                   jax.ShapeDtypeStruct((t, B_WIDTH), BF16)],
        scratch_shapes=[pltpu.VMEM((seq, LANES), F32)] * 2
                       + [pltpu.VMEM((n_pat, seq, LANES), F32)] * 3,
        compiler_params=_params(2),
        name="attention",
    )(proj, proj, proj, bd, g, bias, qb, kb, vb)


def _t5_bucket(dist):
    max_exact = NUM_BUCKETS // 2
    df = jnp.maximum(dist, 1).astype(F32)
    log_bucket = max_exact + (jnp.log(df / max_exact) / math.log(MAX_DISTANCE / max_exact)
                              * (NUM_BUCKETS - max_exact)).astype(jnp.int32)
    log_bucket = jnp.minimum(log_bucket, NUM_BUCKETS - 1)
    return jnp.where(dist < max_exact, dist, log_bucket)


def _band_bias(rel_bias):
    n = BAND_BLOCK
    qi = jnp.arange(n)[:, None]
    kj = jnp.arange(n)[None, :]
    buckets = jnp.arange(NUM_BUCKETS)
    tables = []
    for window, dil in DILATED_PATTERNS:
        steps = window // dil
        halves = []
        for back in (qi - kj + n, qi - kj):
            onehot = (_t5_bucket(jnp.maximum(back, 0) * dil)[:, :, None] == buckets).astype(F32)
            vals = jnp.einsum('qkb,bh->hqk', onehot, rel_bias.astype(F32),
                              precision=lax.Precision.HIGHEST) * LOG2E
            halves.append(jnp.where(((back >= 0) & (back <= steps))[None], vals, MASK_VALUE))
        prev, cur = halves
        normal = jnp.concatenate([prev, cur], axis=-1)
        first = jnp.concatenate([jnp.full_like(prev, MASK_VALUE), cur], axis=-1)
        tables.append(jnp.stack([normal, first], axis=1))
    tab = jnp.stack(tables, axis=1)
    tab = tab.reshape(A_HEADS // 2, 2, len(DILATED_PATTERNS), 2, n, 2 * n)
    return tab.transpose(0, 2, 3, 1, 4, 5).reshape(A_HEADS // 2, len(DILATED_PATTERNS), 2, 2 * n, 2 * n)


def _pack_pairs(v):
    half = v.shape[1] // 2
    bits = pltpu.bitcast(v, jnp.uint32)
    return (bits[:, :half] >> 16) | (bits[:, half:] & jnp.uint32(0xFFFF0000))


def _unpack_pairs(p):
    return (pltpu.bitcast(p << 16, F32), pltpu.bitcast(p & jnp.uint32(0xFFFF0000), F32))


def _out_proj_kernel(x_ref, a_ref, b_ref, w_ref, g_ref, rw_ref, rb_ref,
                     x1_ref, hp_ref, route_ref, route_t_ref, blk_ref, carry):
    i = pl.program_id(0)
    tm = x_ref.shape[0]

    @pl.when(i == 0)
    def _():
        carry[...] = jnp.zeros_like(carry)

    w = w_ref[...]
    x1 = x_ref[...] + _dot(a_ref[...], w[:A_WIDTH]) + _dot(b_ref[...], w[A_WIDTH:])
    x1_ref[...] = x1
    ms = jnp.mean(x1 * x1, axis=-1, keepdims=True)
    h = x1 * lax.rsqrt(ms + RMS_EPS) * g_ref[...]
    hi = h.astype(BF16)
    hi_f = hi.astype(F32)
    hp_ref[...] = _pack_pairs(hi_f)
    lo = (h - hi_f).astype(BF16)
    rw = rw_ref[...]
    hw = _dot(hi, rw)
    logits = hw[:, :LANES] + hw[:, LANES:] + _dot(lo, rw[:, :LANES]) + rb_ref[...]

    lane = lax.broadcasted_iota(jnp.int32, (tm, LANES), 1).astype(F32)
    remaining = logits
    vals, hots = [], []
    for _ in range(TOP_K):
        m = jnp.max(remaining, axis=-1, keepdims=True)
        first = jnp.min(jnp.where(remaining == m, lane, float(LANES)), axis=-1, keepdims=True)
        hot = lane == first
        remaining = jnp.where(hot, -jnp.inf, remaining)
        vals.append(m)
        hots.append(hot)
    exps = [jnp.exp(v - vals[0]) for v in vals]
    den = exps[0] + exps[1] + exps[2] + exps[3]

    chosen = jnp.zeros((tm, LANES), F32)
    for hot in hots:
        chosen = chosen + jnp.where(hot, 1.0, 0.0)
    r = lax.broadcasted_iota(jnp.int32, (tm, tm), 0)
    c = lax.broadcasted_iota(jnp.int32, (tm, tm), 1)
    earlier = jnp.where(r > c, 1.0, 0.0).astype(BF16)
    before = carry[...] + _dot(earlier, chosen.astype(BF16))
    in_block = jnp.sum(chosen, axis=0, keepdims=True)
    carry[...] = carry[...] + in_block
    blk_ref[...] = jnp.broadcast_to(in_block, blk_ref.shape)

    route = jnp.zeros((tm, LANES), F32)
    for k in range(TOP_K):
        first = jnp.sum(jnp.where(hots[k], lane, 0.0), axis=-1, keepdims=True)
        rank = jnp.sum(jnp.where(hots[k], before, 0.0), axis=-1, keepdims=True)
        route = route + jnp.where(lane == float(k), first, 0.0)
        route = route + jnp.where(lane == float(TOP_K + k), rank, 0.0)
        route = route + jnp.where(lane == float(2 * TOP_K + k), exps[k] / den, 0.0)
    route_ref[...] = route
    route_t_ref[...] = route.T[:route_t_ref.shape[0], :]


def _out_proj(x2, out_a, out_b, w, g, rw, rb, tm):
    t, d = x2.shape
    return pl.pallas_call(
        _out_proj_kernel,
        grid=(t // tm,),
        in_specs=[pl.BlockSpec((tm, d), lambda i: (i, 0)),
                  pl.BlockSpec((tm, A_WIDTH), lambda i: (i, 0)),
                  pl.BlockSpec((tm, B_WIDTH), lambda i: (i, 0)),
                  pl.BlockSpec(w.shape, lambda i: (0, 0)),
                  pl.BlockSpec((1, d), lambda i: (0, 0)),
                  pl.BlockSpec(rw.shape, lambda i: (0, 0)),
                  pl.BlockSpec((1, LANES), lambda i: (0, 0))],
        out_specs=[pl.BlockSpec((tm, d), lambda i: (i, 0)),
                   pl.BlockSpec((tm, d // 2), lambda i: (i, 0)),
                   pl.BlockSpec((tm, LANES), lambda i: (i, 0)),
                   pl.BlockSpec((2 * TOP_K, tm), lambda i: (0, i)),
                   pl.BlockSpec((None, 8, LANES), lambda i: (i, 0, 0))],
        out_shape=[jax.ShapeDtypeStruct((t, d), F32),
                   jax.ShapeDtypeStruct((t, d // 2), jnp.uint32),
                   jax.ShapeDtypeStruct((t, LANES), F32),
                   jax.ShapeDtypeStruct((2 * TOP_K, t), F32),
                   jax.ShapeDtypeStruct((t // tm, 8, LANES), F32)],
        scratch_shapes=[pltpu.VMEM((1, LANES), F32)],
        compiler_params=_params(),
        name="out_proj",
    )(x2, out_a, out_b, w, g, rw, rb)


DISPATCH_ROWS = 2048


def _dispatch_kernel(pad_ref, end_ref, nb_ref, dest_hbm, h_ref, xs_hbm, dest_s0, dest_s1, zeros,
                     sem_i, sem_z, sem_o, *, bm):
    i = pl.program_id(0)
    steps = pl.num_programs(0)
    tm = h_ref.shape[0]
    n_blocks = xs_hbm.shape[0] // bm
    dest_s = (dest_s0, dest_s1)

    def idx_copy(step, slot):
        return pltpu.make_async_copy(dest_hbm.at[step], dest_s[slot], sem_i.at[slot])

    @pl.when(i == 0)
    def _():
        idx_copy(0, 0).start()
        zeros[...] = jnp.zeros_like(zeros)

        def zero_block(start):
            return pltpu.make_async_copy(zeros, xs_hbm.at[pl.ds(pl.multiple_of(start, bm), bm), :], sem_z)

        for e in range(N_EXPERTS):
            @pl.when(pad_ref[e] > 0)
            def _():
                zero_block(end_ref[e] - bm).start()

        def start_unused(b, carry):
            zero_block(b * bm).start()
            return carry

        def wait_unused(b, carry):
            zero_block(b * bm).wait()
            return carry

        lax.fori_loop(nb_ref[0], n_blocks, start_unused, 0)
        for e in range(N_EXPERTS):
            @pl.when(pad_ref[e] > 0)
            def _():
                zero_block(end_ref[e] - bm).wait()
        lax.fori_loop(nb_ref[0], n_blocks, wait_unused, 0)

    for slot in range(2):
        @pl.when(i % 2 == slot)
        def _(slot=slot):
            @pl.when(i + 1 < steps)
            def _():
                idx_copy(i + 1, 1 - slot).start()

            idx_copy(i, slot).wait()

            def body(t, carry):
                for k in range(TOP_K):
                    pltpu.make_async_copy(h_ref.at[pl.ds(t, 1), :],
                                          xs_hbm.at[pl.ds(dest_s[slot][k * tm + t], 1), :],
                                          sem_o).start(priority=k % 2)
                return carry

            lax.fori_loop(0, tm, body, 0, unroll=32)
    for _ in range(TOP_K):
        pltpu.make_async_copy(h_ref, xs_hbm.at[pl.ds(0, tm), :], sem_o).wait()


def _dispatch(padded, pad_end, n_used, dest, hp, n_slots, tm, bm):
    t, half = hp.shape
    grid_spec = pltpu.PrefetchScalarGridSpec(
        num_scalar_prefetch=3,
        grid=(t // tm,),
        in_specs=[pl.BlockSpec(memory_space=pl.ANY),
                  pl.BlockSpec((tm, half), lambda i, p, e, n: (i, 0))],
        out_specs=pl.BlockSpec(memory_space=pl.ANY),
        scratch_shapes=[pltpu.SMEM((tm * TOP_K,), jnp.int32),
                        pltpu.SMEM((tm * TOP_K,), jnp.int32),
                        pltpu.VMEM((bm, half), jnp.uint32),
                        pltpu.SemaphoreType.DMA((2,)),
                        pltpu.SemaphoreType.DMA(()),
                        pltpu.SemaphoreType.DMA(())],
    )
    return pl.pallas_call(
        functools.partial(_dispatch_kernel, bm=bm),
        grid_spec=grid_spec,
        out_shape=jax.ShapeDtypeStruct((n_slots, half), jnp.uint32),
        compiler_params=_params(),
        name="dispatch",
    )(padded, pad_end, n_used, dest, hp)


EXPERT_SUB_ROWS = 256


def _expert_ffn_kernel(be_ref, nb_ref, next_ref, x_ref, wgu_hbm, bgu_ref, wd_hbm, bd_ref, y_ref,
                       wgu_f, wd_f, wgu_s, wd_s, sem_w):
    i = pl.program_id(0)
    d_ff = wd_f.shape[0]
    half = x_ref.shape[1]

    def weight_copies(e):
        return (pltpu.make_async_copy(wgu_hbm.at[e], wgu_f, sem_w.at[0]),
                pltpu.make_async_copy(wd_hbm.at[e], wd_f, sem_w.at[1]))

    @pl.when(jnp.logical_and(i == 0, nb_ref[0] > 0))
    def _():
        for copy in weight_copies(be_ref[0]):
            copy.start()

    @pl.when(i >= nb_ref[0])
    def _():
        y_ref[...] = jnp.zeros_like(y_ref)

    @pl.when(i < nb_ref[0])
    def _():
        changed = jnp.logical_or(i == 0, be_ref[i] != be_ref[jnp.maximum(i - 1, 0)])

        @pl.when(changed)
        def _():
            for copy in weight_copies(be_ref[i]):
                copy.wait()
            wgu_s[...] = wgu_f[...].astype(BF16)
            wd_s[...] = wd_f[...].astype(BF16)

            @pl.when(next_ref[i] >= 0)
            def _():
                for copy in weight_copies(next_ref[i]):
                    copy.start()

        for r0 in range(0, x_ref.shape[0], EXPERT_SUB_ROWS):
            rows = slice(r0, r0 + EXPERT_SUB_ROWS)
            x_lo, x_hi = _unpack_pairs(x_ref[rows, :])
            gu = (_dot(x_lo.astype(BF16), wgu_s[:half, :]) + _dot(x_hi.astype(BF16), wgu_s[half:, :])
                  + bgu_ref[...])
            gate = jnp.minimum(gu[:, :d_ff], SWIGLU_LIMIT)
            up = jnp.clip(gu[:, d_ff:], -SWIGLU_LIMIT, SWIGLU_LIMIT)
            glu = gate * jax.nn.sigmoid(SWIGLU_ALPHA * gate)
            act = ((up + 1.0) * glu).astype(BF16)
            y = _dot(act, wd_s[...]) + bd_ref[...]
            y_ref[rows, :] = _pack_pairs(y.astype(BF16).astype(F32))


def _expert_ffn(block_expert, n_used, next_expert, xs, w_gate_up, b_gate_up, w_down, b_down, bm):
    n_slots, half = xs.shape
    n_e, d, two_ff = w_gate_up.shape
    d_ff = two_ff // 2
    n_blocks = n_slots // bm

    def used(i, nb):
        return jnp.minimum(i, jnp.maximum(nb[0] - 1, 0))

    grid_spec = pltpu.PrefetchScalarGridSpec(
        num_scalar_prefetch=3,
        grid=(n_blocks,),
        in_specs=[pl.BlockSpec((bm, half), lambda i, be, nb, nx: (used(i, nb), 0)),
                  pl.BlockSpec(memory_space=pl.ANY),
                  pl.BlockSpec((None, 1, two_ff), lambda i, be, nb, nx: (be[i], 0, 0)),
                  pl.BlockSpec(memory_space=pl.ANY),
                  pl.BlockSpec((None, 1, d), lambda i, be, nb, nx: (be[i], 0, 0))],
        out_specs=pl.BlockSpec((bm, half), lambda i, be, nb, nx: (i, 0)),
        scratch_shapes=[pltpu.VMEM((d, two_ff), F32), pltpu.VMEM((d_ff, d), F32),
                        pltpu.VMEM((d, two_ff), BF16), pltpu.VMEM((d_ff, d), BF16),
                        pltpu.SemaphoreType.DMA((2,))],
    )
    return pl.pallas_call(
        _expert_ffn_kernel,
        grid_spec=grid_spec,
        out_shape=jax.ShapeDtypeStruct((n_slots, half), jnp.uint32),
        compiler_params=_params(),
        name="expert_ffn",
    )(block_expert, n_used, next_expert, xs, w_gate_up, b_gate_up.reshape(n_e, 1, two_ff),
      w_down, b_down.reshape(n_e, 1, d))


COMBINE_LOCAL_ROWS = 2560
SEG_ALIGN = 8


def _pieces(limit):
    sizes = []
    size = 1 << (limit.bit_length() - 1)
    while size >= SEG_ALIGN:
        sizes.append(size)
        size //= 2
    return sizes


def _combine_kernel(src_ref, len_ref, dst_ref, tot_ref, route_ref, local_ref, x1_ref, ys_hbm, o_ref, ybuf,
                    sem_g):
    j = pl.program_id(0)
    blocks = pl.num_programs(0) - 1
    tm = x1_ref.shape[0]
    half = x1_ref.shape[1] // 2
    local_rows = ybuf.shape[1]

    @pl.when(j == 0)
    def _():
        ybuf[...] = jnp.zeros_like(ybuf)

    @pl.when(j < blocks)
    def _():
        for s in range(2):
            @pl.when(j % 2 == s)
            def _(s=s):
                for e in range(N_EXPERTS):
                    seg = j * N_EXPERTS + e
                    n = len_ref[seg]
                    src = src_ref[seg]
                    dst = dst_ref[seg]
                    for size in _pieces(tm + 2 * SEG_ALIGN):
                        @pl.when((n & size) != 0)
                        def _(size=size, src=src, dst=dst):
                            pltpu.make_async_copy(
                                ys_hbm.at[pl.ds(pl.multiple_of(src, SEG_ALIGN), size), :],
                                ybuf.at[s, pl.ds(pl.multiple_of(dst, SEG_ALIGN), size), :],
                                sem_g.at[s]).start()
                        step = jnp.where((n & size) != 0, size, 0)
                        src = src + step
                        dst = dst + step

    @pl.when(j >= 1)
    def _():
        slot = (j - 1) % 2
        total = tot_ref[j - 1]
        for size in _pieces(local_rows):
            @pl.when((total & size) != 0)
            def _(size=size):
                pltpu.make_async_copy(ys_hbm.at[pl.ds(0, size), :], ybuf.at[slot, pl.ds(0, size), :],
                                      sem_g.at[slot]).wait()

        route = route_ref[...]
        local_t = local_ref[...]
        local = jnp.concatenate(
            [local_t, jnp.zeros((LANES - local_t.shape[0], tm), F32)], axis=0).T
        col = lax.broadcasted_iota(jnp.int32, (tm, local_rows), 1).astype(F32)
        g = jnp.zeros((tm, local_rows), F32)
        for k in range(TOP_K):
            pos = local[:, k: k + 1]
            gate = route[:, 2 * TOP_K + k: 2 * TOP_K + k + 1]
            g = jnp.where(col == pos, gate, g)
        g = g.astype(BF16)
        lo, hi = _unpack_pairs(ybuf[slot])
        o_ref[:, :half] = x1_ref[:, :half] + _dot(g, lo.astype(BF16))
        o_ref[:, half:] = x1_ref[:, half:] + _dot(g, hi.astype(BF16))


def _combine(tables, route, local_t, x1, ys, tm, local_rows):
    t, d = x1.shape
    half = d // 2

    def summed(j, *_):
        return (jnp.maximum(j - 1, 0), 0)

    grid_spec = pltpu.PrefetchScalarGridSpec(
        num_scalar_prefetch=4,
        grid=(t // tm + 1,),
        in_specs=[pl.BlockSpec((tm, LANES), summed),
                  pl.BlockSpec((local_t.shape[0], tm), lambda j, *_: (0, jnp.maximum(j - 1, 0))),
                  pl.BlockSpec((tm, d), summed),
                  pl.BlockSpec(memory_space=pl.ANY)],
        out_specs=pl.BlockSpec((tm, d), summed),
        scratch_shapes=[pltpu.VMEM((2, local_rows, half), jnp.uint32),
                        pltpu.SemaphoreType.DMA((2,))],
    )
    return pl.pallas_call(
        _combine_kernel,
        grid_spec=grid_spec,
        out_shape=jax.ShapeDtypeStruct((t, d), F32),
        compiler_params=_params(),
        name="combine",
    )(*tables, route, local_t, x1, ys)


def _per_expert(table, idx):
    out = jnp.zeros(idx.shape, table.dtype)
    for e in range(N_EXPERTS):
        out = jnp.where(idx == e, table[:, e][None, :, None], out)
    return out


def _combine_tables(idx, rank, n, pad_start, tm, local_rows):
    before = jnp.cumsum(n, axis=0) - n
    first = pad_start[None, :] + before
    src = first // SEG_ALIGN * SEG_ALIGN
    length = jnp.where(n > 0, (first + n + SEG_ALIGN - 1) // SEG_ALIGN * SEG_ALIGN - src, 0)
    dst = jnp.cumsum(length, axis=1) - length
    total = jnp.sum(length, axis=1)
    assert local_rows >= tm * TOP_K + N_EXPERTS * 2 * (SEG_ALIGN - 1)
    local = (_per_expert(dst + first - src - before, idx) + rank).reshape(TOP_K, -1).astype(F32)
    tables = (src.reshape(-1).astype(jnp.int32), length.reshape(-1).astype(jnp.int32),
              dst.reshape(-1).astype(jnp.int32), total.astype(jnp.int32))
    return tables, local


def _slot_layout(route_t, n, tm, bm):
    t = route_t.shape[1]
    blocks = t // tm
    idx = route_t[:TOP_K].astype(jnp.int32).reshape(TOP_K, blocks, tm)
    rank = route_t[TOP_K:].astype(jnp.int32).reshape(TOP_K, blocks, tm)
    counts = jnp.sum(n, axis=0)
    padded = (counts + bm - 1) // bm * bm
    pad_end = jnp.cumsum(padded).astype(jnp.int32)
    pad_start = pad_end - padded
    dest = _per_expert(jnp.broadcast_to(pad_start[None, :], n.shape), idx) + rank
    dest = dest.reshape(TOP_K, t // DISPATCH_ROWS, DISPATCH_ROWS).transpose(1, 0, 2)
    dest = dest.reshape(t // DISPATCH_ROWS, TOP_K * DISPATCH_ROWS)
    n_blocks = -(-t * TOP_K // bm) + N_EXPERTS
    block_start = jnp.arange(n_blocks, dtype=jnp.int32) * bm
    block_expert = jnp.minimum(jnp.sum(pad_end[None, :] <= block_start[:, None], axis=1),
                               N_EXPERTS - 1).astype(jnp.int32)
    n_used = (pad_end[-1:] // bm).astype(jnp.int32)
    following = jnp.take(pad_end, block_expert) // bm
    next_expert = jnp.where(following < n_used[0],
                            jnp.take(block_expert, jnp.minimum(following, n_blocks - 1)), -1).astype(jnp.int32)
    return idx, rank, padded, pad_end, dest, block_expert, next_expert, n_used, n_blocks * bm


def _rot_cols(w):
    half = w.shape[-1] // 2
    return jnp.concatenate([-w[..., half:], w[..., :half]], axis=-1)


def _swap_halves(g):
    half = g.shape[-1] // 2
    return jnp.concatenate([g[..., half:], g[..., :half]], axis=-1)


def kernel(x, attn_norm_g, w_in, a_q_norm_g, a_k_norm_g, rel_bias, q_a_norm_g, w_q_b, kv_a_norm_g,
           w_kv_b, b_q_norm_g, b_k_norm_g, w_out, ffn_norm_g, router_w, router_b, w_gate_up,
           b_gate_up, w_down, b_down):
    batch, seq, d = x.shape
    t = batch * seq
    depth = w_in.shape[0]
    tm = 512
    expert_bm = 512

    pos = jnp.arange(seq, dtype=F32)
    inv_freq = ROPE_THETA ** (-jnp.arange(0, QK_ROPE_DIM, 2, dtype=F32) / QK_ROPE_DIM)
    ang = pos[:, None] * inv_freq[None, :]
    cos, sin = jnp.cos(ang), jnp.sin(ang)
    cs = jnp.concatenate([cos, cos, sin, sin], axis=-1)

    row = jnp.arange(LANES)[:, None] // A_HEAD_DIM
    col = jnp.arange(LANES)[None, :] // A_HEAD_DIM
    head_block_ones = (row == col).astype(BF16)
    band_bias = _band_bias(rel_bias)

    x2 = x.reshape(t, d)
    for layer in range(depth):
        kpe_off = 3 * A_WIDTH + Q_LORA_RANK + KV_LORA_RANK
        w_kpe = w_in[layer][:, kpe_off:]
        w_in_r = jnp.concatenate([w_in[layer][:, :kpe_off], w_kpe, _rot_cols(w_kpe)], axis=1).astype(BF16)

        wq = w_q_b[layer].reshape(Q_LORA_RANK, B_HEADS, QK_HEAD_DIM)
        wq_rope = wq[..., QK_NOPE_DIM:]
        wq_r = jnp.concatenate([wq, _rot_cols(wq_rope)], axis=-1).reshape(Q_LORA_RANK, -1).astype(BF16)
        wkv = w_kv_b[layer].reshape(KV_LORA_RANK, B_HEADS, QK_NOPE_DIM + V_HEAD_DIM)
        wkv_r = jnp.concatenate([wkv[..., :QK_NOPE_DIM].reshape(KV_LORA_RANK, -1),
                                 wkv[..., QK_NOPE_DIM:].reshape(KV_LORA_RANK, -1)], axis=1).astype(BF16)
        gc = jnp.concatenate([q_a_norm_g[layer], kv_a_norm_g[layer]])[None, :]
        gq, gk = b_q_norm_g[layer], b_k_norm_g[layer]

        def rope_gain(gr):
            return jnp.concatenate([gr, _swap_halves(gr)])

        gqk = jnp.stack([gq[:QK_NOPE_DIM], rope_gain(gq[QK_NOPE_DIM:]),
                         gk[:QK_NOPE_DIM], rope_gain(gk[QK_NOPE_DIM:])])
        g_a = jnp.stack([jnp.tile(a_q_norm_g[layer], 2), jnp.tile(a_k_norm_g[layer], 2)])
        g_a = jnp.broadcast_to(g_a[None], (A_WIDTH // LANES, 2, LANES))

        proj, qb, kb, vb = _in_proj(x2, attn_norm_g[layer][None, :], w_in_r, cs, wq_r, wkv_r, gc, gqk, seq, tm)
        out_a, out_b = _attention(proj, head_block_ones, g_a, band_bias, qb, kb, vb, batch, seq, 512)

        rw = jnp.pad(router_w[layer], ((0, 0), (0, LANES - N_EXPERTS)))
        rw_hi = rw.astype(BF16)
        rw_lo = (rw - rw_hi.astype(F32)).astype(BF16)
        rb = jnp.pad(router_b[layer], (0, LANES - N_EXPERTS), constant_values=-jnp.inf)[None, :]
        x1, hp, route, route_t, in_block = _out_proj(
            x2, out_a, out_b, w_out[layer].astype(BF16), ffn_norm_g[layer][None, :],
            jnp.concatenate([rw_hi, rw_lo], axis=1), rb, tm)

        n = in_block[:, 0, :N_EXPERTS].astype(jnp.int32)
        idx, rank, padded, pad_end, dest, block_expert, next_expert, n_used, n_slots = _slot_layout(
            route_t, n, tm, expert_bm)
        xs = _dispatch(padded, pad_end, n_used, dest, hp, n_slots, DISPATCH_ROWS, expert_bm)
        ys = _expert_ffn(block_expert, n_used, next_expert, xs, w_gate_up[layer], b_gate_up[layer],
                         w_down[layer], b_down[layer], expert_bm)
        tables, local_t = _combine_tables(idx, rank, n, pad_end - padded, tm, COMBINE_LOCAL_ROWS)
        x2 = _combine(tables, route, local_t, x1, ys, tm, COMBINE_LOCAL_ROWS)
    return x2.reshape(batch, seq, d)
```
